```python
import math
import jax, jax.numpy as jnp
from jax import lax
import numpy as np

D_MODEL = 2048
BATCH = 2
SEQ = 8192
DEPTH = 4

CTX_LEN = 256
GRID_W = 64
N_MIXERS = 4
EPS = 1e-6
HYENA_ORDER = 2
FILTER_EMB = 33
FILTER_HIDDEN = 64
DECAY_TARGET = 1e-2
FAST_DECAY_PCT = 0.3
SLOW_DECAY_PCT = 1.5
MAX_DECAY = math.log(DECAY_TARGET) / FAST_DECAY_PCT
MIN_DECAY = math.log(DECAY_TARGET) / SLOW_DECAY_PCT
MOD_SHIFT = 0.0
FNET_GROUPS = 4
POOL_WINDOWS = (2, 4, 8, 16)
POOL_GROUP = D_MODEL // len(POOL_WINDOWS)
MLA_HEADS = 16
QK_NOPE = 128
QK_ROPE = 64
V_DIM = 128
Q_RANK = 512
KV_RANK = 512
ROPE_THETA = 10000.0
Q_BLOCK = 128
MLA_SCALE = (QK_NOPE + QK_ROPE) ** -0.5
N_GROUPS = 4
EXPERTS_PER_GROUP = 8
N_EXPERTS = N_GROUPS * EXPERTS_PER_GROUP
TOPK_EXPERT = 2
D_EXPERT = D_MODEL // 4
EXPERT_BLOCK = 128

kernel_name = 'hybrid_hyena_fnet_pool_mla_hmoe_dit'


def _n_occ(kind):
    return len(range(kind, DEPTH, N_MIXERS))


def rms_norm(x, g):
    xf = x.astype(jnp.float32)
    y = xf * lax.rsqrt(jnp.mean(xf * xf, axis=-1, keepdims=True) + EPS)
    return (y * g.astype(jnp.float32)).astype(x.dtype)


def modulate(h, shift, scale):
    return h * (1 + scale[:, None, :]) + shift[:, None, :]


def short_conv3(u, w, b):
    up = jnp.pad(u, ((0, 0), (1, 1), (0, 0)))
    return up[:, :-2] * w[0] + up[:, 1:-1] * w[1] + up[:, 2:] * w[2] + b


def hyena_filters(L, f_w1, f_b1, f_w2, f_b2, f_w3, f_freq):
    d = f_w3.shape[-1] // (HYENA_ORDER * 2)
    f32 = jnp.float32
    t = jnp.linspace(0.0, 1.0, L, dtype=f32)[:, None]
    bands = (FILTER_EMB - 1) // 2
    w = 2.0 * math.pi * jnp.arange(L, dtype=f32)[:, None] / L
    f = jnp.linspace(1e-4, bands - 1, bands, dtype=f32)[None, :]
    z = jnp.concatenate([t, jnp.cos(f * w), -jnp.sin(f * w)], axis=-1)
    hdn = jnp.sin(f_freq[0].astype(f32) * (z @ f_w1.astype(f32) + f_b1.astype(f32)))
    hdn = jnp.sin(f_freq[1].astype(f32) * (hdn @ f_w2.astype(f32) + f_b2.astype(f32)))
    filt = (hdn @ f_w3.astype(f32)).reshape(L, HYENA_ORDER, 2, d)
    deltas = jnp.abs(jnp.linspace(MIN_DECAY, MAX_DECAY, d, dtype=f32))
    filt = filt * (jnp.exp(-t * deltas)[:, None, None, :] + MOD_SHIFT)
    fwd, bwd = filt[:, :, 0], filt[:, :, 1]
    k2 = jnp.concatenate([fwd, jnp.zeros((1, HYENA_ORDER, d), f32), bwd[1:][::-1]], axis=0)
    return k2 * lax.rsqrt(jnp.sum(k2 * k2, axis=0, keepdims=True) + EPS)


def long_conv(u, k2, skip):
    L = u.shape[1]
    uf32 = u.astype(jnp.float32)
    uf = jnp.fft.rfft(uf32, n=2 * L, axis=1)
    kf = jnp.fft.rfft(k2, axis=0)
    y = jnp.fft.irfft(uf * kf[None], n=2 * L, axis=1)[:, :L]
    return (y + skip.astype(jnp.float32) * uf32).astype(u.dtype)


def hyena_mixer(h, w_in, b_in, conv_w, conv_b, f_w1, f_b1, f_w2, f_b2, f_w3, f_freq, skip, w_out, b_out):
    L = h.shape[1]
    z = short_conv3(h @ w_in + b_in, conv_w, conv_b)
    x1, x2, v = jnp.split(z, 3, axis=-1)
    k2 = hyena_filters(L, f_w1, f_b1, f_w2, f_b2, f_w3, f_freq)
    v = x1 * long_conv(v, k2[:, 0], skip[0])
    v = x2 * long_conv(v, k2[:, 1], skip[1])
    return v @ w_out + b_out


def fnet_mixer(h, w_out, b_out):
    B, L, D = h.shape
    hg = h.astype(jnp.float32).reshape(B, L, FNET_GROUPS, D // FNET_GROUPS)
    f = jnp.fft.fft2(hg, axes=(1, 3), norm='ortho').real
    return f.reshape(B, L, D).astype(h.dtype) @ w_out + b_out


def pool_mixer(h, w_grp, scale):
    B, L, D = h.shape
    hf = h.astype(jnp.float32)
    S = jnp.concatenate([jnp.zeros((B, 1, D), jnp.float32), jnp.cumsum(hf, axis=1)], axis=1)
    pos = jnp.arange(L)
    outs = []
    for gi, win in enumerate(POOL_WINDOWS):
        sl = slice(gi * POOL_GROUP, (gi + 1) * POOL_GROUP)
        lo = jnp.clip(pos - win // 2, 0, L)
        hi = jnp.clip(pos + win // 2, 0, L)
        Sg = S[..., sl]
        mean = (Sg[:, hi] - Sg[:, lo]) / (hi - lo).astype(jnp.float32)[None, :, None]
        outs.append(mean - hf[..., sl])
    pooled = jnp.stack(outs, axis=2)
    y = jnp.einsum('blgc,gcd->blgd', pooled, w_grp.astype(jnp.float32)).reshape(B, L, D)
    return (y * scale.astype(jnp.float32)).astype(h.dtype)


def axial_rope_tables(L):
    rows = L // GRID_W
    row = jnp.repeat(jnp.arange(rows, dtype=jnp.float32), GRID_W)
    col = jnp.tile(jnp.arange(GRID_W, dtype=jnp.float32), rows)
    half = QK_ROPE // 2
    inv = ROPE_THETA ** (-jnp.arange(0, half, 2, dtype=jnp.float32) / half)
    ang_r = row[:, None] * inv
    ang_c = col[:, None] * inv
    ang = jnp.concatenate([ang_r, ang_r, ang_c, ang_c], axis=-1)
    return jnp.cos(ang), jnp.sin(ang)


def apply_rope(x, cos, sin):
    q = QK_ROPE // 4
    x1, x2, x3, x4 = x[..., :q], x[..., q:2 * q], x[..., 2 * q:3 * q], x[..., 3 * q:]
    rot = jnp.concatenate([-x2, x1, -x4, x3], axis=-1)
    return x * cos.astype(x.dtype) + rot * sin.astype(x.dtype)


def mla_queries(h, w_dq, g_q, w_uq):
    B, L, _ = h.shape
    q = (rms_norm(h @ w_dq, g_q) @ w_uq).reshape(B, L, MLA_HEADS, QK_NOPE + QK_ROPE)
    return q[..., :QK_NOPE], q[..., QK_NOPE:]


def mla_keys_values(h, w_dkv, g_kv, w_ukv):
    B, L, _ = h.shape
    ckv = h @ w_dkv
    kv = (rms_norm(ckv[..., :KV_RANK], g_kv) @ w_ukv).reshape(B, L, MLA_HEADS, QK_NOPE + V_DIM)
    return kv[..., :QK_NOPE], ckv[..., KV_RANK:], kv[..., QK_NOPE:]


def mla_attend(q_nope, q_pe, k_nope, k_pe, v):
    s = jnp.einsum('bqhd,bkhd->bhqk', q_nope, k_nope) + jnp.einsum('bqhr,bkr->bhqk', q_pe, k_pe)
    p = jax.nn.softmax(s.astype(jnp.float32) * MLA_SCALE, axis=-1).astype(v.dtype)
    return jnp.einsum('bhqk,bkhd->bqhd', p, v)


def mla_mixer(h, hc, cos, sin, w_dq, g_q, w_uq, w_dkv, g_kv, w_ukv, w_o, need_ctx):
    B, L, _ = h.shape
    qn, qp = mla_queries(h, w_dq, g_q, w_uq)
    kn, kp, v = mla_keys_values(h, w_dkv, g_kv, w_ukv)
    qp = apply_rope(qp, cos[None, :, None, :], sin[None, :, None, :])
    kp = apply_rope(kp, cos[None], sin[None])
    knc, kpc, vc = mla_keys_values(hc, w_dkv, g_kv, w_ukv)
    k_all = jnp.concatenate([knc, kn], axis=1)
    kp_all = jnp.concatenate([kpc, kp], axis=1)
    v_all = jnp.concatenate([vc, v], axis=1)
    nb = L // Q_BLOCK
    qn_b = qn.reshape(B, nb, Q_BLOCK, MLA_HEADS, QK_NOPE).swapaxes(0, 1)
    qp_b = qp.reshape(B, nb, Q_BLOCK, MLA_HEADS, QK_ROPE).swapaxes(0, 1)
    out = lax.map(lambda a: mla_attend(a[0], a[1], k_all, kp_all, v_all), (qn_b, qp_b))
    out = out.swapaxes(0, 1).reshape(B, L, MLA_HEADS * V_DIM) @ w_o
    out_c = None
    if need_ctx:
        qnc, qpc = mla_queries(hc, w_dq, g_q, w_uq)
        out_c = mla_attend(qnc, qpc, knc, kpc, vc).reshape(B, hc.shape[1], MLA_HEADS * V_DIM) @ w_o
    return out, out_c


def routed_experts(t, eid, wts, w1, w3, w2):
    T, D = t.shape
    n_assign = eid.shape[0]
    tok = jnp.arange(n_assign, dtype=jnp.int32) // TOPK_EXPERT
    order = jnp.argsort(eid)
    e_sorted = eid[order]
    counts = jnp.zeros((N_EXPERTS,), jnp.int32).at[eid].add(1)
    padded = (counts + EXPERT_BLOCK - 1) // EXPERT_BLOCK * EXPERT_BLOCK
    start = jnp.cumsum(counts) - counts
    pend = jnp.cumsum(padded)
    pstart = pend - padded
    dest = pstart[e_sorted] + (jnp.arange(n_assign, dtype=jnp.int32) - start[e_sorted])
    n_blocks = (n_assign + N_EXPERTS * (EXPERT_BLOCK - 1) + EXPERT_BLOCK - 1) // EXPERT_BLOCK
    n_pad = n_blocks * EXPERT_BLOCK
    buf_tok = jnp.zeros((n_pad,), jnp.int32).at[dest].set(tok[order])
    buf_w = jnp.zeros((n_pad,), jnp.float32).at[dest].set(wts[order])
    blk_e = jnp.minimum(jnp.searchsorted(pend, jnp.arange(n_blocks, dtype=jnp.int32) * EXPERT_BLOCK, side='right'), N_EXPERTS - 1)

    def expert_block(a):
        idx, e = a
        xb = t[idx]
        return (jax.nn.silu(xb @ w1[e]) * (xb @ w3[e])) @ w2[e]

    out = lax.map(expert_block, (buf_tok.reshape(n_blocks, EXPERT_BLOCK), blk_e)).reshape(n_pad, D)
    out = out * buf_w.astype(out.dtype)[:, None]
    return jnp.zeros_like(t).at[buf_tok].add(out)


def hier_moe(t, w_group, b_group, w_expert, b_expert, w1, w3, w2):
    T = t.shape[0]
    g_logits = (t @ w_group).astype(jnp.float32) + b_group.astype(jnp.float32)
    p_grp, g_idx = lax.top_k(jax.nn.softmax(g_logits, axis=-1), 1)
    e_logits = ((t @ w_expert).astype(jnp.float32) + b_expert.astype(jnp.float32)).reshape(T, N_GROUPS, EXPERTS_PER_GROUP)
    e_logits = jnp.take_along_axis(e_logits, g_idx[:, :, None], axis=1)[:, 0]
    p_exp, e_idx = lax.top_k(jax.nn.softmax(e_logits, axis=-1), TOPK_EXPERT)
    p_exp = p_exp / jnp.sum(p_exp, axis=-1, keepdims=True)
    wts = (p_grp * p_exp).reshape(-1)
    eid = (g_idx * EXPERTS_PER_GROUP + e_idx).reshape(-1).astype(jnp.int32)
    return routed_experts(t, eid, wts, w1, w3, w2)


def setup_inputs(seed: int = 0) -> dict:
    key = jax.random.key(seed)
    ks = iter(jax.random.split(key, 64))
    D = D_MODEL

    def nrm(shape, scale=1.0):
        return jax.random.normal(next(ks), shape, jnp.float32) * scale

    nA, nB, nC, nD = _n_occ(0), _n_occ(1), _n_occ(2), _n_occ(3)
    return {
        'x': nrm((BATCH, SEQ, D)),
        'c': nrm((BATCH, D)),
        'ctx': nrm((BATCH, CTX_LEN, D)),
        'c_ctx': nrm((D,)),
        'ada_w': nrm((DEPTH, D, 6 * D), 0.5 * D ** -0.5),
        'ada_b': nrm((DEPTH, 6 * D), 0.02),
        'norm_g': 1.0 + nrm((DEPTH, 2, D), 0.1),
        'final_g': 1.0 + nrm((D,), 0.1),
        'hy_w_in': nrm((nA, D, 3 * D), D ** -0.5),
        'hy_b_in': nrm((nA, 3 * D), 0.02),
        'hy_conv_w': nrm((nA, 3, 3 * D), 3 ** -0.5),
        'hy_conv_b': nrm((nA, 3 * D), 0.02),
        'hy_f_w1': nrm((nA, FILTER_EMB, FILTER_HIDDEN), FILTER_EMB ** -0.5),
        'hy_f_b1': nrm((nA, FILTER_HIDDEN), 0.1),
        'hy_f_w2': nrm((nA, FILTER_HIDDEN, FILTER_HIDDEN), FILTER_HIDDEN ** -0.5),
        'hy_f_b2': nrm((nA, FILTER_HIDDEN), 0.1),
        'hy_f_w3': nrm((nA, FILTER_HIDDEN, HYENA_ORDER * 2 * D), FILTER_HIDDEN ** -0.5),
        'hy_f_freq': 1.0 + nrm((nA, 2, FILTER_HIDDEN), 0.1),
        'hy_skip': nrm((nA, HYENA_ORDER, D), 0.5),
        'hy_w_out': nrm((nA, D, D), D ** -0.5),
        'hy_b_out': nrm((nA, D), 0.02),
        'fn_w_out': nrm((nB, D, D), D ** -0.5),
        'fn_b_out': nrm((nB, D), 0.02),
        'pl_w': nrm((nC, len(POOL_WINDOWS), POOL_GROUP, POOL_GROUP), POOL_GROUP ** -0.5),
        'pl_scale': 1.0 + nrm((nC, D), 0.1),
        'mla_w_dq': nrm((nD, D, Q_RANK), D ** -0.5),
        'mla_g_q': 1.0 + nrm((nD, Q_RANK), 0.1),
        'mla_w_uq': nrm((nD, Q_RANK, MLA_HEADS * (QK_NOPE + QK_ROPE)), Q_RANK ** -0.5),
        'mla_w_dkv': nrm((nD, D, KV_RANK + QK_ROPE), D ** -0.5),
        'mla_g_kv': 1.0 + nrm((nD, KV_RANK), 0.1),
        'mla_w_ukv': nrm((nD, KV_RANK, MLA_HEADS * (QK_NOPE + V_DIM)), KV_RANK ** -0.5),
        'mla_w_o': nrm((nD, MLA_HEADS * V_DIM, D), (MLA_HEADS * V_DIM) ** -0.5),
        'moe_w_group': nrm((DEPTH, D, N_GROUPS), D ** -0.5),
        'moe_b_group': nrm((DEPTH, N_GROUPS), 0.01),
        'moe_w_expert': nrm((DEPTH, D, N_EXPERTS), D ** -0.5),
        'moe_b_expert': nrm((DEPTH, N_EXPERTS), 0.01),
        'moe_w1': nrm((DEPTH, N_EXPERTS, D, D_EXPERT), D ** -0.5),
        'moe_w3': nrm((DEPTH, N_EXPERTS, D, D_EXPERT), D ** -0.5),
        'moe_w2': nrm((DEPTH, N_EXPERTS, D_EXPERT, D), D_EXPERT ** -0.5),
    }


def reference(x, c, ctx, c_ctx, ada_w, ada_b, norm_g, final_g,
              hy_w_in, hy_b_in, hy_conv_w, hy_conv_b, hy_f_w1, hy_f_b1, hy_f_w2, hy_f_b2, hy_f_w3, hy_f_freq,
              hy_skip, hy_w_out, hy_b_out, fn_w_out, fn_b_out, pl_w, pl_scale,
              mla_w_dq, mla_g_q, mla_w_uq, mla_w_dkv, mla_g_kv, mla_w_ukv, mla_w_o,
              moe_w_group, moe_b_group, moe_w_expert, moe_b_expert, moe_w1, moe_w3, moe_w2):
    B, L, D = x.shape
    CL = ctx.shape[1]
    xc = ctx
    cos, sin = axial_rope_tables(L)
    s_lat = jax.nn.silu(c)
    s_ctx = jax.nn.silu(c_ctx)[None, :]
    for i in range(DEPTH):
        kind, j = i % N_MIXERS, i // N_MIXERS
        need_ctx = i < DEPTH - 1
        sh1, sc1, g1, sh2, sc2, g2 = jnp.split(s_lat @ ada_w[i] + ada_b[i], 6, axis=-1)
        csh1, csc1, cg1, csh2, csc2, cg2 = jnp.split(s_ctx @ ada_w[i] + ada_b[i], 6, axis=-1)
        h = modulate(rms_norm(x, norm_g[i, 0]), sh1, sc1)
        hc = modulate(rms_norm(xc, norm_g[i, 0]), csh1, csc1)
        mc = None
        if kind == 0:
            hy = (hy_w_in[j], hy_b_in[j], hy_conv_w[j], hy_conv_b[j], hy_f_w1[j], hy_f_b1[j], hy_f_w2[j],
                  hy_f_b2[j], hy_f_w3[j], hy_f_freq[j], hy_skip[j], hy_w_out[j], hy_b_out[j])
            m = hyena_mixer(h, *hy)
            if need_ctx:
                mc = hyena_mixer(hc, *hy)
        elif kind == 1:
            m = fnet_mixer(h, fn_w_out[j], fn_b_out[j])
            if need_ctx:
                mc = fnet_mixer(hc, fn_w_out[j], fn_b_out[j])
        elif kind == 2:
            m = pool_mixer(h, pl_w[j], pl_scale[j])
            if need_ctx:
                mc = pool_mixer(hc, pl_w[j], pl_scale[j])
        else:
            m, mc = mla_mixer(h, hc, cos, sin, mla_w_dq[j], mla_g_q[j], mla_w_uq[j], mla_w_dkv[j],
                              mla_g_kv[j], mla_w_ukv[j], mla_w_o[j], need_ctx)
        x = x + g1[:, None, :] * m
        if need_ctx:
            xc = xc + cg1[:, None, :] * mc
        moe_p = (moe_w_group[i], moe_b_group[i], moe_w_expert[i], moe_b_expert[i], moe_w1[i], moe_w3[i], moe_w2[i])
        h = modulate(rms_norm(x, norm_g[i, 1]), sh2, sc2)
        if need_ctx:
            hc = modulate(rms_norm(xc, norm_g[i, 1]), csh2, csc2)
            y_all = hier_moe(jnp.concatenate([hc.reshape(-1, D), h.reshape(-1, D)], axis=0), *moe_p)
            xc = xc + cg2[:, None, :] * y_all[:B * CL].reshape(B, CL, D)
            y = y_all[B * CL:].reshape(B, L, D)
        else:
            y = hier_moe(h.reshape(-1, D), *moe_p).reshape(B, L, D)
        x = x + g2[:, None, :] * y
    return rms_norm(x, final_g)
```

```python
import functools
import math

import numpy as np
import jax
import jax.numpy as jnp
from jax import lax
from jax.experimental import pallas as pl
from jax.experimental.pallas import tpu as pltpu

F32 = jnp.float32
BF16 = jnp.bfloat16
HIGHEST = lax.Precision.HIGHEST

EPS = 1e-6
LANES = 128
MOD_ROWS = 8
VMEM_LIMIT = 56 * 1024 * 1024

N_MIXERS = 4
HYENA_ORDER = 2
FILTER_EMB = 33
DECAY_TARGET = 1e-2
FAST_DECAY_PCT = 0.3
SLOW_DECAY_PCT = 1.5
MOD_SHIFT = 0.0
FNET_GROUPS = 4
POOL_WINDOWS = (2, 4, 8, 16)
MLA_HEADS = 16
QK_NOPE = 128
QK_ROPE = 64
V_DIM = 128
GRID_W = 64
ROPE_THETA = 10000.0
N_GROUPS = 4
EXPERTS_PER_GROUP = 8
TOPK_EXPERT = 2
ROW_TILE = 256
MOE_ROWS = 256


def _cparams(*sem):
    return pltpu.CompilerParams(dimension_semantics=sem, vmem_limit_bytes=VMEM_LIMIT)


def _ada_kernel(st_ref, w_ref, b_ref, o_ref, *, nrows):
    s = st_ref[...]
    s = s * jax.nn.sigmoid(s)
    w = w_ref[...]
    o_ref[...] = jnp.broadcast_to(b_ref[...], o_ref.shape)
    for r in range(nrows):
        o_ref[r:r + 1, :] = jnp.sum(s[:, r:r + 1] * w, axis=0, keepdims=True) + b_ref[...]


def _ada_mod(c, c_ctx, ada_w, ada_b):
    depth, d, n = ada_w.shape
    nrows = c.shape[0] + 1
    st = jnp.zeros((d, MOD_ROWS), F32).at[:, :nrows - 1].set(c.T).at[:, nrows - 1].set(c_ctx)
    tn = 1024 if n % 1024 == 0 else n
    return pl.pallas_call(
        functools.partial(_ada_kernel, nrows=nrows),
        grid=(depth, n // tn),
        in_specs=[pl.BlockSpec((d, MOD_ROWS), lambda l, j: (0, 0)),
                  pl.BlockSpec((None, d, tn), lambda l, j: (l, 0, j)),
                  pl.BlockSpec((None, 1, tn), lambda l, j: (l, 0, j))],
        out_specs=pl.BlockSpec((None, MOD_ROWS, tn), lambda l, j: (l, 0, j)),
        out_shape=jax.ShapeDtypeStruct((depth, MOD_ROWS, n), F32),
        compiler_params=_cparams("arbitrary", "arbitrary"),
        name="ada_mod",
    )(st, ada_w, ada_b.reshape(depth, 1, n))


class _Rows:
    def __init__(self, batch, ctx_len, seq, tile):
        assert (batch * ctx_len) % tile == 0 and seq % tile == 0
        self.batch, self.ctx_len, self.seq, self.tile = batch, ctx_len, seq, tile
        self.n_ctx = batch * ctx_len
        self.n_lat = batch * seq
        self.n_all = self.n_ctx + self.n_lat
        self.ctx_blocks = self.n_ctx // tile
        self.lat_blocks = seq // tile

    def mod_row(self, i):
        return jnp.where(i < self.ctx_blocks, self.batch, (i - self.ctx_blocks) // self.lat_blocks)


def _rms_mod(x, g, sh, sc):
    y = x * lax.rsqrt(jnp.mean(x * x, axis=-1, keepdims=True) + EPS) * g
    return y * (1.0 + sc) + sh


def _norm_mod_kernel(x_ref, g_ref, sh_ref, sc_ref, o_ref):
    o_ref[...] = _rms_mod(x_ref[...], g_ref[...], sh_ref[...], sc_ref[...]).astype(o_ref.dtype)


def _norm_mod(x_all, g, mod, chunk, rows, out_dtype):
    n, d = x_all.shape
    tm = rows.tile
    mod3 = mod.reshape(MOD_ROWS, 1, -1)
    return pl.pallas_call(
        _norm_mod_kernel,
        grid=(n // tm,),
        in_specs=[pl.BlockSpec((tm, d), lambda i: (i, 0)),
                  pl.BlockSpec((1, d), lambda i: (0, 0)),
                  pl.BlockSpec((None, 1, d), lambda i: (rows.mod_row(i), 0, chunk)),
                  pl.BlockSpec((None, 1, d), lambda i: (rows.mod_row(i), 0, chunk + 1))],
        out_specs=pl.BlockSpec((tm, d), lambda i: (i, 0)),
        out_shape=jax.ShapeDtypeStruct((n, d), out_dtype),
        compiler_params=_cparams("arbitrary"),
        name="norm_mod",
    )(x_all, g.reshape(1, d), mod3, mod3)


def _norm_router_kernel(x_ref, g_ref, sh_ref, sc_ref, wr_ref, br_ref, h_ref, eid_ref, wt_ref,
                        *, n_groups, per_group):
    h = _rms_mod(x_ref[...], g_ref[...], sh_ref[...], sc_ref[...])
    h_ref[...] = h.astype(h_ref.dtype)
    logits = jnp.dot(h, wr_ref[...], precision=HIGHEST, preferred_element_type=F32) + br_ref[...]
    lane = lax.broadcasted_iota(jnp.int32, logits.shape, 1).astype(F32)
    neg = -jnp.inf
    gl = jnp.where(lane < n_groups, logits, neg)
    gmax = jnp.max(gl, axis=-1, keepdims=True)
    p_grp = 1.0 / jnp.sum(jnp.exp(gl - gmax), axis=-1, keepdims=True)
    g_idx = jnp.min(jnp.where(gl == gmax, lane, float(LANES)), axis=-1, keepdims=True)
    lo = n_groups + g_idx * per_group
    el = jnp.where((lane >= lo) & (lane < lo + per_group), logits, neg)
    e1 = jnp.max(el, axis=-1, keepdims=True)
    i1 = jnp.min(jnp.where(el == e1, lane, float(LANES)), axis=-1, keepdims=True)
    el2 = jnp.where(lane == i1, neg, el)
    e2 = jnp.max(el2, axis=-1, keepdims=True)
    i2 = jnp.min(jnp.where(el2 == e2, lane, float(LANES)), axis=-1, keepdims=True)
    r = jnp.exp(e2 - e1)
    w1 = p_grp / (1.0 + r)
    w2 = p_grp * r / (1.0 + r)
    eid = jnp.where(lane == 0, i1 - n_groups, jnp.where(lane == 1, i2 - n_groups, 0.0))
    eid_ref[...] = eid.astype(jnp.int32)
    wt_ref[...] = jnp.where(lane == 0, w1, jnp.where(lane == 1, w2, 0.0))


def _norm_router(x_all, g, mod, chunk, rows, w_group, b_group, w_expert, b_expert):
    n, d = x_all.shape
    tm = rows.tile
    n_groups = w_group.shape[1]
    n_experts = w_expert.shape[1]
    wr = jnp.zeros((d, LANES), F32).at[:, :n_groups].set(w_group).at[:, n_groups:n_groups + n_experts].set(w_expert)
    br = jnp.zeros((1, LANES), F32).at[0, :n_groups].set(b_group).at[0, n_groups:n_groups + n_experts].set(b_expert)
    mod3 = mod.reshape(MOD_ROWS, 1, -1)
    h, eid, wts = pl.pallas_call(
        functools.partial(_norm_router_kernel, n_groups=n_groups, per_group=n_experts // n_groups),
        grid=(n // tm,),
        in_specs=[pl.BlockSpec((tm, d), lambda i: (i, 0)),
                  pl.BlockSpec((1, d), lambda i: (0, 0)),
                  pl.BlockSpec((None, 1, d), lambda i: (rows.mod_row(i), 0, chunk)),
                  pl.BlockSpec((None, 1, d), lambda i: (rows.mod_row(i), 0, chunk + 1)),
                  pl.BlockSpec((d, LANES), lambda i: (0, 0)),
                  pl.BlockSpec((1, LANES), lambda i: (0, 0))],
        out_specs=[pl.BlockSpec((tm, d), lambda i: (i, 0)),
                   pl.BlockSpec((tm, LANES), lambda i: (i, 0)),
                   pl.BlockSpec((tm, LANES), lambda i: (i, 0))],
        out_shape=[jax.ShapeDtypeStruct((n, d), F32),
                   jax.ShapeDtypeStruct((n, LANES), jnp.int32),
                   jax.ShapeDtypeStruct((n, LANES), F32)],
        compiler_params=_cparams("arbitrary"),
        name="norm_router",
    )(x_all, g.reshape(1, d), mod3, mod3, wr, br)
    return h, eid[:, :TOPK_EXPERT], wts[:, :TOPK_EXPERT]


def _moe_dispatch(eid, wts, n_experts, tm):
    n_assign = eid.size
    e_flat = eid.reshape(-1)
    order = jnp.argsort(e_flat).astype(jnp.int32)
    e_sorted = e_flat[order]
    counts = jnp.sum((e_flat[:, None] == jnp.arange(n_experts, dtype=jnp.int32)[None, :]).astype(jnp.int32), axis=0)
    padded = (counts + tm - 1) // tm * tm
    start = jnp.cumsum(counts) - counts
    pend = jnp.cumsum(padded)
    pstart = pend - padded
    dest = pstart[e_sorted] + (jnp.arange(n_assign, dtype=jnp.int32) - start[e_sorted])
    n_blocks = (n_assign + n_experts * (tm - 1) + tm - 1) // tm
    n_pad = n_blocks * tm
    buf_tok = jnp.zeros((n_pad,), jnp.int32).at[dest].set(order // TOPK_EXPERT)
    buf_asg = jnp.zeros((n_pad,), jnp.int32).at[dest].set(order)
    buf_w = jnp.zeros((n_pad,), F32).at[dest].set(wts.reshape(-1)[order])
    blk_start = jnp.arange(n_blocks, dtype=jnp.int32) * tm
    blk_e = jnp.minimum(jnp.searchsorted(pend, blk_start, side='right'), n_experts - 1).astype(jnp.int32)
    blk_rows = jnp.clip(pstart[blk_e] + counts[blk_e] - blk_start, 0, tm).astype(jnp.int32)
    return (buf_tok.reshape(n_blocks, 1, tm), buf_asg.reshape(n_blocks, 1, tm),
            buf_w.reshape(n_blocks, tm, 1), blk_e, blk_rows)


def _moe_kernel(blk_e_ref, blk_rows_ref, tok_ref, tok_next_ref, asg_ref, roww_ref, h_hbm,
                w1_ref, w3_ref, w2_ref, y_hbm, xbuf, ybuf, w1b, w3b, w2b, gsem, ssem):
    b = pl.program_id(0)
    nb = pl.num_programs(0)
    slot = b % 2
    other = 1 - slot

    def row_in(t, r, s):
        return pltpu.make_async_copy(h_hbm.at[pl.ds(t, 1)], xbuf.at[s, pl.ds(r, 1)], gsem.at[s])

    def row_out(r, a, s):
        return pltpu.make_async_copy(ybuf.at[s, pl.ds(r, 1)], y_hbm.at[pl.ds(a, 1)], ssem.at[s])

    def start_gather(ids_ref, n, s):
        lax.fori_loop(0, n, lambda r, c: (row_in(ids_ref[0, r], r, s).start(), c)[1], 0)

    def wait_gather(n, s):
        lax.fori_loop(0, n, lambda r, c: (row_in(0, 0, s).wait(), c)[1], 0)

    def start_scatter(n, s):
        lax.fori_loop(0, n, lambda r, c: (row_out(r, asg_ref[0, r], s).start(), c)[1], 0)

    def wait_scatter(n, s):
        lax.fori_loop(0, n, lambda r, c: (row_out(0, 0, s).wait(), c)[1], 0)

    rows_here = blk_rows_ref[b]

    @pl.when(b == 0)
    def _():
        xbuf[...] = jnp.zeros(xbuf.shape, xbuf.dtype)
        start_gather(tok_ref, rows_here, slot)

    @pl.when(b + 1 < nb)
    def _():
        start_gather(tok_next_ref, blk_rows_ref[jnp.minimum(b + 1, nb - 1)], other)

    @pl.when(b >= 2)
    def _():
        wait_scatter(blk_rows_ref[jnp.maximum(b - 2, 0)], slot)

    wait_gather(rows_here, slot)

    @pl.when((b == 0) | (blk_e_ref[b] != blk_e_ref[jnp.maximum(b - 1, 0)]))
    def _():
        w1b[...] = w1_ref[...].astype(BF16)
        w3b[...] = w3_ref[...].astype(BF16)
        w2b[...] = w2_ref[...].astype(BF16)

    @pl.when(rows_here > 0)
    def _():
        x = xbuf[slot].astype(BF16)
        a = jnp.dot(x, w1b[...], preferred_element_type=F32)
        g = jnp.dot(x, w3b[...], preferred_element_type=F32)
        hm = (a * jax.nn.sigmoid(a) * g).astype(BF16)
        ybuf[slot] = jnp.dot(hm, w2b[...], preferred_element_type=F32) * roww_ref[...]
        start_scatter(rows_here, slot)

    @pl.when(b == nb - 1)
    def _():
        @pl.when(nb >= 2)
        def _():
            wait_scatter(blk_rows_ref[jnp.maximum(b - 1, 0)], other)
        wait_scatter(rows_here, slot)


def _moe_experts(h, eid, wts, w1, w3, w2, layer):
    n, d = h.shape
    n_experts, de = w1.shape[1], w1.shape[3]
    tm = MOE_ROWS
    tok, asg, roww, blk_e, blk_rows = _moe_dispatch(eid, wts, n_experts, tm)
    n_blocks = tok.shape[0]
    smem_blk = lambda f: pl.BlockSpec((None, 1, tm), f, memory_space=pltpu.SMEM)
    grid_spec = pltpu.PrefetchScalarGridSpec(
        num_scalar_prefetch=2,
        grid=(n_blocks,),
        in_specs=[smem_blk(lambda b, be, br: (b, 0, 0)),
                  smem_blk(lambda b, be, br: (jnp.minimum(b + 1, n_blocks - 1), 0, 0)),
                  smem_blk(lambda b, be, br: (b, 0, 0)),
                  pl.BlockSpec((None, tm, 1), lambda b, be, br: (b, 0, 0)),
                  pl.BlockSpec(memory_space=pl.ANY),
                  pl.BlockSpec((None, None, d, de), lambda b, be, br: (layer, be[b], 0, 0)),
                  pl.BlockSpec((None, None, d, de), lambda b, be, br: (layer, be[b], 0, 0)),
                  pl.BlockSpec((None, None, de, d), lambda b, be, br: (layer, be[b], 0, 0))],
        out_specs=pl.BlockSpec(memory_space=pl.ANY),
        scratch_shapes=[pltpu.VMEM((2, tm, d), F32), pltpu.VMEM((2, tm, d), F32),
                        pltpu.VMEM((d, de), BF16), pltpu.VMEM((d, de), BF16), pltpu.VMEM((de, d), BF16),
                        pltpu.SemaphoreType.DMA((2,)), pltpu.SemaphoreType.DMA((2,))],
    )
    return pl.pallas_call(
        _moe_kernel,
        grid_spec=grid_spec,
        out_shape=jax.ShapeDtypeStruct((n * TOPK_EXPERT, d), F32),
        compiler_params=_cparams("arbitrary"),
        name="moe_experts",
    )(blk_e, blk_rows, tok, tok, asg, roww, h, w1, w3, w2)


def _moe_combine_kernel(x_ref, y_ref, gate_ref, o_ref):
    d = x_ref.shape[1]
    y = y_ref[...]
    acc = y[:, :d]
    for k in range(1, TOPK_EXPERT):
        acc = acc + y[:, k * d:(k + 1) * d]
    o_ref[...] = x_ref[...] + gate_ref[...] * acc


def _moe_combine(x_all, y, mod, chunk, rows):
    n, d = x_all.shape
    tm = rows.tile
    mod3 = mod.reshape(MOD_ROWS, 1, -1)
    return pl.pallas_call(
        _moe_combine_kernel,
        grid=(n // tm,),
        in_specs=[pl.BlockSpec((tm, d), lambda i: (i, 0)),
                  pl.BlockSpec((tm, TOPK_EXPERT * d), lambda i: (i, 0)),
                  pl.BlockSpec((None, 1, d), lambda i: (rows.mod_row(i), 0, chunk))],
        out_specs=pl.BlockSpec((tm, d), lambda i: (i, 0)),
        out_shape=jax.ShapeDtypeStruct((n, d), F32),
        input_output_aliases={0: 0},
        compiler_params=_cparams("arbitrary"),
        name="moe_combine",
    )(x_all, y.reshape(n, TOPK_EXPERT * d), mod3)


def _mm_kernel(*refs, has_bias, n_extra, epilogue):
    x_ref, w_ref = refs[0], refs[1]
    pos = 2
    b_ref = None
    if has_bias:
        b_ref = refs[pos]
        pos += 1
    extras = refs[pos:pos + n_extra]
    outs = refs[pos + n_extra:-1]
    wbf = refs[-1]

    @pl.when(pl.program_id(1) == 0)
    def _():
        wbf[...] = w_ref[...].astype(BF16)

    acc = jnp.dot(x_ref[...].astype(BF16), wbf[...], preferred_element_type=F32)
    if has_bias:
        acc = acc + b_ref[...]
    epilogue(acc, extras, outs)


def _store_epilogue(acc, extras, outs):
    outs[0][...] = acc.astype(outs[0].dtype)


def _matmul(x, w, *, w_lead=(), bias=None, tm, tn, m_rows=None, x_row_off=0, extras=(), outs,
            epilogue=_store_epilogue, aliases=None, name="matmul"):
    k = x.shape[1]
    n = w.shape[-1]
    m_rows = x.shape[0] if m_rows is None else m_rows
    assert m_rows % tm == 0 and n % tn == 0 and w.shape[-2] == k
    lead = tuple(w_lead)
    in_specs = [pl.BlockSpec((tm, k), lambda j, i: (i + x_row_off, 0)),
                pl.BlockSpec((None,) * len(lead) + (k, tn), lambda j, i: lead + (0, j))]
    args = [x, w]
    if bias is not None:
        in_specs.append(pl.BlockSpec((1, tn), lambda j, i: (0, j)))
        args.append(bias.reshape(1, n))
    for arr, spec in extras:
        in_specs.append(spec)
        args.append(arr)
    return pl.pallas_call(
        functools.partial(_mm_kernel, has_bias=bias is not None, n_extra=len(extras), epilogue=epilogue),
        grid=(n // tn, m_rows // tm),
        in_specs=in_specs,
        out_specs=[spec for _, spec in outs],
        out_shape=[shape for shape, _ in outs],
        scratch_shapes=[pltpu.VMEM((k, tn), BF16)],
        input_output_aliases=aliases or {},
        compiler_params=_cparams("arbitrary", "arbitrary"),
        name=name,
    )(*args)


def _residual_epilogue(acc, extras, outs):
    res_ref, gate_ref = extras
    outs[0][...] = res_ref[...] + gate_ref[...] * acc


def _proj_residual(xin, w, bias, x_all, mod, chunk, rows, row_off, *, w_lead=(), tn=512, name="proj_residual"):
    tm = rows.tile
    d = x_all.shape[1]
    tn = min(tn, d)
    mod3 = mod.reshape(MOD_ROWS, 1, -1)
    nd = d // tn
    res_spec = pl.BlockSpec((tm, tn), lambda j, i: (i + row_off, j))
    gate_spec = pl.BlockSpec((None, 1, tn), lambda j, i: (rows.mod_row(i + row_off), 0, chunk * nd + j))
    n_extra_before = 2 + (bias is not None)
    out, = _matmul(xin, w, w_lead=w_lead, bias=bias, tm=tm, tn=tn,
                   extras=[(x_all, res_spec), (mod3, gate_spec)],
                   outs=[(jax.ShapeDtypeStruct(x_all.shape, F32), res_spec)],
                   epilogue=_residual_epilogue, aliases={n_extra_before: 0}, name=name)
    return out


POOL_HALO = 8


def _pool_kernel(prev_ref, main_ref, next_ref, w_ref, sc_ref, res_ref, gate_ref, o_ref, ext_ref, wbf_ref,
                 *, tm, seq_len, windows):
    i = pl.program_id(0)
    seq_tiles = seq_len // tm
    t_in_seq = i % seq_tiles

    @pl.when(i == 0)
    def _():
        wbf_ref[...] = w_ref[...].astype(BF16)

    zero_halo = jnp.zeros(prev_ref.shape, F32)
    ext_ref[0:POOL_HALO, :] = jnp.where(t_in_seq == 0, zero_halo, prev_ref[...])
    ext_ref[POOL_HALO:POOL_HALO + tm, :] = main_ref[...]
    ext_ref[POOL_HALO + tm:2 * POOL_HALO + tm, :] = jnp.where(t_in_seq == seq_tiles - 1, zero_halo, next_ref[...])

    pos = t_in_seq * tm + lax.broadcasted_iota(jnp.int32, (tm, 1), 0)
    cg = main_ref.shape[1] // len(windows)
    for gi, win in enumerate(windows):
        half = win // 2
        cols = slice(gi * cg, (gi + 1) * cg)
        s = ext_ref[pl.ds(POOL_HALO - half, tm), cols]
        for dlt in range(-half + 1, half):
            s = s + ext_ref[pl.ds(POOL_HALO + dlt, tm), cols]
        cnt = jnp.minimum(pos + half, seq_len) - jnp.maximum(pos - half, 0)
        pooled = s / cnt.astype(F32) - main_ref[:, cols]
        y = jnp.dot(pooled.astype(BF16), wbf_ref[gi], preferred_element_type=F32) * sc_ref[:, cols]
        o_ref[:, cols] = res_ref[:, cols] + gate_ref[:, cols] * y


def _pool_mixer(h, w_grp, scale, x_all, mod, chunk, rows, row_off, n_rows, seq_len):
    tm = rows.tile
    d = x_all.shape[1]
    assert max(POOL_WINDOWS) // 2 <= POOL_HALO and tm % POOL_HALO == 0 and seq_len % tm == 0
    hb = tm // POOL_HALO
    last_hblk = h.shape[0] // POOL_HALO - 1
    mod3 = mod.reshape(MOD_ROWS, 1, -1)
    main_spec = pl.BlockSpec((tm, d), lambda i: (i + row_off, 0))
    return pl.pallas_call(
        functools.partial(_pool_kernel, tm=tm, seq_len=seq_len, windows=POOL_WINDOWS),
        grid=(n_rows // tm,),
        in_specs=[pl.BlockSpec((POOL_HALO, d), lambda i: (jnp.maximum((i + row_off) * hb - 1, 0), 0)),
                  main_spec,
                  pl.BlockSpec((POOL_HALO, d), lambda i: (jnp.minimum((i + row_off + 1) * hb, last_hblk), 0)),
                  pl.BlockSpec(w_grp.shape, lambda i: (0, 0, 0)),
                  pl.BlockSpec((1, d), lambda i: (0, 0)),
                  main_spec,
                  pl.BlockSpec((None, 1, d), lambda i: (rows.mod_row(i + row_off), 0, chunk))],
        out_specs=main_spec,
        out_shape=jax.ShapeDtypeStruct(x_all.shape, F32),
        scratch_shapes=[pltpu.VMEM((tm + 2 * POOL_HALO, d), F32), pltpu.VMEM(w_grp.shape, BF16)],
        input_output_aliases={5: 0},
        compiler_params=_cparams("arbitrary"),
        name="pool_mixer",
    )(h, h, h, w_grp, scale.reshape(1, d), x_all, mod3)


DFT_N2 = 128


def _cis(num, den):
    ang = (num % den).astype(F32) * (2.0 * math.pi / den)
    return jnp.cos(ang), jnp.sin(ang)


def _iota2(n_rows, n_cols):
    return (lax.broadcasted_iota(jnp.int32, (n_rows, n_cols), 0), lax.broadcasted_iota(jnp.int32, (n_rows, n_cols), 1))


def _left_mm_kernel(w_ref, x_ref, *rest, epilogue):
    x = x_ref[...]
    x = x.reshape(-1, x.shape[-1])
    acc = jnp.dot(w_ref[...], x.astype(BF16), preferred_element_type=F32)
    epilogue(acc, rest[:-1], rest[-1])


def _scaled_store(scale):
    def epilogue(acc, extras, o_ref):
        o_ref[...] = (acc * scale).reshape(o_ref.shape).astype(o_ref.dtype)
    return epilogue


def _left_mm(w, x, *, grid, w_spec, x_spec, out_shape, out_spec, extras=(), epilogue=_scaled_store(1.0),
             aliases=None, name="left_mm"):
    return pl.pallas_call(
        functools.partial(_left_mm_kernel, epilogue=epilogue),
        grid=grid,
        in_specs=[w_spec, x_spec] + [s for _, s in extras],
        out_specs=out_spec,
        out_shape=out_shape,
        input_output_aliases=aliases or {},
        compiler_params=_cparams(*(("arbitrary",) * len(grid))),
        name=name,
    )(w, x, *[a for a, _ in extras])


def _fnet_channel_kernel(x_ref, w_ref, o_ref):
    cg = x_ref.shape[1]
    r = jnp.dot(x_ref[...], w_ref[...], preferred_element_type=F32)
    o_ref[0] = r[:, :cg].astype(o_ref.dtype)
    o_ref[1] = r[:, cg:].astype(o_ref.dtype)


def _fnet_channel(h, rows, row_off, n_seq, seq_len):
    tm = rows.tile
    d = h.shape[1]
    cg = d // FNET_GROUPS
    ci, ki = _iota2(cg, cg)
    cc, sc = _cis(ci * ki, cg)
    wc = jnp.concatenate([cc, -sc], axis=1).astype(BF16)
    st = seq_len // tm
    return pl.pallas_call(
        _fnet_channel_kernel,
        grid=(n_seq * st, FNET_GROUPS),
        in_specs=[pl.BlockSpec((tm, cg), lambda i, g: (i + row_off, g)),
                  pl.BlockSpec((cg, 2 * cg), lambda i, g: (0, 0))],
        out_specs=pl.BlockSpec((None, 2, tm, cg), lambda i, g: (i // st, 0, i % st, g)),
        out_shape=jax.ShapeDtypeStruct((n_seq, 2, seq_len, d), BF16),
        compiler_params=_cparams("arbitrary", "arbitrary"),
        name="fnet_channel",
    )(h, wc)


def _fnet_positions(z, n1, tc):
    n_seq, _, seq_len, d = z.shape
    n2 = seq_len // n1
    scale = 1.0 / math.sqrt(seq_len * (d // FNET_GROUPS))
    if n1 > 1:
        r, cidx = _iota2(2 * n1, 2 * n1)
        k1, ro, ri, nn = r // 2, r % 2, cidx // n1, cidx % n1
        fr, fs = _cis(k1 * nn, n1)
        w1 = jnp.where(ro == ri, fr, jnp.where(ro == 0, fs, -fs)).astype(BF16)
        cols = n2 * d
        t1 = min(cols, 4096)
        a = _left_mm(w1, z.reshape(n_seq, 2 * n1, cols),
                     grid=(n_seq, cols // t1),
                     w_spec=pl.BlockSpec((2 * n1, 2 * n1), lambda s, j: (0, 0)),
                     x_spec=pl.BlockSpec((None, 2 * n1, t1), lambda s, j: (s, 0, j)),
                     out_shape=jax.ShapeDtypeStruct((n_seq, 2 * n1, cols), BF16),
                     out_spec=pl.BlockSpec((None, 2 * n1, t1), lambda s, j: (s, 0, j)),
                     name="fnet_stage1")
        a = a.reshape(n_seq, n1, 2, n2, d)
    else:
        a = z.reshape(n_seq, 1, 2, n2, d)
    k2i, ci2 = _iota2(n2, 2 * n2)
    kk = jnp.arange(n1, dtype=jnp.int32)[:, None, None] + n1 * k2i[None]
    gc, gs = _cis(kk * (ci2 % n2)[None], seq_len)
    g2 = jnp.where((ci2 < n2)[None], gc, gs).astype(BF16)
    nd = d // tc
    out = _left_mm(g2, a,
                   grid=(n_seq, n1, nd),
                   w_spec=pl.BlockSpec((None, n2, 2 * n2), lambda s, k, j: (k, 0, 0)),
                   x_spec=pl.BlockSpec((None, None, 2, n2, tc), lambda s, k, j: (s, k, 0, 0, j)),
                   out_shape=jax.ShapeDtypeStruct((n_seq, n2, n1 * d), BF16),
                   out_spec=pl.BlockSpec((None, n2, tc), lambda s, k, j: (s, 0, k * nd + j)),
                   epilogue=_scaled_store(scale), name="fnet_stage2")
    return out.reshape(n_seq * seq_len, d)


def _layer_fnet(x_all, mod, rows, g, w_out, b_out):
    h = _norm_mod(x_all, g, mod, 0, rows, BF16)
    d = x_all.shape[1]
    zc = _fnet_channel(h, rows, 0, rows.batch, rows.ctx_len)
    fc = _fnet_positions(zc, 1, min(d, 512))
    x_all = _proj_residual(fc, w_out, b_out, x_all, mod, 2, rows, 0, name="fnet_out_ctx")
    zl = _fnet_channel(h, rows, rows.ctx_blocks, rows.batch, rows.seq)
    fl = _fnet_positions(zl, rows.seq // DFT_N2, min(d, 512))
    return _proj_residual(fl, w_out, b_out, x_all, mod, 2, rows, rows.ctx_blocks, name="fnet_out_lat")


HEAD_SLAB = 2 * LANES
MLA_SCALE = (QK_NOPE + QK_ROPE) ** -0.5
ATTN_TQ = 256
ATTN_TK = 1024


def _rope_tables(seq_len, lead_identity_rows):
    n_rows = seq_len // GRID_W
    row = jnp.repeat(jnp.arange(n_rows, dtype=F32), GRID_W)
    col = jnp.tile(jnp.arange(GRID_W, dtype=F32), n_rows)
    half = QK_ROPE // 2
    inv = ROPE_THETA ** (-jnp.arange(0, half, 2, dtype=F32) / half)
    ang_r = row[:, None] * inv
    ang_c = col[:, None] * inv
    ang = jnp.concatenate([ang_r, ang_r, ang_c, ang_c], axis=-1)
    pad = jnp.zeros((seq_len, LANES - QK_ROPE), F32)
    cos = jnp.concatenate([jnp.cos(ang), pad + 1.0], axis=-1)
    sin = jnp.concatenate([jnp.sin(ang), pad], axis=-1)
    if lead_identity_rows:
        cos = jnp.concatenate([jnp.ones((lead_identity_rows, LANES), F32), cos], axis=0)
        sin = jnp.concatenate([jnp.zeros((lead_identity_rows, LANES), F32), sin], axis=0)
    return cos, sin


def _rope(x, cos, sin):
    q = QK_ROPE // 4
    lane = lax.broadcasted_iota(jnp.int32, x.shape, 1)
    even = (lane // q) % 2 == 0
    rot = jnp.where(even, -pltpu.roll(x, LANES - q, 1), pltpu.roll(x, q, 1))
    return x * cos + rot * sin


def _rmsnorm_epilogue(acc, extras, outs):
    g_ref, = extras
    y = acc * lax.rsqrt(jnp.mean(acc * acc, axis=-1, keepdims=True) + EPS) * g_ref[...]
    outs[0][...] = y.astype(outs[0].dtype)


def _q_up_epilogue(acc, extras, outs):
    cos_ref, sin_ref = extras
    o_ref, = outs
    for hh in range(acc.shape[1] // HEAD_SLAB):
        c0 = hh * HEAD_SLAB
        o_ref[:, c0:c0 + LANES] = (acc[:, c0:c0 + LANES] * MLA_SCALE).astype(o_ref.dtype)
        pe = _rope(acc[:, c0 + LANES:c0 + HEAD_SLAB], cos_ref[...], sin_ref[...])
        o_ref[:, c0 + LANES:c0 + HEAD_SLAB] = (pe * MLA_SCALE).astype(o_ref.dtype)


def _kv_down_epilogue(acc, extras, outs, *, rank):
    g_ref, cos_ref, sin_ref = extras
    kvn_ref, kpe_ref = outs
    lat = acc[:, :rank]
    y = lat * lax.rsqrt(jnp.mean(lat * lat, axis=-1, keepdims=True) + EPS) * g_ref[...]
    kvn_ref[...] = y.astype(kvn_ref.dtype)
    kpe_ref[...] = _rope(acc[:, rank:rank + LANES], cos_ref[...], sin_ref[...]).astype(kpe_ref.dtype)


def _k_up_epilogue(acc, extras, outs):
    kpe_ref, = extras
    o_ref, = outs
    for hh in range(acc.shape[1] // LANES):
        o_ref[:, hh * HEAD_SLAB:hh * HEAD_SLAB + LANES] = acc[:, hh * LANES:(hh + 1) * LANES].astype(o_ref.dtype)
        o_ref[:, hh * HEAD_SLAB + LANES:(hh + 1) * HEAD_SLAB] = kpe_ref[...]


def _attn_kernel(q_ref, kc_ref, kl_ref, vc_ref, vl_ref, o_ref, *, tk):
    q = q_ref[...]

    def scores(k_tile):
        return lax.dot_general(k_tile, q, (((1,), (1,)), ((), ())), preferred_element_type=F32)

    s = scores(kc_ref[...])
    m = jnp.max(s, axis=0, keepdims=True)
    p = jnp.exp(s - m)
    l = jnp.sum(p, axis=0, keepdims=True)
    acc = jnp.dot(vc_ref[...], p.astype(BF16), preferred_element_type=F32)
    for j in range(kl_ref.shape[0] // tk):
        s = scores(kl_ref[j * tk:(j + 1) * tk, :])
        m_new = jnp.maximum(m, jnp.max(s, axis=0, keepdims=True))
        alpha = jnp.exp(m - m_new)
        p = jnp.exp(s - m_new)
        l = alpha * l + jnp.sum(p, axis=0, keepdims=True)
        acc = alpha * acc + jnp.dot(vl_ref[:, j * tk:(j + 1) * tk], p.astype(BF16), preferred_element_type=F32)
        m = m_new
    o_ref[...] = (acc / l).astype(o_ref.dtype)


def _attention(q, k_ctx, k_lat, vt_ctx, vt_lat):
    batch, seq, _ = q.shape
    ctx_len = k_ctx.shape[1]
    tq = min(ATTN_TQ, seq)
    tk = min(ATTN_TK, seq)
    assert seq % tq == 0 and seq % tk == 0
    return pl.pallas_call(
        functools.partial(_attn_kernel, tk=tk),
        grid=(batch, MLA_HEADS, seq // tq),
        in_specs=[pl.BlockSpec((None, tq, HEAD_SLAB), lambda b, h, i: (b, i, h)),
                  pl.BlockSpec((None, ctx_len, HEAD_SLAB), lambda b, h, i: (b, 0, h)),
                  pl.BlockSpec((None, seq, HEAD_SLAB), lambda b, h, i: (b, 0, h)),
                  pl.BlockSpec((None, V_DIM, ctx_len), lambda b, h, i: (b, h, 0)),
                  pl.BlockSpec((None, V_DIM, seq), lambda b, h, i: (b, h, 0))],
        out_specs=pl.BlockSpec((None, V_DIM, tq), lambda b, h, i: (b, h, i)),
        out_shape=jax.ShapeDtypeStruct((batch, MLA_HEADS * V_DIM, seq), BF16),
        compiler_params=_cparams("arbitrary", "arbitrary", "arbitrary"),
        name="mla_attention",
    )(q, k_ctx, k_lat, vt_ctx, vt_lat)


def _layer_mla(x_all, mod, rows, g, w_dq, g_q, w_uq, w_dkv, g_kv, w_ukv, w_o, update_ctx=False):
    assert not update_ctx, "attention is the last mixer of the stack: context queries are never needed"
    tm = rows.tile
    d = x_all.shape[1]
    batch, seq, ctx_len = rows.batch, rows.seq, rows.ctx_len
    q_rank, kv_rank = w_dq.shape[1], g_kv.shape[0]
    assert V_DIM == LANES and QK_NOPE == LANES and QK_ROPE <= LANES
    h = _norm_mod(x_all, g, mod, 0, rows, BF16)

    w_uq_s = jnp.pad(w_uq.reshape(q_rank, MLA_HEADS, QK_NOPE + QK_ROPE),
                     ((0, 0), (0, 0), (0, HEAD_SLAB - QK_NOPE - QK_ROPE))).reshape(q_rank, MLA_HEADS * HEAD_SLAB)
    w_dkv_s = jnp.pad(w_dkv, ((0, 0), (0, kv_rank + LANES - w_dkv.shape[1])))
    w_ukv_s = w_ukv.reshape(kv_rank, MLA_HEADS, QK_NOPE + V_DIM)
    w_uk = w_ukv_s[:, :, :QK_NOPE].reshape(kv_rank, MLA_HEADS * QK_NOPE)
    w_uv = w_ukv_s[:, :, QK_NOPE:].reshape(kv_rank, MLA_HEADS * V_DIM)
    cos, sin = _rope_tables(seq, tm)
    lat_blocks = rows.lat_blocks

    cqn, = _matmul(h, w_dq, tm=tm, tn=q_rank, m_rows=rows.n_lat, x_row_off=rows.ctx_blocks,
                   extras=[(g_q.reshape(1, q_rank), pl.BlockSpec((1, q_rank), lambda j, i: (0, 0)))],
                   outs=[(jax.ShapeDtypeStruct((rows.n_lat, q_rank), BF16), pl.BlockSpec((tm, q_rank), lambda j, i: (i, 0)))],
                   epilogue=_rmsnorm_epilogue, name="mla_q_down")
    tnq = 4 * HEAD_SLAB
    rope_lat = pl.BlockSpec((tm, LANES), lambda j, i: (1 + i % lat_blocks, 0))
    q, = _matmul(cqn, w_uq_s, tm=tm, tn=tnq, extras=[(cos, rope_lat), (sin, rope_lat)],
                 outs=[(jax.ShapeDtypeStruct((rows.n_lat, MLA_HEADS * HEAD_SLAB), BF16),
                        pl.BlockSpec((tm, tnq), lambda j, i: (i, j)))],
                 epilogue=_q_up_epilogue, name="mla_q_up")

    def keys_values(row_off, n_rows, rope_spec):
        kvn, kpe = _matmul(h, w_dkv_s, tm=tm, tn=kv_rank + LANES, m_rows=n_rows, x_row_off=row_off,
                           extras=[(g_kv.reshape(1, kv_rank), pl.BlockSpec((1, kv_rank), lambda j, i: (0, 0))),
                                   (cos, rope_spec), (sin, rope_spec)],
                           outs=[(jax.ShapeDtypeStruct((n_rows, kv_rank), BF16), pl.BlockSpec((tm, kv_rank), lambda j, i: (i, 0))),
                                 (jax.ShapeDtypeStruct((n_rows, LANES), BF16), pl.BlockSpec((tm, LANES), lambda j, i: (i, 0)))],
                           epilogue=functools.partial(_kv_down_epilogue, rank=kv_rank), name="mla_kv_down")
        tnk = 4 * LANES
        k, = _matmul(kvn, w_uk, tm=tm, tn=tnk,
                     extras=[(kpe, pl.BlockSpec((tm, LANES), lambda j, i: (i, 0)))],
                     outs=[(jax.ShapeDtypeStruct((n_rows, MLA_HEADS * HEAD_SLAB), BF16),
                            pl.BlockSpec((tm, 2 * tnk), lambda j, i: (i, j)))],
                     epilogue=_k_up_epilogue, name="mla_k_up")
        v, = _matmul(kvn, w_uv, tm=tm, tn=tnk,
                     outs=[(jax.ShapeDtypeStruct((n_rows, MLA_HEADS * V_DIM), BF16), pl.BlockSpec((tm, tnk), lambda j, i: (i, j)))],
                     name="mla_v_up")
        return k, v

    k_c, v_c = keys_values(0, rows.n_ctx, pl.BlockSpec((tm, LANES), lambda j, i: (0, 0)))
    k_l, v_l = keys_values(rows.ctx_blocks, rows.n_lat, rope_lat)
    vt_c = v_c.reshape(batch, ctx_len, -1).swapaxes(1, 2)
    vt_l = v_l.reshape(batch, seq, -1).swapaxes(1, 2)
    ot = _attention(q.reshape(batch, seq, -1), k_c.reshape(batch, ctx_len, -1), k_l.reshape(batch, seq, -1), vt_c, vt_l)
    attn = ot.swapaxes(1, 2).reshape(rows.n_lat, MLA_HEADS * V_DIM)
    return _proj_residual(attn, w_o, None, x_all, mod, 2, rows, rows.ctx_blocks, name="mla_out")


CONV_HALO = 16
MAX_DECAY = math.log(DECAY_TARGET) / FAST_DECAY_PCT
MIN_DECAY = math.log(DECAY_TARGET) / SLOW_DECAY_PCT


def _hyena_in_kernel(prev_ref, main_ref, next_ref, w_ref, b_ref, cw_ref, cb_ref, o_ref, xext_ref, zext_ref, wbf_ref,
                     *, tm, seq_len):
    i = pl.program_id(1)
    seq_tiles = seq_len // tm
    t_in_seq = i % seq_tiles

    @pl.when(i == 0)
    def _():
        wbf_ref[...] = w_ref[...].astype(BF16)

    xext_ref[0:CONV_HALO, :] = prev_ref[...]
    xext_ref[CONV_HALO:CONV_HALO + tm, :] = main_ref[...]
    xext_ref[CONV_HALO + tm:2 * CONV_HALO + tm, :] = next_ref[...]
    zext_ref[...] = jnp.dot(xext_ref[...], wbf_ref[...], preferred_element_type=F32) + b_ref[...]
    row = lax.broadcasted_iota(jnp.int32, (tm, 1), 0)
    prev = jnp.where((row == 0) & (t_in_seq == 0), 0.0, zext_ref[pl.ds(CONV_HALO - 1, tm), :])
    nxt = jnp.where((row == tm - 1) & (t_in_seq == seq_tiles - 1), 0.0, zext_ref[pl.ds(CONV_HALO + 1, tm), :])
    out = prev * cw_ref[0:1, :] + zext_ref[pl.ds(CONV_HALO, tm), :] * cw_ref[1:2, :] + nxt * cw_ref[2:3, :] + cb_ref[...]
    o_ref[...] = out.astype(o_ref.dtype)


def _hyena_in(h, w_in, b_in, conv_w, conv_b, rows, row_off, n_rows, seq_len, tn=512):
    tm = rows.tile
    k, n3 = w_in.shape
    d = n3 // 3
    tn = min(tn, d)
    nd = d // tn
    hb = tm // CONV_HALO
    last_hblk = h.shape[0] // CONV_HALO - 1
    return pl.pallas_call(
        functools.partial(_hyena_in_kernel, tm=tm, seq_len=seq_len),
        grid=(n3 // tn, n_rows // tm),
        in_specs=[pl.BlockSpec((CONV_HALO, k), lambda j, i: (jnp.maximum((i + row_off) * hb - 1, 0), 0)),
                  pl.BlockSpec((tm, k), lambda j, i: (i + row_off, 0)),
                  pl.BlockSpec((CONV_HALO, k), lambda j, i: (jnp.minimum((i + row_off + 1) * hb, last_hblk), 0)),
                  pl.BlockSpec((k, tn), lambda j, i: (0, j)),
                  pl.BlockSpec((1, tn), lambda j, i: (0, j)),
                  pl.BlockSpec((3, tn), lambda j, i: (0, j)),
                  pl.BlockSpec((1, tn), lambda j, i: (0, j))],
        out_specs=pl.BlockSpec((None, tm, tn), lambda j, i: (j // nd, i, j % nd)),
        out_shape=jax.ShapeDtypeStruct((3, n_rows, d), BF16),
        scratch_shapes=[pltpu.VMEM((tm + 2 * CONV_HALO, k), BF16), pltpu.VMEM((tm + 2 * CONV_HALO, tn), F32),
                        pltpu.VMEM((k, tn), BF16)],
        compiler_params=_cparams("arbitrary", "arbitrary"),
        name="hyena_in",
    )(h, h, h, w_in, b_in.reshape(1, n3), conv_w, conv_b.reshape(1, n3))


def _hyena_filter_kernel(z_ref, w1_ref, b1_ref, w2_ref, b2_ref, f0_ref, f1_ref, w3_ref, dl_ref, k_ref, ss_ref,
                         *, tp, seq_len):
    p = pl.program_id(1)
    z = z_ref[...]
    h1 = jnp.sin(f0_ref[...] * (jnp.dot(z, w1_ref[...], precision=HIGHEST, preferred_element_type=F32) + b1_ref[...]))
    h2 = jnp.sin(f1_ref[...] * (jnp.dot(h1, w2_ref[...], precision=HIGHEST, preferred_element_type=F32) + b2_ref[...]))
    filt = jnp.dot(h2, w3_ref[...], precision=HIGHEST, preferred_element_type=F32)
    t = z[:, 0:1]
    kk = filt * (jnp.exp(-t * dl_ref[...]) + MOD_SHIFT)
    circ = p * tp + lax.broadcasted_iota(jnp.int32, (tp, 1), 0)
    kk = jnp.where(circ == seq_len, 0.0, kk)
    k_ref[...] = kk.astype(k_ref.dtype)

    @pl.when(p == 0)
    def _():
        ss_ref[...] = jnp.zeros(ss_ref.shape, F32)

    ss_ref[...] += jnp.sum(kk * kk, axis=0, keepdims=True)


def _hyena_filters(seq_len, d, f_w1, f_b1, f_w2, f_b2, f_w3, f_freq, tp=256, tc=1024):
    f32 = F32
    hid = f_w1.shape[1]
    od = HYENA_ORDER * d
    t = jnp.linspace(0.0, 1.0, seq_len, dtype=f32)[:, None]
    bands = (FILTER_EMB - 1) // 2
    w = 2.0 * math.pi * jnp.arange(seq_len, dtype=f32)[:, None] / seq_len
    f = jnp.linspace(1e-4, bands - 1, bands, dtype=f32)[None, :]
    z = jnp.concatenate([t, jnp.cos(f * w), -jnp.sin(f * w)], axis=-1)
    circ = jnp.arange(2 * seq_len)
    offs = jnp.where(circ < seq_len, circ, jnp.minimum(2 * seq_len - circ, seq_len - 1))
    z2 = jnp.pad(z[offs], ((0, 0), (0, LANES - FILTER_EMB)))
    w1p = jnp.pad(f_w1, ((0, LANES - FILTER_EMB), (0, 0)))
    w3s = f_w3.reshape(hid, HYENA_ORDER, 2, d).transpose(2, 0, 1, 3).reshape(2, hid, od)
    deltas = jnp.tile(jnp.abs(jnp.linspace(MIN_DECAY, MAX_DECAY, d, dtype=f32)), HYENA_ORDER)[None, :]
    tp = min(tp, seq_len)
    tc = min(tc, od)
    side_tiles = seq_len // tp
    small = lambda shape: pl.BlockSpec(shape, lambda j, p: (0,) * len(shape))
    return pl.pallas_call(
        functools.partial(_hyena_filter_kernel, tp=tp, seq_len=seq_len),
        grid=(od // tc, 2 * side_tiles),
        in_specs=[pl.BlockSpec((tp, LANES), lambda j, p: (p, 0)),
                  small((LANES, hid)), small((1, hid)), small((hid, hid)), small((1, hid)), small((1, hid)), small((1, hid)),
                  pl.BlockSpec((None, hid, tc), lambda j, p: (p // side_tiles, 0, j)),
                  pl.BlockSpec((1, tc), lambda j, p: (0, j))],
        out_specs=[pl.BlockSpec((tp, tc), lambda j, p: (p, j)),
                   pl.BlockSpec((1, tc), lambda j, p: (0, j))],
        out_shape=[jax.ShapeDtypeStruct((2 * seq_len, od), BF16), jax.ShapeDtypeStruct((1, od), F32)],
        compiler_params=_cparams("arbitrary", "arbitrary"),
        name="hyena_filters",
    )(z2, w1p, f_b1.reshape(1, hid), f_w2, f_b2.reshape(1, hid), f_freq[0:1], f_freq[1:2], w3s, deltas)


def _stage2_tables(n, n1):
    n2 = n // n1
    r, c = _iota2(2 * n2, 2 * n2)
    kk = jnp.arange(n1, dtype=jnp.int32)[:, None, None] + n1 * (r % n2)[None]
    gc, gs = _cis(kk * (c % n2)[None], n)
    same = ((r < n2) == (c < n2))[None]
    fwd = jnp.where(same, gc, jnp.where((r < n2)[None], gs, -gs))
    return fwd.astype(BF16), fwd.swapaxes(1, 2).astype(BF16)


def _complex_mul(x, kf, n2):
    xr, xi = x[:n2], x[n2:]
    kr, ki = kf[:n2].astype(F32), kf[n2:].astype(F32)
    return jnp.concatenate([xr * kr - xi * ki, xr * ki + xi * kr], axis=0)


def _kf_epilogue(acc, extras, o_ref):
    ss_ref, = extras
    o_ref[...] = (acc * ss_ref[...]).astype(o_ref.dtype)


def _hyena_kf(k2u, sumsq, n1):
    n, od = k2u.shape
    n2 = n // n1
    colscale = lax.rsqrt(sumsq + EPS) / n
    r, c = _iota2(2 * n1, n1)
    fr, fs = _cis((r // 2) * c, n1)
    w1 = jnp.where(r % 2 == 0, fr, -fs).astype(BF16)
    cols = n2 * od
    t1 = min(cols, 4096)
    a = _left_mm(w1, k2u.reshape(n1, cols),
                 grid=(cols // t1,),
                 w_spec=pl.BlockSpec((2 * n1, n1), lambda j: (0, 0)),
                 x_spec=pl.BlockSpec((n1, t1), lambda j: (0, j)),
                 out_shape=jax.ShapeDtypeStruct((2 * n1, cols), BF16),
                 out_spec=pl.BlockSpec((2 * n1, t1), lambda j: (0, j)),
                 name="hyena_kf_stage1")
    g_fwd, _ = _stage2_tables(n, n1)
    tc = min(od, 1024)
    return _left_mm(g_fwd, a.reshape(n1, 2, n2, od),
                    grid=(n1, od // tc),
                    w_spec=pl.BlockSpec((None, 2 * n2, 2 * n2), lambda k, j: (k, 0, 0)),
                    x_spec=pl.BlockSpec((None, 2, n2, tc), lambda k, j: (k, 0, 0, j)),
                    out_shape=jax.ShapeDtypeStruct((n1, 2 * n2, od), BF16),
                    out_spec=pl.BlockSpec((None, 2 * n2, tc), lambda k, j: (k, 0, j)),
                    extras=[(colscale, pl.BlockSpec((1, tc), lambda k, j: (0, j)))],
                    epilogue=_kf_epilogue, name="hyena_kf_stage2")


def _conv_mid_kernel(gf_ref, gi_ref, a_ref, kf_ref, o_ref):
    n2 = a_ref.shape[1]
    a = a_ref[...].reshape(2 * n2, a_ref.shape[2])
    x = jnp.dot(gf_ref[...], a, preferred_element_type=F32)
    y = _complex_mul(x, kf_ref[...], n2)
    o_ref[...] = jnp.dot(gi_ref[...], y.astype(BF16), preferred_element_type=F32).astype(o_ref.dtype)


def _gate_epilogue(acc, extras, o_ref):
    xg_ref, u_ref, skip_ref = extras
    u = u_ref[...].astype(F32)
    o_ref[...] = (xg_ref[...].astype(F32) * (acc + skip_ref[...] * u)).astype(o_ref.dtype)


def _hyena_long_conv(u, xg, kf, kf_col_off, skip, seq_len, tables):
    n_rows, d = u.shape
    assert n_rows == 2 * seq_len
    n = 2 * seq_len
    n2 = DFT_N2
    n1 = n // n2
    hn = n1 // 2
    g_fwd, g_inv = tables
    cols = n2 * d
    m = max(1, min(4096 // d, n2))
    t1 = m * d
    r, c = _iota2(2 * n1, 2 * hn)
    fr, fs = _cis((r // 2) * (c % hn), n1)
    ro, ri = r % 2, c // hn
    w1 = jnp.where(ro == ri, fr, jnp.where(ro == 0, fs, -fs)).astype(BF16)
    a = _left_mm(w1, u.reshape(2 * hn, cols),
                 grid=(cols // t1,),
                 w_spec=pl.BlockSpec((2 * n1, 2 * hn), lambda j: (0, 0)),
                 x_spec=pl.BlockSpec((2 * hn, t1), lambda j: (0, j)),
                 out_shape=jax.ShapeDtypeStruct((2 * n1, cols), BF16),
                 out_spec=pl.BlockSpec((2 * n1, t1), lambda j: (0, j)),
                 name="hyena_conv_stage1")
    tc = min(d, 1024)
    nd = d // tc
    bmid = pl.pallas_call(
        _conv_mid_kernel,
        grid=(n1, nd),
        in_specs=[pl.BlockSpec((None, 2 * n2, 2 * n2), lambda k, j: (k, 0, 0)),
                  pl.BlockSpec((None, 2 * n2, 2 * n2), lambda k, j: (k, 0, 0)),
                  pl.BlockSpec((None, 2, n2, tc), lambda k, j: (k, 0, 0, j)),
                  pl.BlockSpec((None, 2 * n2, tc), lambda k, j: (k, 0, kf_col_off * nd + j))],
        out_specs=pl.BlockSpec((None, 2 * n2, tc), lambda k, j: (k, 0, j)),
        out_shape=jax.ShapeDtypeStruct((n1, 2 * n2, d), BF16),
        compiler_params=_cparams("arbitrary", "arbitrary"),
        name="hyena_conv_mid",
    )(g_fwd, g_inv, a.reshape(n1, 2, n2, d), kf)
    r, c = _iota2(2 * hn, 2 * n1)
    ec, es = _cis((r % hn) * (c // 2), n1)
    ro, ri = r // hn, c % 2
    w3 = jnp.where(ro == ri, ec, jnp.where(ro == 0, -es, es)).astype(BF16)
    blk = pl.BlockSpec((2 * hn, t1), lambda j: (0, j))
    return _left_mm(w3, bmid.reshape(2 * n1, cols),
                    grid=(cols // t1,),
                    w_spec=pl.BlockSpec((2 * hn, 2 * n1), lambda j: (0, 0)),
                    x_spec=pl.BlockSpec((2 * n1, t1), lambda j: (0, j)),
                    out_shape=jax.ShapeDtypeStruct((2 * hn, cols), BF16),
                    out_spec=blk,
                    extras=[(xg.reshape(2 * hn, cols), blk), (u.reshape(2 * hn, cols), blk),
                            (jnp.tile(skip.reshape(1, d), (1, m)), pl.BlockSpec((1, t1), lambda j: (0, 0)))],
                    epilogue=_gate_epilogue, name="hyena_conv_stage3").reshape(n_rows, d)


def _conv_dense_kernel(fk_ref, fz_ref, fzt_ref, k_ref, ss_ref, u_ref, xg_ref, skip_ref, o_ref):
    n = k_ref.shape[0]
    kf = jnp.dot(fk_ref[...], k_ref[...], preferred_element_type=F32) * ss_ref[...]
    u = u_ref[...]
    z = jnp.dot(fz_ref[...], u, preferred_element_type=F32)
    y = _complex_mul(z, kf, n)
    conv = jnp.dot(fzt_ref[...], y.astype(BF16), preferred_element_type=F32)
    o_ref[...] = (xg_ref[...].astype(F32) * (conv + skip_ref[...] * u.astype(F32))).astype(o_ref.dtype)


def _hyena_long_conv_dense(u, xg, k2u, sumsq, order, skip, seq_len):
    n_rows, d = u.shape
    assert n_rows == 2 * seq_len
    n = 2 * seq_len
    colscale = lax.rsqrt(sumsq + EPS) / n
    r, c = _iota2(2 * n, n)
    kc, ks = _cis((r % n) * c, n)
    fk = jnp.where(r < n, kc, -ks).astype(BF16)
    r, c = _iota2(2 * n, 2 * seq_len)
    zc, zs = _cis((r % n) * (c % seq_len), n)
    fz = jnp.where((r < n) == (c < seq_len), zc, jnp.where(r < n, zs, -zs))
    tc = min(d, 512)
    nd = d // tc
    full = lambda a: pl.BlockSpec(a.shape, lambda j: (0, 0))
    fz_b, fzt_b = fz.astype(BF16), fz.T.astype(BF16)
    blk = pl.BlockSpec((n_rows, tc), lambda j: (0, j))
    return pl.pallas_call(
        _conv_dense_kernel,
        grid=(nd,),
        in_specs=[full(fk), full(fz_b), full(fzt_b),
                  pl.BlockSpec((n, tc), lambda j: (0, order * nd + j)),
                  pl.BlockSpec((1, tc), lambda j: (0, order * nd + j)),
                  blk, blk, pl.BlockSpec((1, tc), lambda j: (0, j))],
        out_specs=blk,
        out_shape=jax.ShapeDtypeStruct((n_rows, d), BF16),
        compiler_params=_cparams("arbitrary"),
        name="hyena_conv_dense",
    )(fk, fz_b, fzt_b, k2u, colscale, u, xg, skip.reshape(1, d))


def _hyena_stream(h, rows, row_off, n_rows, seq_len, prm, dense):
    (w_in, b_in, conv_w, conv_b, f_w1, f_b1, f_w2, f_b2, f_w3, f_freq, skip) = prm
    d = w_in.shape[0]
    assert n_rows == 2 * seq_len, "the complex pairing of batch rows needs exactly two sequences"
    zc = _hyena_in(h, w_in, b_in, conv_w, conv_b, rows, row_off, n_rows, seq_len)
    x1, x2, v = zc[0], zc[1], zc[2]
    k2u, sumsq = _hyena_filters(seq_len, d, f_w1, f_b1, f_w2, f_b2, f_w3, f_freq)
    if dense:
        v = _hyena_long_conv_dense(v, x1, k2u, sumsq, 0, skip[0], seq_len)
        return _hyena_long_conv_dense(v, x2, k2u, sumsq, 1, skip[1], seq_len)
    n = 2 * seq_len
    n1 = n // DFT_N2
    kf = _hyena_kf(k2u, sumsq, n1)
    tables = _stage2_tables(n, n1)
    v = _hyena_long_conv(v, x1, kf, 0, skip[0], seq_len, tables)
    return _hyena_long_conv(v, x2, kf, 1, skip[1], seq_len, tables)


def _layer_hyena(x_all, mod, rows, g, prm, w_out, b_out):
    h = _norm_mod(x_all, g, mod, 0, rows, BF16)
    vc = _hyena_stream(h, rows, 0, rows.n_ctx, rows.ctx_len, prm, dense=True)
    x_all = _proj_residual(vc, w_out, b_out, x_all, mod, 2, rows, 0, name="hyena_out_ctx")
    vl = _hyena_stream(h, rows, rows.ctx_blocks, rows.n_lat, rows.seq, prm, dense=False)
    return _proj_residual(vl, w_out, b_out, x_all, mod, 2, rows, rows.ctx_blocks, name="hyena_out_lat")


def _layer_pool(x_all, mod, rows, g, w_grp, scale):
    h = _norm_mod(x_all, g, mod, 0, rows, F32)
    x_all = _pool_mixer(h, w_grp, scale, x_all, mod, 2, rows, 0, rows.n_ctx, rows.ctx_len)
    return _pool_mixer(h, w_grp, scale, x_all, mod, 2, rows, rows.ctx_blocks, rows.n_lat, rows.seq)


def _final_norm_kernel(x_ref, g_ref, o_ref):
    x = x_ref[...]
    o_ref[...] = x * lax.rsqrt(jnp.mean(x * x, axis=-1, keepdims=True) + EPS) * g_ref[...]


def _final_norm(x_all, g, rows):
    tm = rows.tile
    d = x_all.shape[1]
    return pl.pallas_call(
        _final_norm_kernel,
        grid=(rows.n_lat // tm,),
        in_specs=[pl.BlockSpec((tm, d), lambda i: (i + rows.ctx_blocks, 0)),
                  pl.BlockSpec((1, d), lambda i: (0, 0))],
        out_specs=pl.BlockSpec((tm, d), lambda i: (i, 0)),
        out_shape=jax.ShapeDtypeStruct((rows.n_lat, d), F32),
        compiler_params=_cparams("arbitrary"),
        name="final_norm",
    )(x_all, g.reshape(1, d))


def kernel(x, c, ctx, c_ctx, ada_w, ada_b, norm_g, final_g, hy_w_in, hy_b_in, hy_conv_w, hy_conv_b, hy_f_w1, hy_f_b1, hy_f_w2, hy_f_b2, hy_f_w3, hy_f_freq, hy_skip, hy_w_out, hy_b_out, fn_w_out, fn_b_out, pl_w, pl_scale, mla_w_dq, mla_g_q, mla_w_uq, mla_w_dkv, mla_g_kv, mla_w_ukv, mla_w_o, moe_w_group, moe_b_group, moe_w_expert, moe_b_expert, moe_w1, moe_w3, moe_w2):
    batch, seq, d = x.shape
    ctx_len = ctx.shape[1]
    depth = ada_w.shape[0]
    rows = _Rows(batch, ctx_len, seq, ROW_TILE)
    mod = _ada_mod(c, c_ctx, ada_w, ada_b)
    x_all = jnp.concatenate([ctx.reshape(-1, d), x.reshape(-1, d)], axis=0)
    for i in range(depth):
        kind, j = i % N_MIXERS, i // N_MIXERS
        m = mod[i]
        g1 = norm_g[i, 0]
        if kind == 0:
            prm = (hy_w_in[j], hy_b_in[j], hy_conv_w[j], hy_conv_b[j], hy_f_w1[j], hy_f_b1[j], hy_f_w2[j],
                   hy_f_b2[j], hy_f_w3[j], hy_f_freq[j], hy_skip[j])
            x_all = _layer_hyena(x_all, m, rows, g1, prm, hy_w_out[j], hy_b_out[j])
        elif kind == 1:
            x_all = _layer_fnet(x_all, m, rows, g1, fn_w_out[j], fn_b_out[j])
        elif kind == 2:
            x_all = _layer_pool(x_all, m, rows, g1, pl_w[j], pl_scale[j])
        else:
            x_all = _layer_mla(x_all, m, rows, g1, mla_w_dq[j], mla_g_q[j], mla_w_uq[j], mla_w_dkv[j],
                               mla_g_kv[j], mla_w_ukv[j], mla_w_o[j], update_ctx=i < depth - 1)
        h, eid, wts = _norm_router(x_all, norm_g[i, 1], m, 3, rows, moe_w_group[i], moe_b_group[i],
                                   moe_w_expert[i], moe_b_expert[i])
        y = _moe_experts(h, eid, wts, moe_w1, moe_w3, moe_w2, i)
        x_all = _moe_combine(x_all, y, m, 5, rows)
    return _final_norm(x_all, final_g, rows).reshape(batch, seq, d)
```

```python
import functools
import math

import numpy as np
import jax
import jax.numpy as jnp
from jax import lax
from jax.experimental import pallas as pl
from jax.experimental.pallas import tpu as pltpu

F32 = jnp.float32
BF16 = jnp.bfloat16
HIGHEST = lax.Precision.HIGHEST

EPS = 1e-6
LANES = 128
MOD_ROWS = 8
VMEM_LIMIT = 56 * 1024 * 1024

N_MIXERS = 4
HYENA_ORDER = 2
FILTER_EMB = 33
DECAY_TARGET = 1e-2
FAST_DECAY_PCT = 0.3
SLOW_DECAY_PCT = 1.5
MOD_SHIFT = 0.0
FNET_GROUPS = 4
POOL_WINDOWS = (2, 4, 8, 16)
MLA_HEADS = 16
QK_NOPE = 128
QK_ROPE = 64
V_DIM = 128
GRID_W = 64
ROPE_THETA = 10000.0
N_GROUPS = 4
EXPERTS_PER_GROUP = 8
TOPK_EXPERT = 2
ROW_TILE = 256
MOE_ROWS = 256


def _cparams(*sem):
    return pltpu.CompilerParams(dimension_semantics=sem, vmem_limit_bytes=VMEM_LIMIT)


def _ada_kernel(st_ref, w_ref, b_ref, o_ref, *, nrows):
    s = st_ref[...]
    s = s * jax.nn.sigmoid(s)
    w = w_ref[...]
    o_ref[...] = jnp.broadcast_to(b_ref[...], o_ref.shape)
    for r in range(nrows):
        o_ref[r:r + 1, :] = jnp.sum(s[:, r:r + 1] * w, axis=0, keepdims=True) + b_ref[...]


def _ada_mod(c, c_ctx, ada_w, ada_b):
    depth, d, n = ada_w.shape
    nrows = c.shape[0] + 1
    st = jnp.zeros((d, MOD_ROWS), F32).at[:, :nrows - 1].set(c.T).at[:, nrows - 1].set(c_ctx)
    tn = 1024 if n % 1024 == 0 else n
    return pl.pallas_call(
        functools.partial(_ada_kernel, nrows=nrows),
        grid=(depth, n // tn),
        in_specs=[pl.BlockSpec((d, MOD_ROWS), lambda l, j: (0, 0)),
                  pl.BlockSpec((None, d, tn), lambda l, j: (l, 0, j)),
                  pl.BlockSpec((None, 1, tn), lambda l, j: (l, 0, j))],
        out_specs=pl.BlockSpec((None, MOD_ROWS, tn), lambda l, j: (l, 0, j)),
        out_shape=jax.ShapeDtypeStruct((depth, MOD_ROWS, n), F32),
        compiler_params=_cparams("arbitrary", "arbitrary"),
        name="ada_mod",
    )(st, ada_w, ada_b.reshape(depth, 1, n))


class _Rows:
    def __init__(self, batch, ctx_len, seq, tile):
        assert (batch * ctx_len) % tile == 0 and seq % tile == 0
        self.batch, self.ctx_len, self.seq, self.tile = batch, ctx_len, seq, tile
        self.n_ctx = batch * ctx_len
        self.n_lat = batch * seq
        self.n_all = self.n_ctx + self.n_lat
        self.ctx_blocks = self.n_ctx // tile
        self.lat_blocks = seq // tile

    def mod_row(self, i):
        return jnp.where(i < self.ctx_blocks, self.batch, (i - self.ctx_blocks) // self.lat_blocks)


def _rms_mod(x, g, sh, sc):
    y = x * lax.rsqrt(jnp.mean(x * x, axis=-1, keepdims=True) + EPS) * g
    return y * (1.0 + sc) + sh


def _norm_mod_kernel(x_ref, g_ref, sh_ref, sc_ref, o_ref):
    o_ref[...] = _rms_mod(x_ref[...], g_ref[...], sh_ref[...], sc_ref[...]).astype(o_ref.dtype)


def _norm_mod(x_all, g, mod, chunk, rows, out_dtype):
    n, d = x_all.shape
    tm = rows.tile
    mod3 = mod.reshape(MOD_ROWS, 1, -1)
    return pl.pallas_call(
        _norm_mod_kernel,
        grid=(n // tm,),
        in_specs=[pl.BlockSpec((tm, d), lambda i: (i, 0)),
                  pl.BlockSpec((1, d), lambda i: (0, 0)),
                  pl.BlockSpec((None, 1, d), lambda i: (rows.mod_row(i), 0, chunk)),
                  pl.BlockSpec((None, 1, d), lambda i: (rows.mod_row(i), 0, chunk + 1))],
        out_specs=pl.BlockSpec((tm, d), lambda i: (i, 0)),
        out_shape=jax.ShapeDtypeStruct((n, d), out_dtype),
        compiler_params=_cparams("arbitrary"),
        name="norm_mod",
    )(x_all, g.reshape(1, d), mod3, mod3)


def _norm_router_kernel(x_ref, g_ref, sh_ref, sc_ref, wr_ref, br_ref, h_ref, eid_ref, wt_ref,
                        *, n_groups, per_group):
    h = _rms_mod(x_ref[...], g_ref[...], sh_ref[...], sc_ref[...])
    h_ref[...] = h.astype(h_ref.dtype)
    logits = jnp.dot(h, wr_ref[...], precision=HIGHEST, preferred_element_type=F32) + br_ref[...]
    lane = lax.broadcasted_iota(jnp.int32, logits.shape, 1).astype(F32)
    neg = -jnp.inf
    gl = jnp.where(lane < n_groups, logits, neg)
    gmax = jnp.max(gl, axis=-1, keepdims=True)
    p_grp = 1.0 / jnp.sum(jnp.exp(gl - gmax), axis=-1, keepdims=True)
    g_idx = jnp.min(jnp.where(gl == gmax, lane, float(LANES)), axis=-1, keepdims=True)
    lo = n_groups + g_idx * per_group
    el = jnp.where((lane >= lo) & (lane < lo + per_group), logits, neg)
    e1 = jnp.max(el, axis=-1, keepdims=True)
    i1 = jnp.min(jnp.where(el == e1, lane, float(LANES)), axis=-1, keepdims=True)
    el2 = jnp.where(lane == i1, neg, el)
    e2 = jnp.max(el2, axis=-1, keepdims=True)
    i2 = jnp.min(jnp.where(el2 == e2, lane, float(LANES)), axis=-1, keepdims=True)
    r = jnp.exp(e2 - e1)
    w1 = p_grp / (1.0 + r)
    w2 = p_grp * r / (1.0 + r)
    eid = jnp.where(lane == 0, i1 - n_groups, jnp.where(lane == 1, i2 - n_groups, 0.0))
    eid_ref[...] = eid.astype(jnp.int32)
    wt_ref[...] = jnp.where(lane == 0, w1, jnp.where(lane == 1, w2, 0.0))


def _norm_router(x_all, g, mod, chunk, rows, w_group, b_group, w_expert, b_expert):
    n, d = x_all.shape
    tm = rows.tile
    n_groups = w_group.shape[1]
    n_experts = w_expert.shape[1]
    wr = jnp.zeros((d, LANES), F32).at[:, :n_groups].set(w_group).at[:, n_groups:n_groups + n_experts].set(w_expert)
    br = jnp.zeros((1, LANES), F32).at[0, :n_groups].set(b_group).at[0, n_groups:n_groups + n_experts].set(b_expert)
    mod3 = mod.reshape(MOD_ROWS, 1, -1)
    h, eid, wts = pl.pallas_call(
        functools.partial(_norm_router_kernel, n_groups=n_groups, per_group=n_experts // n_groups),
        grid=(n // tm,),
        in_specs=[pl.BlockSpec((tm, d), lambda i: (i, 0)),
                  pl.BlockSpec((1, d), lambda i: (0, 0)),
                  pl.BlockSpec((None, 1, d), lambda i: (rows.mod_row(i), 0, chunk)),
                  pl.BlockSpec((None, 1, d), lambda i: (rows.mod_row(i), 0, chunk + 1)),
                  pl.BlockSpec((d, LANES), lambda i: (0, 0)),
                  pl.BlockSpec((1, LANES), lambda i: (0, 0))],
        out_specs=[pl.BlockSpec((tm, d), lambda i: (i, 0)),
                   pl.BlockSpec((tm, LANES), lambda i: (i, 0)),
                   pl.BlockSpec((tm, LANES), lambda i: (i, 0))],
        out_shape=[jax.ShapeDtypeStruct((n, d), F32),
                   jax.ShapeDtypeStruct((n, LANES), jnp.int32),
                   jax.ShapeDtypeStruct((n, LANES), F32)],
        compiler_params=_cparams("arbitrary"),
        name="norm_router",
    )(x_all, g.reshape(1, d), mod3, mod3, wr, br)
    return h, eid[:, :TOPK_EXPERT], wts[:, :TOPK_EXPERT]


def _moe_dispatch(eid, wts, n_experts, tm):
    n_tok = eid.shape[0]
    n_assign = eid.size
    e_flat = eid.reshape(-1)
    order = jnp.argsort(e_flat).astype(jnp.int32)
    counts = jnp.sum((e_flat[:, None] == jnp.arange(n_experts, dtype=jnp.int32)[None, :]).astype(jnp.int32), axis=0)
    padded = (counts + tm - 1) // tm * tm
    start = jnp.cumsum(counts) - counts
    pend = jnp.cumsum(padded)
    pstart = pend - padded
    n_blocks = (n_assign + n_experts * (tm - 1) + tm - 1) // tm
    blk_start = jnp.arange(n_blocks, dtype=jnp.int32) * tm
    blk_e = jnp.minimum(jnp.sum((pend[None, :] <= blk_start[:, None]).astype(jnp.int32), axis=1), n_experts - 1)
    blk_rows = jnp.clip(pstart[blk_e] + counts[blk_e] - blk_start, 0, tm).astype(jnp.int32)
    row = lax.broadcasted_iota(jnp.int32, (n_blocks, tm), 1)
    slot_e = blk_e[:, None]
    src = start[slot_e] + (blk_start[:, None] + row - pstart[slot_e])
    valid = row < blk_rows[:, None]
    asg = order[jnp.clip(src, 0, n_assign - 1)]
    tok = asg // TOPK_EXPERT
    buf_tok = jnp.where(valid, tok, 0)
    dummy = TOPK_EXPERT * n_tok + (jnp.arange(n_blocks, dtype=jnp.int32)[:, None] % 2) * tm + row
    buf_asg = jnp.where(valid, (asg % TOPK_EXPERT) * n_tok + tok, dummy)
    buf_w = jnp.where(valid, wts.reshape(-1)[asg], 0.0)
    return (buf_tok.reshape(n_blocks, 1, tm), buf_asg.reshape(n_blocks, 1, tm),
            buf_w.reshape(n_blocks, tm, 1), blk_e.astype(jnp.int32), blk_rows)


def _moe_kernel(blk_e_ref, blk_rows_ref, tok_ref, tok_next_ref, asg_ref, roww_ref, h_hbm,
                w1_ref, w3_ref, w2_ref, y_hbm, xbuf, ybuf, w1b, w3b, w2b, gsem, ssem):
    b = pl.program_id(0)
    nb = pl.num_programs(0)
    slot = b % 2
    other = 1 - slot

    tm = xbuf.shape[1]

    def start_gather(ids_ref, s):
        def body(r, c):
            pltpu.make_async_copy(h_hbm.at[pl.ds(ids_ref[0, r], 1)], xbuf.at[s, pl.ds(r, 1)], gsem.at[s]).start()
            return c
        lax.fori_loop(0, tm, body, 0, unroll=8)

    def wait_gather(s):
        pltpu.make_async_copy(h_hbm.at[pl.ds(0, tm)], xbuf.at[s], gsem.at[s]).wait()

    def start_scatter(s):
        def body(r, c):
            pltpu.make_async_copy(ybuf.at[s, pl.ds(r, 1)], y_hbm.at[pl.ds(asg_ref[0, r], 1)], ssem.at[s]).start()
            return c
        lax.fori_loop(0, tm, body, 0, unroll=8)

    def wait_scatter(s):
        pltpu.make_async_copy(ybuf.at[s], y_hbm.at[pl.ds(0, tm)], ssem.at[s]).wait()

    def used(blk):
        return blk_rows_ref[jnp.clip(blk, 0, nb - 1)] > 0

    @pl.when(b == 0)
    def _():
        ybuf[...] = jnp.zeros(ybuf.shape, ybuf.dtype)
        n_real = y_hbm.shape[0] - 2 * tm
        for s in range(2):
            fill = pltpu.make_async_copy(ybuf.at[s], y_hbm.at[pl.ds(n_real + s * tm, tm)], ssem.at[s])
            fill.start()
            fill.wait()

    @pl.when((b == 0) & used(b))
    def _():
        start_gather(tok_ref, slot)

    @pl.when((b + 1 < nb) & used(b + 1))
    def _():
        start_gather(tok_next_ref, other)

    @pl.when((b >= 2) & used(b - 2))
    def _():
        wait_scatter(slot)

    @pl.when(used(b))
    def _():
        wait_gather(slot)

        @pl.when((b == 0) | (blk_e_ref[b] != blk_e_ref[jnp.maximum(b - 1, 0)]))
        def _():
            w1b[...] = w1_ref[...].astype(BF16)
            w3b[...] = w3_ref[...].astype(BF16)
            w2b[...] = w2_ref[...].astype(BF16)

        x = xbuf[slot].astype(BF16)
        a = jnp.dot(x, w1b[...], preferred_element_type=F32)
        g = jnp.dot(x, w3b[...], preferred_element_type=F32)
        hm = (a * jax.nn.sigmoid(a) * g).astype(BF16)
        ybuf[slot] = jnp.dot(hm, w2b[...], preferred_element_type=F32) * roww_ref[...]
        start_scatter(slot)

    @pl.when(b == nb - 1)
    def _():
        @pl.when((nb >= 2) & used(b - 1))
        def _():
            wait_scatter(other)

        @pl.when(used(b))
        def _():
            wait_scatter(slot)


def _moe_experts(h, eid, wts, w1, w3, w2, layer):
    n, d = h.shape
    n_experts, de = w1.shape[1], w1.shape[3]
    tm = MOE_ROWS
    tok, asg, roww, blk_e, blk_rows = _moe_dispatch(eid, wts, n_experts, tm)
    n_blocks = tok.shape[0]
    smem_blk = lambda f: pl.BlockSpec((None, 1, tm), f, memory_space=pltpu.SMEM)
    grid_spec = pltpu.PrefetchScalarGridSpec(
        num_scalar_prefetch=2,
        grid=(n_blocks,),
        in_specs=[smem_blk(lambda b, be, br: (b, 0, 0)),
                  smem_blk(lambda b, be, br: (jnp.minimum(b + 1, n_blocks - 1), 0, 0)),
                  smem_blk(lambda b, be, br: (b, 0, 0)),
                  pl.BlockSpec((None, tm, 1), lambda b, be, br: (b, 0, 0)),
                  pl.BlockSpec(memory_space=pl.ANY),
                  pl.BlockSpec((None, None, d, de), lambda b, be, br: (layer, be[b], 0, 0)),
                  pl.BlockSpec((None, None, d, de), lambda b, be, br: (layer, be[b], 0, 0)),
                  pl.BlockSpec((None, None, de, d), lambda b, be, br: (layer, be[b], 0, 0))],
        out_specs=pl.BlockSpec(memory_space=pl.ANY),
        scratch_shapes=[pltpu.VMEM((2, tm, d), F32), pltpu.VMEM((2, tm, d), F32),
                        pltpu.VMEM((d, de), BF16), pltpu.VMEM((d, de), BF16), pltpu.VMEM((de, d), BF16),
                        pltpu.SemaphoreType.DMA((2,)), pltpu.SemaphoreType.DMA((2,))],
    )
    return pl.pallas_call(
        _moe_kernel,
        grid_spec=grid_spec,
        out_shape=jax.ShapeDtypeStruct((n * TOPK_EXPERT + 2 * tm, d), F32),
        compiler_params=_cparams("arbitrary"),
        name="moe_experts",
    )(blk_e, blk_rows, tok, tok, asg, roww, h, w1, w3, w2)


def _moe_combine_kernel(x_ref, *rest):
    y_refs, gate_ref, o_ref = rest[:TOPK_EXPERT], rest[TOPK_EXPERT], rest[TOPK_EXPERT + 1]
    acc = y_refs[0][...]
    for y_ref in y_refs[1:]:
        acc = acc + y_ref[...]
    o_ref[...] = x_ref[...] + gate_ref[...] * acc


def _moe_combine(x_all, y, mod, chunk, rows):
    n, d = x_all.shape
    tm = rows.tile
    mod3 = mod.reshape(MOD_ROWS, 1, -1)
    nblk = n // tm
    y_specs = [pl.BlockSpec((tm, d), functools.partial(lambda i, k: (i + k * nblk, 0), k=k)) for k in range(TOPK_EXPERT)]
    return pl.pallas_call(
        _moe_combine_kernel,
        grid=(nblk,),
        in_specs=[pl.BlockSpec((tm, d), lambda i: (i, 0))] + y_specs +
                 [pl.BlockSpec((None, 1, d), lambda i: (rows.mod_row(i), 0, chunk))],
        out_specs=pl.BlockSpec((tm, d), lambda i: (i, 0)),
        out_shape=jax.ShapeDtypeStruct((n, d), F32),
        input_output_aliases={0: 0},
        compiler_params=_cparams("arbitrary"),
        name="moe_combine",
    )(x_all, *([y] * TOPK_EXPERT), mod3)


def _mm_kernel(*refs, has_bias, n_extra, epilogue):
    x_ref, w_ref = refs[0], refs[1]
    pos = 2
    b_ref = None
    if has_bias:
        b_ref = refs[pos]
        pos += 1
    extras = refs[pos:pos + n_extra]
    outs = refs[pos + n_extra:-1]
    wbf = refs[-1]

    @pl.when(pl.program_id(1) == 0)
    def _():
        wbf[...] = w_ref[...].astype(BF16)

    acc = jnp.dot(x_ref[...].astype(BF16), wbf[...], preferred_element_type=F32)
    if has_bias:
        acc = acc + b_ref[...]
    epilogue(acc, extras, outs)


def _store_epilogue(acc, extras, outs):
    outs[0][...] = acc.astype(outs[0].dtype)


def _matmul(x, w, *, w_lead=(), bias=None, tm, tn, m_rows=None, x_row_off=0, extras=(), outs,
            epilogue=_store_epilogue, aliases=None, name="matmul"):
    k = x.shape[1]
    n = w.shape[-1]
    m_rows = x.shape[0] if m_rows is None else m_rows
    assert m_rows % tm == 0 and n % tn == 0 and w.shape[-2] == k
    lead = tuple(w_lead)
    in_specs = [pl.BlockSpec((tm, k), lambda j, i: (i + x_row_off, 0)),
                pl.BlockSpec((None,) * len(lead) + (k, tn), lambda j, i: lead + (0, j))]
    args = [x, w]
    if bias is not None:
        in_specs.append(pl.BlockSpec((1, tn), lambda j, i: (0, j)))
        args.append(bias.reshape(1, n))
    for arr, spec in extras:
        in_specs.append(spec)
        args.append(arr)
    return pl.pallas_call(
        functools.partial(_mm_kernel, has_bias=bias is not None, n_extra=len(extras), epilogue=epilogue),
        grid=(n // tn, m_rows // tm),
        in_specs=in_specs,
        out_specs=[spec for _, spec in outs],
        out_shape=[shape for shape, _ in outs],
        scratch_shapes=[pltpu.VMEM((k, tn), BF16)],
        input_output_aliases=aliases or {},
        compiler_params=_cparams("arbitrary", "arbitrary"),
        name=name,
    )(*args)


def _residual_epilogue(acc, extras, outs):
    res_ref, gate_ref = extras
    outs[0][...] = res_ref[...] + gate_ref[...] * acc


def _proj_residual(xin, w, bias, x_all, mod, chunk, rows, row_off, *, w_lead=(), tn=512, name="proj_residual"):
    tm = rows.tile
    d = x_all.shape[1]
    tn = min(tn, d)
    mod3 = mod.reshape(MOD_ROWS, 1, -1)
    nd = d // tn
    res_spec = pl.BlockSpec((tm, tn), lambda j, i: (i + row_off, j))
    gate_spec = pl.BlockSpec((None, 1, tn), lambda j, i: (rows.mod_row(i + row_off), 0, chunk * nd + j))
    n_extra_before = 2 + (bias is not None)
    out, = _matmul(xin, w, w_lead=w_lead, bias=bias, tm=tm, tn=tn,
                   extras=[(x_all, res_spec), (mod3, gate_spec)],
                   outs=[(jax.ShapeDtypeStruct(x_all.shape, F32), res_spec)],
                   epilogue=_residual_epilogue, aliases={n_extra_before: 0}, name=name)
    return out


POOL_HALO = 8


def _pool_kernel(prev_ref, main_ref, next_ref, w_ref, sc_ref, res_ref, gate_ref, o_ref, ext_ref, wbf_ref,
                 *, tm, seq_len, windows):
    i = pl.program_id(0)
    seq_tiles = seq_len // tm
    t_in_seq = i % seq_tiles

    @pl.when(i == 0)
    def _():
        wbf_ref[...] = w_ref[...].astype(BF16)

    zero_halo = jnp.zeros(prev_ref.shape, F32)
    ext_ref[0:POOL_HALO, :] = jnp.where(t_in_seq == 0, zero_halo, prev_ref[...])
    ext_ref[POOL_HALO:POOL_HALO + tm, :] = main_ref[...]
    ext_ref[POOL_HALO + tm:2 * POOL_HALO + tm, :] = jnp.where(t_in_seq == seq_tiles - 1, zero_halo, next_ref[...])

    pos = t_in_seq * tm + lax.broadcasted_iota(jnp.int32, (tm, 1), 0)
    cg = main_ref.shape[1] // len(windows)
    for gi, win in enumerate(windows):
        half = win // 2
        cols = slice(gi * cg, (gi + 1) * cg)
        s = ext_ref[pl.ds(POOL_HALO - half, tm), cols]
        for dlt in range(-half + 1, half):
            s = s + ext_ref[pl.ds(POOL_HALO + dlt, tm), cols]
        cnt = jnp.minimum(pos + half, seq_len) - jnp.maximum(pos - half, 0)
        pooled = s / cnt.astype(F32) - main_ref[:, cols]
        y = jnp.dot(pooled.astype(BF16), wbf_ref[gi], preferred_element_type=F32) * sc_ref[:, cols]
        o_ref[:, cols] = res_ref[:, cols] + gate_ref[:, cols] * y


def _pool_mixer(h, w_grp, scale, x_all, mod, chunk, rows, row_off, n_rows, seq_len):
    tm = rows.tile
    d = x_all.shape[1]
    assert max(POOL_WINDOWS) // 2 <= POOL_HALO and tm % POOL_HALO == 0 and seq_len % tm == 0
    hb = tm // POOL_HALO
    last_hblk = h.shape[0] // POOL_HALO - 1
    mod3 = mod.reshape(MOD_ROWS, 1, -1)
    main_spec = pl.BlockSpec((tm, d), lambda i: (i + row_off, 0))
    return pl.pallas_call(
        functools.partial(_pool_kernel, tm=tm, seq_len=seq_len, windows=POOL_WINDOWS),
        grid=(n_rows // tm,),
        in_specs=[pl.BlockSpec((POOL_HALO, d), lambda i: (jnp.maximum((i + row_off) * hb - 1, 0), 0)),
                  main_spec,
                  pl.BlockSpec((POOL_HALO, d), lambda i: (jnp.minimum((i + row_off + 1) * hb, last_hblk), 0)),
                  pl.BlockSpec(w_grp.shape, lambda i: (0, 0, 0)),
                  pl.BlockSpec((1, d), lambda i: (0, 0)),
                  main_spec,
                  pl.BlockSpec((None, 1, d), lambda i: (rows.mod_row(i + row_off), 0, chunk))],
        out_specs=main_spec,
        out_shape=jax.ShapeDtypeStruct(x_all.shape, F32),
        scratch_shapes=[pltpu.VMEM((tm + 2 * POOL_HALO, d), F32), pltpu.VMEM(w_grp.shape, BF16)],
        input_output_aliases={5: 0},
        compiler_params=_cparams("arbitrary"),
        name="pool_mixer",
    )(h, h, h, w_grp, scale.reshape(1, d), x_all, mod3)


DFT_N2 = 128


def _cis(num, den):
    ang = (num % den).astype(F32) * (2.0 * math.pi / den)
    return jnp.cos(ang), jnp.sin(ang)


def _iota2(n_rows, n_cols):
    return (lax.broadcasted_iota(jnp.int32, (n_rows, n_cols), 0), lax.broadcasted_iota(jnp.int32, (n_rows, n_cols), 1))


def _left_mm_kernel(w_ref, x_ref, *rest, epilogue):
    x = x_ref[...]
    x = x.reshape(-1, x.shape[-1])
    acc = jnp.dot(w_ref[...], x.astype(BF16), preferred_element_type=F32)
    epilogue(acc, rest[:-1], rest[-1])


def _scaled_store(scale):
    def epilogue(acc, extras, o_ref):
        o_ref[...] = (acc * scale).reshape(o_ref.shape).astype(o_ref.dtype)
    return epilogue


def _left_mm(w, x, *, grid, w_spec, x_spec, out_shape, out_spec, extras=(), epilogue=_scaled_store(1.0),
             aliases=None, name="left_mm"):
    return pl.pallas_call(
        functools.partial(_left_mm_kernel, epilogue=epilogue),
        grid=grid,
        in_specs=[w_spec, x_spec] + [s for _, s in extras],
        out_specs=out_spec,
        out_shape=out_shape,
        input_output_aliases=aliases or {},
        compiler_params=_cparams(*(("arbitrary",) * len(grid))),
        name=name,
    )(w, x, *[a for a, _ in extras])


def _fnet_channel_kernel(x_ref, w_ref, o_ref):
    cg = x_ref.shape[1]
    r = jnp.dot(x_ref[...], w_ref[...], preferred_element_type=F32)
    o_ref[0] = r[:, :cg].astype(o_ref.dtype)
    o_ref[1] = r[:, cg:].astype(o_ref.dtype)


def _fnet_channel(h, rows, row_off, n_seq, seq_len):
    tm = rows.tile
    d = h.shape[1]
    cg = d // FNET_GROUPS
    ci, ki = _iota2(cg, cg)
    cc, sc = _cis(ci * ki, cg)
    wc = jnp.concatenate([cc, -sc], axis=1).astype(BF16)
    st = seq_len // tm
    return pl.pallas_call(
        _fnet_channel_kernel,
        grid=(n_seq * st, FNET_GROUPS),
        in_specs=[pl.BlockSpec((tm, cg), lambda i, g: (i + row_off, g)),
                  pl.BlockSpec((cg, 2 * cg), lambda i, g: (0, 0))],
        out_specs=pl.BlockSpec((None, 2, tm, cg), lambda i, g: (i // st, 0, i % st, g)),
        out_shape=jax.ShapeDtypeStruct((n_seq, 2, seq_len, d), BF16),
        compiler_params=_cparams("arbitrary", "arbitrary"),
        name="fnet_channel",
    )(h, wc)


def _fnet_positions(z, n1, tc):
    n_seq, _, seq_len, d = z.shape
    n2 = seq_len // n1
    scale = 1.0 / math.sqrt(seq_len * (d // FNET_GROUPS))
    if n1 > 1:
        r, cidx = _iota2(2 * n1, 2 * n1)
        k1, ro, ri, nn = r // 2, r % 2, cidx // n1, cidx % n1
        fr, fs = _cis(k1 * nn, n1)
        w1 = jnp.where(ro == ri, fr, jnp.where(ro == 0, fs, -fs)).astype(BF16)
        cols = n2 * d
        t1 = min(cols, 4096)
        a = _left_mm(w1, z.reshape(n_seq, 2 * n1, cols),
                     grid=(n_seq, cols // t1),
                     w_spec=pl.BlockSpec((2 * n1, 2 * n1), lambda s, j: (0, 0)),
                     x_spec=pl.BlockSpec((None, 2 * n1, t1), lambda s, j: (s, 0, j)),
                     out_shape=jax.ShapeDtypeStruct((n_seq, 2 * n1, cols), BF16),
                     out_spec=pl.BlockSpec((None, 2 * n1, t1), lambda s, j: (s, 0, j)),
                     name="fnet_stage1")
        a = a.reshape(n_seq, n1, 2, n2, d)
    else:
        a = z.reshape(n_seq, 1, 2, n2, d)
    k2i, ci2 = _iota2(n2, 2 * n2)
    kk = jnp.arange(n1, dtype=jnp.int32)[:, None, None] + n1 * k2i[None]
    gc, gs = _cis(kk * (ci2 % n2)[None], seq_len)
    g2 = jnp.where((ci2 < n2)[None], gc, gs).astype(BF16)
    nd = d // tc
    out = _left_mm(g2, a,
                   grid=(n_seq, n1, nd),
                   w_spec=pl.BlockSpec((None, n2, 2 * n2), lambda s, k, j: (k, 0, 0)),
                   x_spec=pl.BlockSpec((None, None, 2, n2, tc), lambda s, k, j: (s, k, 0, 0, j)),
                   out_shape=jax.ShapeDtypeStruct((n_seq, n2, n1 * d), BF16),
                   out_spec=pl.BlockSpec((None, n2, tc), lambda s, k, j: (s, 0, k * nd + j)),
                   epilogue=_scaled_store(scale), name="fnet_stage2")
    return out.reshape(n_seq * seq_len, d)


def _layer_fnet(x_all, mod, rows, g, w_out, b_out):
    h = _norm_mod(x_all, g, mod, 0, rows, BF16)
    d = x_all.shape[1]
    zc = _fnet_channel(h, rows, 0, rows.batch, rows.ctx_len)
    fc = _fnet_positions(zc, 1, min(d, 512))
    x_all = _proj_residual(fc, w_out, b_out, x_all, mod, 2, rows, 0, name="fnet_out_ctx")
    zl = _fnet_channel(h, rows, rows.ctx_blocks, rows.batch, rows.seq)
    fl = _fnet_positions(zl, rows.seq // DFT_N2, min(d, 512))
    return _proj_residual(fl, w_out, b_out, x_all, mod, 2, rows, rows.ctx_blocks, name="fnet_out_lat")


HEAD_SLAB = 2 * LANES
MLA_SCALE = (QK_NOPE + QK_ROPE) ** -0.5
Q_SCALE = MLA_SCALE * math.log2(math.e)
V_SLAB = V_DIM + 16
ATTN_TQ = 1024
ATTN_TK = 1024


def _rope_tables(seq_len, lead_identity_rows):
    n_rows = seq_len // GRID_W
    row = jnp.repeat(jnp.arange(n_rows, dtype=F32), GRID_W)
    col = jnp.tile(jnp.arange(GRID_W, dtype=F32), n_rows)
    half = QK_ROPE // 2
    inv = ROPE_THETA ** (-jnp.arange(0, half, 2, dtype=F32) / half)
    ang_r = row[:, None] * inv
    ang_c = col[:, None] * inv
    ang = jnp.concatenate([ang_r, ang_r, ang_c, ang_c], axis=-1)
    pad = jnp.zeros((seq_len, LANES - QK_ROPE), F32)
    cos = jnp.concatenate([jnp.cos(ang), pad + 1.0], axis=-1)
    sin = jnp.concatenate([jnp.sin(ang), pad], axis=-1)
    if lead_identity_rows:
        cos = jnp.concatenate([jnp.ones((lead_identity_rows, LANES), F32), cos], axis=0)
        sin = jnp.concatenate([jnp.zeros((lead_identity_rows, LANES), F32), sin], axis=0)
    return cos, sin


def _rope(x, cos, sin):
    q = QK_ROPE // 4
    lane = lax.broadcasted_iota(jnp.int32, x.shape, 1)
    even = (lane // q) % 2 == 0
    rot = jnp.where(even, -pltpu.roll(x, LANES - q, 1), pltpu.roll(x, q, 1))
    return x * cos + rot * sin


def _rmsnorm_epilogue(acc, extras, outs):
    g_ref, = extras
    y = acc * lax.rsqrt(jnp.mean(acc * acc, axis=-1, keepdims=True) + EPS) * g_ref[...]
    outs[0][...] = y.astype(outs[0].dtype)


def _q_up_epilogue(acc, extras, outs):
    cos_ref, sin_ref = extras
    o_ref, = outs
    for hh in range(acc.shape[1] // HEAD_SLAB):
        c0 = hh * HEAD_SLAB
        o_ref[:, c0:c0 + LANES] = (acc[:, c0:c0 + LANES] * Q_SCALE).astype(o_ref.dtype)
        pe = _rope(acc[:, c0 + LANES:c0 + HEAD_SLAB], cos_ref[...], sin_ref[...])
        o_ref[:, c0 + LANES:c0 + HEAD_SLAB] = (pe * Q_SCALE).astype(o_ref.dtype)


def _kv_down_epilogue(acc, extras, outs, *, rank):
    g_ref, cos_ref, sin_ref = extras
    kvn_ref, kpe_ref = outs
    lat = acc[:, :rank]
    y = lat * lax.rsqrt(jnp.mean(lat * lat, axis=-1, keepdims=True) + EPS) * g_ref[...]
    kvn_ref[...] = y.astype(kvn_ref.dtype)
    kpe_ref[...] = _rope(acc[:, rank:rank + LANES], cos_ref[...], sin_ref[...]).astype(kpe_ref.dtype)


def _k_up_epilogue(acc, extras, outs):
    kpe_ref, = extras
    o_ref, = outs
    for hh in range(acc.shape[1] // LANES):
        o_ref[:, hh * HEAD_SLAB:hh * HEAD_SLAB + LANES] = acc[:, hh * LANES:(hh + 1) * LANES].astype(o_ref.dtype)
        o_ref[:, hh * HEAD_SLAB + LANES:(hh + 1) * HEAD_SLAB] = kpe_ref[...]


def _attn_kernel(q_ref, kc_ref, kl_ref, vc_ref, vl_ref, o_ref, *, tk):
    q = q_ref[...]

    def scores(k_tile):
        return lax.dot_general(k_tile, q, (((1,), (1,)), ((), ())), preferred_element_type=F32)

    def probs(s, m):
        return jnp.exp2((s - m).astype(BF16))

    s = scores(kc_ref[...])
    m = jnp.max(s, axis=0, keepdims=True)
    acc = jnp.dot(vc_ref[...], probs(s, m), preferred_element_type=F32)
    for j in range(kl_ref.shape[0] // tk):
        s = scores(kl_ref[j * tk:(j + 1) * tk, :])
        m_new = jnp.maximum(m, jnp.max(s, axis=0, keepdims=True))
        acc = jnp.exp2(m - m_new) * acc + jnp.dot(vl_ref[:, j * tk:(j + 1) * tk], probs(s, m_new),
                                                  preferred_element_type=F32)
        m = m_new
    o_ref[...] = (acc[:V_DIM] / acc[V_DIM:V_DIM + 1]).astype(o_ref.dtype)


def _value_slabs(v, batch, keys):
    v4 = v.reshape(batch, keys, MLA_HEADS, V_DIM)
    ones = jnp.ones((batch, keys, MLA_HEADS, 1), v.dtype)
    pad = jnp.zeros((batch, keys, MLA_HEADS, V_SLAB - V_DIM - 1), v.dtype)
    return jnp.concatenate([v4, ones, pad], axis=-1).transpose(0, 2, 3, 1).reshape(batch, MLA_HEADS * V_SLAB, keys)


def _attention(q, k_ctx, k_lat, vt_ctx, vt_lat):
    batch, seq, _ = q.shape
    ctx_len = k_ctx.shape[1]
    tq = min(ATTN_TQ, seq)
    tk = min(ATTN_TK, seq)
    assert seq % tq == 0 and seq % tk == 0
    return pl.pallas_call(
        functools.partial(_attn_kernel, tk=tk),
        grid=(batch, MLA_HEADS, seq // tq),
        in_specs=[pl.BlockSpec((None, tq, HEAD_SLAB), lambda b, h, i: (b, i, h)),
                  pl.BlockSpec((None, ctx_len, HEAD_SLAB), lambda b, h, i: (b, 0, h)),
                  pl.BlockSpec((None, seq, HEAD_SLAB), lambda b, h, i: (b, 0, h)),
                  pl.BlockSpec((None, V_SLAB, ctx_len), lambda b, h, i: (b, h, 0)),
                  pl.BlockSpec((None, V_SLAB, seq), lambda b, h, i: (b, h, 0))],
        out_specs=pl.BlockSpec((None, V_DIM, tq), lambda b, h, i: (b, h, i)),
        out_shape=jax.ShapeDtypeStruct((batch, MLA_HEADS * V_DIM, seq), BF16),
        compiler_params=_cparams("arbitrary", "arbitrary", "arbitrary"),
        name="mla_attention",
    )(q, k_ctx, k_lat, vt_ctx, vt_lat)


def _layer_mla(x_all, mod, rows, g, w_dq, g_q, w_uq, w_dkv, g_kv, w_ukv, w_o, update_ctx=False):
    assert not update_ctx, "attention is the last mixer of the stack: context queries are never needed"
    tm = rows.tile
    d = x_all.shape[1]
    batch, seq, ctx_len = rows.batch, rows.seq, rows.ctx_len
    q_rank, kv_rank = w_dq.shape[1], g_kv.shape[0]
    assert V_DIM == LANES and QK_NOPE == LANES and QK_ROPE <= LANES
    h = _norm_mod(x_all, g, mod, 0, rows, BF16)

    w_uq_s = jnp.pad(w_uq.reshape(q_rank, MLA_HEADS, QK_NOPE + QK_ROPE),
                     ((0, 0), (0, 0), (0, HEAD_SLAB - QK_NOPE - QK_ROPE))).reshape(q_rank, MLA_HEADS * HEAD_SLAB)
    w_dkv_s = jnp.pad(w_dkv, ((0, 0), (0, kv_rank + LANES - w_dkv.shape[1])))
    w_ukv_s = w_ukv.reshape(kv_rank, MLA_HEADS, QK_NOPE + V_DIM)
    w_uk = w_ukv_s[:, :, :QK_NOPE].reshape(kv_rank, MLA_HEADS * QK_NOPE)
    w_uv = w_ukv_s[:, :, QK_NOPE:].reshape(kv_rank, MLA_HEADS * V_DIM)
    cos, sin = _rope_tables(seq, tm)
    lat_blocks = rows.lat_blocks

    cqn, = _matmul(h, w_dq, tm=tm, tn=q_rank, m_rows=rows.n_lat, x_row_off=rows.ctx_blocks,
                   extras=[(g_q.reshape(1, q_rank), pl.BlockSpec((1, q_rank), lambda j, i: (0, 0)))],
                   outs=[(jax.ShapeDtypeStruct((rows.n_lat, q_rank), BF16), pl.BlockSpec((tm, q_rank), lambda j, i: (i, 0)))],
                   epilogue=_rmsnorm_epilogue, name="mla_q_down")
    tnq = 4 * HEAD_SLAB
    rope_lat = pl.BlockSpec((tm, LANES), lambda j, i: (1 + i % lat_blocks, 0))
    q, = _matmul(cqn, w_uq_s, tm=tm, tn=tnq, extras=[(cos, rope_lat), (sin, rope_lat)],
                 outs=[(jax.ShapeDtypeStruct((rows.n_lat, MLA_HEADS * HEAD_SLAB), BF16),
                        pl.BlockSpec((tm, tnq), lambda j, i: (i, j)))],
                 epilogue=_q_up_epilogue, name="mla_q_up")

    def keys_values(row_off, n_rows, rope_spec):
        kvn, kpe = _matmul(h, w_dkv_s, tm=tm, tn=kv_rank + LANES, m_rows=n_rows, x_row_off=row_off,
                           extras=[(g_kv.reshape(1, kv_rank), pl.BlockSpec((1, kv_rank), lambda j, i: (0, 0))),
                                   (cos, rope_spec), (sin, rope_spec)],
                           outs=[(jax.ShapeDtypeStruct((n_rows, kv_rank), BF16), pl.BlockSpec((tm, kv_rank), lambda j, i: (i, 0))),
                                 (jax.ShapeDtypeStruct((n_rows, LANES), BF16), pl.BlockSpec((tm, LANES), lambda j, i: (i, 0)))],
                           epilogue=functools.partial(_kv_down_epilogue, rank=kv_rank), name="mla_kv_down")
        tnk = 4 * LANES
        k, = _matmul(kvn, w_uk, tm=tm, tn=tnk,
                     extras=[(kpe, pl.BlockSpec((tm, LANES), lambda j, i: (i, 0)))],
                     outs=[(jax.ShapeDtypeStruct((n_rows, MLA_HEADS * HEAD_SLAB), BF16),
                            pl.BlockSpec((tm, 2 * tnk), lambda j, i: (i, j)))],
                     epilogue=_k_up_epilogue, name="mla_k_up")
        v, = _matmul(kvn, w_uv, tm=tm, tn=tnk,
                     outs=[(jax.ShapeDtypeStruct((n_rows, MLA_HEADS * V_DIM), BF16), pl.BlockSpec((tm, tnk), lambda j, i: (i, j)))],
                     name="mla_v_up")
        return k, v

    k_c, v_c = keys_values(0, rows.n_ctx, pl.BlockSpec((tm, LANES), lambda j, i: (0, 0)))
    k_l, v_l = keys_values(rows.ctx_blocks, rows.n_lat, rope_lat)
    vt_c = _value_slabs(v_c, batch, ctx_len)
    vt_l = _value_slabs(v_l, batch, seq)
    ot = _attention(q.reshape(batch, seq, -1), k_c.reshape(batch, ctx_len, -1), k_l.reshape(batch, seq, -1), vt_c, vt_l)
    attn = ot.swapaxes(1, 2).reshape(rows.n_lat, MLA_HEADS * V_DIM)
    return _proj_residual(attn, w_o, None, x_all, mod, 2, rows, rows.ctx_blocks, name="mla_out")


CONV_HALO = 16
MAX_DECAY = math.log(DECAY_TARGET) / FAST_DECAY_PCT
MIN_DECAY = math.log(DECAY_TARGET) / SLOW_DECAY_PCT


def _hyena_in_kernel(prev_ref, main_ref, next_ref, w_ref, b_ref, cw_ref, cb_ref, o_ref, xext_ref, zext_ref, wbf_ref,
                     *, tm, seq_len):
    i = pl.program_id(1)
    seq_tiles = seq_len // tm
    t_in_seq = i % seq_tiles

    @pl.when(i == 0)
    def _():
        wbf_ref[...] = w_ref[...].astype(BF16)

    xext_ref[0:CONV_HALO, :] = prev_ref[...]
    xext_ref[CONV_HALO:CONV_HALO + tm, :] = main_ref[...]
    xext_ref[CONV_HALO + tm:2 * CONV_HALO + tm, :] = next_ref[...]
    zext_ref[...] = jnp.dot(xext_ref[...], wbf_ref[...], preferred_element_type=F32) + b_ref[...]
    row = lax.broadcasted_iota(jnp.int32, (tm, 1), 0)
    prev = jnp.where((row == 0) & (t_in_seq == 0), 0.0, zext_ref[pl.ds(CONV_HALO - 1, tm), :])
    nxt = jnp.where((row == tm - 1) & (t_in_seq == seq_tiles - 1), 0.0, zext_ref[pl.ds(CONV_HALO + 1, tm), :])
    out = prev * cw_ref[0:1, :] + zext_ref[pl.ds(CONV_HALO, tm), :] * cw_ref[1:2, :] + nxt * cw_ref[2:3, :] + cb_ref[...]
    o_ref[...] = out.astype(o_ref.dtype)


def _hyena_in(h, w_in, b_in, conv_w, conv_b, rows, row_off, n_rows, seq_len, tn=512):
    tm = rows.tile
    k, n3 = w_in.shape
    d = n3 // 3
    tn = min(tn, d)
    nd = d // tn
    hb = tm // CONV_HALO
    last_hblk = h.shape[0] // CONV_HALO - 1
    return pl.pallas_call(
        functools.partial(_hyena_in_kernel, tm=tm, seq_len=seq_len),
        grid=(n3 // tn, n_rows // tm),
        in_specs=[pl.BlockSpec((CONV_HALO, k), lambda j, i: (jnp.maximum((i + row_off) * hb - 1, 0), 0)),
                  pl.BlockSpec((tm, k), lambda j, i: (i + row_off, 0)),
                  pl.BlockSpec((CONV_HALO, k), lambda j, i: (jnp.minimum((i + row_off + 1) * hb, last_hblk), 0)),
                  pl.BlockSpec((k, tn), lambda j, i: (0, j)),
                  pl.BlockSpec((1, tn), lambda j, i: (0, j)),
                  pl.BlockSpec((3, tn), lambda j, i: (0, j)),
                  pl.BlockSpec((1, tn), lambda j, i: (0, j))],
        out_specs=pl.BlockSpec((None, tm, tn), lambda j, i: (j // nd, i, j % nd)),
        out_shape=jax.ShapeDtypeStruct((3, n_rows, d), BF16),
        scratch_shapes=[pltpu.VMEM((tm + 2 * CONV_HALO, k), BF16), pltpu.VMEM((tm + 2 * CONV_HALO, tn), F32),
                        pltpu.VMEM((k, tn), BF16)],
        compiler_params=_cparams("arbitrary", "arbitrary"),
        name="hyena_in",
    )(h, h, h, w_in, b_in.reshape(1, n3), conv_w, conv_b.reshape(1, n3))


def _hyena_filter_kernel(z_ref, w1_ref, b1_ref, w2_ref, b2_ref, f0_ref, f1_ref, w3_ref, dl_ref, k_ref, ss_ref,
                         *, tp, seq_len):
    p = pl.program_id(1)
    z = z_ref[...]
    h1 = jnp.sin(f0_ref[...] * (jnp.dot(z, w1_ref[...], precision=HIGHEST, preferred_element_type=F32) + b1_ref[...]))
    h2 = jnp.sin(f1_ref[...] * (jnp.dot(h1, w2_ref[...], precision=HIGHEST, preferred_element_type=F32) + b2_ref[...]))
    filt = jnp.dot(h2.astype(BF16), w3_ref[...].astype(BF16), preferred_element_type=F32)
    t = z[:, 0:1]
    kk = filt * (jnp.exp(-t * dl_ref[...]) + MOD_SHIFT)
    circ = p * tp + lax.broadcasted_iota(jnp.int32, (tp, 1), 0)
    kk = jnp.where(circ == seq_len, 0.0, kk)
    k_ref[...] = kk.astype(k_ref.dtype)

    @pl.when(p == 0)
    def _():
        ss_ref[...] = jnp.zeros(ss_ref.shape, F32)

    ss_ref[...] += jnp.sum(kk * kk, axis=0, keepdims=True)


def _hyena_filters(seq_len, d, f_w1, f_b1, f_w2, f_b2, f_w3, f_freq, tp=256, tc=1024):
    f32 = F32
    hid = f_w1.shape[1]
    od = HYENA_ORDER * d
    t = jnp.linspace(0.0, 1.0, seq_len, dtype=f32)[:, None]
    bands = (FILTER_EMB - 1) // 2
    w = 2.0 * math.pi * jnp.arange(seq_len, dtype=f32)[:, None] / seq_len
    f = jnp.linspace(1e-4, bands - 1, bands, dtype=f32)[None, :]
    z = jnp.concatenate([t, jnp.cos(f * w), -jnp.sin(f * w)], axis=-1)
    circ = jnp.arange(2 * seq_len)
    offs = jnp.where(circ < seq_len, circ, jnp.minimum(2 * seq_len - circ, seq_len - 1))
    z2 = jnp.pad(z[offs], ((0, 0), (0, LANES - FILTER_EMB)))
    w1p = jnp.pad(f_w1, ((0, LANES - FILTER_EMB), (0, 0)))
    w3s = f_w3.reshape(hid, HYENA_ORDER, 2, d).transpose(2, 0, 1, 3).reshape(2, hid, od)
    deltas = jnp.tile(jnp.abs(jnp.linspace(MIN_DECAY, MAX_DECAY, d, dtype=f32)), HYENA_ORDER)[None, :]
    tp = min(tp, seq_len)
    tc = min(tc, od)
    side_tiles = seq_len // tp
    small = lambda shape: pl.BlockSpec(shape, lambda j, p: (0,) * len(shape))
    return pl.pallas_call(
        functools.partial(_hyena_filter_kernel, tp=tp, seq_len=seq_len),
        grid=(od // tc, 2 * side_tiles),
        in_specs=[pl.BlockSpec((tp, LANES), lambda j, p: (p, 0)),
                  small((LANES, hid)), small((1, hid)), small((hid, hid)), small((1, hid)), small((1, hid)), small((1, hid)),
                  pl.BlockSpec((None, hid, tc), lambda j, p: (p // side_tiles, 0, j)),
                  pl.BlockSpec((1, tc), lambda j, p: (0, j))],
        out_specs=[pl.BlockSpec((tp, tc), lambda j, p: (p, j)),
                   pl.BlockSpec((1, tc), lambda j, p: (0, j))],
        out_shape=[jax.ShapeDtypeStruct((2 * seq_len, od), BF16), jax.ShapeDtypeStruct((1, od), F32)],
        compiler_params=_cparams("arbitrary", "arbitrary"),
        name="hyena_filters",
    )(z2, w1p, f_b1.reshape(1, hid), f_w2, f_b2.reshape(1, hid), f_freq[0:1], f_freq[1:2], w3s, deltas)


def _stage2_tables(n, n1):
    n2 = n // n1
    r, c = _iota2(2 * n2, 2 * n2)
    kk = jnp.arange(n1, dtype=jnp.int32)[:, None, None] + n1 * (r % n2)[None]
    gc, gs = _cis(kk * (c % n2)[None], n)
    same = ((r < n2) == (c < n2))[None]
    fwd = jnp.where(same, gc, jnp.where((r < n2)[None], gs, -gs))
    return fwd.astype(BF16), fwd.swapaxes(1, 2).astype(BF16)


def _complex_mul(x, kf, n2):
    xr, xi = x[:n2], x[n2:]
    kr, ki = kf[:n2].astype(F32), kf[n2:].astype(F32)
    return jnp.concatenate([xr * kr - xi * ki, xr * ki + xi * kr], axis=0)


def _kf_epilogue(acc, extras, o_ref):
    ss_ref, = extras
    o_ref[...] = (acc * ss_ref[...]).astype(o_ref.dtype)


def _hyena_kf(k2u, sumsq, n1):
    n, od = k2u.shape
    n2 = n // n1
    colscale = lax.rsqrt(sumsq + EPS) / n
    r, c = _iota2(2 * n1, n1)
    fr, fs = _cis((r // 2) * c, n1)
    w1 = jnp.where(r % 2 == 0, fr, -fs).astype(BF16)
    cols = n2 * od
    t1 = min(cols, 4096)
    a = _left_mm(w1, k2u.reshape(n1, cols),
                 grid=(cols // t1,),
                 w_spec=pl.BlockSpec((2 * n1, n1), lambda j: (0, 0)),
                 x_spec=pl.BlockSpec((n1, t1), lambda j: (0, j)),
                 out_shape=jax.ShapeDtypeStruct((2 * n1, cols), BF16),
                 out_spec=pl.BlockSpec((2 * n1, t1), lambda j: (0, j)),
                 name="hyena_kf_stage1")
    g_fwd, _ = _stage2_tables(n, n1)
    tc = min(od, 1024)
    return _left_mm(g_fwd, a.reshape(n1, 2, n2, od),
                    grid=(n1, od // tc),
                    w_spec=pl.BlockSpec((None, 2 * n2, 2 * n2), lambda k, j: (k, 0, 0)),
                    x_spec=pl.BlockSpec((None, 2, n2, tc), lambda k, j: (k, 0, 0, j)),
                    out_shape=jax.ShapeDtypeStruct((n1, 2 * n2, od), BF16),
                    out_spec=pl.BlockSpec((None, 2 * n2, tc), lambda k, j: (k, 0, j)),
                    extras=[(colscale, pl.BlockSpec((1, tc), lambda k, j: (0, j)))],
                    epilogue=_kf_epilogue, name="hyena_kf_stage2")


def _conv_mid_kernel(gf_ref, gi_ref, a_ref, kf_ref, o_ref):
    n2 = a_ref.shape[1]
    a = a_ref[...].reshape(2 * n2, a_ref.shape[2])
    x = jnp.dot(gf_ref[...], a, preferred_element_type=F32)
    y = _complex_mul(x, kf_ref[...], n2)
    o_ref[...] = jnp.dot(gi_ref[...], y.astype(BF16), preferred_element_type=F32).astype(o_ref.dtype)


def _gate_epilogue(acc, extras, o_ref):
    xg_ref, u_ref, skip_ref = extras
    u = u_ref[...].astype(F32)
    o_ref[...] = (xg_ref[...].astype(F32) * (acc + skip_ref[...] * u)).astype(o_ref.dtype)


def _hyena_long_conv(u, xg, kf, kf_col_off, skip, seq_len, tables):
    n_rows, d = u.shape
    assert n_rows == 2 * seq_len
    n = 2 * seq_len
    n2 = DFT_N2
    n1 = n // n2
    hn = n1 // 2
    g_fwd, g_inv = tables
    cols = n2 * d
    m = max(1, min(4096 // d, n2))
    t1 = m * d
    r, c = _iota2(2 * n1, 2 * hn)
    fr, fs = _cis((r // 2) * (c % hn), n1)
    ro, ri = r % 2, c // hn
    w1 = jnp.where(ro == ri, fr, jnp.where(ro == 0, fs, -fs)).astype(BF16)
    a = _left_mm(w1, u.reshape(2 * hn, cols),
                 grid=(cols // t1,),
                 w_spec=pl.BlockSpec((2 * n1, 2 * hn), lambda j: (0, 0)),
                 x_spec=pl.BlockSpec((2 * hn, t1), lambda j: (0, j)),
                 out_shape=jax.ShapeDtypeStruct((2 * n1, cols), BF16),
                 out_spec=pl.BlockSpec((2 * n1, t1), lambda j: (0, j)),
                 name="hyena_conv_stage1")
    tc = min(d, 1024)
    nd = d // tc
    bmid = pl.pallas_call(
        _conv_mid_kernel,
        grid=(n1, nd),
        in_specs=[pl.BlockSpec((None, 2 * n2, 2 * n2), lambda k, j: (k, 0, 0)),
                  pl.BlockSpec((None, 2 * n2, 2 * n2), lambda k, j: (k, 0, 0)),
                  pl.BlockSpec((None, 2, n2, tc), lambda k, j: (k, 0, 0, j)),
                  pl.BlockSpec((None, 2 * n2, tc), lambda k, j: (k, 0, kf_col_off * nd + j))],
        out_specs=pl.BlockSpec((None, 2 * n2, tc), lambda k, j: (k, 0, j)),
        out_shape=jax.ShapeDtypeStruct((n1, 2 * n2, d), BF16),
        compiler_params=_cparams("arbitrary", "arbitrary"),
        name="hyena_conv_mid",
    )(g_fwd, g_inv, a.reshape(n1, 2, n2, d), kf)
    r, c = _iota2(2 * hn, 2 * n1)
    ec, es = _cis((r % hn) * (c // 2), n1)
    ro, ri = r // hn, c % 2
    w3 = jnp.where(ro == ri, ec, jnp.where(ro == 0, -es, es)).astype(BF16)
    blk = pl.BlockSpec((2 * hn, t1), lambda j: (0, j))
    return _left_mm(w3, bmid.reshape(2 * n1, cols),
                    grid=(cols // t1,),
                    w_spec=pl.BlockSpec((2 * hn, 2 * n1), lambda j: (0, 0)),
                    x_spec=pl.BlockSpec((2 * n1, t1), lambda j: (0, j)),
                    out_shape=jax.ShapeDtypeStruct((2 * hn, cols), BF16),
                    out_spec=blk,
                    extras=[(xg.reshape(2 * hn, cols), blk), (u.reshape(2 * hn, cols), blk),
                            (jnp.tile(skip.reshape(1, d), (1, m)), pl.BlockSpec((1, t1), lambda j: (0, 0)))],
                    epilogue=_gate_epilogue, name="hyena_conv_stage3").reshape(n_rows, d)


def _conv_dense_kernel(fk_ref, fz_ref, fzt_ref, k_ref, ss_ref, u_ref, xg_ref, skip_ref, o_ref):
    n = k_ref.shape[0]
    kf = jnp.dot(fk_ref[...], k_ref[...], preferred_element_type=F32) * ss_ref[...]
    u = u_ref[...]
    z = jnp.dot(fz_ref[...], u, preferred_element_type=F32)
    y = _complex_mul(z, kf, n)
    conv = jnp.dot(fzt_ref[...], y.astype(BF16), preferred_element_type=F32)
    o_ref[...] = (xg_ref[...].astype(F32) * (conv + skip_ref[...] * u.astype(F32))).astype(o_ref.dtype)


def _hyena_long_conv_dense(u, xg, k2u, sumsq, order, skip, seq_len):
    n_rows, d = u.shape
    assert n_rows == 2 * seq_len
    n = 2 * seq_len
    colscale = lax.rsqrt(sumsq + EPS) / n
    r, c = _iota2(2 * n, n)
    kc, ks = _cis((r % n) * c, n)
    fk = jnp.where(r < n, kc, -ks).astype(BF16)
    r, c = _iota2(2 * n, 2 * seq_len)
    zc, zs = _cis((r % n) * (c % seq_len), n)
    fz = jnp.where((r < n) == (c < seq_len), zc, jnp.where(r < n, zs, -zs))
    tc = min(d, 512)
    nd = d // tc
    full = lambda a: pl.BlockSpec(a.shape, lambda j: (0, 0))
    fz_b, fzt_b = fz.astype(BF16), fz.T.astype(BF16)
    blk = pl.BlockSpec((n_rows, tc), lambda j: (0, j))
    return pl.pallas_call(
        _conv_dense_kernel,
        grid=(nd,),
        in_specs=[full(fk), full(fz_b), full(fzt_b),
                  pl.BlockSpec((n, tc), lambda j: (0, order * nd + j)),
                  pl.BlockSpec((1, tc), lambda j: (0, order * nd + j)),
                  blk, blk, pl.BlockSpec((1, tc), lambda j: (0, j))],
        out_specs=blk,
        out_shape=jax.ShapeDtypeStruct((n_rows, d), BF16),
        compiler_params=_cparams("arbitrary"),
        name="hyena_conv_dense",
    )(fk, fz_b, fzt_b, k2u, colscale, u, xg, skip.reshape(1, d))


def _hyena_stream(h, rows, row_off, n_rows, seq_len, prm, dense):
    (w_in, b_in, conv_w, conv_b, f_w1, f_b1, f_w2, f_b2, f_w3, f_freq, skip) = prm
    d = w_in.shape[0]
    assert n_rows == 2 * seq_len, "the complex pairing of batch rows needs exactly two sequences"
    zc = _hyena_in(h, w_in, b_in, conv_w, conv_b, rows, row_off, n_rows, seq_len)
    x1, x2, v = zc[0], zc[1], zc[2]
    k2u, sumsq = _hyena_filters(seq_len, d, f_w1, f_b1, f_w2, f_b2, f_w3, f_freq)
    if dense:
        v = _hyena_long_conv_dense(v, x1, k2u, sumsq, 0, skip[0], seq_len)
        return _hyena_long_conv_dense(v, x2, k2u, sumsq, 1, skip[1], seq_len)
    n = 2 * seq_len
    n1 = n // DFT_N2
    kf = _hyena_kf(k2u, sumsq, n1)
    tables = _stage2_tables(n, n1)
    v = _hyena_long_conv(v, x1, kf, 0, skip[0], seq_len, tables)
    return _hyena_long_conv(v, x2, kf, 1, skip[1], seq_len, tables)


def _layer_hyena(x_all, mod, rows, g, prm, w_out, b_out):
    h = _norm_mod(x_all, g, mod, 0, rows, BF16)
    vc = _hyena_stream(h, rows, 0, rows.n_ctx, rows.ctx_len, prm, dense=True)
    x_all = _proj_residual(vc, w_out, b_out, x_all, mod, 2, rows, 0, name="hyena_out_ctx")
    vl = _hyena_stream(h, rows, rows.ctx_blocks, rows.n_lat, rows.seq, prm, dense=False)
    return _proj_residual(vl, w_out, b_out, x_all, mod, 2, rows, rows.ctx_blocks, name="hyena_out_lat")


def _layer_pool(x_all, mod, rows, g, w_grp, scale):
    h = _norm_mod(x_all, g, mod, 0, rows, F32)
    x_all = _pool_mixer(h, w_grp, scale, x_all, mod, 2, rows, 0, rows.n_ctx, rows.ctx_len)
    return _pool_mixer(h, w_grp, scale, x_all, mod, 2, rows, rows.ctx_blocks, rows.n_lat, rows.seq)


def _final_norm_kernel(x_ref, g_ref, o_ref):
    x = x_ref[...]
    o_ref[...] = x * lax.rsqrt(jnp.mean(x * x, axis=-1, keepdims=True) + EPS) * g_ref[...]


def _final_norm(x_all, g, rows):
    tm = rows.tile
    d = x_all.shape[1]
    return pl.pallas_call(
        _final_norm_kernel,
        grid=(rows.n_lat // tm,),
        in_specs=[pl.BlockSpec((tm, d), lambda i: (i + rows.ctx_blocks, 0)),
                  pl.BlockSpec((1, d), lambda i: (0, 0))],
        out_specs=pl.BlockSpec((tm, d), lambda i: (i, 0)),
        out_shape=jax.ShapeDtypeStruct((rows.n_lat, d), F32),
        compiler_params=_cparams("arbitrary"),
        name="final_norm",
    )(x_all, g.reshape(1, d))


def kernel(x, c, ctx, c_ctx, ada_w, ada_b, norm_g, final_g, hy_w_in, hy_b_in, hy_conv_w, hy_conv_b, hy_f_w1, hy_f_b1, hy_f_w2, hy_f_b2, hy_f_w3, hy_f_freq, hy_skip, hy_w_out, hy_b_out, fn_w_out, fn_b_out, pl_w, pl_scale, mla_w_dq, mla_g_q, mla_w_uq, mla_w_dkv, mla_g_kv, mla_w_ukv, mla_w_o, moe_w_group, moe_b_group, moe_w_expert, moe_b_expert, moe_w1, moe_w3, moe_w2):
    batch, seq, d = x.shape
    ctx_len = ctx.shape[1]
    depth = ada_w.shape[0]
    rows = _Rows(batch, ctx_len, seq, ROW_TILE)
    mod = _ada_mod(c, c_ctx, ada_w, ada_b)
    x_all = jnp.concatenate([ctx.reshape(-1, d), x.reshape(-1, d)], axis=0)
    for i in range(depth):
        kind, j = i % N_MIXERS, i // N_MIXERS
        m = mod[i]
        g1 = norm_g[i, 0]
        if kind == 0:
            prm = (hy_w_in[j], hy_b_in[j], hy_conv_w[j], hy_conv_b[j], hy_f_w1[j], hy_f_b1[j], hy_f_w2[j],
                   hy_f_b2[j], hy_f_w3[j], hy_f_freq[j], hy_skip[j])
            x_all = _layer_hyena(x_all, m, rows, g1, prm, hy_w_out[j], hy_b_out[j])
        elif kind == 1:
            x_all = _layer_fnet(x_all, m, rows, g1, fn_w_out[j], fn_b_out[j])
        elif kind == 2:
            x_all = _layer_pool(x_all, m, rows, g1, pl_w[j], pl_scale[j])
        else:
            x_all = _layer_mla(x_all, m, rows, g1, mla_w_dq[j], mla_g_q[j], mla_w_uq[j], mla_w_dkv[j],
                               mla_g_kv[j], mla_w_ukv[j], mla_w_o[j], update_ctx=i < depth - 1)
        h, eid, wts = _norm_router(x_all, norm_g[i, 1], m, 3, rows, moe_w_group[i], moe_b_group[i],
                                   moe_w_expert[i], moe_b_expert[i])
        y = _moe_experts(h, eid, wts, moe_w1, moe_w3, moe_w2, i)
        x_all = _moe_combine(x_all, y, m, 5, rows)
    return _final_norm(x_all, final_g, rows).reshape(batch, seq, d)
```

```python
import functools
import math

import numpy as np
import jax
import jax.numpy as jnp
from jax import lax
from jax.experimental import pallas as pl
from jax.experimental.pallas import tpu as pltpu

F32 = jnp.float32
BF16 = jnp.bfloat16
HIGHEST = lax.Precision.HIGHEST

EPS = 1e-6
LANES = 128
MOD_ROWS = 8
VMEM_LIMIT = 56 * 1024 * 1024

N_MIXERS = 4
HYENA_ORDER = 2
FILTER_EMB = 33
DECAY_TARGET = 1e-2
FAST_DECAY_PCT = 0.3
SLOW_DECAY_PCT = 1.5
MOD_SHIFT = 0.0
FNET_GROUPS = 4
POOL_WINDOWS = (2, 4, 8, 16)
MLA_HEADS = 16
QK_NOPE = 128
QK_ROPE = 64
V_DIM = 128
GRID_W = 64
ROPE_THETA = 10000.0
N_GROUPS = 4
EXPERTS_PER_GROUP = 8
TOPK_EXPERT = 2
ROW_TILE = 256
PROJ_ROWS = 512
MOE_ROWS = 256


def _cparams(*sem):
    return pltpu.CompilerParams(dimension_semantics=sem, vmem_limit_bytes=VMEM_LIMIT)


def _ada_kernel(st_ref, w_ref, b_ref, o_ref, *, nrows):
    s = st_ref[...]
    s = s * jax.nn.sigmoid(s)
    w = w_ref[...]
    o_ref[...] = jnp.broadcast_to(b_ref[...], o_ref.shape)
    for r in range(nrows):
        o_ref[r:r + 1, :] = jnp.sum(s[:, r:r + 1] * w, axis=0, keepdims=True) + b_ref[...]


def _ada_mod(c, c_ctx, ada_w, ada_b):
    depth, d, n = ada_w.shape
    nrows = c.shape[0] + 1
    st = jnp.zeros((d, MOD_ROWS), F32).at[:, :nrows - 1].set(c.T).at[:, nrows - 1].set(c_ctx)
    tn = 1024 if n % 1024 == 0 else n
    return pl.pallas_call(
        functools.partial(_ada_kernel, nrows=nrows),
        grid=(depth, n // tn),
        in_specs=[pl.BlockSpec((d, MOD_ROWS), lambda l, j: (0, 0)),
                  pl.BlockSpec((None, d, tn), lambda l, j: (l, 0, j)),
                  pl.BlockSpec((None, 1, tn), lambda l, j: (l, 0, j))],
        out_specs=pl.BlockSpec((None, MOD_ROWS, tn), lambda l, j: (l, 0, j)),
        out_shape=jax.ShapeDtypeStruct((depth, MOD_ROWS, n), F32),
        compiler_params=_cparams("arbitrary", "arbitrary"),
        name="ada_mod",
    )(st, ada_w, ada_b.reshape(depth, 1, n))


class _Rows:
    def __init__(self, batch, ctx_len, seq, tile):
        assert seq % tile == 0
        self.batch, self.ctx_len, self.seq, self.tile = batch, ctx_len, seq, tile
        self.n_ctx = batch * ctx_len
        self.n_lat = batch * seq
        self.n_all = self.n_ctx + self.n_lat
        self.lat_blocks = seq // tile
        self.lat_off = 0
        self.ctx_off = self.n_lat // tile

    def mod_row(self, i):
        return jnp.where(i >= self.ctx_off, self.batch, i // self.lat_blocks)


def _pack_rows(x, ctx, rows):
    d = x.shape[-1]
    pad = -rows.n_all % SPLIT_ROWS
    return jnp.concatenate([x.reshape(-1, d), ctx.reshape(-1, d), jnp.zeros((pad, d), x.dtype)], axis=0)


def _rms_mod(x, g, sh, sc):
    y = x * lax.rsqrt(jnp.mean(x * x, axis=-1, keepdims=True) + EPS) * g
    return y * (1.0 + sc) + sh


def _norm_mod_kernel(x_ref, g_ref, sh_ref, sc_ref, o_ref):
    o_ref[...] = _rms_mod(x_ref[...], g_ref[...], sh_ref[...], sc_ref[...]).astype(o_ref.dtype)


def _norm_mod(x_all, g, mod, chunk, rows, out_dtype):
    n, d = rows.n_all, x_all.shape[1]
    tm = rows.tile
    mod3 = mod.reshape(MOD_ROWS, 1, -1)
    return pl.pallas_call(
        _norm_mod_kernel,
        grid=(n // tm,),
        in_specs=[pl.BlockSpec((tm, d), lambda i: (i, 0)),
                  pl.BlockSpec((1, d), lambda i: (0, 0)),
                  pl.BlockSpec((None, 1, d), lambda i: (rows.mod_row(i), 0, chunk)),
                  pl.BlockSpec((None, 1, d), lambda i: (rows.mod_row(i), 0, chunk + 1))],
        out_specs=pl.BlockSpec((tm, d), lambda i: (i, 0)),
        out_shape=jax.ShapeDtypeStruct((n, d), out_dtype),
        compiler_params=_cparams("arbitrary"),
        name="norm_mod",
    )(x_all, g.reshape(1, d), mod3, mod3)


def _norm_router_kernel(x_ref, g_ref, sh_ref, sc_ref, wr_ref, br_ref, h_ref, eid_ref, wt_ref,
                        *, n_groups, per_group):
    h = _rms_mod(x_ref[...], g_ref[...], sh_ref[...], sc_ref[...])
    h_ref[...] = h.astype(h_ref.dtype)
    logits = jnp.dot(h, wr_ref[...], precision=HIGHEST, preferred_element_type=F32) + br_ref[...]
    lane = lax.broadcasted_iota(jnp.int32, logits.shape, 1).astype(F32)
    neg = -jnp.inf
    gl = jnp.where(lane < n_groups, logits, neg)
    gmax = jnp.max(gl, axis=-1, keepdims=True)
    p_grp = 1.0 / jnp.sum(jnp.exp(gl - gmax), axis=-1, keepdims=True)
    g_idx = jnp.min(jnp.where(gl == gmax, lane, float(LANES)), axis=-1, keepdims=True)
    lo = n_groups + g_idx * per_group
    el = jnp.where((lane >= lo) & (lane < lo + per_group), logits, neg)
    e1 = jnp.max(el, axis=-1, keepdims=True)
    i1 = jnp.min(jnp.where(el == e1, lane, float(LANES)), axis=-1, keepdims=True)
    el2 = jnp.where(lane == i1, neg, el)
    e2 = jnp.max(el2, axis=-1, keepdims=True)
    i2 = jnp.min(jnp.where(el2 == e2, lane, float(LANES)), axis=-1, keepdims=True)
    r = jnp.exp(e2 - e1)
    w1 = p_grp / (1.0 + r)
    w2 = p_grp * r / (1.0 + r)
    eid = jnp.where(lane == 0, i1 - n_groups, jnp.where(lane == 1, i2 - n_groups, 0.0))
    eid_ref[...] = eid.astype(jnp.int32)
    wt_ref[...] = jnp.where(lane == 0, w1, jnp.where(lane == 1, w2, 0.0))


def _norm_router(x_all, g, mod, chunk, rows, w_group, b_group, w_expert, b_expert):
    n, d = rows.n_all, x_all.shape[1]
    tm = rows.tile
    n_groups = w_group.shape[1]
    n_experts = w_expert.shape[1]
    wr = jnp.zeros((d, LANES), F32).at[:, :n_groups].set(w_group).at[:, n_groups:n_groups + n_experts].set(w_expert)
    br = jnp.zeros((1, LANES), F32).at[0, :n_groups].set(b_group).at[0, n_groups:n_groups + n_experts].set(b_expert)
    mod3 = mod.reshape(MOD_ROWS, 1, -1)
    h, eid, wts = pl.pallas_call(
        functools.partial(_norm_router_kernel, n_groups=n_groups, per_group=n_experts // n_groups),
        grid=(n // tm,),
        in_specs=[pl.BlockSpec((tm, d), lambda i: (i, 0)),
                  pl.BlockSpec((1, d), lambda i: (0, 0)),
                  pl.BlockSpec((None, 1, d), lambda i: (rows.mod_row(i), 0, chunk)),
                  pl.BlockSpec((None, 1, d), lambda i: (rows.mod_row(i), 0, chunk + 1)),
                  pl.BlockSpec((d, LANES), lambda i: (0, 0)),
                  pl.BlockSpec((1, LANES), lambda i: (0, 0))],
        out_specs=[pl.BlockSpec((tm, d), lambda i: (i, 0)),
                   pl.BlockSpec((tm, LANES), lambda i: (i, 0)),
                   pl.BlockSpec((tm, LANES), lambda i: (i, 0))],
        out_shape=[jax.ShapeDtypeStruct((n, d), F32),
                   jax.ShapeDtypeStruct((n, LANES), jnp.int32),
                   jax.ShapeDtypeStruct((n, LANES), F32)],
        compiler_params=_cparams("arbitrary"),
        name="norm_router",
    )(x_all, g.reshape(1, d), mod3, mod3, wr, br)
    return h, eid[:, :TOPK_EXPERT], wts[:, :TOPK_EXPERT]


def _moe_dispatch(eid, wts, n_experts, tm):
    n_tok = eid.shape[0]
    n_assign = eid.size
    e_flat = eid.reshape(-1)
    order = jnp.argsort(e_flat).astype(jnp.int32)
    counts = jnp.sum((e_flat[:, None] == jnp.arange(n_experts, dtype=jnp.int32)[None, :]).astype(jnp.int32), axis=0)
    padded = (counts + tm - 1) // tm * tm
    start = jnp.cumsum(counts) - counts
    pend = jnp.cumsum(padded)
    pstart = pend - padded
    n_blocks = (n_assign + n_experts * (tm - 1) + tm - 1) // tm
    blk_start = jnp.arange(n_blocks, dtype=jnp.int32) * tm
    blk_e = jnp.minimum(jnp.sum((pend[None, :] <= blk_start[:, None]).astype(jnp.int32), axis=1), n_experts - 1)
    blk_rows = jnp.clip(pstart[blk_e] + counts[blk_e] - blk_start, 0, tm).astype(jnp.int32)
    row = lax.broadcasted_iota(jnp.int32, (n_blocks, tm), 1)
    slot_e = blk_e[:, None]
    src = start[slot_e] + (blk_start[:, None] + row - pstart[slot_e])
    valid = row < blk_rows[:, None]
    asg = order[jnp.clip(src, 0, n_assign - 1)]
    tok = asg // TOPK_EXPERT
    buf_tok = jnp.where(valid, tok, 0)
    dummy = TOPK_EXPERT * n_tok + (jnp.arange(n_blocks, dtype=jnp.int32)[:, None] % 2) * tm + row
    buf_asg = jnp.where(valid, (asg % TOPK_EXPERT) * n_tok + tok, dummy)
    buf_w = jnp.where(valid, wts.reshape(-1)[asg], 0.0)
    return (buf_tok.reshape(n_blocks, 1, tm), buf_asg.reshape(n_blocks, 1, tm),
            buf_w.reshape(n_blocks, tm, 1), blk_e.astype(jnp.int32), blk_rows)


def _moe_kernel(blk_e_ref, blk_rows_ref, tok_ref, tok_next_ref, asg_ref, roww_ref, h_hbm,
                w1_ref, w3_ref, w2_ref, y_hbm, xbuf, ybuf, w1b, w3b, w2b, gsem, ssem):
    b = pl.program_id(0)
    nb = pl.num_programs(0)
    slot = b % 2
    other = 1 - slot

    tm = xbuf.shape[1]

    def start_gather(ids_ref, s):
        def body(r, c):
            pltpu.make_async_copy(h_hbm.at[pl.ds(ids_ref[0, r], 1)], xbuf.at[s, pl.ds(r, 1)], gsem.at[s]).start()
            return c
        lax.fori_loop(0, tm, body, 0, unroll=8)

    def wait_gather(s):
        pltpu.make_async_copy(h_hbm.at[pl.ds(0, tm)], xbuf.at[s], gsem.at[s]).wait()

    def start_scatter(s):
        def body(r, c):
            pltpu.make_async_copy(ybuf.at[s, pl.ds(r, 1)], y_hbm.at[pl.ds(asg_ref[0, r], 1)], ssem.at[s]).start()
            return c
        lax.fori_loop(0, tm, body, 0, unroll=8)

    def wait_scatter(s):
        pltpu.make_async_copy(ybuf.at[s], y_hbm.at[pl.ds(0, tm)], ssem.at[s]).wait()

    def used(blk):
        return blk_rows_ref[jnp.clip(blk, 0, nb - 1)] > 0

    @pl.when(b == 0)
    def _():
        ybuf[...] = jnp.zeros(ybuf.shape, ybuf.dtype)
        n_real = y_hbm.shape[0] - 2 * tm
        for s in range(2):
            fill = pltpu.make_async_copy(ybuf.at[s], y_hbm.at[pl.ds(n_real + s * tm, tm)], ssem.at[s])
            fill.start()
            fill.wait()

    @pl.when((b == 0) & used(b))
    def _():
        start_gather(tok_ref, slot)

    @pl.when((b + 1 < nb) & used(b + 1))
    def _():
        start_gather(tok_next_ref, other)

    @pl.when((b >= 2) & used(b - 2))
    def _():
        wait_scatter(slot)

    @pl.when(used(b))
    def _():
        wait_gather(slot)

        @pl.when((b == 0) | (blk_e_ref[b] != blk_e_ref[jnp.maximum(b - 1, 0)]))
        def _():
            w1b[...] = w1_ref[...].astype(BF16)
            w3b[...] = w3_ref[...].astype(BF16)
            w2b[...] = w2_ref[...].astype(BF16)

        x = xbuf[slot].astype(BF16)
        a = jnp.dot(x, w1b[...], preferred_element_type=F32)
        g = jnp.dot(x, w3b[...], preferred_element_type=F32)
        hm = (a * jax.nn.sigmoid(a) * g).astype(BF16)
        ybuf[slot] = jnp.dot(hm, w2b[...], preferred_element_type=F32) * roww_ref[...]
        start_scatter(slot)

    @pl.when(b == nb - 1)
    def _():
        @pl.when((nb >= 2) & used(b - 1))
        def _():
            wait_scatter(other)

        @pl.when(used(b))
        def _():
            wait_scatter(slot)


def _moe_experts(h, eid, wts, w1, w3, w2, layer):
    n, d = h.shape
    n_experts, de = w1.shape[1], w1.shape[3]
    tm = MOE_ROWS
    tok, asg, roww, blk_e, blk_rows = _moe_dispatch(eid, wts, n_experts, tm)
    n_blocks = tok.shape[0]
    smem_blk = lambda f: pl.BlockSpec((None, 1, tm), f, memory_space=pltpu.SMEM)
    grid_spec = pltpu.PrefetchScalarGridSpec(
        num_scalar_prefetch=2,
        grid=(n_blocks,),
        in_specs=[smem_blk(lambda b, be, br: (b, 0, 0)),
                  smem_blk(lambda b, be, br: (jnp.minimum(b + 1, n_blocks - 1), 0, 0)),
                  smem_blk(lambda b, be, br: (b, 0, 0)),
                  pl.BlockSpec((None, tm, 1), lambda b, be, br: (b, 0, 0)),
                  pl.BlockSpec(memory_space=pl.ANY),
                  pl.BlockSpec((None, None, d, de), lambda b, be, br: (layer, be[b], 0, 0)),
                  pl.BlockSpec((None, None, d, de), lambda b, be, br: (layer, be[b], 0, 0)),
                  pl.BlockSpec((None, None, de, d), lambda b, be, br: (layer, be[b], 0, 0))],
        out_specs=pl.BlockSpec(memory_space=pl.ANY),
        scratch_shapes=[pltpu.VMEM((2, tm, d), F32), pltpu.VMEM((2, tm, d), F32),
                        pltpu.VMEM((d, de), BF16), pltpu.VMEM((d, de), BF16), pltpu.VMEM((de, d), BF16),
                        pltpu.SemaphoreType.DMA((2,)), pltpu.SemaphoreType.DMA((2,))],
    )
    return pl.pallas_call(
        _moe_kernel,
        grid_spec=grid_spec,
        out_shape=jax.ShapeDtypeStruct((n * TOPK_EXPERT + 2 * tm, d), F32),
        compiler_params=_cparams("arbitrary"),
        name="moe_experts",
    )(blk_e, blk_rows, tok, tok, asg, roww, h, w1, w3, w2)


def _moe_combine_kernel(x_ref, *rest):
    y_refs, gate_ref, o_ref = rest[:TOPK_EXPERT], rest[TOPK_EXPERT], rest[TOPK_EXPERT + 1]
    acc = y_refs[0][...]
    for y_ref in y_refs[1:]:
        acc = acc + y_ref[...]
    o_ref[...] = x_ref[...] + gate_ref[...] * acc


def _moe_combine(x_all, y, mod, chunk, rows):
    n, d = rows.n_all, x_all.shape[1]
    tm = rows.tile
    mod3 = mod.reshape(MOD_ROWS, 1, -1)
    nblk = n // tm
    y_specs = [pl.BlockSpec((tm, d), functools.partial(lambda i, k: (i + k * nblk, 0), k=k)) for k in range(TOPK_EXPERT)]
    return pl.pallas_call(
        _moe_combine_kernel,
        grid=(nblk,),
        in_specs=[pl.BlockSpec((tm, d), lambda i: (i, 0))] + y_specs +
                 [pl.BlockSpec((None, 1, d), lambda i: (rows.mod_row(i), 0, chunk))],
        out_specs=pl.BlockSpec((tm, d), lambda i: (i, 0)),
        out_shape=jax.ShapeDtypeStruct(x_all.shape, F32),
        input_output_aliases={0: 0},
        compiler_params=_cparams("arbitrary"),
        name="moe_combine",
    )(x_all, *([y] * TOPK_EXPERT), mod3)


def _mm_kernel(*refs, has_bias, n_extra, epilogue):
    x_ref, w_ref = refs[0], refs[1]
    pos = 2
    b_ref = None
    if has_bias:
        b_ref = refs[pos]
        pos += 1
    extras = refs[pos:pos + n_extra]
    outs = refs[pos + n_extra:-1]
    wbf = refs[-1]

    @pl.when(pl.program_id(1) == 0)
    def _():
        wbf[...] = w_ref[...].astype(BF16)

    acc = jnp.dot(x_ref[...].astype(BF16), wbf[...], preferred_element_type=F32)
    if has_bias:
        acc = acc + b_ref[...]
    epilogue(acc, extras, outs)


def _store_epilogue(acc, extras, outs):
    outs[0][...] = acc.astype(outs[0].dtype)


def _matmul(x, w, *, w_lead=(), bias=None, tm, tn, m_rows=None, x_row_off=0, extras=(), outs,
            epilogue=_store_epilogue, aliases=None, name="matmul"):
    k = x.shape[1]
    n = w.shape[-1]
    m_rows = x.shape[0] if m_rows is None else m_rows
    assert m_rows % tm == 0 and n % tn == 0 and w.shape[-2] == k
    lead = tuple(w_lead)
    in_specs = [pl.BlockSpec((tm, k), lambda j, i: (i + x_row_off, 0)),
                pl.BlockSpec((None,) * len(lead) + (k, tn), lambda j, i: lead + (0, j))]
    args = [x, w]
    if bias is not None:
        in_specs.append(pl.BlockSpec((1, tn), lambda j, i: (0, j)))
        args.append(bias.reshape(1, n))
    for arr, spec in extras:
        in_specs.append(spec)
        args.append(arr)
    return pl.pallas_call(
        functools.partial(_mm_kernel, has_bias=bias is not None, n_extra=len(extras), epilogue=epilogue),
        grid=(n // tn, m_rows // tm),
        in_specs=in_specs,
        out_specs=[spec for _, spec in outs],
        out_shape=[shape for shape, _ in outs],
        scratch_shapes=[pltpu.VMEM((k, tn), BF16)],
        input_output_aliases=aliases or {},
        compiler_params=_cparams("arbitrary", "arbitrary"),
        name=name,
    )(*args)


def _residual_epilogue(acc, extras, outs):
    res_ref, gate_ref = extras
    outs[0][...] = res_ref[...] + gate_ref[...] * acc


def _proj_residual(xin, w, bias, x_all, mod, chunk, rows, row_off, *, w_lead=(), tn=512, name="proj_residual"):
    tm = rows.tile
    d = x_all.shape[1]
    tn = min(tn, d)
    mod3 = mod.reshape(MOD_ROWS, 1, -1)
    nd = d // tn
    res_spec = pl.BlockSpec((tm, tn), lambda j, i: (i + row_off, j))
    gate_spec = pl.BlockSpec((None, 1, tn), lambda j, i: (rows.mod_row(i + row_off), 0, chunk * nd + j))
    n_extra_before = 2 + (bias is not None)
    out, = _matmul(xin, w, w_lead=w_lead, bias=bias, tm=tm, tn=tn,
                   extras=[(x_all, res_spec), (mod3, gate_spec)],
                   outs=[(jax.ShapeDtypeStruct(x_all.shape, F32), res_spec)],
                   epilogue=_residual_epilogue, aliases={n_extra_before: 0}, name=name)
    return out


POOL_HALO = 8


def _pool_kernel(prev_ref, main_ref, next_ref, w_ref, sc_ref, res_ref, gate_ref, o_ref, ext_ref, wbf_ref,
                 *, tm, seq_len, windows):
    i = pl.program_id(0)
    seq_tiles = seq_len // tm
    t_in_seq = i % seq_tiles

    @pl.when(i == 0)
    def _():
        wbf_ref[...] = w_ref[...].astype(BF16)

    zero_halo = jnp.zeros(prev_ref.shape, F32)
    ext_ref[0:POOL_HALO, :] = jnp.where(t_in_seq == 0, zero_halo, prev_ref[...])
    ext_ref[POOL_HALO:POOL_HALO + tm, :] = main_ref[...]
    ext_ref[POOL_HALO + tm:2 * POOL_HALO + tm, :] = jnp.where(t_in_seq == seq_tiles - 1, zero_halo, next_ref[...])

    pos = t_in_seq * tm + lax.broadcasted_iota(jnp.int32, (tm, 1), 0)
    cg = main_ref.shape[1] // len(windows)
    for gi, win in enumerate(windows):
        half = win // 2
        cols = slice(gi * cg, (gi + 1) * cg)
        s = ext_ref[pl.ds(POOL_HALO - half, tm), cols]
        for dlt in range(-half + 1, half):
            s = s + ext_ref[pl.ds(POOL_HALO + dlt, tm), cols]
        cnt = jnp.minimum(pos + half, seq_len) - jnp.maximum(pos - half, 0)
        pooled = s / cnt.astype(F32) - main_ref[:, cols]
        y = jnp.dot(pooled.astype(BF16), wbf_ref[gi], preferred_element_type=F32) * sc_ref[:, cols]
        o_ref[:, cols] = res_ref[:, cols] + gate_ref[:, cols] * y


def _pool_mixer(h, w_grp, scale, x_all, mod, chunk, rows, row_off, n_rows, seq_len):
    tm = rows.tile
    d = x_all.shape[1]
    assert max(POOL_WINDOWS) // 2 <= POOL_HALO and tm % POOL_HALO == 0 and seq_len % tm == 0
    hb = tm // POOL_HALO
    last_hblk = h.shape[0] // POOL_HALO - 1
    mod3 = mod.reshape(MOD_ROWS, 1, -1)
    main_spec = pl.BlockSpec((tm, d), lambda i: (i + row_off, 0))
    return pl.pallas_call(
        functools.partial(_pool_kernel, tm=tm, seq_len=seq_len, windows=POOL_WINDOWS),
        grid=(n_rows // tm,),
        in_specs=[pl.BlockSpec((POOL_HALO, d), lambda i: (jnp.maximum((i + row_off) * hb - 1, 0), 0)),
                  main_spec,
                  pl.BlockSpec((POOL_HALO, d), lambda i: (jnp.minimum((i + row_off + 1) * hb, last_hblk), 0)),
                  pl.BlockSpec(w_grp.shape, lambda i: (0, 0, 0)),
                  pl.BlockSpec((1, d), lambda i: (0, 0)),
                  main_spec,
                  pl.BlockSpec((None, 1, d), lambda i: (rows.mod_row(i + row_off), 0, chunk))],
        out_specs=main_spec,
        out_shape=jax.ShapeDtypeStruct(x_all.shape, F32),
        scratch_shapes=[pltpu.VMEM((tm + 2 * POOL_HALO, d), F32), pltpu.VMEM(w_grp.shape, BF16)],
        input_output_aliases={5: 0},
        compiler_params=_cparams("arbitrary"),
        name="pool_mixer",
    )(h, h, h, w_grp, scale.reshape(1, d), x_all, mod3)


DFT_N2 = 128


def _cis(num, den):
    ang = (num % den).astype(F32) * (2.0 * math.pi / den)
    return jnp.cos(ang), jnp.sin(ang)


def _iota2(n_rows, n_cols):
    return (lax.broadcasted_iota(jnp.int32, (n_rows, n_cols), 0), lax.broadcasted_iota(jnp.int32, (n_rows, n_cols), 1))


def _slab_pitch(rows):
    return rows + 8


def _slab_scratch(g, rows, width):
    return pltpu.VMEM((width // LANES, g * _slab_pitch(rows), LANES), F32)


def _slab_store(scr, s, val, rows):
    p = _slab_pitch(rows)
    for l in range(scr.shape[0]):
        scr[l, s * p:s * p + rows, :] = val[:, l * LANES:(l + 1) * LANES]


def _slab_load(scr, s, rows):
    p = _slab_pitch(rows)
    return jnp.concatenate([scr[l, s * p:s * p + rows, :] for l in range(scr.shape[0])], axis=1)


def _rows_gather(scr, q, g, rows):
    p = _slab_pitch(rows)
    return jnp.concatenate([scr[l, pl.ds(q, g, stride=p), :] for l in range(scr.shape[0])], axis=1)


def _rows_scatter(scr, q, val, rows):
    p = _slab_pitch(rows)
    for l in range(scr.shape[0]):
        scr[l, pl.ds(q, val.shape[0], stride=p), :] = val[:, l * LANES:(l + 1) * LANES]


def _left_mm_kernel(w_ref, x_ref, *rest, epilogue):
    x = x_ref[...]
    x = x.reshape(-1, x.shape[-1])
    acc = jnp.dot(w_ref[...], x.astype(BF16), preferred_element_type=F32)
    epilogue(acc, rest[:-1], rest[-1])


def _scaled_store(scale):
    def epilogue(acc, extras, o_ref):
        o_ref[...] = (acc * scale).reshape(o_ref.shape).astype(o_ref.dtype)
    return epilogue


def _left_mm(w, x, *, grid, w_spec, x_spec, out_shape, out_spec, extras=(), epilogue=_scaled_store(1.0),
             aliases=None, name="left_mm"):
    return pl.pallas_call(
        functools.partial(_left_mm_kernel, epilogue=epilogue),
        grid=grid,
        in_specs=[w_spec, x_spec] + [s for _, s in extras],
        out_specs=out_spec,
        out_shape=out_shape,
        input_output_aliases=aliases or {},
        compiler_params=_cparams(*(("arbitrary",) * len(grid))),
        name=name,
    )(w, x, *[a for a, _ in extras])


def _fnet_channel_kernel(x_ref, w_ref, o_ref):
    cg = x_ref.shape[1]
    r = jnp.dot(x_ref[...], w_ref[...], preferred_element_type=F32)
    o_ref[0] = r[:, :cg].astype(o_ref.dtype)
    o_ref[1] = r[:, cg:].astype(o_ref.dtype)


def _fnet_channel(h, rows, row_off, n_seq, seq_len):
    tm = rows.tile
    d = h.shape[1]
    cg = d // FNET_GROUPS
    ci, ki = _iota2(cg, cg)
    cc, sc = _cis(ci * ki, cg)
    wc = jnp.concatenate([cc, -sc], axis=1).astype(BF16)
    st = seq_len // tm
    return pl.pallas_call(
        _fnet_channel_kernel,
        grid=(n_seq * st, FNET_GROUPS),
        in_specs=[pl.BlockSpec((tm, cg), lambda i, g: (i + row_off, g)),
                  pl.BlockSpec((cg, 2 * cg), lambda i, g: (0, 0))],
        out_specs=pl.BlockSpec((None, 2, tm, cg), lambda i, g: (i // st, 0, i % st, g)),
        out_shape=jax.ShapeDtypeStruct((n_seq, 2, seq_len, d), BF16),
        compiler_params=_cparams("arbitrary", "arbitrary"),
        name="fnet_channel",
    )(h, wc)


def _fnet_positions(z, n1, tc):
    n_seq, _, seq_len, d = z.shape
    n2 = seq_len // n1
    scale = 1.0 / math.sqrt(seq_len * (d // FNET_GROUPS))
    if n1 > 1:
        r, cidx = _iota2(2 * n1, 2 * n1)
        k1, ro, ri, nn = r // 2, r % 2, cidx // n1, cidx % n1
        fr, fs = _cis(k1 * nn, n1)
        w1 = jnp.where(ro == ri, fr, jnp.where(ro == 0, fs, -fs)).astype(BF16)
        cols = n2 * d
        t1 = min(cols, 4096)
        a = _left_mm(w1, z.reshape(n_seq, 2 * n1, cols),
                     grid=(n_seq, cols // t1),
                     w_spec=pl.BlockSpec((2 * n1, 2 * n1), lambda s, j: (0, 0)),
                     x_spec=pl.BlockSpec((None, 2 * n1, t1), lambda s, j: (s, 0, j)),
                     out_shape=jax.ShapeDtypeStruct((n_seq, 2 * n1, cols), BF16),
                     out_spec=pl.BlockSpec((None, 2 * n1, t1), lambda s, j: (s, 0, j)),
                     name="fnet_stage1")
        a = a.reshape(n_seq, n1, 2, n2, d)
    else:
        a = z.reshape(n_seq, 1, 2, n2, d)
    k2i, ci2 = _iota2(n2, 2 * n2)
    kk = jnp.arange(n1, dtype=jnp.int32)[:, None, None] + n1 * k2i[None]
    gc, gs = _cis(kk * (ci2 % n2)[None], seq_len)
    g2 = jnp.where((ci2 < n2)[None], gc, gs).astype(BF16)
    nd = d // tc
    out = _left_mm(g2, a,
                   grid=(n_seq, n1, nd),
                   w_spec=pl.BlockSpec((None, n2, 2 * n2), lambda s, k, j: (k, 0, 0)),
                   x_spec=pl.BlockSpec((None, None, 2, n2, tc), lambda s, k, j: (s, k, 0, 0, j)),
                   out_shape=jax.ShapeDtypeStruct((n_seq, n2, n1 * d), BF16),
                   out_spec=pl.BlockSpec((None, n2, tc), lambda s, k, j: (s, 0, k * nd + j)),
                   epilogue=_scaled_store(scale), name="fnet_stage2")
    return out.reshape(n_seq * seq_len, d)


FNET_ROWS = 2048
FNET_K1_GROUP = 16


def _fnet_channel_split_kernel(x_ref, w_ref, zr_ref, zi_ref, rg_ref):
    cg = x_ref.shape[1]
    g = x_ref.shape[0] // DFT_N2
    r = jnp.dot(x_ref[...], w_ref[...], preferred_element_type=F32)
    for plane, z_ref in enumerate((zr_ref, zi_ref)):
        for s in range(g):
            _slab_store(rg_ref, s, r[s * DFT_N2:(s + 1) * DFT_N2, plane * cg:(plane + 1) * cg], DFT_N2)

        def body(n2, c):
            z_ref[n2] = _rows_gather(rg_ref, n2, g, DFT_N2).astype(z_ref.dtype)
            return c
        lax.fori_loop(0, DFT_N2, body, 0)


def _fnet_channel_split(h, n_seq, seq_len):
    tm = FNET_ROWS
    d = h.shape[1]
    cg = d // FNET_GROUPS
    ci, ki = _iota2(cg, cg)
    cc, sc = _cis(ci * ki, cg)
    wc = jnp.concatenate([cc, -sc], axis=1).astype(BF16)
    st = seq_len // tm
    g = tm // DFT_N2
    assert seq_len % tm == 0 and g % 16 == 0
    plane = jax.ShapeDtypeStruct((n_seq, DFT_N2, seq_len // DFT_N2, d), BF16)
    out_spec = pl.BlockSpec((None, DFT_N2, g, cg), lambda i, c: (i // st, 0, i % st, c))
    return pl.pallas_call(
        _fnet_channel_split_kernel,
        grid=(n_seq * st, FNET_GROUPS),
        in_specs=[pl.BlockSpec((tm, cg), lambda i, c: (i, c)),
                  pl.BlockSpec((cg, 2 * cg), lambda i, c: (0, 0))],
        out_specs=[out_spec, out_spec],
        out_shape=[plane, plane],
        scratch_shapes=[_slab_scratch(g, DFT_N2, cg)],
        compiler_params=_cparams("arbitrary", "arbitrary"),
        name="fnet_channel",
    )(h, wc)


def _fnet_stage1_kernel(w_ref, zr_ref, zi_ref, o_ref):
    w = w_ref[...]
    for s in range(zr_ref.shape[0]):
        z = jnp.concatenate([zr_ref[s], zi_ref[s]], axis=0)
        o_ref[s] = jnp.dot(w, z, preferred_element_type=F32).astype(o_ref.dtype)


def _fnet_stage2_kernel(g_ref, a_ref, o_ref, rin, rout, *, scale):
    kg = a_ref.shape[1] // 2

    def body_in(q, c):
        _rows_scatter(rin, q, a_ref[q].astype(F32), DFT_N2)
        return c
    lax.fori_loop(0, a_ref.shape[0], body_in, 0)
    for j in range(kg):
        r = jnp.dot(g_ref[j], _stage2_operand(rin, j), preferred_element_type=F32)
        _slab_store(rout, j, r * scale, DFT_N2)

    def body_out(q, c):
        o_ref[q] = _rows_gather(rout, q, kg, DFT_N2).astype(o_ref.dtype)
        return c
    lax.fori_loop(0, o_ref.shape[0], body_out, 0)


def _fnet_positions_split(zr, zi):
    n_seq, n2, n1, d = zr.shape
    seq_len = n1 * n2
    scale = 1.0 / math.sqrt(seq_len * (d // FNET_GROUPS))
    r, cidx = _iota2(2 * n1, 2 * n1)
    k1, ro, ri, nn = r // 2, r % 2, cidx // n1, cidx % n1
    fr, fs = _cis(k1 * nn, n1)
    w1 = jnp.where(ro == ri, fr, jnp.where(ro == 0, fs, -fs)).astype(BF16)
    tc = min(SPLIT_TC, d)
    zblk = pl.BlockSpec((None, N2_GROUP, n1, tc), lambda s, g, j: (s, g, 0, j))
    a = pl.pallas_call(
        _fnet_stage1_kernel,
        grid=(n_seq, n2 // N2_GROUP, d // tc),
        in_specs=[pl.BlockSpec((2 * n1, 2 * n1), lambda s, g, j: (0, 0)), zblk, zblk],
        out_specs=pl.BlockSpec((None, N2_GROUP, 2 * n1, tc), lambda s, g, j: (s, g, 0, j)),
        out_shape=jax.ShapeDtypeStruct((n_seq, n2, 2 * n1, d), BF16),
        compiler_params=_cparams("arbitrary", "arbitrary", "arbitrary"),
        name="fnet_stage1",
    )(w1, zr, zi)
    k2i, ci2 = _iota2(n2, 2 * n2)
    kk = jnp.arange(n1, dtype=jnp.int32)[:, None, None] + n1 * k2i[None]
    gc, gs = _cis(kk * (ci2 % n2)[None], seq_len)
    g2 = jnp.where((ci2 < n2)[None], gc, gs).astype(BF16)
    kg = FNET_K1_GROUP
    assert n1 % kg == 0
    out = pl.pallas_call(
        functools.partial(_fnet_stage2_kernel, scale=scale),
        grid=(n_seq, n1 // kg, d // tc),
        in_specs=[pl.BlockSpec((kg, n2, 2 * n2), lambda s, k, j: (k, 0, 0)),
                  pl.BlockSpec((None, n2, 2 * kg, tc), lambda s, k, j: (s, 0, k, j))],
        out_specs=pl.BlockSpec((None, n2, kg, tc), lambda s, k, j: (s, 0, k, j)),
        out_shape=jax.ShapeDtypeStruct((n_seq, n2, n1, d), BF16),
        scratch_shapes=[_slab_scratch(2 * kg, n2, tc), _slab_scratch(kg, n2, tc)],
        compiler_params=_cparams("arbitrary", "arbitrary", "arbitrary"),
        name="fnet_stage2",
    )(g2, a)
    return out.reshape(n_seq * seq_len, d)


def _layer_fnet(x_all, mod, rows, g, w_out, b_out):
    h = _norm_mod(x_all, g, mod, 0, rows, BF16)
    d = x_all.shape[1]
    zc = _fnet_channel(h, rows, rows.ctx_off, rows.batch, rows.ctx_len)
    fc = _fnet_positions(zc, 1, min(d, 512))
    x_all = _proj_residual(fc, w_out, b_out, x_all, mod, 2, rows, rows.ctx_off, name="fnet_out_ctx")
    zr, zi = _fnet_channel_split(h, rows.batch, rows.seq)
    fl = _fnet_positions_split(zr, zi)
    big = _Rows(rows.batch, rows.ctx_len, rows.seq, PROJ_ROWS)
    return _proj_residual(fl, w_out, b_out, x_all, mod, 2, big, big.lat_off, tn=1024, name="fnet_out_lat")


HEAD_SLAB = 2 * LANES
MLA_SCALE = (QK_NOPE + QK_ROPE) ** -0.5
Q_SCALE = MLA_SCALE * math.log2(math.e)
V_SLAB = V_DIM + 16
ATTN_TQ = 1024
ATTN_TK = 1024


def _rope_tables(seq_len, lead_identity_rows):
    n_rows = seq_len // GRID_W
    row = jnp.repeat(jnp.arange(n_rows, dtype=F32), GRID_W)
    col = jnp.tile(jnp.arange(GRID_W, dtype=F32), n_rows)
    half = QK_ROPE // 2
    inv = ROPE_THETA ** (-jnp.arange(0, half, 2, dtype=F32) / half)
    ang_r = row[:, None] * inv
    ang_c = col[:, None] * inv
    ang = jnp.concatenate([ang_r, ang_r, ang_c, ang_c], axis=-1)
    pad = jnp.zeros((seq_len, LANES - QK_ROPE), F32)
    cos = jnp.concatenate([jnp.cos(ang), pad + 1.0], axis=-1)
    sin = jnp.concatenate([jnp.sin(ang), pad], axis=-1)
    if lead_identity_rows:
        cos = jnp.concatenate([jnp.ones((lead_identity_rows, LANES), F32), cos], axis=0)
        sin = jnp.concatenate([jnp.zeros((lead_identity_rows, LANES), F32), sin], axis=0)
    return cos, sin


def _rope(x, cos, sin):
    q = QK_ROPE // 4
    lane = lax.broadcasted_iota(jnp.int32, x.shape, 1)
    even = (lane // q) % 2 == 0
    rot = jnp.where(even, -pltpu.roll(x, LANES - q, 1), pltpu.roll(x, q, 1))
    return x * cos + rot * sin


def _rmsnorm_epilogue(acc, extras, outs):
    g_ref, = extras
    y = acc * lax.rsqrt(jnp.mean(acc * acc, axis=-1, keepdims=True) + EPS) * g_ref[...]
    outs[0][...] = y.astype(outs[0].dtype)


def _q_up_epilogue(acc, extras, outs):
    cos_ref, sin_ref = extras
    o_ref, = outs
    for hh in range(acc.shape[1] // HEAD_SLAB):
        c0 = hh * HEAD_SLAB
        o_ref[:, c0:c0 + LANES] = (acc[:, c0:c0 + LANES] * Q_SCALE).astype(o_ref.dtype)
        pe = _rope(acc[:, c0 + LANES:c0 + HEAD_SLAB], cos_ref[...], sin_ref[...])
        o_ref[:, c0 + LANES:c0 + HEAD_SLAB] = (pe * Q_SCALE).astype(o_ref.dtype)


def _kv_down_epilogue(acc, extras, outs, *, rank):
    g_ref, cos_ref, sin_ref = extras
    kvn_ref, kpe_ref = outs
    lat = acc[:, :rank]
    y = lat * lax.rsqrt(jnp.mean(lat * lat, axis=-1, keepdims=True) + EPS) * g_ref[...]
    kvn_ref[...] = y.astype(kvn_ref.dtype)
    kpe_ref[...] = _rope(acc[:, rank:rank + LANES], cos_ref[...], sin_ref[...]).astype(kpe_ref.dtype)


def _k_up_epilogue(acc, extras, outs):
    kpe_ref, = extras
    o_ref, = outs
    for hh in range(acc.shape[1] // LANES):
        o_ref[:, hh * HEAD_SLAB:hh * HEAD_SLAB + LANES] = acc[:, hh * LANES:(hh + 1) * LANES].astype(o_ref.dtype)
        o_ref[:, hh * HEAD_SLAB + LANES:(hh + 1) * HEAD_SLAB] = kpe_ref[...]


def _attn_kernel(q_ref, kc_ref, kl_ref, vc_ref, vl_ref, o_ref, *, tk):
    q = q_ref[...]

    def scores(k_tile):
        return lax.dot_general(k_tile, q, (((1,), (1,)), ((), ())), preferred_element_type=F32)

    def probs(s, m):
        return jnp.exp2((s - m).astype(BF16))

    s = scores(kc_ref[...])
    m = jnp.max(s, axis=0, keepdims=True)
    acc = jnp.dot(vc_ref[...], probs(s, m), preferred_element_type=F32)
    for j in range(kl_ref.shape[0] // tk):
        s = scores(kl_ref[j * tk:(j + 1) * tk, :])
        m_new = jnp.maximum(m, jnp.max(s, axis=0, keepdims=True))
        acc = jnp.exp2(m - m_new) * acc + jnp.dot(vl_ref[:, j * tk:(j + 1) * tk], probs(s, m_new),
                                                  preferred_element_type=F32)
        m = m_new
    o_ref[...] = (acc[:V_DIM] / acc[V_DIM:V_DIM + 1]).astype(o_ref.dtype)


def _value_slabs(v, batch, keys):
    v4 = v.reshape(batch, keys, MLA_HEADS, V_DIM)
    ones = jnp.ones((batch, keys, MLA_HEADS, 1), v.dtype)
    pad = jnp.zeros((batch, keys, MLA_HEADS, V_SLAB - V_DIM - 1), v.dtype)
    return jnp.concatenate([v4, ones, pad], axis=-1).transpose(0, 2, 3, 1).reshape(batch, MLA_HEADS * V_SLAB, keys)


def _attention(q, k_ctx, k_lat, vt_ctx, vt_lat):
    batch, seq, _ = q.shape
    ctx_len = k_ctx.shape[1]
    tq = min(ATTN_TQ, seq)
    tk = min(ATTN_TK, seq)
    assert seq % tq == 0 and seq % tk == 0
    return pl.pallas_call(
        functools.partial(_attn_kernel, tk=tk),
        grid=(batch, MLA_HEADS, seq // tq),
        in_specs=[pl.BlockSpec((None, tq, HEAD_SLAB), lambda b, h, i: (b, i, h)),
                  pl.BlockSpec((None, ctx_len, HEAD_SLAB), lambda b, h, i: (b, 0, h)),
                  pl.BlockSpec((None, seq, HEAD_SLAB), lambda b, h, i: (b, 0, h)),
                  pl.BlockSpec((None, V_SLAB, ctx_len), lambda b, h, i: (b, h, 0)),
                  pl.BlockSpec((None, V_SLAB, seq), lambda b, h, i: (b, h, 0))],
        out_specs=pl.BlockSpec((None, V_DIM, tq), lambda b, h, i: (b, h, i)),
        out_shape=jax.ShapeDtypeStruct((batch, MLA_HEADS * V_DIM, seq), BF16),
        compiler_params=_cparams("arbitrary", "arbitrary", "arbitrary"),
        name="mla_attention",
    )(q, k_ctx, k_lat, vt_ctx, vt_lat)


def _layer_mla(x_all, mod, rows, g, w_dq, g_q, w_uq, w_dkv, g_kv, w_ukv, w_o, update_ctx=False):
    assert not update_ctx, "attention is the last mixer of the stack: context queries are never needed"
    rows = _Rows(rows.batch, rows.ctx_len, rows.seq, PROJ_ROWS)
    tm = rows.tile
    d = x_all.shape[1]
    batch, seq, ctx_len = rows.batch, rows.seq, rows.ctx_len
    q_rank, kv_rank = w_dq.shape[1], g_kv.shape[0]
    assert V_DIM == LANES and QK_NOPE == LANES and QK_ROPE <= LANES
    h = _norm_mod(x_all, g, mod, 0, rows, BF16)

    w_uq_s = jnp.pad(w_uq.reshape(q_rank, MLA_HEADS, QK_NOPE + QK_ROPE),
                     ((0, 0), (0, 0), (0, HEAD_SLAB - QK_NOPE - QK_ROPE))).reshape(q_rank, MLA_HEADS * HEAD_SLAB)
    w_dkv_s = jnp.pad(w_dkv, ((0, 0), (0, kv_rank + LANES - w_dkv.shape[1])))
    w_ukv_s = w_ukv.reshape(kv_rank, MLA_HEADS, QK_NOPE + V_DIM)
    w_uk = w_ukv_s[:, :, :QK_NOPE].reshape(kv_rank, MLA_HEADS * QK_NOPE)
    w_uv = w_ukv_s[:, :, QK_NOPE:].reshape(kv_rank, MLA_HEADS * V_DIM)
    cos, sin = _rope_tables(seq, tm)
    lat_blocks = rows.lat_blocks

    cqn, = _matmul(h, w_dq, tm=tm, tn=q_rank, m_rows=rows.n_lat, x_row_off=rows.lat_off,
                   extras=[(g_q.reshape(1, q_rank), pl.BlockSpec((1, q_rank), lambda j, i: (0, 0)))],
                   outs=[(jax.ShapeDtypeStruct((rows.n_lat, q_rank), BF16), pl.BlockSpec((tm, q_rank), lambda j, i: (i, 0)))],
                   epilogue=_rmsnorm_epilogue, name="mla_q_down")
    tnq = 4 * HEAD_SLAB
    rope_lat = pl.BlockSpec((tm, LANES), lambda j, i: (1 + i % lat_blocks, 0))
    q, = _matmul(cqn, w_uq_s, tm=tm, tn=tnq, extras=[(cos, rope_lat), (sin, rope_lat)],
                 outs=[(jax.ShapeDtypeStruct((rows.n_lat, MLA_HEADS * HEAD_SLAB), BF16),
                        pl.BlockSpec((tm, tnq), lambda j, i: (i, j)))],
                 epilogue=_q_up_epilogue, name="mla_q_up")

    def keys_values(row_off, n_rows, rope_spec):
        kvn, kpe = _matmul(h, w_dkv_s, tm=tm, tn=kv_rank + LANES, m_rows=n_rows, x_row_off=row_off,
                           extras=[(g_kv.reshape(1, kv_rank), pl.BlockSpec((1, kv_rank), lambda j, i: (0, 0))),
                                   (cos, rope_spec), (sin, rope_spec)],
                           outs=[(jax.ShapeDtypeStruct((n_rows, kv_rank), BF16), pl.BlockSpec((tm, kv_rank), lambda j, i: (i, 0))),
                                 (jax.ShapeDtypeStruct((n_rows, LANES), BF16), pl.BlockSpec((tm, LANES), lambda j, i: (i, 0)))],
                           epilogue=functools.partial(_kv_down_epilogue, rank=kv_rank), name="mla_kv_down")
        tnk = 4 * LANES
        k, = _matmul(kvn, w_uk, tm=tm, tn=tnk,
                     extras=[(kpe, pl.BlockSpec((tm, LANES), lambda j, i: (i, 0)))],
                     outs=[(jax.ShapeDtypeStruct((n_rows, MLA_HEADS * HEAD_SLAB), BF16),
                            pl.BlockSpec((tm, 2 * tnk), lambda j, i: (i, j)))],
                     epilogue=_k_up_epilogue, name="mla_k_up")
        v, = _matmul(kvn, w_uv, tm=tm, tn=tnk,
                     outs=[(jax.ShapeDtypeStruct((n_rows, MLA_HEADS * V_DIM), BF16), pl.BlockSpec((tm, tnk), lambda j, i: (i, j)))],
                     name="mla_v_up")
        return k, v

    k_c, v_c = keys_values(rows.ctx_off, rows.n_ctx,pl.BlockSpec((tm, LANES), lambda j, i: (0, 0)))
    k_l, v_l = keys_values(rows.lat_off, rows.n_lat, rope_lat)
    vt_c = _value_slabs(v_c, batch, ctx_len)
    vt_l = _value_slabs(v_l, batch, seq)
    ot = _attention(q.reshape(batch, seq, -1), k_c.reshape(batch, ctx_len, -1), k_l.reshape(batch, seq, -1), vt_c, vt_l)
    attn = ot.swapaxes(1, 2).reshape(rows.n_lat, MLA_HEADS * V_DIM)
    return _proj_residual(attn, w_o, None, x_all, mod, 2, rows, rows.lat_off, name="mla_out")


CONV_HALO = 16
MAX_DECAY = math.log(DECAY_TARGET) / FAST_DECAY_PCT
MIN_DECAY = math.log(DECAY_TARGET) / SLOW_DECAY_PCT


def _hyena_in_kernel(prev_ref, main_ref, next_ref, w_ref, b_ref, cw_ref, cb_ref, o_ref, xext_ref, zext_ref, wbf_ref,
                     *rg, tm, seq_len):
    i = pl.program_id(1)
    seq_tiles = seq_len // tm
    t_in_seq = i % seq_tiles

    @pl.when(i == 0)
    def _():
        wbf_ref[...] = w_ref[...].astype(BF16)

    xext_ref[0:CONV_HALO, :] = prev_ref[...]
    xext_ref[CONV_HALO:CONV_HALO + tm, :] = main_ref[...]
    xext_ref[CONV_HALO + tm:2 * CONV_HALO + tm, :] = next_ref[...]
    zext_ref[...] = jnp.dot(xext_ref[...], wbf_ref[...], preferred_element_type=F32) + b_ref[...]
    row = lax.broadcasted_iota(jnp.int32, (tm, 1), 0)
    prev = jnp.where((row == 0) & (t_in_seq == 0), 0.0, zext_ref[pl.ds(CONV_HALO - 1, tm), :])
    nxt = jnp.where((row == tm - 1) & (t_in_seq == seq_tiles - 1), 0.0, zext_ref[pl.ds(CONV_HALO + 1, tm), :])
    out = prev * cw_ref[0:1, :] + zext_ref[pl.ds(CONV_HALO, tm), :] * cw_ref[1:2, :] + nxt * cw_ref[2:3, :] + cb_ref[...]
    if not rg:
        o_ref[...] = out.astype(o_ref.dtype)
        return
    rg_ref, = rg
    for s in range(tm // DFT_N2):
        _slab_store(rg_ref, s, out[s * DFT_N2:(s + 1) * DFT_N2, :], DFT_N2)

    def body(n2, c):
        o_ref[n2] = _rows_gather(rg_ref, n2, tm // DFT_N2, DFT_N2).astype(o_ref.dtype)
        return c
    lax.fori_loop(0, DFT_N2, body, 0)


def _hyena_in(h, w_in, b_in, conv_w, conv_b, rows, row_off, n_rows, seq_len, tn=512, time_split=False):
    tm = rows.tile
    k, n3 = w_in.shape
    d = n3 // 3
    tn = min(tn, d)
    nd = d // tn
    hb = tm // CONV_HALO
    last_hblk = h.shape[0] // CONV_HALO - 1
    if time_split:
        g = tm // DFT_N2
        assert tm % DFT_N2 == 0 and g % 8 == 0 and seq_len % tm == 0
        out_spec = pl.BlockSpec((None, DFT_N2, g, tn), lambda j, i: (j // nd, 0, i, j % nd))
        out_shape = jax.ShapeDtypeStruct((3, DFT_N2, n_rows // DFT_N2, d), F32)
        extra_scratch = [_slab_scratch(g, DFT_N2, tn)]
    else:
        out_spec = pl.BlockSpec((None, tm, tn), lambda j, i: (j // nd, i, j % nd))
        out_shape = jax.ShapeDtypeStruct((3, n_rows, d), BF16)
        extra_scratch = []
    return pl.pallas_call(
        functools.partial(_hyena_in_kernel, tm=tm, seq_len=seq_len),
        grid=(n3 // tn, n_rows // tm),
        in_specs=[pl.BlockSpec((CONV_HALO, k), lambda j, i: (jnp.maximum((i + row_off) * hb - 1, 0), 0)),
                  pl.BlockSpec((tm, k), lambda j, i: (i + row_off, 0)),
                  pl.BlockSpec((CONV_HALO, k), lambda j, i: (jnp.minimum((i + row_off + 1) * hb, last_hblk), 0)),
                  pl.BlockSpec((k, tn), lambda j, i: (0, j)),
                  pl.BlockSpec((1, tn), lambda j, i: (0, j)),
                  pl.BlockSpec((3, tn), lambda j, i: (0, j)),
                  pl.BlockSpec((1, tn), lambda j, i: (0, j))],
        out_specs=out_spec,
        out_shape=out_shape,
        scratch_shapes=[pltpu.VMEM((tm + 2 * CONV_HALO, k), BF16), pltpu.VMEM((tm + 2 * CONV_HALO, tn), F32),
                        pltpu.VMEM((k, tn), BF16)] + extra_scratch,
        compiler_params=_cparams("arbitrary", "arbitrary"),
        name="hyena_in",
    )(h, h, h, w_in, b_in.reshape(1, n3), conv_w, conv_b.reshape(1, n3))


def _hyena_filter_kernel(z_ref, w1_ref, b1_ref, w2_ref, b2_ref, f0_ref, f1_ref, w3_ref, dl_ref, k_ref, ss_ref,
                         *rg, tp, seq_len):
    p = pl.program_id(1)
    z = z_ref[...]
    h1 = jnp.sin(f0_ref[...] * (jnp.dot(z, w1_ref[...], precision=HIGHEST, preferred_element_type=F32) + b1_ref[...]))
    h2 = jnp.sin(f1_ref[...] * (jnp.dot(h1, w2_ref[...], precision=HIGHEST, preferred_element_type=F32) + b2_ref[...]))
    filt = jnp.dot(h2.astype(BF16), w3_ref[...].astype(BF16), preferred_element_type=F32)
    t = z[:, 0:1]
    kk = filt * (jnp.exp(-t * dl_ref[...]) + MOD_SHIFT)
    circ = p * tp + lax.broadcasted_iota(jnp.int32, (tp, 1), 0)
    kk = jnp.where(circ == seq_len, 0.0, kk)
    if rg:
        rg_ref, = rg
        for s in range(tp // DFT_N2):
            _slab_store(rg_ref, s, kk[s * DFT_N2:(s + 1) * DFT_N2, :], DFT_N2)

        def body(n2, c):
            k_ref[n2] = _rows_gather(rg_ref, n2, tp // DFT_N2, DFT_N2).astype(k_ref.dtype)
            return c
        lax.fori_loop(0, DFT_N2, body, 0)
    else:
        k_ref[...] = kk.astype(k_ref.dtype)

    @pl.when(p == 0)
    def _():
        ss_ref[...] = jnp.zeros(ss_ref.shape, F32)

    ss_ref[...] += jnp.sum(kk * kk, axis=0, keepdims=True)


def _hyena_filters(seq_len, d, f_w1, f_b1, f_w2, f_b2, f_w3, f_freq, tp=256, tc=1024, time_split=False):
    f32 = F32
    hid = f_w1.shape[1]
    od = HYENA_ORDER * d
    t = jnp.linspace(0.0, 1.0, seq_len, dtype=f32)[:, None]
    bands = (FILTER_EMB - 1) // 2
    w = 2.0 * math.pi * jnp.arange(seq_len, dtype=f32)[:, None] / seq_len
    f = jnp.linspace(1e-4, bands - 1, bands, dtype=f32)[None, :]
    z = jnp.concatenate([t, jnp.cos(f * w), -jnp.sin(f * w)], axis=-1)
    circ = jnp.arange(2 * seq_len)
    offs = jnp.where(circ < seq_len, circ, jnp.minimum(2 * seq_len - circ, seq_len - 1))
    z2 = jnp.pad(z[offs], ((0, 0), (0, LANES - FILTER_EMB)))
    w1p = jnp.pad(f_w1, ((0, LANES - FILTER_EMB), (0, 0)))
    w3s = f_w3.reshape(hid, HYENA_ORDER, 2, d).transpose(2, 0, 1, 3).reshape(2, hid, od)
    deltas = jnp.tile(jnp.abs(jnp.linspace(MIN_DECAY, MAX_DECAY, d, dtype=f32)), HYENA_ORDER)[None, :]
    tp = min(tp, seq_len)
    tc = min(tc, od)
    side_tiles = seq_len // tp
    small = lambda shape: pl.BlockSpec(shape, lambda j, p: (0,) * len(shape))
    if time_split:
        g = tp // DFT_N2
        assert tp % DFT_N2 == 0 and g % 16 == 0
        k_spec = pl.BlockSpec((DFT_N2, g, tc), lambda j, p: (0, p, j))
        k_shape = jax.ShapeDtypeStruct((DFT_N2, 2 * seq_len // DFT_N2, od), BF16)
        extra_scratch = [_slab_scratch(g, DFT_N2, tc)]
    else:
        k_spec = pl.BlockSpec((tp, tc), lambda j, p: (p, j))
        k_shape = jax.ShapeDtypeStruct((2 * seq_len, od), BF16)
        extra_scratch = []
    return pl.pallas_call(
        functools.partial(_hyena_filter_kernel, tp=tp, seq_len=seq_len),
        grid=(od // tc, 2 * side_tiles),
        in_specs=[pl.BlockSpec((tp, LANES), lambda j, p: (p, 0)),
                  small((LANES, hid)), small((1, hid)), small((hid, hid)), small((1, hid)), small((1, hid)), small((1, hid)),
                  pl.BlockSpec((None, hid, tc), lambda j, p: (p // side_tiles, 0, j)),
                  pl.BlockSpec((1, tc), lambda j, p: (0, j))],
        out_specs=[k_spec, pl.BlockSpec((1, tc), lambda j, p: (0, j))],
        out_shape=[k_shape, jax.ShapeDtypeStruct((1, od), F32)],
        scratch_shapes=extra_scratch,
        compiler_params=_cparams("arbitrary", "arbitrary"),
        name="hyena_filters",
    )(z2, w1p, f_b1.reshape(1, hid), f_w2, f_b2.reshape(1, hid), f_freq[0:1], f_freq[1:2], w3s, deltas)


def _stage2_tables(n, n1):
    n2 = n // n1
    r, c = _iota2(2 * n2, 2 * n2)
    kk = jnp.arange(n1, dtype=jnp.int32)[:, None, None] + n1 * (r % n2)[None]
    gc, gs = _cis(kk * (c % n2)[None], n)
    same = ((r < n2) == (c < n2))[None]
    fwd = jnp.where(same, gc, jnp.where((r < n2)[None], gs, -gs))
    return fwd.astype(BF16), fwd.swapaxes(1, 2).astype(BF16)


def _complex_mul(x, kf, n2):
    xr, xi = x[:n2], x[n2:]
    kr, ki = kf[:n2].astype(F32), kf[n2:].astype(F32)
    return jnp.concatenate([xr * kr - xi * ki, xr * ki + xi * kr], axis=0)


def _kf_epilogue(acc, extras, o_ref):
    ss_ref, = extras
    o_ref[...] = (acc * ss_ref[...]).astype(o_ref.dtype)


def _hyena_kf(k2u, sumsq, n1):
    n, od = k2u.shape
    n2 = n // n1
    colscale = lax.rsqrt(sumsq + EPS) / n
    r, c = _iota2(2 * n1, n1)
    fr, fs = _cis((r // 2) * c, n1)
    w1 = jnp.where(r % 2 == 0, fr, -fs).astype(BF16)
    cols = n2 * od
    t1 = min(cols, 4096)
    a = _left_mm(w1, k2u.reshape(n1, cols),
                 grid=(cols // t1,),
                 w_spec=pl.BlockSpec((2 * n1, n1), lambda j: (0, 0)),
                 x_spec=pl.BlockSpec((n1, t1), lambda j: (0, j)),
                 out_shape=jax.ShapeDtypeStruct((2 * n1, cols), BF16),
                 out_spec=pl.BlockSpec((2 * n1, t1), lambda j: (0, j)),
                 name="hyena_kf_stage1")
    g_fwd, _ = _stage2_tables(n, n1)
    tc = min(od, 1024)
    return _left_mm(g_fwd, a.reshape(n1, 2, n2, od),
                    grid=(n1, od // tc),
                    w_spec=pl.BlockSpec((None, 2 * n2, 2 * n2), lambda k, j: (k, 0, 0)),
                    x_spec=pl.BlockSpec((None, 2, n2, tc), lambda k, j: (k, 0, 0, j)),
                    out_shape=jax.ShapeDtypeStruct((n1, 2 * n2, od), BF16),
                    out_spec=pl.BlockSpec((None, 2 * n2, tc), lambda k, j: (k, 0, j)),
                    extras=[(colscale, pl.BlockSpec((1, tc), lambda k, j: (0, j)))],
                    epilogue=_kf_epilogue, name="hyena_kf_stage2")


def _conv_mid_kernel(gf_ref, gi_ref, a_ref, kf_ref, o_ref):
    n2 = a_ref.shape[1]
    a = a_ref[...].reshape(2 * n2, a_ref.shape[2])
    x = jnp.dot(gf_ref[...], a, preferred_element_type=F32)
    y = _complex_mul(x, kf_ref[...], n2)
    o_ref[...] = jnp.dot(gi_ref[...], y.astype(BF16), preferred_element_type=F32).astype(o_ref.dtype)


def _gate_epilogue(acc, extras, o_ref):
    xg_ref, u_ref, skip_ref = extras
    u = u_ref[...].astype(F32)
    o_ref[...] = (xg_ref[...].astype(F32) * (acc + skip_ref[...] * u)).astype(o_ref.dtype)


def _hyena_long_conv(u, xg, kf, kf_col_off, skip, seq_len, tables):
    n_rows, d = u.shape
    assert n_rows == 2 * seq_len
    n = 2 * seq_len
    n2 = DFT_N2
    n1 = n // n2
    hn = n1 // 2
    g_fwd, g_inv = tables
    cols = n2 * d
    m = max(1, min(4096 // d, n2))
    t1 = m * d
    r, c = _iota2(2 * n1, 2 * hn)
    fr, fs = _cis((r // 2) * (c % hn), n1)
    ro, ri = r % 2, c // hn
    w1 = jnp.where(ro == ri, fr, jnp.where(ro == 0, fs, -fs)).astype(BF16)
    a = _left_mm(w1, u.reshape(2 * hn, cols),
                 grid=(cols // t1,),
                 w_spec=pl.BlockSpec((2 * n1, 2 * hn), lambda j: (0, 0)),
                 x_spec=pl.BlockSpec((2 * hn, t1), lambda j: (0, j)),
                 out_shape=jax.ShapeDtypeStruct((2 * n1, cols), BF16),
                 out_spec=pl.BlockSpec((2 * n1, t1), lambda j: (0, j)),
                 name="hyena_conv_stage1")
    tc = min(d, 1024)
    nd = d // tc
    bmid = pl.pallas_call(
        _conv_mid_kernel,
        grid=(n1, nd),
        in_specs=[pl.BlockSpec((None, 2 * n2, 2 * n2), lambda k, j: (k, 0, 0)),
                  pl.BlockSpec((None, 2 * n2, 2 * n2), lambda k, j: (k, 0, 0)),
                  pl.BlockSpec((None, 2, n2, tc), lambda k, j: (k, 0, 0, j)),
                  pl.BlockSpec((None, 2 * n2, tc), lambda k, j: (k, 0, kf_col_off * nd + j))],
        out_specs=pl.BlockSpec((None, 2 * n2, tc), lambda k, j: (k, 0, j)),
        out_shape=jax.ShapeDtypeStruct((n1, 2 * n2, d), BF16),
        compiler_params=_cparams("arbitrary", "arbitrary"),
        name="hyena_conv_mid",
    )(g_fwd, g_inv, a.reshape(n1, 2, n2, d), kf)
    r, c = _iota2(2 * hn, 2 * n1)
    ec, es = _cis((r % hn) * (c // 2), n1)
    ro, ri = r // hn, c % 2
    w3 = jnp.where(ro == ri, ec, jnp.where(ro == 0, -es, es)).astype(BF16)
    blk = pl.BlockSpec((2 * hn, t1), lambda j: (0, j))
    return _left_mm(w3, bmid.reshape(2 * n1, cols),
                    grid=(cols // t1,),
                    w_spec=pl.BlockSpec((2 * hn, 2 * n1), lambda j: (0, 0)),
                    x_spec=pl.BlockSpec((2 * n1, t1), lambda j: (0, j)),
                    out_shape=jax.ShapeDtypeStruct((2 * hn, cols), BF16),
                    out_spec=blk,
                    extras=[(xg.reshape(2 * hn, cols), blk), (u.reshape(2 * hn, cols), blk),
                            (jnp.tile(skip.reshape(1, d), (1, m)), pl.BlockSpec((1, t1), lambda j: (0, 0)))],
                    epilogue=_gate_epilogue, name="hyena_conv_stage3").reshape(n_rows, d)


def _conv_dense_kernel(fk_ref, fz_ref, fzt_ref, k_ref, ss_ref, u_ref, xg_ref, skip_ref, o_ref):
    n = k_ref.shape[0]
    kf = jnp.dot(fk_ref[...], k_ref[...], preferred_element_type=F32) * ss_ref[...]
    u = u_ref[...]
    z = jnp.dot(fz_ref[...], u, preferred_element_type=F32)
    y = _complex_mul(z, kf, n)
    conv = jnp.dot(fzt_ref[...], y.astype(BF16), preferred_element_type=F32)
    o_ref[...] = (xg_ref[...].astype(F32) * (conv + skip_ref[...] * u.astype(F32))).astype(o_ref.dtype)


def _hyena_long_conv_dense(u, xg, k2u, sumsq, order, skip, seq_len):
    n_rows, d = u.shape
    assert n_rows == 2 * seq_len
    n = 2 * seq_len
    colscale = lax.rsqrt(sumsq + EPS) / n
    r, c = _iota2(2 * n, n)
    kc, ks = _cis((r % n) * c, n)
    fk = jnp.where(r < n, kc, -ks).astype(BF16)
    r, c = _iota2(2 * n, 2 * seq_len)
    zc, zs = _cis((r % n) * (c % seq_len), n)
    fz = jnp.where((r < n) == (c < seq_len), zc, jnp.where(r < n, zs, -zs))
    tc = min(d, 512)
    nd = d // tc
    full = lambda a: pl.BlockSpec(a.shape, lambda j: (0, 0))
    fz_b, fzt_b = fz.astype(BF16), fz.T.astype(BF16)
    blk = pl.BlockSpec((n_rows, tc), lambda j: (0, j))
    return pl.pallas_call(
        _conv_dense_kernel,
        grid=(nd,),
        in_specs=[full(fk), full(fz_b), full(fzt_b),
                  pl.BlockSpec((n, tc), lambda j: (0, order * nd + j)),
                  pl.BlockSpec((1, tc), lambda j: (0, order * nd + j)),
                  blk, blk, pl.BlockSpec((1, tc), lambda j: (0, j))],
        out_specs=blk,
        out_shape=jax.ShapeDtypeStruct((n_rows, d), BF16),
        compiler_params=_cparams("arbitrary"),
        name="hyena_conv_dense",
    )(fk, fz_b, fzt_b, k2u, colscale, u, xg, skip.reshape(1, d))


SPLIT_ROWS = 1024
N2_GROUP = 8
K1_GROUP = 8
SPLIT_TC = 512


def _slab_mm_kernel(w_ref, x_ref, *rest, epilogue):
    w = w_ref[...]
    for s in range(x_ref.shape[0]):
        acc = jnp.dot(w, x_ref[s].astype(BF16), preferred_element_type=F32)
        epilogue(acc, s, rest[:-1], rest[-1])


def _slab_plain(acc, s, extras, o_ref):
    o_ref[s] = acc.astype(o_ref.dtype)


def _slab_gate(acc, s, extras, o_ref):
    xg_ref, u_ref, skip_ref = extras
    o_ref[s] = (xg_ref[s] * (acc + skip_ref[...] * u_ref[s])).astype(o_ref.dtype)


def _slab_mm(w, x, out_dtype, slab_extras=(), row_extras=(), epilogue=_slab_plain, name="slab_mm"):
    x_arr, x_lead = x
    n2, k, c = x_arr.shape[len(x_lead):]
    tc = min(SPLIT_TC, c)

    def blk(r, lead=()):
        return pl.BlockSpec((None,) * len(lead) + (N2_GROUP, r, tc), lambda g, j: tuple(lead) + (g, 0, j))

    in_specs = [pl.BlockSpec(w.shape, lambda g, j: (0, 0)), blk(k, x_lead)]
    in_specs += [blk(a.shape[-2], lead) for a, lead in slab_extras]
    in_specs += [pl.BlockSpec((1, tc), lambda g, j: (0, j)) for _ in row_extras]
    return pl.pallas_call(
        functools.partial(_slab_mm_kernel, epilogue=epilogue),
        grid=(n2 // N2_GROUP, c // tc),
        in_specs=in_specs,
        out_specs=blk(w.shape[0]),
        out_shape=jax.ShapeDtypeStruct((n2, w.shape[0], c), out_dtype),
        compiler_params=_cparams("arbitrary", "arbitrary"),
        name=name,
    )(w, x_arr, *[a for a, _ in slab_extras], *row_extras)


def _stage2_regroup_in(a_ref, rin):
    def body(q, c):
        _rows_scatter(rin, q, a_ref[q].astype(F32), DFT_N2)
        return c
    lax.fori_loop(0, a_ref.shape[0], body, 0)


def _stage2_operand(rin, j):
    return jnp.concatenate([_slab_load(rin, 2 * j, DFT_N2), _slab_load(rin, 2 * j + 1, DFT_N2)], axis=0).astype(BF16)


def _kf_stage2_kernel(g_ref, a_ref, ss_ref, o_ref, rin):
    _stage2_regroup_in(a_ref, rin)
    for j in range(K1_GROUP):
        kf = jnp.dot(g_ref[j], _stage2_operand(rin, j), preferred_element_type=F32)
        o_ref[j] = (kf * ss_ref[...]).astype(o_ref.dtype)


def _conv_stage2_kernel(gf_ref, gi_ref, a_ref, kf_ref, o_ref, rin, rout):
    _stage2_regroup_in(a_ref, rin)
    for j in range(K1_GROUP):
        x = jnp.dot(gf_ref[j], _stage2_operand(rin, j), preferred_element_type=F32)
        y = _complex_mul(x, kf_ref[j], DFT_N2)
        b = jnp.dot(gi_ref[j], y.astype(BF16), preferred_element_type=F32)
        _slab_store(rout, 2 * j, b[:DFT_N2], DFT_N2)
        _slab_store(rout, 2 * j + 1, b[DFT_N2:], DFT_N2)

    def body(q, c):
        o_ref[q] = _rows_gather(rout, q, 2 * K1_GROUP, DFT_N2).astype(o_ref.dtype)
        return c
    lax.fori_loop(0, o_ref.shape[0], body, 0)


def _hyena_kf_split(k3, sumsq, tables):
    n2, n1, od = k3.shape
    n = n1 * n2
    colscale = lax.rsqrt(sumsq + EPS) / n
    r, c = _iota2(2 * n1, n1)
    fr, fs = _cis((r // 2) * c, n1)
    w1 = jnp.where(r % 2 == 0, fr, -fs).astype(BF16)
    a = _slab_mm(w1, (k3, ()), BF16, name="hyena_kf_stage1")
    tc = min(SPLIT_TC, od)
    return pl.pallas_call(
        _kf_stage2_kernel,
        grid=(n1 // K1_GROUP, od // tc),
        in_specs=[pl.BlockSpec((K1_GROUP, 2 * n2, 2 * n2), lambda k, j: (k, 0, 0)),
                  pl.BlockSpec((n2, 2 * K1_GROUP, tc), lambda k, j: (0, k, j)),
                  pl.BlockSpec((1, tc), lambda k, j: (0, j))],
        out_specs=pl.BlockSpec((K1_GROUP, 2 * n2, tc), lambda k, j: (k, 0, j)),
        out_shape=jax.ShapeDtypeStruct((n1, 2 * n2, od), BF16),
        scratch_shapes=[_slab_scratch(2 * K1_GROUP, n2, tc)],
        compiler_params=_cparams("arbitrary", "arbitrary"),
        name="hyena_kf_stage2",
    )(tables[0], a, colscale)


def _hyena_long_conv_split(v, xg, kf, kf_col_off, skip, tables):
    n2, m, d = v[0].shape[len(v[1]):]
    hn = m // 2
    n1 = 2 * hn
    g_fwd, g_inv = tables
    r, c = _iota2(2 * n1, 2 * hn)
    fr, fs = _cis((r // 2) * (c % hn), n1)
    ro, ri = r % 2, c // hn
    w1 = jnp.where(ro == ri, fr, jnp.where(ro == 0, fs, -fs)).astype(BF16)
    a = _slab_mm(w1, v, BF16, name="hyena_conv_stage1")
    tc = min(SPLIT_TC, d)
    nd = d // tc
    grp = pl.BlockSpec((n2, 2 * K1_GROUP, tc), lambda k, j: (0, k, j))
    tab = pl.BlockSpec((K1_GROUP, 2 * n2, 2 * n2), lambda k, j: (k, 0, 0))
    b = pl.pallas_call(
        _conv_stage2_kernel,
        grid=(n1 // K1_GROUP, nd),
        in_specs=[tab, tab, grp,
                  pl.BlockSpec((K1_GROUP, 2 * n2, tc), lambda k, j: (k, 0, kf_col_off * nd + j))],
        out_specs=grp,
        out_shape=jax.ShapeDtypeStruct((n2, 2 * n1, d), BF16),
        scratch_shapes=[_slab_scratch(2 * K1_GROUP, n2, tc), _slab_scratch(2 * K1_GROUP, n2, tc)],
        compiler_params=_cparams("arbitrary", "arbitrary"),
        name="hyena_conv_stage2",
    )(g_fwd, g_inv, a, kf)
    r, c = _iota2(2 * hn, 2 * n1)
    ec, es = _cis((r % hn) * (c // 2), n1)
    ro, ri = r // hn, c % 2
    w3 = jnp.where(ro == ri, ec, jnp.where(ro == 0, -es, es)).astype(BF16)
    return _slab_mm(w3, (b, ()), F32, slab_extras=[xg, v], row_extras=[skip.reshape(1, d)],
                    epilogue=_slab_gate, name="hyena_conv_stage3")


def _proj_split_kernel(x_ref, w_ref, b_ref, res_ref, gate_ref, o_ref, wbf_ref, acc_ref, rg_ref):
    @pl.when(pl.program_id(1) == 0)
    def _():
        wbf_ref[...] = w_ref[...].astype(BF16)

    n2, g, k = x_ref.shape
    x = x_ref[...].reshape(n2 * g, k).astype(BF16)
    acc_ref[...] = jnp.dot(x, wbf_ref[...], preferred_element_type=F32) + b_ref[...]

    def body(q, c):
        _rows_scatter(rg_ref, q, acc_ref[pl.ds(pl.multiple_of(q * g, g), g), :], n2)
        return c
    lax.fori_loop(0, n2, body, 0)
    for s in range(g):
        rows = slice(s * n2, (s + 1) * n2)
        o_ref[rows, :] = res_ref[rows, :] + gate_ref[...] * _slab_load(rg_ref, s, n2)


def _proj_residual_split(v3, w, bias, x_all, mod, chunk, seq_len, tn=512):
    n2, m, k = v3.shape
    d = x_all.shape[1]
    tn = min(tn, d)
    nd = d // tn
    g = SPLIT_ROWS // n2
    tm = n2 * g
    seq_tiles = seq_len // tm
    mod3 = mod.reshape(MOD_ROWS, 1, -1)
    res_spec = pl.BlockSpec((tm, tn), lambda j, i: (i, j))
    return pl.pallas_call(
        _proj_split_kernel,
        grid=(nd, m // g),
        in_specs=[pl.BlockSpec((n2, g, k), lambda j, i: (0, i, 0)),
                  pl.BlockSpec((k, tn), lambda j, i: (0, j)),
                  pl.BlockSpec((1, tn), lambda j, i: (0, j)),
                  res_spec,
                  pl.BlockSpec((None, 1, tn), lambda j, i: (i // seq_tiles, 0, chunk * nd + j))],
        out_specs=res_spec,
        out_shape=jax.ShapeDtypeStruct(x_all.shape, F32),
        scratch_shapes=[pltpu.VMEM((k, tn), BF16), pltpu.VMEM((tm, tn), F32), _slab_scratch(g, n2, tn)],
        input_output_aliases={3: 0},
        compiler_params=_cparams("arbitrary", "arbitrary"),
        name="hyena_out_lat",
    )(v3, w, bias.reshape(1, d), x_all, mod3)


def _layer_hyena(x_all, mod, rows, g, prm, w_out, b_out):
    (w_in, b_in, conv_w, conv_b, f_w1, f_b1, f_w2, f_b2, f_w3, f_freq, skip) = prm
    d = w_in.shape[0]
    assert rows.batch == 2, "the long convolution carries the two batch rows as one complex sequence"
    h = _norm_mod(x_all, g, mod, 0, rows, BF16)
    fprm = (f_w1, f_b1, f_w2, f_b2, f_w3, f_freq)
    zc = _hyena_in(h, w_in, b_in, conv_w, conv_b, rows, rows.ctx_off, rows.n_ctx, rows.ctx_len)
    k2u, sumsq = _hyena_filters(rows.ctx_len, d, *fprm)
    vc = _hyena_long_conv_dense(zc[2], zc[0], k2u, sumsq, 0, skip[0], rows.ctx_len)
    vc = _hyena_long_conv_dense(vc, zc[1], k2u, sumsq, 1, skip[1], rows.ctx_len)
    x_all = _proj_residual(vc, w_out, b_out, x_all, mod, 2, rows, rows.ctx_off, name="hyena_out_ctx")
    seq = rows.seq
    split_rows = _Rows(rows.batch, rows.ctx_len, seq, SPLIT_ROWS)
    zl = _hyena_in(h, w_in, b_in, conv_w, conv_b, split_rows, 0, rows.n_lat, seq, time_split=True)
    k3, sumsq = _hyena_filters(seq, d, *fprm, tp=2 * SPLIT_ROWS, tc=SPLIT_TC, time_split=True)
    tables = _stage2_tables(2 * seq, 2 * seq // DFT_N2)
    kf = _hyena_kf_split(k3, sumsq, tables)
    v = _hyena_long_conv_split((zl, (2,)), (zl, (0,)), kf, 0, skip[0], tables)
    v = _hyena_long_conv_split((v, ()), (zl, (1,)), kf, 1, skip[1], tables)
    return _proj_residual_split(v, w_out, b_out, x_all, mod, 2, seq)


def _layer_pool(x_all, mod, rows, g, w_grp, scale):
    h = _norm_mod(x_all, g, mod, 0, rows, F32)
    x_all = _pool_mixer(h, w_grp, scale, x_all, mod, 2, rows, rows.ctx_off, rows.n_ctx, rows.ctx_len)
    return _pool_mixer(h, w_grp, scale, x_all, mod, 2, rows, rows.lat_off, rows.n_lat, rows.seq)


def _final_norm_kernel(x_ref, g_ref, o_ref):
    x = x_ref[...]
    o_ref[...] = x * lax.rsqrt(jnp.mean(x * x, axis=-1, keepdims=True) + EPS) * g_ref[...]


def _final_norm(x_all, g, rows):
    tm = rows.tile
    d = x_all.shape[1]
    return pl.pallas_call(
        _final_norm_kernel,
        grid=(rows.n_lat // tm,),
        in_specs=[pl.BlockSpec((tm, d), lambda i: (i + rows.lat_off, 0)),
                  pl.BlockSpec((1, d), lambda i: (0, 0))],
        out_specs=pl.BlockSpec((tm, d), lambda i: (i, 0)),
        out_shape=jax.ShapeDtypeStruct((rows.n_lat, d), F32),
        compiler_params=_cparams("arbitrary"),
        name="final_norm",
    )(x_all, g.reshape(1, d))


def kernel(x, c, ctx, c_ctx, ada_w, ada_b, norm_g, final_g, hy_w_in, hy_b_in, hy_conv_w, hy_conv_b, hy_f_w1, hy_f_b1, hy_f_w2, hy_f_b2, hy_f_w3, hy_f_freq, hy_skip, hy_w_out, hy_b_out, fn_w_out, fn_b_out, pl_w, pl_scale, mla_w_dq, mla_g_q, mla_w_uq, mla_w_dkv, mla_g_kv, mla_w_ukv, mla_w_o, moe_w_group, moe_b_group, moe_w_expert, moe_b_expert, moe_w1, moe_w3, moe_w2):
    batch, seq, d = x.shape
    ctx_len = ctx.shape[1]
    depth = ada_w.shape[0]
    rows = _Rows(batch, ctx_len, seq, ROW_TILE)
    mod = _ada_mod(c, c_ctx, ada_w, ada_b)
    x_all = _pack_rows(x, ctx, rows)
    for i in range(depth):
        kind, j = i % N_MIXERS, i // N_MIXERS
        m = mod[i]
        g1 = norm_g[i, 0]
        if kind == 0:
            prm = (hy_w_in[j], hy_b_in[j], hy_conv_w[j], hy_conv_b[j], hy_f_w1[j], hy_f_b1[j], hy_f_w2[j],
                   hy_f_b2[j], hy_f_w3[j], hy_f_freq[j], hy_skip[j])
            x_all = _layer_hyena(x_all, m, rows, g1, prm, hy_w_out[j], hy_b_out[j])
        elif kind == 1:
            x_all = _layer_fnet(x_all, m, rows, g1, fn_w_out[j], fn_b_out[j])
        elif kind == 2:
            x_all = _layer_pool(x_all, m, rows, g1, pl_w[j], pl_scale[j])
        else:
            x_all = _layer_mla(x_all, m, rows, g1, mla_w_dq[j], mla_g_q[j], mla_w_uq[j], mla_w_dkv[j],
                               mla_g_kv[j], mla_w_ukv[j], mla_w_o[j], update_ctx=i < depth - 1)
        h, eid, wts = _norm_router(x_all, norm_g[i, 1], m, 3, rows, moe_w_group[i], moe_b_group[i],
                                   moe_w_expert[i], moe_b_expert[i])
        y = _moe_experts(h, eid, wts, moe_w1, moe_w3, moe_w2, i)
        x_all = _moe_combine(x_all, y, m, 5, rows)
    return _final_norm(x_all, final_g, rows).reshape(batch, seq, d)
```

```python
import functools
import math

import numpy as np
import jax
import jax.numpy as jnp
from jax import lax
from jax.experimental import pallas as pl
from jax.experimental.pallas import tpu as pltpu

F32 = jnp.float32
BF16 = jnp.bfloat16
HIGHEST = lax.Precision.HIGHEST

EPS = 1e-6
LANES = 128
MOD_ROWS = 8
VMEM_LIMIT = 56 * 1024 * 1024

N_MIXERS = 4
HYENA_ORDER = 2
FILTER_EMB = 33
DECAY_TARGET = 1e-2
FAST_DECAY_PCT = 0.3
SLOW_DECAY_PCT = 1.5
MOD_SHIFT = 0.0
FNET_GROUPS = 4
POOL_WINDOWS = (2, 4, 8, 16)
MLA_HEADS = 16
QK_NOPE = 128
QK_ROPE = 64
V_DIM = 128
GRID_W = 64
ROPE_THETA = 10000.0
N_GROUPS = 4
EXPERTS_PER_GROUP = 8
TOPK_EXPERT = 2
ROW_TILE = 256
PROJ_ROWS = 512
MOE_ROWS = 256


def _cparams(*sem):
    return pltpu.CompilerParams(dimension_semantics=sem, vmem_limit_bytes=VMEM_LIMIT)


def _ada_kernel(st_ref, w_ref, b_ref, o_ref, *, nrows):
    s = st_ref[...]
    s = s * jax.nn.sigmoid(s)
    w = w_ref[...]
    o_ref[...] = jnp.broadcast_to(b_ref[...], o_ref.shape)
    for r in range(nrows):
        o_ref[r:r + 1, :] = jnp.sum(s[:, r:r + 1] * w, axis=0, keepdims=True) + b_ref[...]


def _ada_mod(c, c_ctx, ada_w, ada_b):
    depth, d, n = ada_w.shape
    nrows = c.shape[0] + 1
    st = jnp.zeros((d, MOD_ROWS), F32).at[:, :nrows - 1].set(c.T).at[:, nrows - 1].set(c_ctx)
    tn = 1024 if n % 1024 == 0 else n
    return pl.pallas_call(
        functools.partial(_ada_kernel, nrows=nrows),
        grid=(depth, n // tn),
        in_specs=[pl.BlockSpec((d, MOD_ROWS), lambda l, j: (0, 0)),
                  pl.BlockSpec((None, d, tn), lambda l, j: (l, 0, j)),
                  pl.BlockSpec((None, 1, tn), lambda l, j: (l, 0, j))],
        out_specs=pl.BlockSpec((None, MOD_ROWS, tn), lambda l, j: (l, 0, j)),
        out_shape=jax.ShapeDtypeStruct((depth, MOD_ROWS, n), F32),
        compiler_params=_cparams("arbitrary", "arbitrary"),
        name="ada_mod",
    )(st, ada_w, ada_b.reshape(depth, 1, n))


class _Rows:
    def __init__(self, batch, ctx_len, seq, tile):
        assert seq % tile == 0
        self.batch, self.ctx_len, self.seq, self.tile = batch, ctx_len, seq, tile
        self.n_ctx = batch * ctx_len
        self.n_lat = batch * seq
        self.n_all = self.n_ctx + self.n_lat
        self.lat_blocks = seq // tile
        self.lat_off = 0
        self.ctx_off = self.n_lat // tile

    def mod_row(self, i):
        return jnp.where(i >= self.ctx_off, self.batch, i // self.lat_blocks)


def _pack_rows(x, ctx, rows):
    d = x.shape[-1]
    pad = -rows.n_all % SPLIT_ROWS
    return jnp.concatenate([x.reshape(-1, d), ctx.reshape(-1, d), jnp.zeros((pad, d), x.dtype)], axis=0)


def _rms_mod(x, g, sh, sc):
    y = x * lax.rsqrt(jnp.mean(x * x, axis=-1, keepdims=True) + EPS) * g
    return y * (1.0 + sc) + sh


def _norm_mod_kernel(x_ref, g_ref, sh_ref, sc_ref, o_ref):
    o_ref[...] = _rms_mod(x_ref[...], g_ref[...], sh_ref[...], sc_ref[...]).astype(o_ref.dtype)


def _norm_mod(x_all, g, mod, chunk, rows, out_dtype):
    n, d = rows.n_all, x_all.shape[1]
    tm = rows.tile
    mod3 = mod.reshape(MOD_ROWS, 1, -1)
    return pl.pallas_call(
        _norm_mod_kernel,
        grid=(n // tm,),
        in_specs=[pl.BlockSpec((tm, d), lambda i: (i, 0)),
                  pl.BlockSpec((1, d), lambda i: (0, 0)),
                  pl.BlockSpec((None, 1, d), lambda i: (rows.mod_row(i), 0, chunk)),
                  pl.BlockSpec((None, 1, d), lambda i: (rows.mod_row(i), 0, chunk + 1))],
        out_specs=pl.BlockSpec((tm, d), lambda i: (i, 0)),
        out_shape=jax.ShapeDtypeStruct((n, d), out_dtype),
        compiler_params=_cparams("arbitrary"),
        name="norm_mod",
    )(x_all, g.reshape(1, d), mod3, mod3)


def _norm_router_kernel(x_ref, g_ref, sh_ref, sc_ref, wr_ref, br_ref, h_ref, eid_ref, wt_ref, whi_ref, wlo_ref,
                        *, n_groups, per_group):
    h = _rms_mod(x_ref[...], g_ref[...], sh_ref[...], sc_ref[...])
    h_ref[...] = h.astype(h_ref.dtype)

    @pl.when(pl.program_id(0) == 0)
    def _():
        w = wr_ref[...]
        w_hi = w.astype(BF16)
        whi_ref[...] = w_hi
        wlo_ref[...] = (w - w_hi.astype(F32)).astype(BF16)

    h_hi = h.astype(BF16)
    h_lo = (h - h_hi.astype(F32)).astype(BF16)
    logits = (jnp.dot(h_hi, whi_ref[...], preferred_element_type=F32)
              + jnp.dot(h_lo, whi_ref[...], preferred_element_type=F32)
              + jnp.dot(h_hi, wlo_ref[...], preferred_element_type=F32)) + br_ref[...]
    lane = lax.broadcasted_iota(jnp.int32, logits.shape, 1).astype(F32)
    neg = -jnp.inf
    gl = jnp.where(lane < n_groups, logits, neg)
    gmax = jnp.max(gl, axis=-1, keepdims=True)
    p_grp = 1.0 / jnp.sum(jnp.exp(gl - gmax), axis=-1, keepdims=True)
    g_idx = jnp.min(jnp.where(gl == gmax, lane, float(LANES)), axis=-1, keepdims=True)
    lo = n_groups + g_idx * per_group
    el = jnp.where((lane >= lo) & (lane < lo + per_group), logits, neg)
    e1 = jnp.max(el, axis=-1, keepdims=True)
    i1 = jnp.min(jnp.where(el == e1, lane, float(LANES)), axis=-1, keepdims=True)
    el2 = jnp.where(lane == i1, neg, el)
    e2 = jnp.max(el2, axis=-1, keepdims=True)
    i2 = jnp.min(jnp.where(el2 == e2, lane, float(LANES)), axis=-1, keepdims=True)
    r = jnp.exp(e2 - e1)
    w1 = p_grp / (1.0 + r)
    w2 = p_grp * r / (1.0 + r)
    eid = jnp.where(lane == 0, i1 - n_groups, jnp.where(lane == 1, i2 - n_groups, 0.0))
    eid_ref[...] = eid.astype(jnp.int32)
    wt_ref[...] = jnp.where(lane == 0, w1, jnp.where(lane == 1, w2, 0.0))


def _norm_router(x_all, g, mod, chunk, rows, w_group, b_group, w_expert, b_expert):
    n, d = rows.n_all, x_all.shape[1]
    tm = rows.tile
    n_groups = w_group.shape[1]
    n_experts = w_expert.shape[1]
    wr = jnp.zeros((d, LANES), F32).at[:, :n_groups].set(w_group).at[:, n_groups:n_groups + n_experts].set(w_expert)
    br = jnp.zeros((1, LANES), F32).at[0, :n_groups].set(b_group).at[0, n_groups:n_groups + n_experts].set(b_expert)
    mod3 = mod.reshape(MOD_ROWS, 1, -1)
    h, eid, wts = pl.pallas_call(
        functools.partial(_norm_router_kernel, n_groups=n_groups, per_group=n_experts // n_groups),
        grid=(n // tm,),
        in_specs=[pl.BlockSpec((tm, d), lambda i: (i, 0)),
                  pl.BlockSpec((1, d), lambda i: (0, 0)),
                  pl.BlockSpec((None, 1, d), lambda i: (rows.mod_row(i), 0, chunk)),
                  pl.BlockSpec((None, 1, d), lambda i: (rows.mod_row(i), 0, chunk + 1)),
                  pl.BlockSpec((d, LANES), lambda i: (0, 0)),
                  pl.BlockSpec((1, LANES), lambda i: (0, 0))],
        out_specs=[pl.BlockSpec((tm, d), lambda i: (i, 0)),
                   pl.BlockSpec((tm, LANES), lambda i: (i, 0)),
                   pl.BlockSpec((tm, LANES), lambda i: (i, 0))],
        out_shape=[jax.ShapeDtypeStruct((n, d), F32),
                   jax.ShapeDtypeStruct((n, LANES), jnp.int32),
                   jax.ShapeDtypeStruct((n, LANES), F32)],
        scratch_shapes=[pltpu.VMEM((d, LANES), BF16), pltpu.VMEM((d, LANES), BF16)],
        compiler_params=_cparams("arbitrary"),
        name="norm_router",
    )(x_all, g.reshape(1, d), mod3, mod3, wr, br)
    return h, eid[:, :TOPK_EXPERT], wts[:, :TOPK_EXPERT]


def _moe_dispatch(eid, wts, n_experts, tm):
    n_tok = eid.shape[0]
    n_assign = eid.size
    e_flat = eid.reshape(-1)
    order = jnp.argsort(e_flat).astype(jnp.int32)
    counts = jnp.sum((e_flat[:, None] == jnp.arange(n_experts, dtype=jnp.int32)[None, :]).astype(jnp.int32), axis=0)
    padded = (counts + tm - 1) // tm * tm
    start = jnp.cumsum(counts) - counts
    pend = jnp.cumsum(padded)
    pstart = pend - padded
    n_blocks = (n_assign + n_experts * (tm - 1) + tm - 1) // tm
    blk_start = jnp.arange(n_blocks, dtype=jnp.int32) * tm
    blk_e = jnp.minimum(jnp.sum((pend[None, :] <= blk_start[:, None]).astype(jnp.int32), axis=1), n_experts - 1)
    blk_rows = jnp.clip(pstart[blk_e] + counts[blk_e] - blk_start, 0, tm).astype(jnp.int32)
    row = lax.broadcasted_iota(jnp.int32, (n_blocks, tm), 1)
    slot_e = blk_e[:, None]
    src = start[slot_e] + (blk_start[:, None] + row - pstart[slot_e])
    valid = row < blk_rows[:, None]
    asg = order[jnp.clip(src, 0, n_assign - 1)]
    tok = asg // TOPK_EXPERT
    buf_tok = jnp.where(valid, tok, 0)
    dummy = TOPK_EXPERT * n_tok + (jnp.arange(n_blocks, dtype=jnp.int32)[:, None] % 2) * tm + row
    buf_asg = jnp.where(valid, (asg % TOPK_EXPERT) * n_tok + tok, dummy)
    buf_w = jnp.where(valid, wts.reshape(-1)[asg], 0.0)
    return (buf_tok.reshape(n_blocks, 1, tm), buf_asg.reshape(n_blocks, 1, tm),
            buf_w.reshape(n_blocks, tm, 1), blk_e.astype(jnp.int32), blk_rows)


def _moe_kernel(blk_e_ref, blk_rows_ref, tok_ref, tok_next_ref, asg_ref, roww_ref, h_hbm,
                w1_ref, w3_ref, w2_ref, y_hbm, xbuf, ybuf, w1b, w3b, w2b, gsem, ssem):
    b = pl.program_id(0)
    nb = pl.num_programs(0)
    slot = b % 2
    other = 1 - slot

    tm = xbuf.shape[1]

    def start_gather(ids_ref, s):
        for r in range(tm):
            pltpu.make_async_copy(h_hbm.at[pl.ds(ids_ref[0, r], 1)], xbuf.at[s, pl.ds(r, 1)], gsem.at[s]).start()

    def wait_gather(s):
        pltpu.make_async_copy(h_hbm.at[pl.ds(0, tm)], xbuf.at[s], gsem.at[s]).wait()

    def start_scatter(s):
        for r in range(tm):
            pltpu.make_async_copy(ybuf.at[s, pl.ds(r, 1)], y_hbm.at[pl.ds(asg_ref[0, r], 1)], ssem.at[s]).start()

    def wait_scatter(s):
        pltpu.make_async_copy(ybuf.at[s], y_hbm.at[pl.ds(0, tm)], ssem.at[s]).wait()

    def used(blk):
        return blk_rows_ref[jnp.clip(blk, 0, nb - 1)] > 0

    @pl.when(b == 0)
    def _():
        ybuf[...] = jnp.zeros(ybuf.shape, ybuf.dtype)
        n_real = y_hbm.shape[0] - 2 * tm
        for s in range(2):
            fill = pltpu.make_async_copy(ybuf.at[s], y_hbm.at[pl.ds(n_real + s * tm, tm)], ssem.at[s])
            fill.start()
            fill.wait()

    @pl.when((b == 0) & used(b))
    def _():
        start_gather(tok_ref, slot)

    @pl.when((b + 1 < nb) & used(b + 1))
    def _():
        start_gather(tok_next_ref, other)

    @pl.when((b >= 2) & used(b - 2))
    def _():
        wait_scatter(slot)

    @pl.when(used(b))
    def _():
        wait_gather(slot)

        @pl.when((b == 0) | (blk_e_ref[b] != blk_e_ref[jnp.maximum(b - 1, 0)]))
        def _():
            w1b[...] = w1_ref[...].astype(BF16)
            w3b[...] = w3_ref[...].astype(BF16)
            w2b[...] = w2_ref[...].astype(BF16)

        x = xbuf[slot].astype(BF16)
        a = jnp.dot(x, w1b[...], preferred_element_type=F32)
        g = jnp.dot(x, w3b[...], preferred_element_type=F32)
        hm = (a * jax.nn.sigmoid(a) * g).astype(BF16)
        ybuf[slot] = jnp.dot(hm, w2b[...], preferred_element_type=F32) * roww_ref[...]
        start_scatter(slot)

    @pl.when(b == nb - 1)
    def _():
        @pl.when((nb >= 2) & used(b - 1))
        def _():
            wait_scatter(other)

        @pl.when(used(b))
        def _():
            wait_scatter(slot)


def _moe_experts(h, eid, wts, w1, w3, w2, layer):
    n, d = h.shape
    n_experts, de = w1.shape[1], w1.shape[3]
    tm = MOE_ROWS
    tok, asg, roww, blk_e, blk_rows = _moe_dispatch(eid, wts, n_experts, tm)
    n_blocks = tok.shape[0]
    smem_blk = lambda f: pl.BlockSpec((None, 1, tm), f, memory_space=pltpu.SMEM)
    grid_spec = pltpu.PrefetchScalarGridSpec(
        num_scalar_prefetch=2,
        grid=(n_blocks,),
        in_specs=[smem_blk(lambda b, be, br: (b, 0, 0)),
                  smem_blk(lambda b, be, br: (jnp.minimum(b + 1, n_blocks - 1), 0, 0)),
                  smem_blk(lambda b, be, br: (b, 0, 0)),
                  pl.BlockSpec((None, tm, 1), lambda b, be, br: (b, 0, 0)),
                  pl.BlockSpec(memory_space=pl.ANY),
                  pl.BlockSpec((None, None, d, de), lambda b, be, br: (layer, be[b], 0, 0)),
                  pl.BlockSpec((None, None, d, de), lambda b, be, br: (layer, be[b], 0, 0)),
                  pl.BlockSpec((None, None, de, d), lambda b, be, br: (layer, be[b], 0, 0))],
        out_specs=pl.BlockSpec(memory_space=pl.ANY),
        scratch_shapes=[pltpu.VMEM((2, tm, d), F32), pltpu.VMEM((2, tm, d), F32),
                        pltpu.VMEM((d, de), BF16), pltpu.VMEM((d, de), BF16), pltpu.VMEM((de, d), BF16),
                        pltpu.SemaphoreType.DMA((2,)), pltpu.SemaphoreType.DMA((2,))],
    )
    return pl.pallas_call(
        _moe_kernel,
        grid_spec=grid_spec,
        out_shape=jax.ShapeDtypeStruct((n * TOPK_EXPERT + 2 * tm, d), F32),
        compiler_params=_cparams("arbitrary"),
        name="moe_experts",
    )(blk_e, blk_rows, tok, tok, asg, roww, h, w1, w3, w2)


def _moe_combine_kernel(x_ref, *rest):
    y_refs, gate_ref, o_ref = rest[:TOPK_EXPERT], rest[TOPK_EXPERT], rest[TOPK_EXPERT + 1]
    acc = y_refs[0][...]
    for y_ref in y_refs[1:]:
        acc = acc + y_ref[...]
    o_ref[...] = x_ref[...] + gate_ref[...] * acc


def _moe_combine(x_all, y, mod, chunk, rows):
    n, d = rows.n_all, x_all.shape[1]
    tm = rows.tile
    mod3 = mod.reshape(MOD_ROWS, 1, -1)
    nblk = n // tm
    y_specs = [pl.BlockSpec((tm, d), functools.partial(lambda i, k: (i + k * nblk, 0), k=k)) for k in range(TOPK_EXPERT)]
    return pl.pallas_call(
        _moe_combine_kernel,
        grid=(nblk,),
        in_specs=[pl.BlockSpec((tm, d), lambda i: (i, 0))] + y_specs +
                 [pl.BlockSpec((None, 1, d), lambda i: (rows.mod_row(i), 0, chunk))],
        out_specs=pl.BlockSpec((tm, d), lambda i: (i, 0)),
        out_shape=jax.ShapeDtypeStruct(x_all.shape, F32),
        input_output_aliases={0: 0},
        compiler_params=_cparams("arbitrary"),
        name="moe_combine",
    )(x_all, *([y] * TOPK_EXPERT), mod3)


def _mm_kernel(*refs, has_bias, n_extra, epilogue):
    x_ref, w_ref = refs[0], refs[1]
    pos = 2
    b_ref = None
    if has_bias:
        b_ref = refs[pos]
        pos += 1
    extras = refs[pos:pos + n_extra]
    outs = refs[pos + n_extra:-1]
    wbf = refs[-1]

    @pl.when(pl.program_id(1) == 0)
    def _():
        wbf[...] = w_ref[...].astype(BF16)

    acc = jnp.dot(x_ref[...].astype(BF16), wbf[...], preferred_element_type=F32)
    if has_bias:
        acc = acc + b_ref[...]
    epilogue(acc, extras, outs)


def _store_epilogue(acc, extras, outs):
    outs[0][...] = acc.astype(outs[0].dtype)


def _matmul(x, w, *, w_lead=(), bias=None, tm, tn, m_rows=None, x_row_off=0, extras=(), outs,
            epilogue=_store_epilogue, aliases=None, name="matmul"):
    k = x.shape[1]
    n = w.shape[-1]
    m_rows = x.shape[0] if m_rows is None else m_rows
    assert m_rows % tm == 0 and n % tn == 0 and w.shape[-2] == k
    lead = tuple(w_lead)
    in_specs = [pl.BlockSpec((tm, k), lambda j, i: (i + x_row_off, 0)),
                pl.BlockSpec((None,) * len(lead) + (k, tn), lambda j, i: lead + (0, j))]
    args = [x, w]
    if bias is not None:
        in_specs.append(pl.BlockSpec((1, tn), lambda j, i: (0, j)))
        args.append(bias.reshape(1, n))
    for arr, spec in extras:
        in_specs.append(spec)
        args.append(arr)
    return pl.pallas_call(
        functools.partial(_mm_kernel, has_bias=bias is not None, n_extra=len(extras), epilogue=epilogue),
        grid=(n // tn, m_rows // tm),
        in_specs=in_specs,
        out_specs=[spec for _, spec in outs],
        out_shape=[shape for shape, _ in outs],
        scratch_shapes=[pltpu.VMEM((k, tn), BF16)],
        input_output_aliases=aliases or {},
        compiler_params=_cparams("arbitrary", "arbitrary"),
        name=name,
    )(*args)


def _residual_epilogue(acc, extras, outs):
    res_ref, gate_ref = extras
    outs[0][...] = res_ref[...] + gate_ref[...] * acc


def _proj_residual(xin, w, bias, x_all, mod, chunk, rows, row_off, *, w_lead=(), tn=512, name="proj_residual"):
    tm = rows.tile
    d = x_all.shape[1]
    tn = min(tn, d)
    mod3 = mod.reshape(MOD_ROWS, 1, -1)
    nd = d // tn
    res_spec = pl.BlockSpec((tm, tn), lambda j, i: (i + row_off, j))
    gate_spec = pl.BlockSpec((None, 1, tn), lambda j, i: (rows.mod_row(i + row_off), 0, chunk * nd + j))
    n_extra_before = 2 + (bias is not None)
    out, = _matmul(xin, w, w_lead=w_lead, bias=bias, tm=tm, tn=tn,
                   extras=[(x_all, res_spec), (mod3, gate_spec)],
                   outs=[(jax.ShapeDtypeStruct(x_all.shape, F32), res_spec)],
                   epilogue=_residual_epilogue, aliases={n_extra_before: 0}, name=name)
    return out


POOL_HALO = 8


def _pool_kernel(prev_ref, main_ref, next_ref, w_ref, sc_ref, res_ref, gate_ref, o_ref, ext_ref, wbf_ref,
                 *, tm, seq_len, windows):
    i = pl.program_id(0)
    seq_tiles = seq_len // tm
    t_in_seq = i % seq_tiles

    @pl.when(i == 0)
    def _():
        wbf_ref[...] = w_ref[...].astype(BF16)

    zero_halo = jnp.zeros(prev_ref.shape, F32)
    ext_ref[0:POOL_HALO, :] = jnp.where(t_in_seq == 0, zero_halo, prev_ref[...])
    ext_ref[POOL_HALO:POOL_HALO + tm, :] = main_ref[...]
    ext_ref[POOL_HALO + tm:2 * POOL_HALO + tm, :] = jnp.where(t_in_seq == seq_tiles - 1, zero_halo, next_ref[...])

    pos = t_in_seq * tm + lax.broadcasted_iota(jnp.int32, (tm, 1), 0)
    cg = main_ref.shape[1] // len(windows)
    for gi, win in enumerate(windows):
        half = win // 2
        cols = slice(gi * cg, (gi + 1) * cg)
        s = ext_ref[pl.ds(POOL_HALO - half, tm), cols]
        for dlt in range(-half + 1, half):
            s = s + ext_ref[pl.ds(POOL_HALO + dlt, tm), cols]
        cnt = jnp.minimum(pos + half, seq_len) - jnp.maximum(pos - half, 0)
        pooled = s / cnt.astype(F32) - main_ref[:, cols]
        y = jnp.dot(pooled.astype(BF16), wbf_ref[gi], preferred_element_type=F32) * sc_ref[:, cols]
        o_ref[:, cols] = res_ref[:, cols] + gate_ref[:, cols] * y


def _pool_mixer(h, w_grp, scale, x_all, mod, chunk, rows, row_off, n_rows, seq_len):
    tm = rows.tile
    d = x_all.shape[1]
    assert max(POOL_WINDOWS) // 2 <= POOL_HALO and tm % POOL_HALO == 0 and seq_len % tm == 0
    hb = tm // POOL_HALO
    last_hblk = h.shape[0] // POOL_HALO - 1
    mod3 = mod.reshape(MOD_ROWS, 1, -1)
    main_spec = pl.BlockSpec((tm, d), lambda i: (i + row_off, 0))
    return pl.pallas_call(
        functools.partial(_pool_kernel, tm=tm, seq_len=seq_len, windows=POOL_WINDOWS),
        grid=(n_rows // tm,),
        in_specs=[pl.BlockSpec((POOL_HALO, d), lambda i: (jnp.maximum((i + row_off) * hb - 1, 0), 0)),
                  main_spec,
                  pl.BlockSpec((POOL_HALO, d), lambda i: (jnp.minimum((i + row_off + 1) * hb, last_hblk), 0)),
                  pl.BlockSpec(w_grp.shape, lambda i: (0, 0, 0)),
                  pl.BlockSpec((1, d), lambda i: (0, 0)),
                  main_spec,
                  pl.BlockSpec((None, 1, d), lambda i: (rows.mod_row(i + row_off), 0, chunk))],
        out_specs=main_spec,
        out_shape=jax.ShapeDtypeStruct(x_all.shape, F32),
        scratch_shapes=[pltpu.VMEM((tm + 2 * POOL_HALO, d), F32), pltpu.VMEM(w_grp.shape, BF16)],
        input_output_aliases={5: 0},
        compiler_params=_cparams("arbitrary"),
        name="pool_mixer",
    )(h, h, h, w_grp, scale.reshape(1, d), x_all, mod3)


DFT_N2 = 128


def _cis(num, den):
    ang = (num % den).astype(F32) * (2.0 * math.pi / den)
    return jnp.cos(ang), jnp.sin(ang)


def _iota2(n_rows, n_cols):
    return (lax.broadcasted_iota(jnp.int32, (n_rows, n_cols), 0), lax.broadcasted_iota(jnp.int32, (n_rows, n_cols), 1))


def _slab_pitch(rows):
    return rows + 8


def _slab_scratch(g, rows, width):
    return pltpu.VMEM((width // LANES, g * _slab_pitch(rows), LANES), F32)


def _slab_store(scr, s, val, rows):
    p = _slab_pitch(rows)
    for l in range(scr.shape[0]):
        scr[l, s * p:s * p + rows, :] = val[:, l * LANES:(l + 1) * LANES]


def _slab_load(scr, s, rows):
    p = _slab_pitch(rows)
    return jnp.concatenate([scr[l, s * p:s * p + rows, :] for l in range(scr.shape[0])], axis=1)


def _rows_gather(scr, q, g, rows):
    p = _slab_pitch(rows)
    return jnp.concatenate([scr[l, pl.ds(q, g, stride=p), :] for l in range(scr.shape[0])], axis=1)


def _rows_scatter(scr, q, val, rows):
    p = _slab_pitch(rows)
    for l in range(scr.shape[0]):
        scr[l, pl.ds(q, val.shape[0], stride=p), :] = val[:, l * LANES:(l + 1) * LANES]


def _left_mm_kernel(w_ref, x_ref, *rest, epilogue):
    x = x_ref[...]
    x = x.reshape(-1, x.shape[-1])
    acc = jnp.dot(w_ref[...], x.astype(BF16), preferred_element_type=F32)
    epilogue(acc, rest[:-1], rest[-1])


def _scaled_store(scale):
    def epilogue(acc, extras, o_ref):
        o_ref[...] = (acc * scale).reshape(o_ref.shape).astype(o_ref.dtype)
    return epilogue


def _left_mm(w, x, *, grid, w_spec, x_spec, out_shape, out_spec, extras=(), epilogue=_scaled_store(1.0),
             aliases=None, name="left_mm"):
    return pl.pallas_call(
        functools.partial(_left_mm_kernel, epilogue=epilogue),
        grid=grid,
        in_specs=[w_spec, x_spec] + [s for _, s in extras],
        out_specs=out_spec,
        out_shape=out_shape,
        input_output_aliases=aliases or {},
        compiler_params=_cparams(*(("arbitrary",) * len(grid))),
        name=name,
    )(w, x, *[a for a, _ in extras])


def _fnet_channel_kernel(x_ref, w_ref, o_ref):
    cg = x_ref.shape[1]
    r = jnp.dot(x_ref[...], w_ref[...], preferred_element_type=F32)
    o_ref[0] = r[:, :cg].astype(o_ref.dtype)
    o_ref[1] = r[:, cg:].astype(o_ref.dtype)


def _fnet_channel(h, rows, row_off, n_seq, seq_len):
    tm = rows.tile
    d = h.shape[1]
    cg = d // FNET_GROUPS
    ci, ki = _iota2(cg, cg)
    cc, sc = _cis(ci * ki, cg)
    wc = jnp.concatenate([cc, -sc], axis=1).astype(BF16)
    st = seq_len // tm
    return pl.pallas_call(
        _fnet_channel_kernel,
        grid=(n_seq * st, FNET_GROUPS),
        in_specs=[pl.BlockSpec((tm, cg), lambda i, g: (i + row_off, g)),
                  pl.BlockSpec((cg, 2 * cg), lambda i, g: (0, 0))],
        out_specs=pl.BlockSpec((None, 2, tm, cg), lambda i, g: (i // st, 0, i % st, g)),
        out_shape=jax.ShapeDtypeStruct((n_seq, 2, seq_len, d), BF16),
        compiler_params=_cparams("arbitrary", "arbitrary"),
        name="fnet_channel",
    )(h, wc)


def _fnet_positions(z, n1, tc):
    n_seq, _, seq_len, d = z.shape
    n2 = seq_len // n1
    scale = 1.0 / math.sqrt(seq_len * (d // FNET_GROUPS))
    if n1 > 1:
        r, cidx = _iota2(2 * n1, 2 * n1)
        k1, ro, ri, nn = r // 2, r % 2, cidx // n1, cidx % n1
        fr, fs = _cis(k1 * nn, n1)
        w1 = jnp.where(ro == ri, fr, jnp.where(ro == 0, fs, -fs)).astype(BF16)
        cols = n2 * d
        t1 = min(cols, 4096)
        a = _left_mm(w1, z.reshape(n_seq, 2 * n1, cols),
                     grid=(n_seq, cols // t1),
                     w_spec=pl.BlockSpec((2 * n1, 2 * n1), lambda s, j: (0, 0)),
                     x_spec=pl.BlockSpec((None, 2 * n1, t1), lambda s, j: (s, 0, j)),
                     out_shape=jax.ShapeDtypeStruct((n_seq, 2 * n1, cols), BF16),
                     out_spec=pl.BlockSpec((None, 2 * n1, t1), lambda s, j: (s, 0, j)),
                     name="fnet_stage1")
        a = a.reshape(n_seq, n1, 2, n2, d)
    else:
        a = z.reshape(n_seq, 1, 2, n2, d)
    k2i, ci2 = _iota2(n2, 2 * n2)
    kk = jnp.arange(n1, dtype=jnp.int32)[:, None, None] + n1 * k2i[None]
    gc, gs = _cis(kk * (ci2 % n2)[None], seq_len)
    g2 = jnp.where((ci2 < n2)[None], gc, gs).astype(BF16)
    nd = d // tc
    out = _left_mm(g2, a,
                   grid=(n_seq, n1, nd),
                   w_spec=pl.BlockSpec((None, n2, 2 * n2), lambda s, k, j: (k, 0, 0)),
                   x_spec=pl.BlockSpec((None, None, 2, n2, tc), lambda s, k, j: (s, k, 0, 0, j)),
                   out_shape=jax.ShapeDtypeStruct((n_seq, n2, n1 * d), BF16),
                   out_spec=pl.BlockSpec((None, n2, tc), lambda s, k, j: (s, 0, k * nd + j)),
                   epilogue=_scaled_store(scale), name="fnet_stage2")
    return out.reshape(n_seq * seq_len, d)


FNET_ROWS = 2048
FNET_K1_GROUP = 16


def _fnet_channel_split_kernel(x_ref, w_ref, zr_ref, zi_ref, rg_ref):
    cg = x_ref.shape[1]
    g = x_ref.shape[0] // DFT_N2
    r = jnp.dot(x_ref[...], w_ref[...], preferred_element_type=F32)
    for plane, z_ref in enumerate((zr_ref, zi_ref)):
        for s in range(g):
            _slab_store(rg_ref, s, r[s * DFT_N2:(s + 1) * DFT_N2, plane * cg:(plane + 1) * cg], DFT_N2)

        def body(n2, c):
            z_ref[n2] = _rows_gather(rg_ref, n2, g, DFT_N2).astype(z_ref.dtype)
            return c
        lax.fori_loop(0, DFT_N2, body, 0)


def _fnet_channel_split(h, n_seq, seq_len):
    tm = FNET_ROWS
    d = h.shape[1]
    cg = d // FNET_GROUPS
    ci, ki = _iota2(cg, cg)
    cc, sc = _cis(ci * ki, cg)
    wc = jnp.concatenate([cc, -sc], axis=1).astype(BF16)
    st = seq_len // tm
    g = tm // DFT_N2
    assert seq_len % tm == 0 and g % 16 == 0
    plane = jax.ShapeDtypeStruct((n_seq, DFT_N2, seq_len // DFT_N2, d), BF16)
    out_spec = pl.BlockSpec((None, DFT_N2, g, cg), lambda i, c: (i // st, 0, i % st, c))
    return pl.pallas_call(
        _fnet_channel_split_kernel,
        grid=(n_seq * st, FNET_GROUPS),
        in_specs=[pl.BlockSpec((tm, cg), lambda i, c: (i, c)),
                  pl.BlockSpec((cg, 2 * cg), lambda i, c: (0, 0))],
        out_specs=[out_spec, out_spec],
        out_shape=[plane, plane],
        scratch_shapes=[_slab_scratch(g, DFT_N2, cg)],
        compiler_params=_cparams("arbitrary", "arbitrary"),
        name="fnet_channel",
    )(h, wc)


def _fnet_stage1_kernel(w_ref, zr_ref, zi_ref, o_ref):
    w = w_ref[...]
    for s in range(zr_ref.shape[0]):
        z = jnp.concatenate([zr_ref[s], zi_ref[s]], axis=0)
        o_ref[s] = jnp.dot(w, z, preferred_element_type=F32).astype(o_ref.dtype)


def _fnet_stage2_kernel(g_ref, a_ref, o_ref, rin, rout, *, scale):
    kg = a_ref.shape[1] // 2

    def body_in(q, c):
        _rows_scatter(rin, q, a_ref[q].astype(F32), DFT_N2)
        return c
    lax.fori_loop(0, a_ref.shape[0], body_in, 0)
    for j in range(kg):
        r = jnp.dot(g_ref[j], _stage2_operand(rin, j), preferred_element_type=F32)
        _slab_store(rout, j, r * scale, DFT_N2)

    def body_out(q, c):
        o_ref[q] = _rows_gather(rout, q, kg, DFT_N2).astype(o_ref.dtype)
        return c
    lax.fori_loop(0, o_ref.shape[0], body_out, 0)


def _fnet_positions_split(zr, zi):
    n_seq, n2, n1, d = zr.shape
    seq_len = n1 * n2
    scale = 1.0 / math.sqrt(seq_len * (d // FNET_GROUPS))
    r, cidx = _iota2(2 * n1, 2 * n1)
    k1, ro, ri, nn = r // 2, r % 2, cidx // n1, cidx % n1
    fr, fs = _cis(k1 * nn, n1)
    w1 = jnp.where(ro == ri, fr, jnp.where(ro == 0, fs, -fs)).astype(BF16)
    tc = min(SPLIT_TC, d)
    zblk = pl.BlockSpec((None, N2_GROUP, n1, tc), lambda s, g, j: (s, g, 0, j))
    a = pl.pallas_call(
        _fnet_stage1_kernel,
        grid=(n_seq, n2 // N2_GROUP, d // tc),
        in_specs=[pl.BlockSpec((2 * n1, 2 * n1), lambda s, g, j: (0, 0)), zblk, zblk],
        out_specs=pl.BlockSpec((None, N2_GROUP, 2 * n1, tc), lambda s, g, j: (s, g, 0, j)),
        out_shape=jax.ShapeDtypeStruct((n_seq, n2, 2 * n1, d), BF16),
        compiler_params=_cparams("arbitrary", "arbitrary", "arbitrary"),
        name="fnet_stage1",
    )(w1, zr, zi)
    k2i, ci2 = _iota2(n2, 2 * n2)
    kk = jnp.arange(n1, dtype=jnp.int32)[:, None, None] + n1 * k2i[None]
    gc, gs = _cis(kk * (ci2 % n2)[None], seq_len)
    g2 = jnp.where((ci2 < n2)[None], gc, gs).astype(BF16)
    kg = FNET_K1_GROUP
    assert n1 % kg == 0
    out = pl.pallas_call(
        functools.partial(_fnet_stage2_kernel, scale=scale),
        grid=(n_seq, n1 // kg, d // tc),
        in_specs=[pl.BlockSpec((kg, n2, 2 * n2), lambda s, k, j: (k, 0, 0)),
                  pl.BlockSpec((None, n2, 2 * kg, tc), lambda s, k, j: (s, 0, k, j))],
        out_specs=pl.BlockSpec((None, n2, kg, tc), lambda s, k, j: (s, 0, k, j)),
        out_shape=jax.ShapeDtypeStruct((n_seq, n2, n1, d), BF16),
        scratch_shapes=[_slab_scratch(2 * kg, n2, tc), _slab_scratch(kg, n2, tc)],
        compiler_params=_cparams("arbitrary", "arbitrary", "arbitrary"),
        name="fnet_stage2",
    )(g2, a)
    return out.reshape(n_seq * seq_len, d)


def _layer_fnet(x_all, mod, rows, g, w_out, b_out):
    h = _norm_mod(x_all, g, mod, 0, rows, BF16)
    d = x_all.shape[1]
    zc = _fnet_channel(h, rows, rows.ctx_off, rows.batch, rows.ctx_len)
    fc = _fnet_positions(zc, 1, min(d, 512))
    x_all = _proj_residual(fc, w_out, b_out, x_all, mod, 2, rows, rows.ctx_off, name="fnet_out_ctx")
    zr, zi = _fnet_channel_split(h, rows.batch, rows.seq)
    fl = _fnet_positions_split(zr, zi)
    big = _Rows(rows.batch, rows.ctx_len, rows.seq, PROJ_ROWS)
    return _proj_residual(fl, w_out, b_out, x_all, mod, 2, big, big.lat_off, tn=1024, name="fnet_out_lat")


HEAD_SLAB = 2 * LANES
MLA_SCALE = (QK_NOPE + QK_ROPE) ** -0.5
Q_SCALE = MLA_SCALE * math.log2(math.e)
V_SLAB = V_DIM + 16
ATTN_TQ = 1024
ATTN_TK = 1024


def _rope_tables(seq_len, lead_identity_rows):
    n_rows = seq_len // GRID_W
    row = jnp.repeat(jnp.arange(n_rows, dtype=F32), GRID_W)
    col = jnp.tile(jnp.arange(GRID_W, dtype=F32), n_rows)
    half = QK_ROPE // 2
    inv = ROPE_THETA ** (-jnp.arange(0, half, 2, dtype=F32) / half)
    ang_r = row[:, None] * inv
    ang_c = col[:, None] * inv
    ang = jnp.concatenate([ang_r, ang_r, ang_c, ang_c], axis=-1)
    pad = jnp.zeros((seq_len, LANES - QK_ROPE), F32)
    cos = jnp.concatenate([jnp.cos(ang), pad + 1.0], axis=-1)
    sin = jnp.concatenate([jnp.sin(ang), pad], axis=-1)
    if lead_identity_rows:
        cos = jnp.concatenate([jnp.ones((lead_identity_rows, LANES), F32), cos], axis=0)
        sin = jnp.concatenate([jnp.zeros((lead_identity_rows, LANES), F32), sin], axis=0)
    return cos, sin


def _rope(x, cos, sin):
    q = QK_ROPE // 4
    lane = lax.broadcasted_iota(jnp.int32, x.shape, 1)
    even = (lane // q) % 2 == 0
    rot = jnp.where(even, -pltpu.roll(x, LANES - q, 1), pltpu.roll(x, q, 1))
    return x * cos + rot * sin


def _rmsnorm_epilogue(acc, extras, outs):
    g_ref, = extras
    y = acc * lax.rsqrt(jnp.mean(acc * acc, axis=-1, keepdims=True) + EPS) * g_ref[...]
    outs[0][...] = y.astype(outs[0].dtype)


def _q_up_epilogue(acc, extras, outs):
    cos_ref, sin_ref = extras
    o_ref, = outs
    for hh in range(acc.shape[1] // HEAD_SLAB):
        c0 = hh * HEAD_SLAB
        o_ref[:, c0:c0 + LANES] = (acc[:, c0:c0 + LANES] * Q_SCALE).astype(o_ref.dtype)
        pe = _rope(acc[:, c0 + LANES:c0 + HEAD_SLAB], cos_ref[...], sin_ref[...])
        o_ref[:, c0 + LANES:c0 + HEAD_SLAB] = (pe * Q_SCALE).astype(o_ref.dtype)


def _kv_down_epilogue(acc, extras, outs, *, rank):
    g_ref, cos_ref, sin_ref = extras
    kvn_ref, kpe_ref = outs
    lat = acc[:, :rank]
    y = lat * lax.rsqrt(jnp.mean(lat * lat, axis=-1, keepdims=True) + EPS) * g_ref[...]
    kvn_ref[...] = y.astype(kvn_ref.dtype)
    kpe_ref[...] = _rope(acc[:, rank:rank + LANES], cos_ref[...], sin_ref[...]).astype(kpe_ref.dtype)


def _k_up_epilogue(acc, extras, outs):
    kpe_ref, = extras
    o_ref, = outs
    for hh in range(acc.shape[1] // LANES):
        o_ref[:, hh * HEAD_SLAB:hh * HEAD_SLAB + LANES] = acc[:, hh * LANES:(hh + 1) * LANES].astype(o_ref.dtype)
        o_ref[:, hh * HEAD_SLAB + LANES:(hh + 1) * HEAD_SLAB] = kpe_ref[...]


def _attn_kernel(q_ref, kc_ref, kl_ref, vc_ref, vl_ref, o_ref, *, tk):
    q = q_ref[...]

    def scores(k_tile):
        return lax.dot_general(k_tile, q, (((1,), (1,)), ((), ())), preferred_element_type=F32)

    def probs(s, m):
        return jnp.exp2((s - m).astype(BF16))

    s = scores(kc_ref[...])
    m = jnp.max(s, axis=0, keepdims=True)
    acc = jnp.dot(vc_ref[...], probs(s, m), preferred_element_type=F32)
    for j in range(kl_ref.shape[0] // tk):
        s = scores(kl_ref[j * tk:(j + 1) * tk, :])
        m_new = jnp.maximum(m, jnp.max(s, axis=0, keepdims=True))
        acc = jnp.exp2(m - m_new) * acc + jnp.dot(vl_ref[:, j * tk:(j + 1) * tk], probs(s, m_new),
                                                  preferred_element_type=F32)
        m = m_new
    o_ref[...] = (acc[:V_DIM] / acc[V_DIM:V_DIM + 1]).astype(o_ref.dtype)


def _value_slabs(v, batch, keys):
    v4 = v.reshape(batch, keys, MLA_HEADS, V_DIM)
    ones = jnp.ones((batch, keys, MLA_HEADS, 1), v.dtype)
    pad = jnp.zeros((batch, keys, MLA_HEADS, V_SLAB - V_DIM - 1), v.dtype)
    return jnp.concatenate([v4, ones, pad], axis=-1).transpose(0, 2, 3, 1).reshape(batch, MLA_HEADS * V_SLAB, keys)


def _attention(q, k_ctx, k_lat, vt_ctx, vt_lat):
    batch, seq, _ = q.shape
    ctx_len = k_ctx.shape[1]
    tq = min(ATTN_TQ, seq)
    tk = min(ATTN_TK, seq)
    assert seq % tq == 0 and seq % tk == 0
    return pl.pallas_call(
        functools.partial(_attn_kernel, tk=tk),
        grid=(batch, MLA_HEADS, seq // tq),
        in_specs=[pl.BlockSpec((None, tq, HEAD_SLAB), lambda b, h, i: (b, i, h)),
                  pl.BlockSpec((None, ctx_len, HEAD_SLAB), lambda b, h, i: (b, 0, h)),
                  pl.BlockSpec((None, seq, HEAD_SLAB), lambda b, h, i: (b, 0, h)),
                  pl.BlockSpec((None, V_SLAB, ctx_len), lambda b, h, i: (b, h, 0)),
                  pl.BlockSpec((None, V_SLAB, seq), lambda b, h, i: (b, h, 0))],
        out_specs=pl.BlockSpec((None, V_DIM, tq), lambda b, h, i: (b, h, i)),
        out_shape=jax.ShapeDtypeStruct((batch, MLA_HEADS * V_DIM, seq), BF16),
        compiler_params=_cparams("arbitrary", "arbitrary", "arbitrary"),
        name="mla_attention",
    )(q, k_ctx, k_lat, vt_ctx, vt_lat)


def _layer_mla(x_all, mod, rows, g, w_dq, g_q, w_uq, w_dkv, g_kv, w_ukv, w_o, update_ctx=False):
    assert not update_ctx, "attention is the last mixer of the stack: context queries are never needed"
    rows = _Rows(rows.batch, rows.ctx_len, rows.seq, PROJ_ROWS)
    tm = rows.tile
    d = x_all.shape[1]
    batch, seq, ctx_len = rows.batch, rows.seq, rows.ctx_len
    q_rank, kv_rank = w_dq.shape[1], g_kv.shape[0]
    assert V_DIM == LANES and QK_NOPE == LANES and QK_ROPE <= LANES
    h = _norm_mod(x_all, g, mod, 0, rows, BF16)

    w_uq_s = jnp.pad(w_uq.reshape(q_rank, MLA_HEADS, QK_NOPE + QK_ROPE),
                     ((0, 0), (0, 0), (0, HEAD_SLAB - QK_NOPE - QK_ROPE))).reshape(q_rank, MLA_HEADS * HEAD_SLAB)
    w_dkv_s = jnp.pad(w_dkv, ((0, 0), (0, kv_rank + LANES - w_dkv.shape[1])))
    w_ukv_s = w_ukv.reshape(kv_rank, MLA_HEADS, QK_NOPE + V_DIM)
    w_uk = w_ukv_s[:, :, :QK_NOPE].reshape(kv_rank, MLA_HEADS * QK_NOPE)
    w_uv = w_ukv_s[:, :, QK_NOPE:].reshape(kv_rank, MLA_HEADS * V_DIM)
    cos, sin = _rope_tables(seq, tm)
    lat_blocks = rows.lat_blocks

    cqn, = _matmul(h, w_dq, tm=tm, tn=q_rank, m_rows=rows.n_lat, x_row_off=rows.lat_off,
                   extras=[(g_q.reshape(1, q_rank), pl.BlockSpec((1, q_rank), lambda j, i: (0, 0)))],
                   outs=[(jax.ShapeDtypeStruct((rows.n_lat, q_rank), BF16), pl.BlockSpec((tm, q_rank), lambda j, i: (i, 0)))],
                   epilogue=_rmsnorm_epilogue, name="mla_q_down")
    tnq = 4 * HEAD_SLAB
    rope_lat = pl.BlockSpec((tm, LANES), lambda j, i: (1 + i % lat_blocks, 0))
    q, = _matmul(cqn, w_uq_s, tm=tm, tn=tnq, extras=[(cos, rope_lat), (sin, rope_lat)],
                 outs=[(jax.ShapeDtypeStruct((rows.n_lat, MLA_HEADS * HEAD_SLAB), BF16),
                        pl.BlockSpec((tm, tnq), lambda j, i: (i, j)))],
                 epilogue=_q_up_epilogue, name="mla_q_up")

    def keys_values(row_off, n_rows, rope_spec):
        kvn, kpe = _matmul(h, w_dkv_s, tm=tm, tn=kv_rank + LANES, m_rows=n_rows, x_row_off=row_off,
                           extras=[(g_kv.reshape(1, kv_rank), pl.BlockSpec((1, kv_rank), lambda j, i: (0, 0))),
                                   (cos, rope_spec), (sin, rope_spec)],
                           outs=[(jax.ShapeDtypeStruct((n_rows, kv_rank), BF16), pl.BlockSpec((tm, kv_rank), lambda j, i: (i, 0))),
                                 (jax.ShapeDtypeStruct((n_rows, LANES), BF16), pl.BlockSpec((tm, LANES), lambda j, i: (i, 0)))],
                           epilogue=functools.partial(_kv_down_epilogue, rank=kv_rank), name="mla_kv_down")
        tnk = 4 * LANES
        k, = _matmul(kvn, w_uk, tm=tm, tn=tnk,
                     extras=[(kpe, pl.BlockSpec((tm, LANES), lambda j, i: (i, 0)))],
                     outs=[(jax.ShapeDtypeStruct((n_rows, MLA_HEADS * HEAD_SLAB), BF16),
                            pl.BlockSpec((tm, 2 * tnk), lambda j, i: (i, j)))],
                     epilogue=_k_up_epilogue, name="mla_k_up")
        v, = _matmul(kvn, w_uv, tm=tm, tn=tnk,
                     outs=[(jax.ShapeDtypeStruct((n_rows, MLA_HEADS * V_DIM), BF16), pl.BlockSpec((tm, tnk), lambda j, i: (i, j)))],
                     name="mla_v_up")
        return k, v

    k_c, v_c = keys_values(rows.ctx_off, rows.n_ctx,pl.BlockSpec((tm, LANES), lambda j, i: (0, 0)))
    k_l, v_l = keys_values(rows.lat_off, rows.n_lat, rope_lat)
    vt_c = _value_slabs(v_c, batch, ctx_len)
    vt_l = _value_slabs(v_l, batch, seq)
    ot = _attention(q.reshape(batch, seq, -1), k_c.reshape(batch, ctx_len, -1), k_l.reshape(batch, seq, -1), vt_c, vt_l)
    attn = ot.swapaxes(1, 2).reshape(rows.n_lat, MLA_HEADS * V_DIM)
    return _proj_residual(attn, w_o, None, x_all, mod, 2, rows, rows.lat_off, tn=1024, name="mla_out")


CONV_HALO = 16
MAX_DECAY = math.log(DECAY_TARGET) / FAST_DECAY_PCT
MIN_DECAY = math.log(DECAY_TARGET) / SLOW_DECAY_PCT


def _hyena_in_kernel(prev_ref, main_ref, next_ref, w_ref, b_ref, cw_ref, cb_ref, o_ref, xext_ref, zext_ref, wbf_ref,
                     *rg, tm, seq_len):
    i = pl.program_id(1)
    seq_tiles = seq_len // tm
    t_in_seq = i % seq_tiles

    @pl.when(i == 0)
    def _():
        wbf_ref[...] = w_ref[...].astype(BF16)

    xext_ref[0:CONV_HALO, :] = prev_ref[...]
    xext_ref[CONV_HALO:CONV_HALO + tm, :] = main_ref[...]
    xext_ref[CONV_HALO + tm:2 * CONV_HALO + tm, :] = next_ref[...]
    zext_ref[...] = jnp.dot(xext_ref[...], wbf_ref[...], preferred_element_type=F32) + b_ref[...]
    row = lax.broadcasted_iota(jnp.int32, (tm, 1), 0)
    prev = jnp.where((row == 0) & (t_in_seq == 0), 0.0, zext_ref[pl.ds(CONV_HALO - 1, tm), :])
    nxt = jnp.where((row == tm - 1) & (t_in_seq == seq_tiles - 1), 0.0, zext_ref[pl.ds(CONV_HALO + 1, tm), :])
    out = prev * cw_ref[0:1, :] + zext_ref[pl.ds(CONV_HALO, tm), :] * cw_ref[1:2, :] + nxt * cw_ref[2:3, :] + cb_ref[...]
    if not rg:
        o_ref[...] = out.astype(o_ref.dtype)
        return
    rg_ref, = rg
    for s in range(tm // DFT_N2):
        _slab_store(rg_ref, s, out[s * DFT_N2:(s + 1) * DFT_N2, :], DFT_N2)

    def body(n2, c):
        o_ref[n2] = _rows_gather(rg_ref, n2, tm // DFT_N2, DFT_N2).astype(o_ref.dtype)
        return c
    lax.fori_loop(0, DFT_N2, body, 0)


def _hyena_in(h, w_in, b_in, conv_w, conv_b, rows, row_off, n_rows, seq_len, tn=512, time_split=False):
    tm = rows.tile
    k, n3 = w_in.shape
    d = n3 // 3
    tn = min(tn, d)
    nd = d // tn
    hb = tm // CONV_HALO
    last_hblk = h.shape[0] // CONV_HALO - 1
    if time_split:
        g = tm // DFT_N2
        assert tm % DFT_N2 == 0 and g % 8 == 0 and seq_len % tm == 0
        out_spec = pl.BlockSpec((None, DFT_N2, g, tn), lambda j, i: (j // nd, 0, i, j % nd))
        out_shape = jax.ShapeDtypeStruct((3, DFT_N2, n_rows // DFT_N2, d), F32)
        extra_scratch = [_slab_scratch(g, DFT_N2, tn)]
    else:
        out_spec = pl.BlockSpec((None, tm, tn), lambda j, i: (j // nd, i, j % nd))
        out_shape = jax.ShapeDtypeStruct((3, n_rows, d), BF16)
        extra_scratch = []
    return pl.pallas_call(
        functools.partial(_hyena_in_kernel, tm=tm, seq_len=seq_len),
        grid=(n3 // tn, n_rows // tm),
        in_specs=[pl.BlockSpec((CONV_HALO, k), lambda j, i: (jnp.maximum((i + row_off) * hb - 1, 0), 0)),
                  pl.BlockSpec((tm, k), lambda j, i: (i + row_off, 0)),
                  pl.BlockSpec((CONV_HALO, k), lambda j, i: (jnp.minimum((i + row_off + 1) * hb, last_hblk), 0)),
                  pl.BlockSpec((k, tn), lambda j, i: (0, j)),
                  pl.BlockSpec((1, tn), lambda j, i: (0, j)),
                  pl.BlockSpec((3, tn), lambda j, i: (0, j)),
                  pl.BlockSpec((1, tn), lambda j, i: (0, j))],
        out_specs=out_spec,
        out_shape=out_shape,
        scratch_shapes=[pltpu.VMEM((tm + 2 * CONV_HALO, k), BF16), pltpu.VMEM((tm + 2 * CONV_HALO, tn), F32),
                        pltpu.VMEM((k, tn), BF16)] + extra_scratch,
        compiler_params=_cparams("arbitrary", "arbitrary"),
        name="hyena_in",
    )(h, h, h, w_in, b_in.reshape(1, n3), conv_w, conv_b.reshape(1, n3))


def _hyena_filter_kernel(z_ref, w1_ref, b1_ref, w2_ref, b2_ref, f0_ref, f1_ref, w3_ref, dl_ref, k_ref, ss_ref,
                         h2_ref, *rg, tp, seq_len):
    p = pl.program_id(0)
    z = z_ref[...]

    @pl.when(pl.program_id(1) == 0)
    def _():
        h1 = jnp.sin(f0_ref[...] * (jnp.dot(z, w1_ref[...], precision=HIGHEST, preferred_element_type=F32) + b1_ref[...]))
        h2 = jnp.sin(f1_ref[...] * (jnp.dot(h1, w2_ref[...], precision=HIGHEST, preferred_element_type=F32) + b2_ref[...]))
        h2_ref[...] = h2.astype(BF16)

    filt = jnp.dot(h2_ref[...], w3_ref[...].astype(BF16), preferred_element_type=F32)
    t = z[:, 0:1]
    kk = filt * (jnp.exp(-t * dl_ref[...]) + MOD_SHIFT)
    circ = p * tp + lax.broadcasted_iota(jnp.int32, (tp, 1), 0)
    kk = jnp.where(circ == seq_len, 0.0, kk)
    if rg:
        rg_ref, = rg
        for s in range(tp // DFT_N2):
            _slab_store(rg_ref, s, kk[s * DFT_N2:(s + 1) * DFT_N2, :], DFT_N2)

        def body(n2, c):
            k_ref[n2] = _rows_gather(rg_ref, n2, tp // DFT_N2, DFT_N2).astype(k_ref.dtype)
            return c
        lax.fori_loop(0, DFT_N2, body, 0)
    else:
        k_ref[...] = kk.astype(k_ref.dtype)
    ss_ref[...] = jnp.sum(kk * kk, axis=0, keepdims=True)


def _hyena_filters(seq_len, d, f_w1, f_b1, f_w2, f_b2, f_w3, f_freq, tp=256, tc=1024, time_split=False):
    f32 = F32
    hid = f_w1.shape[1]
    od = HYENA_ORDER * d
    t = jnp.linspace(0.0, 1.0, seq_len, dtype=f32)[:, None]
    bands = (FILTER_EMB - 1) // 2
    w = 2.0 * math.pi * jnp.arange(seq_len, dtype=f32)[:, None] / seq_len
    f = jnp.linspace(1e-4, bands - 1, bands, dtype=f32)[None, :]
    z = jnp.concatenate([t, jnp.cos(f * w), -jnp.sin(f * w)], axis=-1)
    circ = jnp.arange(2 * seq_len)
    offs = jnp.where(circ < seq_len, circ, jnp.minimum(2 * seq_len - circ, seq_len - 1))
    z2 = jnp.pad(z[offs], ((0, 0), (0, LANES - FILTER_EMB)))
    w1p = jnp.pad(f_w1, ((0, LANES - FILTER_EMB), (0, 0)))
    w3s = f_w3.reshape(hid, HYENA_ORDER, 2, d).transpose(2, 0, 1, 3).reshape(2, hid, od)
    deltas = jnp.tile(jnp.abs(jnp.linspace(MIN_DECAY, MAX_DECAY, d, dtype=f32)), HYENA_ORDER)[None, :]
    tp = min(tp, seq_len)
    tc = min(tc, od)
    side_tiles = seq_len // tp
    small = lambda shape: pl.BlockSpec(shape, lambda p, j: (0,) * len(shape))
    n_ptiles = 2 * side_tiles
    if time_split:
        g = tp // DFT_N2
        assert tp % DFT_N2 == 0 and g % 16 == 0
        k_spec = pl.BlockSpec((DFT_N2, g, tc), lambda p, j: (0, p, j))
        k_shape = jax.ShapeDtypeStruct((DFT_N2, 2 * seq_len // DFT_N2, od), BF16)
        extra_scratch = [_slab_scratch(g, DFT_N2, tc)]
    else:
        k_spec = pl.BlockSpec((tp, tc), lambda p, j: (p, j))
        k_shape = jax.ShapeDtypeStruct((2 * seq_len, od), BF16)
        extra_scratch = []
    k, ss_parts = pl.pallas_call(
        functools.partial(_hyena_filter_kernel, tp=tp, seq_len=seq_len),
        grid=(n_ptiles, od // tc),
        in_specs=[pl.BlockSpec((tp, LANES), lambda p, j: (p, 0)),
                  small((LANES, hid)), small((1, hid)), small((hid, hid)), small((1, hid)), small((1, hid)), small((1, hid)),
                  pl.BlockSpec((None, hid, tc), lambda p, j: (p // side_tiles, 0, j)),
                  pl.BlockSpec((1, tc), lambda p, j: (0, j))],
        out_specs=[k_spec, pl.BlockSpec((None, 1, tc), lambda p, j: (p, 0, j))],
        out_shape=[k_shape, jax.ShapeDtypeStruct((n_ptiles, 1, od), F32)],
        scratch_shapes=[pltpu.VMEM((tp, hid), BF16)] + extra_scratch,
        compiler_params=_cparams("arbitrary", "arbitrary"),
        name="hyena_filters",
    )(z2, w1p, f_b1.reshape(1, hid), f_w2, f_b2.reshape(1, hid), f_freq[0:1], f_freq[1:2], w3s, deltas)
    return k, jnp.sum(ss_parts, axis=0)


def _stage2_tables(n, n1):
    n2 = n // n1
    r, c = _iota2(2 * n2, 2 * n2)
    kk = jnp.arange(n1, dtype=jnp.int32)[:, None, None] + n1 * (r % n2)[None]
    gc, gs = _cis(kk * (c % n2)[None], n)
    same = ((r < n2) == (c < n2))[None]
    fwd = jnp.where(same, gc, jnp.where((r < n2)[None], gs, -gs))
    return fwd.astype(BF16), fwd.swapaxes(1, 2).astype(BF16)


def _complex_mul(x, kf, n2):
    xr, xi = x[:n2], x[n2:]
    kr, ki = kf[:n2].astype(F32), kf[n2:].astype(F32)
    return jnp.concatenate([xr * kr - xi * ki, xr * ki + xi * kr], axis=0)


def _kf_epilogue(acc, extras, o_ref):
    ss_ref, = extras
    o_ref[...] = (acc * ss_ref[...]).astype(o_ref.dtype)


def _hyena_kf(k2u, sumsq, n1):
    n, od = k2u.shape
    n2 = n // n1
    colscale = lax.rsqrt(sumsq + EPS) / n
    r, c = _iota2(2 * n1, n1)
    fr, fs = _cis((r // 2) * c, n1)
    w1 = jnp.where(r % 2 == 0, fr, -fs).astype(BF16)
    cols = n2 * od
    t1 = min(cols, 4096)
    a = _left_mm(w1, k2u.reshape(n1, cols),
                 grid=(cols // t1,),
                 w_spec=pl.BlockSpec((2 * n1, n1), lambda j: (0, 0)),
                 x_spec=pl.BlockSpec((n1, t1), lambda j: (0, j)),
                 out_shape=jax.ShapeDtypeStruct((2 * n1, cols), BF16),
                 out_spec=pl.BlockSpec((2 * n1, t1), lambda j: (0, j)),
                 name="hyena_kf_stage1")
    g_fwd, _ = _stage2_tables(n, n1)
    tc = min(od, 1024)
    return _left_mm(g_fwd, a.reshape(n1, 2, n2, od),
                    grid=(n1, od // tc),
                    w_spec=pl.BlockSpec((None, 2 * n2, 2 * n2), lambda k, j: (k, 0, 0)),
                    x_spec=pl.BlockSpec((None, 2, n2, tc), lambda k, j: (k, 0, 0, j)),
                    out_shape=jax.ShapeDtypeStruct((n1, 2 * n2, od), BF16),
                    out_spec=pl.BlockSpec((None, 2 * n2, tc), lambda k, j: (k, 0, j)),
                    extras=[(colscale, pl.BlockSpec((1, tc), lambda k, j: (0, j)))],
                    epilogue=_kf_epilogue, name="hyena_kf_stage2")


def _conv_mid_kernel(gf_ref, gi_ref, a_ref, kf_ref, o_ref):
    n2 = a_ref.shape[1]
    a = a_ref[...].reshape(2 * n2, a_ref.shape[2])
    x = jnp.dot(gf_ref[...], a, preferred_element_type=F32)
    y = _complex_mul(x, kf_ref[...], n2)
    o_ref[...] = jnp.dot(gi_ref[...], y.astype(BF16), preferred_element_type=F32).astype(o_ref.dtype)


def _gate_epilogue(acc, extras, o_ref):
    xg_ref, u_ref, skip_ref = extras
    u = u_ref[...].astype(F32)
    o_ref[...] = (xg_ref[...].astype(F32) * (acc + skip_ref[...] * u)).astype(o_ref.dtype)


def _hyena_long_conv(u, xg, kf, kf_col_off, skip, seq_len, tables):
    n_rows, d = u.shape
    assert n_rows == 2 * seq_len
    n = 2 * seq_len
    n2 = DFT_N2
    n1 = n // n2
    hn = n1 // 2
    g_fwd, g_inv = tables
    cols = n2 * d
    m = max(1, min(4096 // d, n2))
    t1 = m * d
    r, c = _iota2(2 * n1, 2 * hn)
    fr, fs = _cis((r // 2) * (c % hn), n1)
    ro, ri = r % 2, c // hn
    w1 = jnp.where(ro == ri, fr, jnp.where(ro == 0, fs, -fs)).astype(BF16)
    a = _left_mm(w1, u.reshape(2 * hn, cols),
                 grid=(cols // t1,),
                 w_spec=pl.BlockSpec((2 * n1, 2 * hn), lambda j: (0, 0)),
                 x_spec=pl.BlockSpec((2 * hn, t1), lambda j: (0, j)),
                 out_shape=jax.ShapeDtypeStruct((2 * n1, cols), BF16),
                 out_spec=pl.BlockSpec((2 * n1, t1), lambda j: (0, j)),
                 name="hyena_conv_stage1")
    tc = min(d, 1024)
    nd = d // tc
    bmid = pl.pallas_call(
        _conv_mid_kernel,
        grid=(n1, nd),
        in_specs=[pl.BlockSpec((None, 2 * n2, 2 * n2), lambda k, j: (k, 0, 0)),
                  pl.BlockSpec((None, 2 * n2, 2 * n2), lambda k, j: (k, 0, 0)),
                  pl.BlockSpec((None, 2, n2, tc), lambda k, j: (k, 0, 0, j)),
                  pl.BlockSpec((None, 2 * n2, tc), lambda k, j: (k, 0, kf_col_off * nd + j))],
        out_specs=pl.BlockSpec((None, 2 * n2, tc), lambda k, j: (k, 0, j)),
        out_shape=jax.ShapeDtypeStruct((n1, 2 * n2, d), BF16),
        compiler_params=_cparams("arbitrary", "arbitrary"),
        name="hyena_conv_mid",
    )(g_fwd, g_inv, a.reshape(n1, 2, n2, d), kf)
    r, c = _iota2(2 * hn, 2 * n1)
    ec, es = _cis((r % hn) * (c // 2), n1)
    ro, ri = r // hn, c % 2
    w3 = jnp.where(ro == ri, ec, jnp.where(ro == 0, -es, es)).astype(BF16)
    blk = pl.BlockSpec((2 * hn, t1), lambda j: (0, j))
    return _left_mm(w3, bmid.reshape(2 * n1, cols),
                    grid=(cols // t1,),
                    w_spec=pl.BlockSpec((2 * hn, 2 * n1), lambda j: (0, 0)),
                    x_spec=pl.BlockSpec((2 * n1, t1), lambda j: (0, j)),
                    out_shape=jax.ShapeDtypeStruct((2 * hn, cols), BF16),
                    out_spec=blk,
                    extras=[(xg.reshape(2 * hn, cols), blk), (u.reshape(2 * hn, cols), blk),
                            (jnp.tile(skip.reshape(1, d), (1, m)), pl.BlockSpec((1, t1), lambda j: (0, 0)))],
                    epilogue=_gate_epilogue, name="hyena_conv_stage3").reshape(n_rows, d)


def _conv_dense_kernel(fk_ref, fz_ref, fzt_ref, k_ref, ss_ref, u_ref, xg_ref, skip_ref, o_ref):
    n = k_ref.shape[0]
    kf = jnp.dot(fk_ref[...], k_ref[...], preferred_element_type=F32) * ss_ref[...]
    u = u_ref[...]
    z = jnp.dot(fz_ref[...], u, preferred_element_type=F32)
    y = _complex_mul(z, kf, n)
    conv = jnp.dot(fzt_ref[...], y.astype(BF16), preferred_element_type=F32)
    o_ref[...] = (xg_ref[...].astype(F32) * (conv + skip_ref[...] * u.astype(F32))).astype(o_ref.dtype)


def _hyena_long_conv_dense(u, xg, k2u, sumsq, order, skip, seq_len):
    n_rows, d = u.shape
    assert n_rows == 2 * seq_len
    n = 2 * seq_len
    colscale = lax.rsqrt(sumsq + EPS) / n
    r, c = _iota2(2 * n, n)
    kc, ks = _cis((r % n) * c, n)
    fk = jnp.where(r < n, kc, -ks).astype(BF16)
    r, c = _iota2(2 * n, 2 * seq_len)
    zc, zs = _cis((r % n) * (c % seq_len), n)
    fz = jnp.where((r < n) == (c < seq_len), zc, jnp.where(r < n, zs, -zs))
    tc = min(d, 512)
    nd = d // tc
    full = lambda a: pl.BlockSpec(a.shape, lambda j: (0, 0))
    fz_b, fzt_b = fz.astype(BF16), fz.T.astype(BF16)
    blk = pl.BlockSpec((n_rows, tc), lambda j: (0, j))
    return pl.pallas_call(
        _conv_dense_kernel,
        grid=(nd,),
        in_specs=[full(fk), full(fz_b), full(fzt_b),
                  pl.BlockSpec((n, tc), lambda j: (0, order * nd + j)),
                  pl.BlockSpec((1, tc), lambda j: (0, order * nd + j)),
                  blk, blk, pl.BlockSpec((1, tc), lambda j: (0, j))],
        out_specs=blk,
        out_shape=jax.ShapeDtypeStruct((n_rows, d), BF16),
        compiler_params=_cparams("arbitrary"),
        name="hyena_conv_dense",
    )(fk, fz_b, fzt_b, k2u, colscale, u, xg, skip.reshape(1, d))


SPLIT_ROWS = 1024
N2_GROUP = 8
K1_GROUP = 8
SPLIT_TC = 512


def _slab_mm_kernel(w_ref, x_ref, *rest, epilogue):
    w = w_ref[...]
    for s in range(x_ref.shape[0]):
        acc = jnp.dot(w, x_ref[s].astype(BF16), preferred_element_type=F32)
        epilogue(acc, s, rest[:-1], rest[-1])


def _slab_plain(acc, s, extras, o_ref):
    o_ref[s] = acc.astype(o_ref.dtype)


def _slab_gate(acc, s, extras, o_ref):
    xg_ref, u_ref, skip_ref = extras
    o_ref[s] = (xg_ref[s] * (acc + skip_ref[...] * u_ref[s])).astype(o_ref.dtype)


def _slab_mm(w, x, out_dtype, slab_extras=(), row_extras=(), epilogue=_slab_plain, name="slab_mm"):
    x_arr, x_lead = x
    n2, k, c = x_arr.shape[len(x_lead):]
    tc = min(SPLIT_TC, c)

    def blk(r, lead=()):
        return pl.BlockSpec((None,) * len(lead) + (N2_GROUP, r, tc), lambda g, j: tuple(lead) + (g, 0, j))

    in_specs = [pl.BlockSpec(w.shape, lambda g, j: (0, 0)), blk(k, x_lead)]
    in_specs += [blk(a.shape[-2], lead) for a, lead in slab_extras]
    in_specs += [pl.BlockSpec((1, tc), lambda g, j: (0, j)) for _ in row_extras]
    return pl.pallas_call(
        functools.partial(_slab_mm_kernel, epilogue=epilogue),
        grid=(n2 // N2_GROUP, c // tc),
        in_specs=in_specs,
        out_specs=blk(w.shape[0]),
        out_shape=jax.ShapeDtypeStruct((n2, w.shape[0], c), out_dtype),
        compiler_params=_cparams("arbitrary", "arbitrary"),
        name=name,
    )(w, x_arr, *[a for a, _ in slab_extras], *row_extras)


def _stage2_regroup_in(a_ref, rin):
    def body(q, c):
        _rows_scatter(rin, q, a_ref[q].astype(F32), DFT_N2)
        return c
    lax.fori_loop(0, a_ref.shape[0], body, 0)


def _stage2_operand(rin, j):
    return jnp.concatenate([_slab_load(rin, 2 * j, DFT_N2), _slab_load(rin, 2 * j + 1, DFT_N2)], axis=0).astype(BF16)


def _kf_stage2_kernel(g_ref, a_ref, ss_ref, o_ref, rin):
    _stage2_regroup_in(a_ref, rin)
    for j in range(K1_GROUP):
        kf = jnp.dot(g_ref[j], _stage2_operand(rin, j), preferred_element_type=F32)
        o_ref[j] = (kf * ss_ref[...]).astype(o_ref.dtype)


def _conv_stage2_kernel(gf_ref, gi_ref, a_ref, kf_ref, o_ref, rin, rout):
    _stage2_regroup_in(a_ref, rin)
    for j in range(K1_GROUP):
        x = jnp.dot(gf_ref[j], _stage2_operand(rin, j), preferred_element_type=F32)
        y = _complex_mul(x, kf_ref[j], DFT_N2)
        b = jnp.dot(gi_ref[j], y.astype(BF16), preferred_element_type=F32)
        _slab_store(rout, 2 * j, b[:DFT_N2], DFT_N2)
        _slab_store(rout, 2 * j + 1, b[DFT_N2:], DFT_N2)

    def body(q, c):
        o_ref[q] = _rows_gather(rout, q, 2 * K1_GROUP, DFT_N2).astype(o_ref.dtype)
        return c
    lax.fori_loop(0, o_ref.shape[0], body, 0)


def _hyena_kf_split(k3, sumsq, tables):
    n2, n1, od = k3.shape
    n = n1 * n2
    colscale = lax.rsqrt(sumsq + EPS) / n
    r, c = _iota2(2 * n1, n1)
    fr, fs = _cis((r // 2) * c, n1)
    w1 = jnp.where(r % 2 == 0, fr, -fs).astype(BF16)
    a = _slab_mm(w1, (k3, ()), BF16, name="hyena_kf_stage1")
    tc = min(SPLIT_TC, od)
    return pl.pallas_call(
        _kf_stage2_kernel,
        grid=(n1 // K1_GROUP, od // tc),
        in_specs=[pl.BlockSpec((K1_GROUP, 2 * n2, 2 * n2), lambda k, j: (k, 0, 0)),
                  pl.BlockSpec((n2, 2 * K1_GROUP, tc), lambda k, j: (0, k, j)),
                  pl.BlockSpec((1, tc), lambda k, j: (0, j))],
        out_specs=pl.BlockSpec((K1_GROUP, 2 * n2, tc), lambda k, j: (k, 0, j)),
        out_shape=jax.ShapeDtypeStruct((n1, 2 * n2, od), BF16),
        scratch_shapes=[_slab_scratch(2 * K1_GROUP, n2, tc)],
        compiler_params=_cparams("arbitrary", "arbitrary"),
        name="hyena_kf_stage2",
    )(tables[0], a, colscale)


def _hyena_long_conv_split(v, xg, kf, kf_col_off, skip, tables):
    n2, m, d = v[0].shape[len(v[1]):]
    hn = m // 2
    n1 = 2 * hn
    g_fwd, g_inv = tables
    r, c = _iota2(2 * n1, 2 * hn)
    fr, fs = _cis((r // 2) * (c % hn), n1)
    ro, ri = r % 2, c // hn
    w1 = jnp.where(ro == ri, fr, jnp.where(ro == 0, fs, -fs)).astype(BF16)
    a = _slab_mm(w1, v, BF16, name="hyena_conv_stage1")
    tc = min(SPLIT_TC, d)
    nd = d // tc
    grp = pl.BlockSpec((n2, 2 * K1_GROUP, tc), lambda k, j: (0, k, j))
    tab = pl.BlockSpec((K1_GROUP, 2 * n2, 2 * n2), lambda k, j: (k, 0, 0))
    b = pl.pallas_call(
        _conv_stage2_kernel,
        grid=(n1 // K1_GROUP, nd),
        in_specs=[tab, tab, grp,
                  pl.BlockSpec((K1_GROUP, 2 * n2, tc), lambda k, j: (k, 0, kf_col_off * nd + j))],
        out_specs=grp,
        out_shape=jax.ShapeDtypeStruct((n2, 2 * n1, d), BF16),
        scratch_shapes=[_slab_scratch(2 * K1_GROUP, n2, tc), _slab_scratch(2 * K1_GROUP, n2, tc)],
        compiler_params=_cparams("arbitrary", "arbitrary"),
        name="hyena_conv_stage2",
    )(g_fwd, g_inv, a, kf)
    r, c = _iota2(2 * hn, 2 * n1)
    ec, es = _cis((r % hn) * (c // 2), n1)
    ro, ri = r // hn, c % 2
    w3 = jnp.where(ro == ri, ec, jnp.where(ro == 0, -es, es)).astype(BF16)
    return _slab_mm(w3, (b, ()), F32, slab_extras=[xg, v], row_extras=[skip.reshape(1, d)],
                    epilogue=_slab_gate, name="hyena_conv_stage3")


def _proj_split_kernel(x_ref, w_ref, b_ref, res_ref, gate_ref, o_ref, wbf_ref, acc_ref, rg_ref):
    @pl.when(pl.program_id(1) == 0)
    def _():
        wbf_ref[...] = w_ref[...].astype(BF16)

    n2, g, k = x_ref.shape
    x = x_ref[...].reshape(n2 * g, k).astype(BF16)
    acc_ref[...] = jnp.dot(x, wbf_ref[...], preferred_element_type=F32) + b_ref[...]

    def body(q, c):
        _rows_scatter(rg_ref, q, acc_ref[pl.ds(pl.multiple_of(q * g, g), g), :], n2)
        return c
    lax.fori_loop(0, n2, body, 0)
    for s in range(g):
        rows = slice(s * n2, (s + 1) * n2)
        o_ref[rows, :] = res_ref[rows, :] + gate_ref[...] * _slab_load(rg_ref, s, n2)


def _proj_residual_split(v3, w, bias, x_all, mod, chunk, seq_len, tn=512):
    n2, m, k = v3.shape
    d = x_all.shape[1]
    tn = min(tn, d)
    nd = d // tn
    g = SPLIT_ROWS // n2
    tm = n2 * g
    seq_tiles = seq_len // tm
    mod3 = mod.reshape(MOD_ROWS, 1, -1)
    res_spec = pl.BlockSpec((tm, tn), lambda j, i: (i, j))
    return pl.pallas_call(
        _proj_split_kernel,
        grid=(nd, m // g),
        in_specs=[pl.BlockSpec((n2, g, k), lambda j, i: (0, i, 0)),
                  pl.BlockSpec((k, tn), lambda j, i: (0, j)),
                  pl.BlockSpec((1, tn), lambda j, i: (0, j)),
                  res_spec,
                  pl.BlockSpec((None, 1, tn), lambda j, i: (i // seq_tiles, 0, chunk * nd + j))],
        out_specs=res_spec,
        out_shape=jax.ShapeDtypeStruct(x_all.shape, F32),
        scratch_shapes=[pltpu.VMEM((k, tn), BF16), pltpu.VMEM((tm, tn), F32), _slab_scratch(g, n2, tn)],
        input_output_aliases={3: 0},
        compiler_params=_cparams("arbitrary", "arbitrary"),
        name="hyena_out_lat",
    )(v3, w, bias.reshape(1, d), x_all, mod3)


def _layer_hyena(x_all, mod, rows, g, prm, w_out, b_out):
    (w_in, b_in, conv_w, conv_b, f_w1, f_b1, f_w2, f_b2, f_w3, f_freq, skip) = prm
    d = w_in.shape[0]
    assert rows.batch == 2, "the long convolution carries the two batch rows as one complex sequence"
    h = _norm_mod(x_all, g, mod, 0, rows, BF16)
    fprm = (f_w1, f_b1, f_w2, f_b2, f_w3, f_freq)
    zc = _hyena_in(h, w_in, b_in, conv_w, conv_b, rows, rows.ctx_off, rows.n_ctx, rows.ctx_len)
    k2u, sumsq = _hyena_filters(rows.ctx_len, d, *fprm)
    vc = _hyena_long_conv_dense(zc[2], zc[0], k2u, sumsq, 0, skip[0], rows.ctx_len)
    vc = _hyena_long_conv_dense(vc, zc[1], k2u, sumsq, 1, skip[1], rows.ctx_len)
    x_all = _proj_residual(vc, w_out, b_out, x_all, mod, 2, rows, rows.ctx_off, name="hyena_out_ctx")
    seq = rows.seq
    split_rows = _Rows(rows.batch, rows.ctx_len, seq, SPLIT_ROWS)
    zl = _hyena_in(h, w_in, b_in, conv_w, conv_b, split_rows, 0, rows.n_lat, seq, time_split=True)
    k3, sumsq = _hyena_filters(seq, d, *fprm, tp=2 * SPLIT_ROWS, tc=SPLIT_TC, time_split=True)
    tables = _stage2_tables(2 * seq, 2 * seq // DFT_N2)
    kf = _hyena_kf_split(k3, sumsq, tables)
    v = _hyena_long_conv_split((zl, (2,)), (zl, (0,)), kf, 0, skip[0], tables)
    v = _hyena_long_conv_split((v, ()), (zl, (1,)), kf, 1, skip[1], tables)
    return _proj_residual_split(v, w_out, b_out, x_all, mod, 2, seq)


def _layer_pool(x_all, mod, rows, g, w_grp, scale):
    h = _norm_mod(x_all, g, mod, 0, rows, F32)
    x_all = _pool_mixer(h, w_grp, scale, x_all, mod, 2, rows, rows.ctx_off, rows.n_ctx, rows.ctx_len)
    return _pool_mixer(h, w_grp, scale, x_all, mod, 2, rows, rows.lat_off, rows.n_lat, rows.seq)


def _final_norm_kernel(x_ref, g_ref, o_ref):
    x = x_ref[...]
    o_ref[...] = x * lax.rsqrt(jnp.mean(x * x, axis=-1, keepdims=True) + EPS) * g_ref[...]


def _final_norm(x_all, g, rows):
    tm = rows.tile
    d = x_all.shape[1]
    return pl.pallas_call(
        _final_norm_kernel,
        grid=(rows.n_lat // tm,),
        in_specs=[pl.BlockSpec((tm, d), lambda i: (i + rows.lat_off, 0)),
                  pl.BlockSpec((1, d), lambda i: (0, 0))],
        out_specs=pl.BlockSpec((tm, d), lambda i: (i, 0)),
        out_shape=jax.ShapeDtypeStruct((rows.n_lat, d), F32),
        compiler_params=_cparams("arbitrary"),
        name="final_norm",
    )(x_all, g.reshape(1, d))


def kernel(x, c, ctx, c_ctx, ada_w, ada_b, norm_g, final_g, hy_w_in, hy_b_in, hy_conv_w, hy_conv_b, hy_f_w1, hy_f_b1, hy_f_w2, hy_f_b2, hy_f_w3, hy_f_freq, hy_skip, hy_w_out, hy_b_out, fn_w_out, fn_b_out, pl_w, pl_scale, mla_w_dq, mla_g_q, mla_w_uq, mla_w_dkv, mla_g_kv, mla_w_ukv, mla_w_o, moe_w_group, moe_b_group, moe_w_expert, moe_b_expert, moe_w1, moe_w3, moe_w2):
    batch, seq, d = x.shape
    ctx_len = ctx.shape[1]
    depth = ada_w.shape[0]
    rows = _Rows(batch, ctx_len, seq, ROW_TILE)
    mod = _ada_mod(c, c_ctx, ada_w, ada_b)
    x_all = _pack_rows(x, ctx, rows)
    for i in range(depth):
        kind, j = i % N_MIXERS, i // N_MIXERS
        m = mod[i]
        g1 = norm_g[i, 0]
        if kind == 0:
            prm = (hy_w_in[j], hy_b_in[j], hy_conv_w[j], hy_conv_b[j], hy_f_w1[j], hy_f_b1[j], hy_f_w2[j],
                   hy_f_b2[j], hy_f_w3[j], hy_f_freq[j], hy_skip[j])
            x_all = _layer_hyena(x_all, m, rows, g1, prm, hy_w_out[j], hy_b_out[j])
        elif kind == 1:
            x_all = _layer_fnet(x_all, m, rows, g1, fn_w_out[j], fn_b_out[j])
        elif kind == 2:
            x_all = _layer_pool(x_all, m, rows, g1, pl_w[j], pl_scale[j])
        else:
            x_all = _layer_mla(x_all, m, rows, g1, mla_w_dq[j], mla_g_q[j], mla_w_uq[j], mla_w_dkv[j],
                               mla_g_kv[j], mla_w_ukv[j], mla_w_o[j], update_ctx=i < depth - 1)
        h, eid, wts = _norm_router(x_all, norm_g[i, 1], m, 3, rows, moe_w_group[i], moe_b_group[i],
                                   moe_w_expert[i], moe_b_expert[i])
        y = _moe_experts(h, eid, wts, moe_w1, moe_w3, moe_w2, i)
        x_all = _moe_combine(x_all, y, m, 5, rows)
    return _final_norm(x_all, final_g, rows).reshape(batch, seq, d)
```

```python
import functools
import math

import numpy as np
import jax
import jax.numpy as jnp
from jax import lax
from jax.experimental import pallas as pl
from jax.experimental.pallas import tpu as pltpu

F32 = jnp.float32
BF16 = jnp.bfloat16
HIGHEST = lax.Precision.HIGHEST

EPS = 1e-6
LANES = 128
MOD_ROWS = 8
VMEM_LIMIT = 56 * 1024 * 1024

N_MIXERS = 4
HYENA_ORDER = 2
FILTER_EMB = 33
DECAY_TARGET = 1e-2
FAST_DECAY_PCT = 0.3
SLOW_DECAY_PCT = 1.5
MOD_SHIFT = 0.0
FNET_GROUPS = 4
POOL_WINDOWS = (2, 4, 8, 16)
MLA_HEADS = 16
QK_NOPE = 128
QK_ROPE = 64
V_DIM = 128
GRID_W = 64
ROPE_THETA = 10000.0
N_GROUPS = 4
EXPERTS_PER_GROUP = 8
TOPK_EXPERT = 2
ROW_TILE = 256
PROJ_ROWS = 512
MOE_ROWS = 256


def _cparams(*sem):
    return pltpu.CompilerParams(dimension_semantics=sem, vmem_limit_bytes=VMEM_LIMIT)


def _ada_kernel(st_ref, w_ref, b_ref, o_ref, *, nrows):
    s = st_ref[...]
    s = s * jax.nn.sigmoid(s)
    w = w_ref[...]
    o_ref[...] = jnp.broadcast_to(b_ref[...], o_ref.shape)
    for r in range(nrows):
        o_ref[r:r + 1, :] = jnp.sum(s[:, r:r + 1] * w, axis=0, keepdims=True) + b_ref[...]


def _ada_mod(c, c_ctx, ada_w, ada_b):
    depth, d, n = ada_w.shape
    nrows = c.shape[0] + 1
    st = jnp.zeros((d, MOD_ROWS), F32).at[:, :nrows - 1].set(c.T).at[:, nrows - 1].set(c_ctx)
    tn = 1024 if n % 1024 == 0 else n
    return pl.pallas_call(
        functools.partial(_ada_kernel, nrows=nrows),
        grid=(depth, n // tn),
        in_specs=[pl.BlockSpec((d, MOD_ROWS), lambda l, j: (0, 0)),
                  pl.BlockSpec((None, d, tn), lambda l, j: (l, 0, j)),
                  pl.BlockSpec((None, 1, tn), lambda l, j: (l, 0, j))],
        out_specs=pl.BlockSpec((None, MOD_ROWS, tn), lambda l, j: (l, 0, j)),
        out_shape=jax.ShapeDtypeStruct((depth, MOD_ROWS, n), F32),
        compiler_params=_cparams("arbitrary", "arbitrary"),
        name="ada_mod",
    )(st, ada_w, ada_b.reshape(depth, 1, n))


class _Rows:
    def __init__(self, batch, ctx_len, seq, tile):
        assert seq % tile == 0
        self.batch, self.ctx_len, self.seq, self.tile = batch, ctx_len, seq, tile
        self.n_ctx = batch * ctx_len
        self.n_lat = batch * seq
        self.n_all = self.n_ctx + self.n_lat
        self.lat_blocks = seq // tile
        self.lat_off = 0
        self.ctx_off = self.n_lat // tile

    def mod_row(self, i):
        return jnp.where(i >= self.ctx_off, self.batch, i // self.lat_blocks)


def _pack_rows(x, ctx, rows):
    d = x.shape[-1]
    pad = -rows.n_all % SPLIT_ROWS
    return jnp.concatenate([x.reshape(-1, d), ctx.reshape(-1, d), jnp.zeros((pad, d), x.dtype)], axis=0)


def _rms_mod(x, g, sh, sc):
    y = x * lax.rsqrt(jnp.mean(x * x, axis=-1, keepdims=True) + EPS) * g
    return y * (1.0 + sc) + sh


def _norm_mod_kernel(x_ref, g_ref, sh_ref, sc_ref, o_ref):
    o_ref[...] = _rms_mod(x_ref[...], g_ref[...], sh_ref[...], sc_ref[...]).astype(o_ref.dtype)


def _norm_mod(x_all, g, mod, chunk, rows, out_dtype):
    n, d = rows.n_all, x_all.shape[1]
    tm = rows.tile
    mod3 = mod.reshape(MOD_ROWS, 1, -1)
    return pl.pallas_call(
        _norm_mod_kernel,
        grid=(n // tm,),
        in_specs=[pl.BlockSpec((tm, d), lambda i: (i, 0)),
                  pl.BlockSpec((1, d), lambda i: (0, 0)),
                  pl.BlockSpec((None, 1, d), lambda i: (rows.mod_row(i), 0, chunk)),
                  pl.BlockSpec((None, 1, d), lambda i: (rows.mod_row(i), 0, chunk + 1))],
        out_specs=pl.BlockSpec((tm, d), lambda i: (i, 0)),
        out_shape=jax.ShapeDtypeStruct((n, d), out_dtype),
        compiler_params=_cparams("arbitrary"),
        name="norm_mod",
    )(x_all, g.reshape(1, d), mod3, mod3)


def _norm_router_kernel(x_ref, g_ref, sh_ref, sc_ref, wr_ref, br_ref, h_ref, eid_ref, wt_ref, whi_ref, wlo_ref,
                        *, n_groups, per_group):
    h = _rms_mod(x_ref[...], g_ref[...], sh_ref[...], sc_ref[...])
    h_ref[...] = h.astype(h_ref.dtype)

    @pl.when(pl.program_id(0) == 0)
    def _():
        w = wr_ref[...]
        w_hi = w.astype(BF16)
        whi_ref[...] = w_hi
        wlo_ref[...] = (w - w_hi.astype(F32)).astype(BF16)

    h_hi = h.astype(BF16)
    h_lo = (h - h_hi.astype(F32)).astype(BF16)
    logits = (jnp.dot(h_hi, whi_ref[...], preferred_element_type=F32)
              + jnp.dot(h_lo, whi_ref[...], preferred_element_type=F32)
              + jnp.dot(h_hi, wlo_ref[...], preferred_element_type=F32)) + br_ref[...]
    lane = lax.broadcasted_iota(jnp.int32, logits.shape, 1).astype(F32)
    neg = -jnp.inf
    gl = jnp.where(lane < n_groups, logits, neg)
    gmax = jnp.max(gl, axis=-1, keepdims=True)
    p_grp = 1.0 / jnp.sum(jnp.exp(gl - gmax), axis=-1, keepdims=True)
    g_idx = jnp.min(jnp.where(gl == gmax, lane, float(LANES)), axis=-1, keepdims=True)
    lo = n_groups + g_idx * per_group
    el = jnp.where((lane >= lo) & (lane < lo + per_group), logits, neg)
    e1 = jnp.max(el, axis=-1, keepdims=True)
    i1 = jnp.min(jnp.where(el == e1, lane, float(LANES)), axis=-1, keepdims=True)
    el2 = jnp.where(lane == i1, neg, el)
    e2 = jnp.max(el2, axis=-1, keepdims=True)
    i2 = jnp.min(jnp.where(el2 == e2, lane, float(LANES)), axis=-1, keepdims=True)
    r = jnp.exp(e2 - e1)
    w1 = p_grp / (1.0 + r)
    w2 = p_grp * r / (1.0 + r)
    eid = jnp.where(lane == 0, i1 - n_groups, jnp.where(lane == 1, i2 - n_groups, 0.0))
    eid_ref[...] = eid.astype(jnp.int32)
    wt_ref[...] = jnp.where(lane == 0, w1, jnp.where(lane == 1, w2, 0.0))


def _norm_router(x_all, g, mod, chunk, rows, w_group, b_group, w_expert, b_expert):
    n, d = rows.n_all, x_all.shape[1]
    tm = rows.tile
    n_groups = w_group.shape[1]
    n_experts = w_expert.shape[1]
    wr = jnp.zeros((d, LANES), F32).at[:, :n_groups].set(w_group).at[:, n_groups:n_groups + n_experts].set(w_expert)
    br = jnp.zeros((1, LANES), F32).at[0, :n_groups].set(b_group).at[0, n_groups:n_groups + n_experts].set(b_expert)
    mod3 = mod.reshape(MOD_ROWS, 1, -1)
    h, eid, wts = pl.pallas_call(
        functools.partial(_norm_router_kernel, n_groups=n_groups, per_group=n_experts // n_groups),
        grid=(n // tm,),
        in_specs=[pl.BlockSpec((tm, d), lambda i: (i, 0)),
                  pl.BlockSpec((1, d), lambda i: (0, 0)),
                  pl.BlockSpec((None, 1, d), lambda i: (rows.mod_row(i), 0, chunk)),
                  pl.BlockSpec((None, 1, d), lambda i: (rows.mod_row(i), 0, chunk + 1)),
                  pl.BlockSpec((d, LANES), lambda i: (0, 0)),
                  pl.BlockSpec((1, LANES), lambda i: (0, 0))],
        out_specs=[pl.BlockSpec((tm, d), lambda i: (i, 0)),
                   pl.BlockSpec((tm, LANES), lambda i: (i, 0)),
                   pl.BlockSpec((tm, LANES), lambda i: (i, 0))],
        out_shape=[jax.ShapeDtypeStruct((n, d), F32),
                   jax.ShapeDtypeStruct((n, LANES), jnp.int32),
                   jax.ShapeDtypeStruct((n, LANES), F32)],
        scratch_shapes=[pltpu.VMEM((d, LANES), BF16), pltpu.VMEM((d, LANES), BF16)],
        compiler_params=_cparams("arbitrary"),
        name="norm_router",
    )(x_all, g.reshape(1, d), mod3, mod3, wr, br)
    return h, eid[:, :TOPK_EXPERT], wts[:, :TOPK_EXPERT]


def _moe_dispatch(eid, wts, n_experts, tm):
    n_tok = eid.shape[0]
    n_assign = eid.size
    e_flat = eid.reshape(-1)
    order = jnp.argsort(e_flat).astype(jnp.int32)
    counts = jnp.sum((e_flat[:, None] == jnp.arange(n_experts, dtype=jnp.int32)[None, :]).astype(jnp.int32), axis=0)
    padded = (counts + tm - 1) // tm * tm
    start = jnp.cumsum(counts) - counts
    pend = jnp.cumsum(padded)
    pstart = pend - padded
    n_blocks = (n_assign + n_experts * (tm - 1) + tm - 1) // tm
    blk_start = jnp.arange(n_blocks, dtype=jnp.int32) * tm
    blk_e = jnp.minimum(jnp.sum((pend[None, :] <= blk_start[:, None]).astype(jnp.int32), axis=1), n_experts - 1)
    blk_rows = jnp.clip(pstart[blk_e] + counts[blk_e] - blk_start, 0, tm).astype(jnp.int32)
    row = lax.broadcasted_iota(jnp.int32, (n_blocks, tm), 1)
    slot_e = blk_e[:, None]
    src = start[slot_e] + (blk_start[:, None] + row - pstart[slot_e])
    valid = row < blk_rows[:, None]
    asg = order[jnp.clip(src, 0, n_assign - 1)]
    tok = asg // TOPK_EXPERT
    buf_tok = jnp.where(valid, tok, 0)
    dummy = TOPK_EXPERT * n_tok + (jnp.arange(n_blocks, dtype=jnp.int32)[:, None] % 2) * tm + row
    buf_asg = jnp.where(valid, (asg % TOPK_EXPERT) * n_tok + tok, dummy)
    buf_w = jnp.where(valid, wts.reshape(-1)[asg], 0.0)
    return (buf_tok.reshape(n_blocks, 1, tm), buf_asg.reshape(n_blocks, 1, tm),
            buf_w.reshape(n_blocks, tm, 1), blk_e.astype(jnp.int32), blk_rows)


def _moe_kernel(blk_e_ref, blk_rows_ref, tok_ref, tok_next_ref, asg_ref, roww_ref, h_hbm,
                w1_ref, w3_ref, w2_ref, y_hbm, xbuf, ybuf, w1b, w3b, w2b, gsem, ssem):
    b = pl.program_id(0)
    nb = pl.num_programs(0)
    slot = b % 2
    other = 1 - slot

    tm = xbuf.shape[1]

    def start_gather(ids_ref, s):
        for r in range(tm):
            pltpu.make_async_copy(h_hbm.at[pl.ds(ids_ref[0, r], 1)], xbuf.at[s, pl.ds(r, 1)], gsem.at[s]).start(priority=r % 2)

    def wait_gather(s):
        pltpu.make_async_copy(h_hbm.at[pl.ds(0, tm)], xbuf.at[s], gsem.at[s]).wait()

    def start_scatter(s):
        for r in range(tm):
            pltpu.make_async_copy(ybuf.at[s, pl.ds(r, 1)], y_hbm.at[pl.ds(asg_ref[0, r], 1)], ssem.at[s]).start(priority=r % 2)

    def wait_scatter(s):
        pltpu.make_async_copy(ybuf.at[s], y_hbm.at[pl.ds(0, tm)], ssem.at[s]).wait()

    def used(blk):
        return blk_rows_ref[jnp.clip(blk, 0, nb - 1)] > 0

    @pl.when(b == 0)
    def _():
        ybuf[...] = jnp.zeros(ybuf.shape, ybuf.dtype)
        n_real = y_hbm.shape[0] - 2 * tm
        for s in range(2):
            fill = pltpu.make_async_copy(ybuf.at[s], y_hbm.at[pl.ds(n_real + s * tm, tm)], ssem.at[s])
            fill.start()
            fill.wait()

    @pl.when((b == 0) & used(b))
    def _():
        start_gather(tok_ref, slot)

    @pl.when((b + 1 < nb) & used(b + 1))
    def _():
        start_gather(tok_next_ref, other)

    @pl.when((b >= 2) & used(b - 2))
    def _():
        wait_scatter(slot)

    @pl.when(used(b))
    def _():
        wait_gather(slot)

        @pl.when((b == 0) | (blk_e_ref[b] != blk_e_ref[jnp.maximum(b - 1, 0)]))
        def _():
            w1b[...] = w1_ref[...].astype(BF16)
            w3b[...] = w3_ref[...].astype(BF16)
            w2b[...] = w2_ref[...].astype(BF16)

        x = xbuf[slot].astype(BF16)
        a = jnp.dot(x, w1b[...], preferred_element_type=F32)
        g = jnp.dot(x, w3b[...], preferred_element_type=F32)
        hm = (a * jax.nn.sigmoid(a) * g).astype(BF16)
        ybuf[slot] = jnp.dot(hm, w2b[...], preferred_element_type=F32) * roww_ref[...]
        start_scatter(slot)

    @pl.when(b == nb - 1)
    def _():
        @pl.when((nb >= 2) & used(b - 1))
        def _():
            wait_scatter(other)

        @pl.when(used(b))
        def _():
            wait_scatter(slot)


def _moe_experts(h, eid, wts, w1, w3, w2, layer):
    n, d = h.shape
    n_experts, de = w1.shape[1], w1.shape[3]
    tm = MOE_ROWS
    tok, asg, roww, blk_e, blk_rows = _moe_dispatch(eid, wts, n_experts, tm)
    n_blocks = tok.shape[0]
    smem_blk = lambda f: pl.BlockSpec((None, 1, tm), f, memory_space=pltpu.SMEM)
    grid_spec = pltpu.PrefetchScalarGridSpec(
        num_scalar_prefetch=2,
        grid=(n_blocks,),
        in_specs=[smem_blk(lambda b, be, br: (b, 0, 0)),
                  smem_blk(lambda b, be, br: (jnp.minimum(b + 1, n_blocks - 1), 0, 0)),
                  smem_blk(lambda b, be, br: (b, 0, 0)),
                  pl.BlockSpec((None, tm, 1), lambda b, be, br: (b, 0, 0)),
                  pl.BlockSpec(memory_space=pl.ANY),
                  pl.BlockSpec((None, None, d, de), lambda b, be, br: (layer, be[b], 0, 0)),
                  pl.BlockSpec((None, None, d, de), lambda b, be, br: (layer, be[b], 0, 0)),
                  pl.BlockSpec((None, None, de, d), lambda b, be, br: (layer, be[b], 0, 0))],
        out_specs=pl.BlockSpec(memory_space=pl.ANY),
        scratch_shapes=[pltpu.VMEM((2, tm, d), F32), pltpu.VMEM((2, tm, d), F32),
                        pltpu.VMEM((d, de), BF16), pltpu.VMEM((d, de), BF16), pltpu.VMEM((de, d), BF16),
                        pltpu.SemaphoreType.DMA((2,)), pltpu.SemaphoreType.DMA((2,))],
    )
    return pl.pallas_call(
        _moe_kernel,
        grid_spec=grid_spec,
        out_shape=jax.ShapeDtypeStruct((n * TOPK_EXPERT + 2 * tm, d), F32),
        compiler_params=_cparams("arbitrary"),
        name="moe_experts",
    )(blk_e, blk_rows, tok, tok, asg, roww, h, w1, w3, w2)


def _moe_combine_kernel(x_ref, *rest, follow):
    y_refs, gate_ref, rest = rest[:TOPK_EXPERT], rest[TOPK_EXPERT], rest[TOPK_EXPERT + 1:]
    acc = y_refs[0][...]
    for y_ref in y_refs[1:]:
        acc = acc + y_ref[...]
    x_new = x_ref[...] + gate_ref[...] * acc
    if follow == "next_norm":
        g_ref, sh_ref, sc_ref, o_ref, h_ref = rest
        o_ref[...] = x_new
        h_ref[...] = _rms_mod(x_new, g_ref[...], sh_ref[...], sc_ref[...]).astype(h_ref.dtype)
    else:
        g_ref, o_ref = rest
        o_ref[...] = x_new * lax.rsqrt(jnp.mean(x_new * x_new, axis=-1, keepdims=True) + EPS) * g_ref[...]


def _moe_combine(x_all, y, mod, chunk, rows, *, next_norm=None, final_g=None):
    n, d = rows.n_all, x_all.shape[1]
    tm = rows.tile
    mod3 = mod.reshape(MOD_ROWS, 1, -1)
    nblk = n // tm
    row_blk = pl.BlockSpec((tm, d), lambda i: (i, 0))
    vec = pl.BlockSpec((1, d), lambda i: (0, 0))
    y_specs = [pl.BlockSpec((tm, d), functools.partial(lambda i, k: (i + k * nblk, 0), k=k)) for k in range(TOPK_EXPERT)]
    in_specs = [row_blk] + y_specs + [pl.BlockSpec((None, 1, d), lambda i: (rows.mod_row(i), 0, chunk))]
    args = [x_all] + [y] * TOPK_EXPERT + [mod3]
    if next_norm is not None:
        g, mod_next, out_dtype = next_norm
        modn = mod_next.reshape(MOD_ROWS, 1, -1)
        in_specs += [vec, pl.BlockSpec((None, 1, d), lambda i: (rows.mod_row(i), 0, 0)),
                     pl.BlockSpec((None, 1, d), lambda i: (rows.mod_row(i), 0, 1))]
        args += [g.reshape(1, d), modn, modn]
        return pl.pallas_call(
            functools.partial(_moe_combine_kernel, follow="next_norm"),
            grid=(nblk,),
            in_specs=in_specs,
            out_specs=[row_blk, row_blk],
            out_shape=[jax.ShapeDtypeStruct(x_all.shape, F32), jax.ShapeDtypeStruct((n, d), out_dtype)],
            input_output_aliases={0: 0},
            compiler_params=_cparams("arbitrary"),
            name="moe_combine",
        )(*args)
    return pl.pallas_call(
        functools.partial(_moe_combine_kernel, follow="final_norm"),
        grid=(rows.n_lat // tm,),
        in_specs=in_specs + [vec],
        out_specs=row_blk,
        out_shape=jax.ShapeDtypeStruct((rows.n_lat, d), F32),
        compiler_params=_cparams("arbitrary"),
        name="moe_combine_final",
    )(*args, final_g.reshape(1, d))


def _mm_kernel(*refs, has_bias, n_extra, epilogue):
    x_ref, w_ref = refs[0], refs[1]
    pos = 2
    b_ref = None
    if has_bias:
        b_ref = refs[pos]
        pos += 1
    extras = refs[pos:pos + n_extra]
    outs = refs[pos + n_extra:-1]
    wbf = refs[-1]

    @pl.when(pl.program_id(1) == 0)
    def _():
        wbf[...] = w_ref[...].astype(BF16)

    acc = jnp.dot(x_ref[...].astype(BF16), wbf[...], preferred_element_type=F32)
    if has_bias:
        acc = acc + b_ref[...]
    epilogue(acc, extras, outs)


def _store_epilogue(acc, extras, outs):
    outs[0][...] = acc.astype(outs[0].dtype)


def _matmul(x, w, *, w_lead=(), bias=None, tm, tn, m_rows=None, x_row_off=0, extras=(), outs,
            epilogue=_store_epilogue, aliases=None, name="matmul"):
    k = x.shape[1]
    n = w.shape[-1]
    m_rows = x.shape[0] if m_rows is None else m_rows
    assert m_rows % tm == 0 and n % tn == 0 and w.shape[-2] == k
    lead = tuple(w_lead)
    in_specs = [pl.BlockSpec((tm, k), lambda j, i: (i + x_row_off, 0)),
                pl.BlockSpec((None,) * len(lead) + (k, tn), lambda j, i: lead + (0, j))]
    args = [x, w]
    if bias is not None:
        in_specs.append(pl.BlockSpec((1, tn), lambda j, i: (0, j)))
        args.append(bias.reshape(1, n))
    for arr, spec in extras:
        in_specs.append(spec)
        args.append(arr)
    return pl.pallas_call(
        functools.partial(_mm_kernel, has_bias=bias is not None, n_extra=len(extras), epilogue=epilogue),
        grid=(n // tn, m_rows // tm),
        in_specs=in_specs,
        out_specs=[spec for _, spec in outs],
        out_shape=[shape for shape, _ in outs],
        scratch_shapes=[pltpu.VMEM((k, tn), BF16)],
        input_output_aliases=aliases or {},
        compiler_params=_cparams("arbitrary", "arbitrary"),
        name=name,
    )(*args)


def _residual_epilogue(acc, extras, outs):
    res_ref, gate_ref = extras
    outs[0][...] = res_ref[...] + gate_ref[...] * acc


def _proj_residual(xin, w, bias, x_all, mod, chunk, rows, row_off, *, w_lead=(), tn=512, name="proj_residual"):
    tm = rows.tile
    d = x_all.shape[1]
    tn = min(tn, d)
    mod3 = mod.reshape(MOD_ROWS, 1, -1)
    nd = d // tn
    res_spec = pl.BlockSpec((tm, tn), lambda j, i: (i + row_off, j))
    gate_spec = pl.BlockSpec((None, 1, tn), lambda j, i: (rows.mod_row(i + row_off), 0, chunk * nd + j))
    n_extra_before = 2 + (bias is not None)
    out, = _matmul(xin, w, w_lead=w_lead, bias=bias, tm=tm, tn=tn,
                   extras=[(x_all, res_spec), (mod3, gate_spec)],
                   outs=[(jax.ShapeDtypeStruct(x_all.shape, F32), res_spec)],
                   epilogue=_residual_epilogue, aliases={n_extra_before: 0}, name=name)
    return out


POOL_HALO = 8


def _pool_kernel(prev_ref, main_ref, next_ref, w_ref, sc_ref, res_ref, gate_ref, o_ref, ext_ref, wbf_ref,
                 *, tm, seq_len, windows):
    i = pl.program_id(0)
    seq_tiles = seq_len // tm
    t_in_seq = i % seq_tiles

    @pl.when(i == 0)
    def _():
        wbf_ref[...] = w_ref[...].astype(BF16)

    zero_halo = jnp.zeros(prev_ref.shape, F32)
    ext_ref[0:POOL_HALO, :] = jnp.where(t_in_seq == 0, zero_halo, prev_ref[...])
    ext_ref[POOL_HALO:POOL_HALO + tm, :] = main_ref[...]
    ext_ref[POOL_HALO + tm:2 * POOL_HALO + tm, :] = jnp.where(t_in_seq == seq_tiles - 1, zero_halo, next_ref[...])

    pos = t_in_seq * tm + lax.broadcasted_iota(jnp.int32, (tm, 1), 0)
    cg = main_ref.shape[1] // len(windows)
    for gi, win in enumerate(windows):
        half = win // 2
        cols = slice(gi * cg, (gi + 1) * cg)
        s = ext_ref[pl.ds(POOL_HALO - half, tm), cols]
        for dlt in range(-half + 1, half):
            s = s + ext_ref[pl.ds(POOL_HALO + dlt, tm), cols]
        cnt = jnp.minimum(pos + half, seq_len) - jnp.maximum(pos - half, 0)
        pooled = s / cnt.astype(F32) - main_ref[:, cols]
        y = jnp.dot(pooled.astype(BF16), wbf_ref[gi], preferred_element_type=F32) * sc_ref[:, cols]
        o_ref[:, cols] = res_ref[:, cols] + gate_ref[:, cols] * y


def _pool_mixer(h, w_grp, scale, x_all, mod, chunk, rows, row_off, n_rows, seq_len):
    tm = rows.tile
    d = x_all.shape[1]
    assert max(POOL_WINDOWS) // 2 <= POOL_HALO and tm % POOL_HALO == 0 and seq_len % tm == 0
    hb = tm // POOL_HALO
    last_hblk = h.shape[0] // POOL_HALO - 1
    mod3 = mod.reshape(MOD_ROWS, 1, -1)
    main_spec = pl.BlockSpec((tm, d), lambda i: (i + row_off, 0))
    return pl.pallas_call(
        functools.partial(_pool_kernel, tm=tm, seq_len=seq_len, windows=POOL_WINDOWS),
        grid=(n_rows // tm,),
        in_specs=[pl.BlockSpec((POOL_HALO, d), lambda i: (jnp.maximum((i + row_off) * hb - 1, 0), 0)),
                  main_spec,
                  pl.BlockSpec((POOL_HALO, d), lambda i: (jnp.minimum((i + row_off + 1) * hb, last_hblk), 0)),
                  pl.BlockSpec(w_grp.shape, lambda i: (0, 0, 0)),
                  pl.BlockSpec((1, d), lambda i: (0, 0)),
                  main_spec,
                  pl.BlockSpec((None, 1, d), lambda i: (rows.mod_row(i + row_off), 0, chunk))],
        out_specs=main_spec,
        out_shape=jax.ShapeDtypeStruct(x_all.shape, F32),
        scratch_shapes=[pltpu.VMEM((tm + 2 * POOL_HALO, d), F32), pltpu.VMEM(w_grp.shape, BF16)],
        input_output_aliases={5: 0},
        compiler_params=_cparams("arbitrary"),
        name="pool_mixer",
    )(h, h, h, w_grp, scale.reshape(1, d), x_all, mod3)


DFT_N2 = 128


def _cis(num, den):
    ang = (num % den).astype(F32) * (2.0 * math.pi / den)
    return jnp.cos(ang), jnp.sin(ang)


def _iota2(n_rows, n_cols):
    return (lax.broadcasted_iota(jnp.int32, (n_rows, n_cols), 0), lax.broadcasted_iota(jnp.int32, (n_rows, n_cols), 1))


def _slab_pitch(rows):
    return rows + 8


def _slab_scratch(g, rows, width):
    return pltpu.VMEM((width // LANES, g * _slab_pitch(rows), LANES), F32)


def _slab_store(scr, s, val, rows):
    p = _slab_pitch(rows)
    for l in range(scr.shape[0]):
        scr[l, s * p:s * p + rows, :] = val[:, l * LANES:(l + 1) * LANES]


def _slab_load(scr, s, rows):
    p = _slab_pitch(rows)
    return jnp.concatenate([scr[l, s * p:s * p + rows, :] for l in range(scr.shape[0])], axis=1)


def _rows_gather(scr, q, g, rows):
    p = _slab_pitch(rows)
    return jnp.concatenate([scr[l, pl.ds(q, g, stride=p), :] for l in range(scr.shape[0])], axis=1)


def _rows_scatter(scr, q, val, rows):
    p = _slab_pitch(rows)
    for l in range(scr.shape[0]):
        scr[l, pl.ds(q, val.shape[0], stride=p), :] = val[:, l * LANES:(l + 1) * LANES]


def _left_mm_kernel(w_ref, x_ref, *rest, epilogue):
    x = x_ref[...]
    x = x.reshape(-1, x.shape[-1])
    acc = jnp.dot(w_ref[...], x.astype(BF16), preferred_element_type=F32)
    epilogue(acc, rest[:-1], rest[-1])


def _scaled_store(scale):
    def epilogue(acc, extras, o_ref):
        o_ref[...] = (acc * scale).reshape(o_ref.shape).astype(o_ref.dtype)
    return epilogue


def _left_mm(w, x, *, grid, w_spec, x_spec, out_shape, out_spec, extras=(), epilogue=_scaled_store(1.0),
             aliases=None, name="left_mm"):
    return pl.pallas_call(
        functools.partial(_left_mm_kernel, epilogue=epilogue),
        grid=grid,
        in_specs=[w_spec, x_spec] + [s for _, s in extras],
        out_specs=out_spec,
        out_shape=out_shape,
        input_output_aliases=aliases or {},
        compiler_params=_cparams(*(("arbitrary",) * len(grid))),
        name=name,
    )(w, x, *[a for a, _ in extras])


def _fnet_channel_kernel(x_ref, w_ref, o_ref):
    cg = x_ref.shape[1]
    r = jnp.dot(x_ref[...], w_ref[...], preferred_element_type=F32)
    o_ref[0] = r[:, :cg].astype(o_ref.dtype)
    o_ref[1] = r[:, cg:].astype(o_ref.dtype)


def _fnet_channel(h, rows, row_off, n_seq, seq_len):
    tm = rows.tile
    d = h.shape[1]
    cg = d // FNET_GROUPS
    ci, ki = _iota2(cg, cg)
    cc, sc = _cis(ci * ki, cg)
    wc = jnp.concatenate([cc, -sc], axis=1).astype(BF16)
    st = seq_len // tm
    return pl.pallas_call(
        _fnet_channel_kernel,
        grid=(n_seq * st, FNET_GROUPS),
        in_specs=[pl.BlockSpec((tm, cg), lambda i, g: (i + row_off, g)),
                  pl.BlockSpec((cg, 2 * cg), lambda i, g: (0, 0))],
        out_specs=pl.BlockSpec((None, 2, tm, cg), lambda i, g: (i // st, 0, i % st, g)),
        out_shape=jax.ShapeDtypeStruct((n_seq, 2, seq_len, d), BF16),
        compiler_params=_cparams("arbitrary", "arbitrary"),
        name="fnet_channel",
    )(h, wc)


def _fnet_positions_dense(z, tc):
    n_seq, _, seq_len, d = z.shape
    scale = 1.0 / math.sqrt(seq_len * (d // FNET_GROUPS))
    ki, ci2 = _iota2(seq_len, 2 * seq_len)
    gc, gs = _cis(ki * (ci2 % seq_len), seq_len)
    g2 = jnp.where(ci2 < seq_len, gc, gs).astype(BF16)
    out = _left_mm(g2, z,
                   grid=(n_seq, d // tc),
                   w_spec=pl.BlockSpec((seq_len, 2 * seq_len), lambda s, j: (0, 0)),
                   x_spec=pl.BlockSpec((None, 2, seq_len, tc), lambda s, j: (s, 0, 0, j)),
                   out_shape=jax.ShapeDtypeStruct((n_seq, seq_len, d), BF16),
                   out_spec=pl.BlockSpec((None, seq_len, tc), lambda s, j: (s, 0, j)),
                   epilogue=_scaled_store(scale), name="fnet_dense")
    return out.reshape(n_seq * seq_len, d)


FNET_ROWS = 2048
FNET_K1_GROUP = 16


def _fnet_channel_split_kernel(x_ref, w_ref, zr_ref, zi_ref, rg_ref):
    cg = x_ref.shape[1]
    g = x_ref.shape[0] // DFT_N2
    r = jnp.dot(x_ref[...], w_ref[...], preferred_element_type=F32)
    for plane, z_ref in enumerate((zr_ref, zi_ref)):
        for s in range(g):
            _slab_store(rg_ref, s, r[s * DFT_N2:(s + 1) * DFT_N2, plane * cg:(plane + 1) * cg], DFT_N2)

        def body(n2, c):
            z_ref[n2] = _rows_gather(rg_ref, n2, g, DFT_N2).astype(z_ref.dtype)
            return c
        lax.fori_loop(0, DFT_N2, body, 0)


def _fnet_channel_split(h, n_seq, seq_len):
    tm = FNET_ROWS
    d = h.shape[1]
    cg = d // FNET_GROUPS
    ci, ki = _iota2(cg, cg)
    cc, sc = _cis(ci * ki, cg)
    wc = jnp.concatenate([cc, -sc], axis=1).astype(BF16)
    st = seq_len // tm
    g = tm // DFT_N2
    assert seq_len % tm == 0 and g % 16 == 0
    plane = jax.ShapeDtypeStruct((n_seq, DFT_N2, seq_len // DFT_N2, d), BF16)
    out_spec = pl.BlockSpec((None, DFT_N2, g, cg), lambda i, c: (i // st, 0, i % st, c))
    return pl.pallas_call(
        _fnet_channel_split_kernel,
        grid=(n_seq * st, FNET_GROUPS),
        in_specs=[pl.BlockSpec((tm, cg), lambda i, c: (i, c)),
                  pl.BlockSpec((cg, 2 * cg), lambda i, c: (0, 0))],
        out_specs=[out_spec, out_spec],
        out_shape=[plane, plane],
        scratch_shapes=[_slab_scratch(g, DFT_N2, cg)],
        compiler_params=_cparams("arbitrary", "arbitrary"),
        name="fnet_channel",
    )(h, wc)


def _fnet_stage1_kernel(w_ref, zr_ref, zi_ref, o_ref):
    w = w_ref[...]
    for s in range(zr_ref.shape[0]):
        z = jnp.concatenate([zr_ref[s], zi_ref[s]], axis=0)
        o_ref[s] = jnp.dot(w, z, preferred_element_type=F32).astype(o_ref.dtype)


def _fnet_stage2_kernel(g_ref, a_ref, o_ref, rin, rout, *, scale):
    kg = a_ref.shape[1] // 2

    def body_in(q, c):
        _rows_scatter(rin, q, a_ref[q].astype(F32), DFT_N2)
        return c
    lax.fori_loop(0, a_ref.shape[0], body_in, 0)
    for j in range(kg):
        r = jnp.dot(g_ref[j], _stage2_operand(rin, j), preferred_element_type=F32)
        _slab_store(rout, j, r * scale, DFT_N2)

    def body_out(q, c):
        o_ref[q] = _rows_gather(rout, q, kg, DFT_N2).astype(o_ref.dtype)
        return c
    lax.fori_loop(0, o_ref.shape[0], body_out, 0)


def _fnet_positions_split(zr, zi):
    n_seq, n2, n1, d = zr.shape
    seq_len = n1 * n2
    scale = 1.0 / math.sqrt(seq_len * (d // FNET_GROUPS))
    r, cidx = _iota2(2 * n1, 2 * n1)
    k1, ro, ri, nn = r // 2, r % 2, cidx // n1, cidx % n1
    fr, fs = _cis(k1 * nn, n1)
    w1 = jnp.where(ro == ri, fr, jnp.where(ro == 0, fs, -fs)).astype(BF16)
    tc = min(SPLIT_TC, d)
    zblk = pl.BlockSpec((None, N2_GROUP, n1, tc), lambda s, g, j: (s, g, 0, j))
    a = pl.pallas_call(
        _fnet_stage1_kernel,
        grid=(n_seq, n2 // N2_GROUP, d // tc),
        in_specs=[pl.BlockSpec((2 * n1, 2 * n1), lambda s, g, j: (0, 0)), zblk, zblk],
        out_specs=pl.BlockSpec((None, N2_GROUP, 2 * n1, tc), lambda s, g, j: (s, g, 0, j)),
        out_shape=jax.ShapeDtypeStruct((n_seq, n2, 2 * n1, d), BF16),
        compiler_params=_cparams("arbitrary", "arbitrary", "arbitrary"),
        name="fnet_stage1",
    )(w1, zr, zi)
    k2i, ci2 = _iota2(n2, 2 * n2)
    kk = jnp.arange(n1, dtype=jnp.int32)[:, None, None] + n1 * k2i[None]
    gc, gs = _cis(kk * (ci2 % n2)[None], seq_len)
    g2 = jnp.where((ci2 < n2)[None], gc, gs).astype(BF16)
    kg = FNET_K1_GROUP
    assert n1 % kg == 0
    out = pl.pallas_call(
        functools.partial(_fnet_stage2_kernel, scale=scale),
        grid=(n_seq, n1 // kg, d // tc),
        in_specs=[pl.BlockSpec((kg, n2, 2 * n2), lambda s, k, j: (k, 0, 0)),
                  pl.BlockSpec((None, n2, 2 * kg, tc), lambda s, k, j: (s, 0, k, j))],
        out_specs=pl.BlockSpec((None, n2, kg, tc), lambda s, k, j: (s, 0, k, j)),
        out_shape=jax.ShapeDtypeStruct((n_seq, n2, n1, d), BF16),
        scratch_shapes=[_slab_scratch(2 * kg, n2, tc), _slab_scratch(kg, n2, tc)],
        compiler_params=_cparams("arbitrary", "arbitrary", "arbitrary"),
        name="fnet_stage2",
    )(g2, a)
    return out.reshape(n_seq * seq_len, d)


def _layer_fnet(x_all, h, mod, rows, w_out, b_out):
    d = x_all.shape[1]
    zc = _fnet_channel(h, rows, rows.ctx_off, rows.batch, rows.ctx_len)
    fc = _fnet_positions_dense(zc, min(d, 512))
    x_all = _proj_residual(fc, w_out, b_out, x_all, mod, 2, rows, rows.ctx_off, name="fnet_out_ctx")
    zr, zi = _fnet_channel_split(h, rows.batch, rows.seq)
    fl = _fnet_positions_split(zr, zi)
    big = _Rows(rows.batch, rows.ctx_len, rows.seq, PROJ_ROWS)
    return _proj_residual(fl, w_out, b_out, x_all, mod, 2, big, big.lat_off, tn=1024, name="fnet_out_lat")


HEAD_SLAB = 2 * LANES
MLA_SCALE = (QK_NOPE + QK_ROPE) ** -0.5
Q_SCALE = MLA_SCALE * math.log2(math.e)
V_SLAB = V_DIM + 16
ATTN_TQ = 2048
ATTN_TK = 1024


def _rope_tables(seq_len, lead_identity_rows):
    n_rows = seq_len // GRID_W
    row = jnp.repeat(jnp.arange(n_rows, dtype=F32), GRID_W)
    col = jnp.tile(jnp.arange(GRID_W, dtype=F32), n_rows)
    half = QK_ROPE // 2
    inv = ROPE_THETA ** (-jnp.arange(0, half, 2, dtype=F32) / half)
    ang_r = row[:, None] * inv
    ang_c = col[:, None] * inv
    ang = jnp.concatenate([ang_r, ang_r, ang_c, ang_c], axis=-1)
    pad = jnp.zeros((seq_len, LANES - QK_ROPE), F32)
    cos = jnp.concatenate([jnp.cos(ang), pad + 1.0], axis=-1)
    sin = jnp.concatenate([jnp.sin(ang), pad], axis=-1)
    if lead_identity_rows:
        cos = jnp.concatenate([jnp.ones((lead_identity_rows, LANES), F32), cos], axis=0)
        sin = jnp.concatenate([jnp.zeros((lead_identity_rows, LANES), F32), sin], axis=0)
    return cos, sin


def _rope(x, cos, sin):
    q = QK_ROPE // 4
    lane = lax.broadcasted_iota(jnp.int32, x.shape, 1)
    even = (lane // q) % 2 == 0
    rot = jnp.where(even, -pltpu.roll(x, LANES - q, 1), pltpu.roll(x, q, 1))
    return x * cos + rot * sin


def _rmsnorm_epilogue(acc, extras, outs):
    g_ref, = extras
    y = acc * lax.rsqrt(jnp.mean(acc * acc, axis=-1, keepdims=True) + EPS) * g_ref[...]
    outs[0][...] = y.astype(outs[0].dtype)


def _q_up_epilogue(acc, extras, outs):
    cos_ref, sin_ref = extras
    o_ref, = outs
    for hh in range(acc.shape[1] // HEAD_SLAB):
        c0 = hh * HEAD_SLAB
        o_ref[:, c0:c0 + LANES] = (acc[:, c0:c0 + LANES] * Q_SCALE).astype(o_ref.dtype)
        pe = _rope(acc[:, c0 + LANES:c0 + HEAD_SLAB], cos_ref[...], sin_ref[...])
        o_ref[:, c0 + LANES:c0 + HEAD_SLAB] = (pe * Q_SCALE).astype(o_ref.dtype)


def _kv_down_epilogue(acc, extras, outs, *, rank):
    g_ref, cos_ref, sin_ref = extras
    kvn_ref, kpe_ref = outs
    lat = acc[:, :rank]
    y = lat * lax.rsqrt(jnp.mean(lat * lat, axis=-1, keepdims=True) + EPS) * g_ref[...]
    kvn_ref[...] = y.astype(kvn_ref.dtype)
    kpe_ref[...] = _rope(acc[:, rank:rank + LANES], cos_ref[...], sin_ref[...]).astype(kpe_ref.dtype)


def _k_up_epilogue(acc, extras, outs):
    kpe_ref, = extras
    o_ref, = outs
    for hh in range(acc.shape[1] // LANES):
        o_ref[:, hh * HEAD_SLAB:hh * HEAD_SLAB + LANES] = acc[:, hh * LANES:(hh + 1) * LANES].astype(o_ref.dtype)
        o_ref[:, hh * HEAD_SLAB + LANES:(hh + 1) * HEAD_SLAB] = kpe_ref[...]


def _attn_kernel(q_ref, kc_ref, kl_ref, vc_ref, vl_ref, o_ref, *, tk):
    q = q_ref[...]

    def scores(k_tile):
        return lax.dot_general(k_tile, q, (((1,), (1,)), ((), ())), preferred_element_type=F32)

    def probs(s, m):
        return jnp.exp2((s - m).astype(BF16))

    s = scores(kc_ref[...])
    m = jnp.max(s, axis=0, keepdims=True)
    acc = jnp.dot(vc_ref[...], probs(s, m), preferred_element_type=F32)
    for j in range(kl_ref.shape[0] // tk):
        s = scores(kl_ref[j * tk:(j + 1) * tk, :])
        m_new = jnp.maximum(m, jnp.max(s, axis=0, keepdims=True))
        acc = jnp.exp2(m - m_new) * acc + jnp.dot(vl_ref[:, j * tk:(j + 1) * tk], probs(s, m_new),
                                                  preferred_element_type=F32)
        m = m_new
    o_ref[...] = (acc[:V_DIM] / acc[V_DIM:V_DIM + 1]).astype(o_ref.dtype)


def _value_slabs(v, batch, keys):
    v4 = v.reshape(batch, keys, MLA_HEADS, V_DIM)
    ones = jnp.ones((batch, keys, MLA_HEADS, 1), v.dtype)
    pad = jnp.zeros((batch, keys, MLA_HEADS, V_SLAB - V_DIM - 1), v.dtype)
    return jnp.concatenate([v4, ones, pad], axis=-1).transpose(0, 2, 3, 1).reshape(batch, MLA_HEADS * V_SLAB, keys)


def _attention(q, k_ctx, k_lat, vt_ctx, vt_lat):
    batch, seq, _ = q.shape
    ctx_len = k_ctx.shape[1]
    tq = min(ATTN_TQ, seq)
    tk = min(ATTN_TK, seq)
    assert seq % tq == 0 and seq % tk == 0
    return pl.pallas_call(
        functools.partial(_attn_kernel, tk=tk),
        grid=(batch, MLA_HEADS, seq // tq),
        in_specs=[pl.BlockSpec((None, tq, HEAD_SLAB), lambda b, h, i: (b, i, h)),
                  pl.BlockSpec((None, ctx_len, HEAD_SLAB), lambda b, h, i: (b, 0, h)),
                  pl.BlockSpec((None, seq, HEAD_SLAB), lambda b, h, i: (b, 0, h)),
                  pl.BlockSpec((None, V_SLAB, ctx_len), lambda b, h, i: (b, h, 0)),
                  pl.BlockSpec((None, V_SLAB, seq), lambda b, h, i: (b, h, 0))],
        out_specs=pl.BlockSpec((None, V_DIM, tq), lambda b, h, i: (b, h, i)),
        out_shape=jax.ShapeDtypeStruct((batch, MLA_HEADS * V_DIM, seq), BF16),
        compiler_params=_cparams("arbitrary", "arbitrary", "arbitrary"),
        name="mla_attention",
    )(q, k_ctx, k_lat, vt_ctx, vt_lat)


def _layer_mla(x_all, h, mod, rows, w_dq, g_q, w_uq, w_dkv, g_kv, w_ukv, w_o, update_ctx=False):
    assert not update_ctx, "attention is the last mixer of the stack: context queries are never needed"
    rows = _Rows(rows.batch, rows.ctx_len, rows.seq, PROJ_ROWS)
    tm = rows.tile
    d = x_all.shape[1]
    batch, seq, ctx_len = rows.batch, rows.seq, rows.ctx_len
    q_rank, kv_rank = w_dq.shape[1], g_kv.shape[0]
    assert V_DIM == LANES and QK_NOPE == LANES and QK_ROPE <= LANES

    w_uq_s = jnp.pad(w_uq.reshape(q_rank, MLA_HEADS, QK_NOPE + QK_ROPE),
                     ((0, 0), (0, 0), (0, HEAD_SLAB - QK_NOPE - QK_ROPE))).reshape(q_rank, MLA_HEADS * HEAD_SLAB)
    w_dkv_s = jnp.pad(w_dkv, ((0, 0), (0, kv_rank + LANES - w_dkv.shape[1])))
    w_ukv_s = w_ukv.reshape(kv_rank, MLA_HEADS, QK_NOPE + V_DIM)
    w_uk = w_ukv_s[:, :, :QK_NOPE].reshape(kv_rank, MLA_HEADS * QK_NOPE)
    w_uv = w_ukv_s[:, :, QK_NOPE:].reshape(kv_rank, MLA_HEADS * V_DIM)
    cos, sin = _rope_tables(seq, tm)
    lat_blocks = rows.lat_blocks

    cqn, = _matmul(h, w_dq, tm=tm, tn=q_rank, m_rows=rows.n_lat, x_row_off=rows.lat_off,
                   extras=[(g_q.reshape(1, q_rank), pl.BlockSpec((1, q_rank), lambda j, i: (0, 0)))],
                   outs=[(jax.ShapeDtypeStruct((rows.n_lat, q_rank), BF16), pl.BlockSpec((tm, q_rank), lambda j, i: (i, 0)))],
                   epilogue=_rmsnorm_epilogue, name="mla_q_down")
    tnq = 4 * HEAD_SLAB
    rope_lat = pl.BlockSpec((tm, LANES), lambda j, i: (1 + i % lat_blocks, 0))
    q, = _matmul(cqn, w_uq_s, tm=tm, tn=tnq, extras=[(cos, rope_lat), (sin, rope_lat)],
                 outs=[(jax.ShapeDtypeStruct((rows.n_lat, MLA_HEADS * HEAD_SLAB), BF16),
                        pl.BlockSpec((tm, tnq), lambda j, i: (i, j)))],
                 epilogue=_q_up_epilogue, name="mla_q_up")

    def keys_values(row_off, n_rows, rope_spec):
        kvn, kpe = _matmul(h, w_dkv_s, tm=tm, tn=kv_rank + LANES, m_rows=n_rows, x_row_off=row_off,
                           extras=[(g_kv.reshape(1, kv_rank), pl.BlockSpec((1, kv_rank), lambda j, i: (0, 0))),
                                   (cos, rope_spec), (sin, rope_spec)],
                           outs=[(jax.ShapeDtypeStruct((n_rows, kv_rank), BF16), pl.BlockSpec((tm, kv_rank), lambda j, i: (i, 0))),
                                 (jax.ShapeDtypeStruct((n_rows, LANES), BF16), pl.BlockSpec((tm, LANES), lambda j, i: (i, 0)))],
                           epilogue=functools.partial(_kv_down_epilogue, rank=kv_rank), name="mla_kv_down")
        tnk = 4 * LANES
        k, = _matmul(kvn, w_uk, tm=tm, tn=tnk,
                     extras=[(kpe, pl.BlockSpec((tm, LANES), lambda j, i: (i, 0)))],
                     outs=[(jax.ShapeDtypeStruct((n_rows, MLA_HEADS * HEAD_SLAB), BF16),
                            pl.BlockSpec((tm, 2 * tnk), lambda j, i: (i, j)))],
                     epilogue=_k_up_epilogue, name="mla_k_up")
        v, = _matmul(kvn, w_uv, tm=tm, tn=tnk,
                     outs=[(jax.ShapeDtypeStruct((n_rows, MLA_HEADS * V_DIM), BF16), pl.BlockSpec((tm, tnk), lambda j, i: (i, j)))],
                     name="mla_v_up")
        return k, v

    k_c, v_c = keys_values(rows.ctx_off, rows.n_ctx,pl.BlockSpec((tm, LANES), lambda j, i: (0, 0)))
    k_l, v_l = keys_values(rows.lat_off, rows.n_lat, rope_lat)
    vt_c = _value_slabs(v_c, batch, ctx_len)
    vt_l = _value_slabs(v_l, batch, seq)
    ot = _attention(q.reshape(batch, seq, -1), k_c.reshape(batch, ctx_len, -1), k_l.reshape(batch, seq, -1), vt_c, vt_l)
    attn = ot.swapaxes(1, 2).reshape(rows.n_lat, MLA_HEADS * V_DIM)
    return _proj_residual(attn, w_o, None, x_all, mod, 2, rows, rows.lat_off, tn=1024, name="mla_out")


CONV_HALO = 16
MAX_DECAY = math.log(DECAY_TARGET) / FAST_DECAY_PCT
MIN_DECAY = math.log(DECAY_TARGET) / SLOW_DECAY_PCT


def _hyena_in_kernel(prev_ref, main_ref, next_ref, w_ref, b_ref, cw_ref, cb_ref, o_ref, xext_ref, zext_ref, wbf_ref,
                     *rg, tm, seq_len):
    i = pl.program_id(1)
    seq_tiles = seq_len // tm
    t_in_seq = i % seq_tiles

    @pl.when(i == 0)
    def _():
        wbf_ref[...] = w_ref[...].astype(BF16)

    xext_ref[0:CONV_HALO, :] = prev_ref[...]
    xext_ref[CONV_HALO:CONV_HALO + tm, :] = main_ref[...]
    xext_ref[CONV_HALO + tm:2 * CONV_HALO + tm, :] = next_ref[...]
    zext_ref[...] = jnp.dot(xext_ref[...], wbf_ref[...], preferred_element_type=F32) + b_ref[...]
    row = lax.broadcasted_iota(jnp.int32, (tm, 1), 0)
    prev = jnp.where((row == 0) & (t_in_seq == 0), 0.0, zext_ref[pl.ds(CONV_HALO - 1, tm), :])
    nxt = jnp.where((row == tm - 1) & (t_in_seq == seq_tiles - 1), 0.0, zext_ref[pl.ds(CONV_HALO + 1, tm), :])
    out = prev * cw_ref[0:1, :] + zext_ref[pl.ds(CONV_HALO, tm), :] * cw_ref[1:2, :] + nxt * cw_ref[2:3, :] + cb_ref[...]
    if not rg:
        o_ref[...] = out.astype(o_ref.dtype)
        return
    rg_ref, = rg
    for s in range(tm // DFT_N2):
        _slab_store(rg_ref, s, out[s * DFT_N2:(s + 1) * DFT_N2, :], DFT_N2)

    def body(n2, c):
        o_ref[n2] = _rows_gather(rg_ref, n2, tm // DFT_N2, DFT_N2).astype(o_ref.dtype)
        return c
    lax.fori_loop(0, DFT_N2, body, 0)


def _hyena_in(h, w_in, b_in, conv_w, conv_b, rows, row_off, n_rows, seq_len, tn=512, time_split=False):
    tm = rows.tile
    k, n3 = w_in.shape
    d = n3 // 3
    tn = min(tn, d)
    nd = d // tn
    hb = tm // CONV_HALO
    last_hblk = h.shape[0] // CONV_HALO - 1
    if time_split:
        g = tm // DFT_N2
        assert tm % DFT_N2 == 0 and g % 8 == 0 and seq_len % tm == 0
        out_spec = pl.BlockSpec((None, DFT_N2, g, tn), lambda j, i: (j // nd, 0, i, j % nd))
        out_shape = jax.ShapeDtypeStruct((3, DFT_N2, n_rows // DFT_N2, d), F32)
        extra_scratch = [_slab_scratch(g, DFT_N2, tn)]
    else:
        out_spec = pl.BlockSpec((None, tm, tn), lambda j, i: (j // nd, i, j % nd))
        out_shape = jax.ShapeDtypeStruct((3, n_rows, d), BF16)
        extra_scratch = []
    return pl.pallas_call(
        functools.partial(_hyena_in_kernel, tm=tm, seq_len=seq_len),
        grid=(n3 // tn, n_rows // tm),
        in_specs=[pl.BlockSpec((CONV_HALO, k), lambda j, i: (jnp.maximum((i + row_off) * hb - 1, 0), 0)),
                  pl.BlockSpec((tm, k), lambda j, i: (i + row_off, 0)),
                  pl.BlockSpec((CONV_HALO, k), lambda j, i: (jnp.minimum((i + row_off + 1) * hb, last_hblk), 0)),
                  pl.BlockSpec((k, tn), lambda j, i: (0, j)),
                  pl.BlockSpec((1, tn), lambda j, i: (0, j)),
                  pl.BlockSpec((3, tn), lambda j, i: (0, j)),
                  pl.BlockSpec((1, tn), lambda j, i: (0, j))],
        out_specs=out_spec,
        out_shape=out_shape,
        scratch_shapes=[pltpu.VMEM((tm + 2 * CONV_HALO, k), BF16), pltpu.VMEM((tm + 2 * CONV_HALO, tn), F32),
                        pltpu.VMEM((k, tn), BF16)] + extra_scratch,
        compiler_params=_cparams("arbitrary", "arbitrary"),
        name="hyena_in",
    )(h, h, h, w_in, b_in.reshape(1, n3), conv_w, conv_b.reshape(1, n3))


def _hyena_filter_kernel(z_ref, w1_ref, b1_ref, w2_ref, b2_ref, f0_ref, f1_ref, w3_ref, dl_ref, k_ref, ss_ref,
                         h2_ref, *rg, tp, seq_len):
    p = pl.program_id(0)
    z = z_ref[...]

    @pl.when(pl.program_id(1) == 0)
    def _():
        h1 = jnp.sin(f0_ref[...] * (jnp.dot(z, w1_ref[...], precision=HIGHEST, preferred_element_type=F32) + b1_ref[...]))
        h2 = jnp.sin(f1_ref[...] * (jnp.dot(h1, w2_ref[...], precision=HIGHEST, preferred_element_type=F32) + b2_ref[...]))
        h2_ref[...] = h2.astype(BF16)

    filt = jnp.dot(h2_ref[...], w3_ref[...].astype(BF16), preferred_element_type=F32)
    t = z[:, 0:1]
    kk = filt * (jnp.exp(-t * dl_ref[...]) + MOD_SHIFT)
    circ = p * tp + lax.broadcasted_iota(jnp.int32, (tp, 1), 0)
    kk = jnp.where(circ == seq_len, 0.0, kk)
    if rg:
        rg_ref, = rg
        for s in range(tp // DFT_N2):
            _slab_store(rg_ref, s, kk[s * DFT_N2:(s + 1) * DFT_N2, :], DFT_N2)

        def body(n2, c):
            k_ref[n2] = _rows_gather(rg_ref, n2, tp // DFT_N2, DFT_N2).astype(k_ref.dtype)
            return c
        lax.fori_loop(0, DFT_N2, body, 0)
    else:
        k_ref[...] = kk.astype(k_ref.dtype)
    ss_ref[...] = jnp.sum(kk * kk, axis=0, keepdims=True)


def _hyena_filters(seq_len, d, f_w1, f_b1, f_w2, f_b2, f_w3, f_freq, tp=256, tc=1024, time_split=False):
    f32 = F32
    hid = f_w1.shape[1]
    od = HYENA_ORDER * d
    t = jnp.linspace(0.0, 1.0, seq_len, dtype=f32)[:, None]
    bands = (FILTER_EMB - 1) // 2
    w = 2.0 * math.pi * jnp.arange(seq_len, dtype=f32)[:, None] / seq_len
    f = jnp.linspace(1e-4, bands - 1, bands, dtype=f32)[None, :]
    z = jnp.concatenate([t, jnp.cos(f * w), -jnp.sin(f * w)], axis=-1)
    circ = jnp.arange(2 * seq_len)
    offs = jnp.where(circ < seq_len, circ, jnp.minimum(2 * seq_len - circ, seq_len - 1))
    z2 = jnp.pad(z[offs], ((0, 0), (0, LANES - FILTER_EMB)))
    w1p = jnp.pad(f_w1, ((0, LANES - FILTER_EMB), (0, 0)))
    w3s = f_w3.reshape(hid, HYENA_ORDER, 2, d).transpose(2, 0, 1, 3).reshape(2, hid, od)
    deltas = jnp.tile(jnp.abs(jnp.linspace(MIN_DECAY, MAX_DECAY, d, dtype=f32)), HYENA_ORDER)[None, :]
    tp = min(tp, seq_len)
    tc = min(tc, od)
    side_tiles = seq_len // tp
    small = lambda shape: pl.BlockSpec(shape, lambda p, j: (0,) * len(shape))
    n_ptiles = 2 * side_tiles
    if time_split:
        g = tp // DFT_N2
        assert tp % DFT_N2 == 0 and g % 16 == 0
        k_spec = pl.BlockSpec((DFT_N2, g, tc), lambda p, j: (0, p, j))
        k_shape = jax.ShapeDtypeStruct((DFT_N2, 2 * seq_len // DFT_N2, od), BF16)
        extra_scratch = [_slab_scratch(g, DFT_N2, tc)]
    else:
        k_spec = pl.BlockSpec((tp, tc), lambda p, j: (p, j))
        k_shape = jax.ShapeDtypeStruct((2 * seq_len, od), BF16)
        extra_scratch = []
    k, ss_parts = pl.pallas_call(
        functools.partial(_hyena_filter_kernel, tp=tp, seq_len=seq_len),
        grid=(n_ptiles, od // tc),
        in_specs=[pl.BlockSpec((tp, LANES), lambda p, j: (p, 0)),
                  small((LANES, hid)), small((1, hid)), small((hid, hid)), small((1, hid)), small((1, hid)), small((1, hid)),
                  pl.BlockSpec((None, hid, tc), lambda p, j: (p // side_tiles, 0, j)),
                  pl.BlockSpec((1, tc), lambda p, j: (0, j))],
        out_specs=[k_spec, pl.BlockSpec((None, 1, tc), lambda p, j: (p, 0, j))],
        out_shape=[k_shape, jax.ShapeDtypeStruct((n_ptiles, 1, od), F32)],
        scratch_shapes=[pltpu.VMEM((tp, hid), BF16)] + extra_scratch,
        compiler_params=_cparams("arbitrary", "arbitrary"),
        name="hyena_filters",
    )(z2, w1p, f_b1.reshape(1, hid), f_w2, f_b2.reshape(1, hid), f_freq[0:1], f_freq[1:2], w3s, deltas)
    return k, jnp.sum(ss_parts, axis=0)


def _stage2_tables(n, n1):
    n2 = n // n1
    r, c = _iota2(2 * n2, 2 * n2)
    kk = jnp.arange(n1, dtype=jnp.int32)[:, None, None] + n1 * (r % n2)[None]
    gc, gs = _cis(kk * (c % n2)[None], n)
    same = ((r < n2) == (c < n2))[None]
    fwd = jnp.where(same, gc, jnp.where((r < n2)[None], gs, -gs))
    return fwd.astype(BF16), fwd.swapaxes(1, 2).astype(BF16)


def _complex_mul(x, kf, n2):
    xr, xi = x[:n2], x[n2:]
    kr, ki = kf[:n2].astype(F32), kf[n2:].astype(F32)
    return jnp.concatenate([xr * kr - xi * ki, xr * ki + xi * kr], axis=0)


def _conv_dense_kernel(fk_ref, fz_ref, fzt_ref, k_ref, ss_ref, u_ref, xg_ref, skip_ref, o_ref):
    n = k_ref.shape[0]
    kf = jnp.dot(fk_ref[...], k_ref[...], preferred_element_type=F32) * ss_ref[...]
    u = u_ref[...]
    z = jnp.dot(fz_ref[...], u, preferred_element_type=F32)
    y = _complex_mul(z, kf, n)
    conv = jnp.dot(fzt_ref[...], y.astype(BF16), preferred_element_type=F32)
    o_ref[...] = (xg_ref[...].astype(F32) * (conv + skip_ref[...] * u.astype(F32))).astype(o_ref.dtype)


def _hyena_long_conv_dense(u, xg, k2u, sumsq, order, skip, seq_len):
    n_rows, d = u.shape
    assert n_rows == 2 * seq_len
    n = 2 * seq_len
    colscale = lax.rsqrt(sumsq + EPS) / n
    r, c = _iota2(2 * n, n)
    kc, ks = _cis((r % n) * c, n)
    fk = jnp.where(r < n, kc, -ks).astype(BF16)
    r, c = _iota2(2 * n, 2 * seq_len)
    zc, zs = _cis((r % n) * (c % seq_len), n)
    fz = jnp.where((r < n) == (c < seq_len), zc, jnp.where(r < n, zs, -zs))
    tc = min(d, 512)
    nd = d // tc
    full = lambda a: pl.BlockSpec(a.shape, lambda j: (0, 0))
    fz_b, fzt_b = fz.astype(BF16), fz.T.astype(BF16)
    blk = pl.BlockSpec((n_rows, tc), lambda j: (0, j))
    return pl.pallas_call(
        _conv_dense_kernel,
        grid=(nd,),
        in_specs=[full(fk), full(fz_b), full(fzt_b),
                  pl.BlockSpec((n, tc), lambda j: (0, order * nd + j)),
                  pl.BlockSpec((1, tc), lambda j: (0, order * nd + j)),
                  blk, blk, pl.BlockSpec((1, tc), lambda j: (0, j))],
        out_specs=blk,
        out_shape=jax.ShapeDtypeStruct((n_rows, d), BF16),
        compiler_params=_cparams("arbitrary"),
        name="hyena_conv_dense",
    )(fk, fz_b, fzt_b, k2u, colscale, u, xg, skip.reshape(1, d))


SPLIT_ROWS = 1024
N2_GROUP = 8
K1_GROUP = 8
SPLIT_TC = 512


def _slab_mm_kernel(w_ref, x_ref, *rest, epilogue):
    w = w_ref[...]
    for s in range(x_ref.shape[0]):
        acc = jnp.dot(w, x_ref[s].astype(BF16), preferred_element_type=F32)
        epilogue(acc, s, rest[:-1], rest[-1])


def _slab_plain(acc, s, extras, o_ref):
    o_ref[s] = acc.astype(o_ref.dtype)


def _slab_gate(acc, s, extras, o_ref):
    xg_ref, u_ref, skip_ref = extras
    o_ref[s] = (xg_ref[s] * (acc + skip_ref[...] * u_ref[s])).astype(o_ref.dtype)


def _slab_mm(w, x, out_dtype, slab_extras=(), row_extras=(), epilogue=_slab_plain, name="slab_mm"):
    x_arr, x_lead = x
    n2, k, c = x_arr.shape[len(x_lead):]
    tc = min(SPLIT_TC, c)

    def blk(r, lead=()):
        return pl.BlockSpec((None,) * len(lead) + (N2_GROUP, r, tc), lambda g, j: tuple(lead) + (g, 0, j))

    in_specs = [pl.BlockSpec(w.shape, lambda g, j: (0, 0)), blk(k, x_lead)]
    in_specs += [blk(a.shape[-2], lead) for a, lead in slab_extras]
    in_specs += [pl.BlockSpec((1, tc), lambda g, j: (0, j)) for _ in row_extras]
    return pl.pallas_call(
        functools.partial(_slab_mm_kernel, epilogue=epilogue),
        grid=(n2 // N2_GROUP, c // tc),
        in_specs=in_specs,
        out_specs=blk(w.shape[0]),
        out_shape=jax.ShapeDtypeStruct((n2, w.shape[0], c), out_dtype),
        compiler_params=_cparams("arbitrary", "arbitrary"),
        name=name,
    )(w, x_arr, *[a for a, _ in slab_extras], *row_extras)


def _stage2_regroup_in(a_ref, rin):
    def body(q, c):
        _rows_scatter(rin, q, a_ref[q].astype(F32), DFT_N2)
        return c
    lax.fori_loop(0, a_ref.shape[0], body, 0)


def _stage2_operand(rin, j):
    return jnp.concatenate([_slab_load(rin, 2 * j, DFT_N2), _slab_load(rin, 2 * j + 1, DFT_N2)], axis=0).astype(BF16)


def _kf_stage2_kernel(g_ref, a_ref, ss_ref, o_ref, rin):
    _stage2_regroup_in(a_ref, rin)
    for j in range(K1_GROUP):
        kf = jnp.dot(g_ref[j], _stage2_operand(rin, j), preferred_element_type=F32)
        o_ref[j] = (kf * ss_ref[...]).astype(o_ref.dtype)


def _conv_stage2_kernel(gf_ref, gi_ref, a_ref, kf_ref, o_ref, rin, rout):
    _stage2_regroup_in(a_ref, rin)
    for j in range(K1_GROUP):
        x = jnp.dot(gf_ref[j], _stage2_operand(rin, j), preferred_element_type=F32)
        y = _complex_mul(x, kf_ref[j], DFT_N2)
        b = jnp.dot(gi_ref[j], y.astype(BF16), preferred_element_type=F32)
        _slab_store(rout, 2 * j, b[:DFT_N2], DFT_N2)
        _slab_store(rout, 2 * j + 1, b[DFT_N2:], DFT_N2)

    def body(q, c):
        o_ref[q] = _rows_gather(rout, q, 2 * K1_GROUP, DFT_N2).astype(o_ref.dtype)
        return c
    lax.fori_loop(0, o_ref.shape[0], body, 0)


def _hyena_kf_split(k3, sumsq, tables):
    n2, n1, od = k3.shape
    n = n1 * n2
    colscale = lax.rsqrt(sumsq + EPS) / n
    r, c = _iota2(2 * n1, n1)
    fr, fs = _cis((r // 2) * c, n1)
    w1 = jnp.where(r % 2 == 0, fr, -fs).astype(BF16)
    a = _slab_mm(w1, (k3, ()), BF16, name="hyena_kf_stage1")
    tc = min(SPLIT_TC, od)
    return pl.pallas_call(
        _kf_stage2_kernel,
        grid=(n1 // K1_GROUP, od // tc),
        in_specs=[pl.BlockSpec((K1_GROUP, 2 * n2, 2 * n2), lambda k, j: (k, 0, 0)),
                  pl.BlockSpec((n2, 2 * K1_GROUP, tc), lambda k, j: (0, k, j)),
                  pl.BlockSpec((1, tc), lambda k, j: (0, j))],
        out_specs=pl.BlockSpec((K1_GROUP, 2 * n2, tc), lambda k, j: (k, 0, j)),
        out_shape=jax.ShapeDtypeStruct((n1, 2 * n2, od), BF16),
        scratch_shapes=[_slab_scratch(2 * K1_GROUP, n2, tc)],
        compiler_params=_cparams("arbitrary", "arbitrary"),
        name="hyena_kf_stage2",
    )(tables[0], a, colscale)


def _hyena_long_conv_split(v, xg, kf, kf_col_off, skip, tables):
    n2, m, d = v[0].shape[len(v[1]):]
    hn = m // 2
    n1 = 2 * hn
    g_fwd, g_inv = tables
    r, c = _iota2(2 * n1, 2 * hn)
    fr, fs = _cis((r // 2) * (c % hn), n1)
    ro, ri = r % 2, c // hn
    w1 = jnp.where(ro == ri, fr, jnp.where(ro == 0, fs, -fs)).astype(BF16)
    a = _slab_mm(w1, v, BF16, name="hyena_conv_stage1")
    tc = min(SPLIT_TC, d)
    nd = d // tc
    grp = pl.BlockSpec((n2, 2 * K1_GROUP, tc), lambda k, j: (0, k, j))
    tab = pl.BlockSpec((K1_GROUP, 2 * n2, 2 * n2), lambda k, j: (k, 0, 0))
    b = pl.pallas_call(
        _conv_stage2_kernel,
        grid=(n1 // K1_GROUP, nd),
        in_specs=[tab, tab, grp,
                  pl.BlockSpec((K1_GROUP, 2 * n2, tc), lambda k, j: (k, 0, kf_col_off * nd + j))],
        out_specs=grp,
        out_shape=jax.ShapeDtypeStruct((n2, 2 * n1, d), BF16),
        scratch_shapes=[_slab_scratch(2 * K1_GROUP, n2, tc), _slab_scratch(2 * K1_GROUP, n2, tc)],
        compiler_params=_cparams("arbitrary", "arbitrary"),
        name="hyena_conv_stage2",
    )(g_fwd, g_inv, a, kf)
    r, c = _iota2(2 * hn, 2 * n1)
    ec, es = _cis((r % hn) * (c // 2), n1)
    ro, ri = r // hn, c % 2
    w3 = jnp.where(ro == ri, ec, jnp.where(ro == 0, -es, es)).astype(BF16)
    return _slab_mm(w3, (b, ()), F32, slab_extras=[xg, v], row_extras=[skip.reshape(1, d)],
                    epilogue=_slab_gate, name="hyena_conv_stage3")


def _proj_split_kernel(x_ref, w_ref, b_ref, res_ref, gate_ref, o_ref, wbf_ref, acc_ref, rg_ref):
    @pl.when(pl.program_id(1) == 0)
    def _():
        wbf_ref[...] = w_ref[...].astype(BF16)

    n2, g, k = x_ref.shape
    x = x_ref[...].reshape(n2 * g, k).astype(BF16)
    acc_ref[...] = jnp.dot(x, wbf_ref[...], preferred_element_type=F32) + b_ref[...]

    def body(q, c):
        _rows_scatter(rg_ref, q, acc_ref[pl.ds(pl.multiple_of(q * g, g), g), :], n2)
        return c
    lax.fori_loop(0, n2, body, 0)
    for s in range(g):
        rows = slice(s * n2, (s + 1) * n2)
        o_ref[rows, :] = res_ref[rows, :] + gate_ref[...] * _slab_load(rg_ref, s, n2)


def _proj_residual_split(v3, w, bias, x_all, mod, chunk, seq_len, tn=512):
    n2, m, k = v3.shape
    d = x_all.shape[1]
    tn = min(tn, d)
    nd = d // tn
    g = SPLIT_ROWS // n2
    tm = n2 * g
    seq_tiles = seq_len // tm
    mod3 = mod.reshape(MOD_ROWS, 1, -1)
    res_spec = pl.BlockSpec((tm, tn), lambda j, i: (i, j))
    return pl.pallas_call(
        _proj_split_kernel,
        grid=(nd, m // g),
        in_specs=[pl.BlockSpec((n2, g, k), lambda j, i: (0, i, 0)),
                  pl.BlockSpec((k, tn), lambda j, i: (0, j)),
                  pl.BlockSpec((1, tn), lambda j, i: (0, j)),
                  res_spec,
                  pl.BlockSpec((None, 1, tn), lambda j, i: (i // seq_tiles, 0, chunk * nd + j))],
        out_specs=res_spec,
        out_shape=jax.ShapeDtypeStruct(x_all.shape, F32),
        scratch_shapes=[pltpu.VMEM((k, tn), BF16), pltpu.VMEM((tm, tn), F32), _slab_scratch(g, n2, tn)],
        input_output_aliases={3: 0},
        compiler_params=_cparams("arbitrary", "arbitrary"),
        name="hyena_out_lat",
    )(v3, w, bias.reshape(1, d), x_all, mod3)


def _layer_hyena(x_all, h, mod, rows, prm, w_out, b_out):
    (w_in, b_in, conv_w, conv_b, f_w1, f_b1, f_w2, f_b2, f_w3, f_freq, skip) = prm
    d = w_in.shape[0]
    assert rows.batch == 2, "the long convolution carries the two batch rows as one complex sequence"
    fprm = (f_w1, f_b1, f_w2, f_b2, f_w3, f_freq)
    zc = _hyena_in(h, w_in, b_in, conv_w, conv_b, rows, rows.ctx_off, rows.n_ctx, rows.ctx_len)
    k2u, sumsq = _hyena_filters(rows.ctx_len, d, *fprm)
    vc = _hyena_long_conv_dense(zc[2], zc[0], k2u, sumsq, 0, skip[0], rows.ctx_len)
    vc = _hyena_long_conv_dense(vc, zc[1], k2u, sumsq, 1, skip[1], rows.ctx_len)
    x_all = _proj_residual(vc, w_out, b_out, x_all, mod, 2, rows, rows.ctx_off, name="hyena_out_ctx")
    seq = rows.seq
    split_rows = _Rows(rows.batch, rows.ctx_len, seq, SPLIT_ROWS)
    zl = _hyena_in(h, w_in, b_in, conv_w, conv_b, split_rows, 0, rows.n_lat, seq, time_split=True)
    k3, sumsq = _hyena_filters(seq, d, *fprm, tp=2 * SPLIT_ROWS, tc=SPLIT_TC, time_split=True)
    tables = _stage2_tables(2 * seq, 2 * seq // DFT_N2)
    kf = _hyena_kf_split(k3, sumsq, tables)
    v = _hyena_long_conv_split((zl, (2,)), (zl, (0,)), kf, 0, skip[0], tables)
    v = _hyena_long_conv_split((v, ()), (zl, (1,)), kf, 1, skip[1], tables)
    return _proj_residual_split(v, w_out, b_out, x_all, mod, 2, seq)


def _layer_pool(x_all, h, mod, rows, w_grp, scale):
    x_all = _pool_mixer(h, w_grp, scale, x_all, mod, 2, rows, rows.ctx_off, rows.n_ctx, rows.ctx_len)
    return _pool_mixer(h, w_grp, scale, x_all, mod, 2, rows, rows.lat_off, rows.n_lat, rows.seq)


def kernel(x, c, ctx, c_ctx, ada_w, ada_b, norm_g, final_g, hy_w_in, hy_b_in, hy_conv_w, hy_conv_b, hy_f_w1, hy_f_b1, hy_f_w2, hy_f_b2, hy_f_w3, hy_f_freq, hy_skip, hy_w_out, hy_b_out, fn_w_out, fn_b_out, pl_w, pl_scale, mla_w_dq, mla_g_q, mla_w_uq, mla_w_dkv, mla_g_kv, mla_w_ukv, mla_w_o, moe_w_group, moe_b_group, moe_w_expert, moe_b_expert, moe_w1, moe_w3, moe_w2):
    batch, seq, d = x.shape
    ctx_len = ctx.shape[1]
    depth = ada_w.shape[0]
    rows = _Rows(batch, ctx_len, seq, ROW_TILE)
    mod = _ada_mod(c, c_ctx, ada_w, ada_b)
    x_all = _pack_rows(x, ctx, rows)
    h_dtype = {0: BF16, 1: BF16, 2: F32, 3: BF16}
    h = _norm_mod(x_all, norm_g[0, 0], mod[0], 0, rows, h_dtype[0])
    for i in range(depth):
        kind, j = i % N_MIXERS, i // N_MIXERS
        m = mod[i]
        if kind == 0:
            prm = (hy_w_in[j], hy_b_in[j], hy_conv_w[j], hy_conv_b[j], hy_f_w1[j], hy_f_b1[j], hy_f_w2[j],
                   hy_f_b2[j], hy_f_w3[j], hy_f_freq[j], hy_skip[j])
            x_all = _layer_hyena(x_all, h, m, rows, prm, hy_w_out[j], hy_b_out[j])
        elif kind == 1:
            x_all = _layer_fnet(x_all, h, m, rows, fn_w_out[j], fn_b_out[j])
        elif kind == 2:
            x_all = _layer_pool(x_all, h, m, rows, pl_w[j], pl_scale[j])
        else:
            x_all = _layer_mla(x_all, h, m, rows, mla_w_dq[j], mla_g_q[j], mla_w_uq[j], mla_w_dkv[j],
                               mla_g_kv[j], mla_w_ukv[j], mla_w_o[j], update_ctx=i < depth - 1)
        ht, eid, wts = _norm_router(x_all, norm_g[i, 1], m, 3, rows, moe_w_group[i], moe_b_group[i],
                                    moe_w_expert[i], moe_b_expert[i])
        y = _moe_experts(ht, eid, wts, moe_w1, moe_w3, moe_w2, i)
        if i + 1 < depth:
            x_all, h = _moe_combine(x_all, y, m, 5, rows,
                                    next_norm=(norm_g[i + 1, 0], mod[i + 1], h_dtype[(i + 1) % N_MIXERS]))
    return _moe_combine(x_all, y, m, 5, rows, final_g=final_g).reshape(batch, seq, d)
```

```python
import functools
import math

import numpy as np
import jax
import jax.numpy as jnp
from jax import lax
from jax.experimental import pallas as pl
from jax.experimental.pallas import tpu as pltpu

F32 = jnp.float32
BF16 = jnp.bfloat16
HIGHEST = lax.Precision.HIGHEST

EPS = 1e-6
LANES = 128
MOD_ROWS = 8
VMEM_LIMIT = 56 * 1024 * 1024

N_MIXERS = 4
HYENA_ORDER = 2
FILTER_EMB = 33
DECAY_TARGET = 1e-2
FAST_DECAY_PCT = 0.3
SLOW_DECAY_PCT = 1.5
MOD_SHIFT = 0.0
FNET_GROUPS = 4
POOL_WINDOWS = (2, 4, 8, 16)
MLA_HEADS = 16
QK_NOPE = 128
QK_ROPE = 64
V_DIM = 128
GRID_W = 64
ROPE_THETA = 10000.0
N_GROUPS = 4
EXPERTS_PER_GROUP = 8
TOPK_EXPERT = 2
ROW_TILE = 256
PROJ_ROWS = 512
MOE_ROWS = 256


def _cparams(*sem):
    return pltpu.CompilerParams(dimension_semantics=sem, vmem_limit_bytes=VMEM_LIMIT)


def _ada_kernel(st_ref, w_ref, b_ref, o_ref, *, nrows):
    s = st_ref[...]
    s = s * jax.nn.sigmoid(s)
    w = w_ref[...]
    o_ref[...] = jnp.broadcast_to(b_ref[...], o_ref.shape)
    for r in range(nrows):
        o_ref[r:r + 1, :] = jnp.sum(s[:, r:r + 1] * w, axis=0, keepdims=True) + b_ref[...]


def _ada_mod(c, c_ctx, ada_w, ada_b):
    depth, d, n = ada_w.shape
    nrows = c.shape[0] + 1
    st = jnp.zeros((d, MOD_ROWS), F32).at[:, :nrows - 1].set(c.T).at[:, nrows - 1].set(c_ctx)
    tn = 1024 if n % 1024 == 0 else n
    return pl.pallas_call(
        functools.partial(_ada_kernel, nrows=nrows),
        grid=(depth, n // tn),
        in_specs=[pl.BlockSpec((d, MOD_ROWS), lambda l, j: (0, 0)),
                  pl.BlockSpec((None, d, tn), lambda l, j: (l, 0, j)),
                  pl.BlockSpec((None, 1, tn), lambda l, j: (l, 0, j))],
        out_specs=pl.BlockSpec((None, MOD_ROWS, tn), lambda l, j: (l, 0, j)),
        out_shape=jax.ShapeDtypeStruct((depth, MOD_ROWS, n), F32),
        compiler_params=_cparams("arbitrary", "arbitrary"),
        name="ada_mod",
    )(st, ada_w, ada_b.reshape(depth, 1, n))


class _Rows:
    def __init__(self, batch, ctx_len, seq, tile):
        assert seq % tile == 0
        self.batch, self.ctx_len, self.seq, self.tile = batch, ctx_len, seq, tile
        self.n_ctx = batch * ctx_len
        self.n_lat = batch * seq
        self.n_all = self.n_ctx + self.n_lat
        self.lat_blocks = seq // tile
        self.lat_off = 0
        self.ctx_off = self.n_lat // tile

    def mod_row(self, i):
        return jnp.where(i >= self.ctx_off, self.batch, i // self.lat_blocks)


def _pack_rows(x, ctx, rows):
    d = x.shape[-1]
    pad = -rows.n_all % SPLIT_ROWS
    return jnp.concatenate([x.reshape(-1, d), ctx.reshape(-1, d), jnp.zeros((pad, d), x.dtype)], axis=0)


HI_MASK = -65536


def _pack_halves(x):
    half = x.shape[1] // 2
    lo = lax.bitcast_convert_type(x[:, :half].astype(BF16).astype(F32), jnp.int32)
    hi = lax.bitcast_convert_type(x[:, half:].astype(BF16).astype(F32), jnp.int32)
    return lax.shift_right_logical(lo, 16) | (hi & HI_MASK)


def _unpack_halves(w):
    lo = lax.bitcast_convert_type(lax.shift_left(w, 16), F32)
    hi = lax.bitcast_convert_type(w & HI_MASK, F32)
    return lo, hi


def _rms_mod(x, g, sh, sc):
    y = x * lax.rsqrt(jnp.mean(x * x, axis=-1, keepdims=True) + EPS) * g
    return y * (1.0 + sc) + sh


def _norm_mod_kernel(x_ref, g_ref, sh_ref, sc_ref, o_ref):
    o_ref[...] = _rms_mod(x_ref[...], g_ref[...], sh_ref[...], sc_ref[...]).astype(o_ref.dtype)


def _norm_mod(x_all, g, mod, chunk, rows, out_dtype):
    n, d = rows.n_all, x_all.shape[1]
    tm = rows.tile
    mod3 = mod.reshape(MOD_ROWS, 1, -1)
    return pl.pallas_call(
        _norm_mod_kernel,
        grid=(n // tm,),
        in_specs=[pl.BlockSpec((tm, d), lambda i: (i, 0)),
                  pl.BlockSpec((1, d), lambda i: (0, 0)),
                  pl.BlockSpec((None, 1, d), lambda i: (rows.mod_row(i), 0, chunk)),
                  pl.BlockSpec((None, 1, d), lambda i: (rows.mod_row(i), 0, chunk + 1))],
        out_specs=pl.BlockSpec((tm, d), lambda i: (i, 0)),
        out_shape=jax.ShapeDtypeStruct((n, d), out_dtype),
        compiler_params=_cparams("arbitrary"),
        name="norm_mod",
    )(x_all, g.reshape(1, d), mod3, mod3)


def _norm_router_kernel(x_ref, g_ref, sh_ref, sc_ref, wr_ref, br_ref, h_ref, eid_ref, wt_ref, whi_ref, wlo_ref,
                        *, n_groups, per_group):
    h = _rms_mod(x_ref[...], g_ref[...], sh_ref[...], sc_ref[...])
    h_ref[...] = _pack_halves(h)

    @pl.when(pl.program_id(0) == 0)
    def _():
        w = wr_ref[...]
        w_hi = w.astype(BF16)
        whi_ref[...] = w_hi
        wlo_ref[...] = (w - w_hi.astype(F32)).astype(BF16)

    h_hi = h.astype(BF16)
    h_lo = (h - h_hi.astype(F32)).astype(BF16)
    logits = (jnp.dot(h_hi, whi_ref[...], preferred_element_type=F32)
              + jnp.dot(h_lo, whi_ref[...], preferred_element_type=F32)
              + jnp.dot(h_hi, wlo_ref[...], preferred_element_type=F32)) + br_ref[...]
    lane = lax.broadcasted_iota(jnp.int32, logits.shape, 1).astype(F32)
    neg = -jnp.inf
    gl = jnp.where(lane < n_groups, logits, neg)
    gmax = jnp.max(gl, axis=-1, keepdims=True)
    p_grp = 1.0 / jnp.sum(jnp.exp(gl - gmax), axis=-1, keepdims=True)
    g_idx = jnp.min(jnp.where(gl == gmax, lane, float(LANES)), axis=-1, keepdims=True)
    lo = n_groups + g_idx * per_group
    el = jnp.where((lane >= lo) & (lane < lo + per_group), logits, neg)
    e1 = jnp.max(el, axis=-1, keepdims=True)
    i1 = jnp.min(jnp.where(el == e1, lane, float(LANES)), axis=-1, keepdims=True)
    el2 = jnp.where(lane == i1, neg, el)
    e2 = jnp.max(el2, axis=-1, keepdims=True)
    i2 = jnp.min(jnp.where(el2 == e2, lane, float(LANES)), axis=-1, keepdims=True)
    r = jnp.exp(e2 - e1)
    w1 = p_grp / (1.0 + r)
    w2 = p_grp * r / (1.0 + r)
    eid = jnp.where(lane == 0, i1 - n_groups, jnp.where(lane == 1, i2 - n_groups, 0.0))
    eid_ref[...] = eid.astype(jnp.int32)
    wt_ref[...] = jnp.where(lane == 0, w1, jnp.where(lane == 1, w2, 0.0))


def _norm_router(x_all, g, mod, chunk, rows, w_group, b_group, w_expert, b_expert):
    n, d = rows.n_all, x_all.shape[1]
    tm = rows.tile
    n_groups = w_group.shape[1]
    n_experts = w_expert.shape[1]
    wr = jnp.zeros((d, LANES), F32).at[:, :n_groups].set(w_group).at[:, n_groups:n_groups + n_experts].set(w_expert)
    br = jnp.zeros((1, LANES), F32).at[0, :n_groups].set(b_group).at[0, n_groups:n_groups + n_experts].set(b_expert)
    mod3 = mod.reshape(MOD_ROWS, 1, -1)
    h, eid, wts = pl.pallas_call(
        functools.partial(_norm_router_kernel, n_groups=n_groups, per_group=n_experts // n_groups),
        grid=(n // tm,),
        in_specs=[pl.BlockSpec((tm, d), lambda i: (i, 0)),
                  pl.BlockSpec((1, d), lambda i: (0, 0)),
                  pl.BlockSpec((None, 1, d), lambda i: (rows.mod_row(i), 0, chunk)),
                  pl.BlockSpec((None, 1, d), lambda i: (rows.mod_row(i), 0, chunk + 1)),
                  pl.BlockSpec((d, LANES), lambda i: (0, 0)),
                  pl.BlockSpec((1, LANES), lambda i: (0, 0))],
        out_specs=[pl.BlockSpec((tm, d // 2), lambda i: (i, 0)),
                   pl.BlockSpec((tm, LANES), lambda i: (i, 0)),
                   pl.BlockSpec((tm, LANES), lambda i: (i, 0))],
        out_shape=[jax.ShapeDtypeStruct((n, d // 2), jnp.int32),
                   jax.ShapeDtypeStruct((n, LANES), jnp.int32),
                   jax.ShapeDtypeStruct((n, LANES), F32)],
        scratch_shapes=[pltpu.VMEM((d, LANES), BF16), pltpu.VMEM((d, LANES), BF16)],
        compiler_params=_cparams("arbitrary"),
        name="norm_router",
    )(x_all, g.reshape(1, d), mod3, mod3, wr, br)
    return h, eid[:, :TOPK_EXPERT], wts[:, :TOPK_EXPERT]


def _moe_dispatch(eid, wts, n_experts, tm):
    n_tok = eid.shape[0]
    n_assign = eid.size
    e_flat = eid.reshape(-1)
    order = jnp.argsort(e_flat).astype(jnp.int32)
    counts = jnp.sum((e_flat[:, None] == jnp.arange(n_experts, dtype=jnp.int32)[None, :]).astype(jnp.int32), axis=0)
    padded = (counts + tm - 1) // tm * tm
    start = jnp.cumsum(counts) - counts
    pend = jnp.cumsum(padded)
    pstart = pend - padded
    n_blocks = (n_assign + n_experts * (tm - 1) + tm - 1) // tm
    blk_start = jnp.arange(n_blocks, dtype=jnp.int32) * tm
    blk_e = jnp.minimum(jnp.sum((pend[None, :] <= blk_start[:, None]).astype(jnp.int32), axis=1), n_experts - 1)
    blk_rows = jnp.clip(pstart[blk_e] + counts[blk_e] - blk_start, 0, tm).astype(jnp.int32)
    row = lax.broadcasted_iota(jnp.int32, (n_blocks, tm), 1)
    slot_e = blk_e[:, None]
    src = start[slot_e] + (blk_start[:, None] + row - pstart[slot_e])
    valid = row < blk_rows[:, None]
    asg = order[jnp.clip(src, 0, n_assign - 1)]
    tok = asg // TOPK_EXPERT
    buf_tok = jnp.where(valid, tok, 0)
    dummy = TOPK_EXPERT * n_tok + (jnp.arange(n_blocks, dtype=jnp.int32)[:, None] % 2) * tm + row
    buf_asg = jnp.where(valid, (asg % TOPK_EXPERT) * n_tok + tok, dummy)
    buf_w = jnp.where(valid, wts.reshape(-1)[asg], 0.0)
    return (buf_tok.reshape(n_blocks, 1, tm), buf_asg.reshape(n_blocks, 1, tm),
            buf_w.reshape(n_blocks, tm, 1), blk_e.astype(jnp.int32), blk_rows)


def _moe_kernel(blk_e_ref, blk_rows_ref, tok_ref, tok_next_ref, asg_ref, roww_ref, h_hbm,
                w1_ref, w3_ref, w2_ref, y_hbm, xbuf, ybuf, w1b, w3b, w2b, gsem, ssem):
    b = pl.program_id(0)
    nb = pl.num_programs(0)
    slot = b % 2
    other = 1 - slot

    tm = xbuf.shape[1]

    def start_gather(ids_ref, s):
        for r in range(tm):
            pltpu.make_async_copy(h_hbm.at[pl.ds(ids_ref[0, r], 1)], xbuf.at[s, pl.ds(r, 1)], gsem.at[s]).start(priority=r % 2)

    def wait_gather(s):
        pltpu.make_async_copy(h_hbm.at[pl.ds(0, tm)], xbuf.at[s], gsem.at[s]).wait()

    def start_scatter(s):
        for r in range(tm):
            pltpu.make_async_copy(ybuf.at[s, pl.ds(r, 1)], y_hbm.at[pl.ds(asg_ref[0, r], 1)], ssem.at[s]).start(priority=r % 2)

    def wait_scatter(s):
        pltpu.make_async_copy(ybuf.at[s], y_hbm.at[pl.ds(0, tm)], ssem.at[s]).wait()

    def used(blk):
        return blk_rows_ref[jnp.clip(blk, 0, nb - 1)] > 0

    @pl.when(b == 0)
    def _():
        ybuf[...] = jnp.zeros(ybuf.shape, ybuf.dtype)
        n_real = y_hbm.shape[0] - 2 * tm
        for s in range(2):
            fill = pltpu.make_async_copy(ybuf.at[s], y_hbm.at[pl.ds(n_real + s * tm, tm)], ssem.at[s])
            fill.start()
            fill.wait()

    @pl.when((b == 0) & used(b))
    def _():
        start_gather(tok_ref, slot)

    @pl.when((b + 1 < nb) & used(b + 1))
    def _():
        start_gather(tok_next_ref, other)

    @pl.when((b >= 2) & used(b - 2))
    def _():
        wait_scatter(slot)

    @pl.when(used(b))
    def _():
        wait_gather(slot)

        @pl.when((b == 0) | (blk_e_ref[b] != blk_e_ref[jnp.maximum(b - 1, 0)]))
        def _():
            w1b[...] = w1_ref[...].astype(BF16)
            w3b[...] = w3_ref[...].astype(BF16)
            w2b[...] = w2_ref[...].astype(BF16)

        x_lo, x_hi = _unpack_halves(xbuf[slot])
        x_lo, x_hi = x_lo.astype(BF16), x_hi.astype(BF16)
        half = x_lo.shape[1]

        def up(w_ref):
            return (jnp.dot(x_lo, w_ref[0:half, :], preferred_element_type=F32)
                    + jnp.dot(x_hi, w_ref[half:2 * half, :], preferred_element_type=F32))

        a = up(w1b)
        g = up(w3b)
        hm = (a * jax.nn.sigmoid(a) * g).astype(BF16)
        ybuf[slot] = _pack_halves(jnp.dot(hm, w2b[...], preferred_element_type=F32) * roww_ref[...])
        start_scatter(slot)

    @pl.when(b == nb - 1)
    def _():
        @pl.when((nb >= 2) & used(b - 1))
        def _():
            wait_scatter(other)

        @pl.when(used(b))
        def _():
            wait_scatter(slot)


def _moe_experts(h, eid, wts, w1, w3, w2, layer):
    n, dh = h.shape
    d = 2 * dh
    n_experts, de = w1.shape[1], w1.shape[3]
    tm = MOE_ROWS
    tok, asg, roww, blk_e, blk_rows = _moe_dispatch(eid, wts, n_experts, tm)
    n_blocks = tok.shape[0]
    smem_blk = lambda f: pl.BlockSpec((None, 1, tm), f, memory_space=pltpu.SMEM)
    grid_spec = pltpu.PrefetchScalarGridSpec(
        num_scalar_prefetch=2,
        grid=(n_blocks,),
        in_specs=[smem_blk(lambda b, be, br: (b, 0, 0)),
                  smem_blk(lambda b, be, br: (jnp.minimum(b + 1, n_blocks - 1), 0, 0)),
                  smem_blk(lambda b, be, br: (b, 0, 0)),
                  pl.BlockSpec((None, tm, 1), lambda b, be, br: (b, 0, 0)),
                  pl.BlockSpec(memory_space=pl.ANY),
                  pl.BlockSpec((None, None, d, de), lambda b, be, br: (layer, be[b], 0, 0)),
                  pl.BlockSpec((None, None, d, de), lambda b, be, br: (layer, be[b], 0, 0)),
                  pl.BlockSpec((None, None, de, d), lambda b, be, br: (layer, be[b], 0, 0))],
        out_specs=pl.BlockSpec(memory_space=pl.ANY),
        scratch_shapes=[pltpu.VMEM((2, tm, dh), jnp.int32), pltpu.VMEM((2, tm, dh), jnp.int32),
                        pltpu.VMEM((d, de), BF16), pltpu.VMEM((d, de), BF16), pltpu.VMEM((de, d), BF16),
                        pltpu.SemaphoreType.DMA((2,)), pltpu.SemaphoreType.DMA((2,))],
    )
    return pl.pallas_call(
        _moe_kernel,
        grid_spec=grid_spec,
        out_shape=jax.ShapeDtypeStruct((n * TOPK_EXPERT + 2 * tm, dh), jnp.int32),
        compiler_params=_cparams("arbitrary"),
        name="moe_experts",
    )(blk_e, blk_rows, tok, tok, asg, roww, h, w1, w3, w2)


def _moe_combine_kernel(x_ref, *rest, follow):
    y_refs, gate_ref, rest = rest[:TOPK_EXPERT], rest[TOPK_EXPERT], rest[TOPK_EXPERT + 1:]
    def expert_out(y_ref):
        return jnp.concatenate(_unpack_halves(y_ref[...]), axis=1)

    acc = expert_out(y_refs[0])
    for y_ref in y_refs[1:]:
        acc = acc + expert_out(y_ref)
    x_new = x_ref[...] + gate_ref[...] * acc
    if follow == "next_norm":
        g_ref, sh_ref, sc_ref, o_ref, h_ref = rest
        o_ref[...] = x_new
        h_ref[...] = _rms_mod(x_new, g_ref[...], sh_ref[...], sc_ref[...]).astype(h_ref.dtype)
    else:
        g_ref, o_ref = rest
        o_ref[...] = x_new * lax.rsqrt(jnp.mean(x_new * x_new, axis=-1, keepdims=True) + EPS) * g_ref[...]


def _moe_combine(x_all, y, mod, chunk, rows, *, next_norm=None, final_g=None):
    n, d = rows.n_all, x_all.shape[1]
    tm = rows.tile
    mod3 = mod.reshape(MOD_ROWS, 1, -1)
    nblk = n // tm
    row_blk = pl.BlockSpec((tm, d), lambda i: (i, 0))
    vec = pl.BlockSpec((1, d), lambda i: (0, 0))
    y_specs = [pl.BlockSpec((tm, d // 2), functools.partial(lambda i, k: (i + k * nblk, 0), k=k))
               for k in range(TOPK_EXPERT)]
    in_specs = [row_blk] + y_specs + [pl.BlockSpec((None, 1, d), lambda i: (rows.mod_row(i), 0, chunk))]
    args = [x_all] + [y] * TOPK_EXPERT + [mod3]
    if next_norm is not None:
        g, mod_next, out_dtype = next_norm
        modn = mod_next.reshape(MOD_ROWS, 1, -1)
        in_specs += [vec, pl.BlockSpec((None, 1, d), lambda i: (rows.mod_row(i), 0, 0)),
                     pl.BlockSpec((None, 1, d), lambda i: (rows.mod_row(i), 0, 1))]
        args += [g.reshape(1, d), modn, modn]
        return pl.pallas_call(
            functools.partial(_moe_combine_kernel, follow="next_norm"),
            grid=(nblk,),
            in_specs=in_specs,
            out_specs=[row_blk, row_blk],
            out_shape=[jax.ShapeDtypeStruct(x_all.shape, F32), jax.ShapeDtypeStruct((n, d), out_dtype)],
            input_output_aliases={0: 0},
            compiler_params=_cparams("arbitrary"),
            name="moe_combine",
        )(*args)
    return pl.pallas_call(
        functools.partial(_moe_combine_kernel, follow="final_norm"),
        grid=(rows.n_lat // tm,),
        in_specs=in_specs + [vec],
        out_specs=row_blk,
        out_shape=jax.ShapeDtypeStruct((rows.n_lat, d), F32),
        compiler_params=_cparams("arbitrary"),
        name="moe_combine_final",
    )(*args, final_g.reshape(1, d))


def _mm_kernel(*refs, has_bias, n_extra, epilogue):
    x_ref, w_ref = refs[0], refs[1]
    pos = 2
    b_ref = None
    if has_bias:
        b_ref = refs[pos]
        pos += 1
    extras = refs[pos:pos + n_extra]
    outs = refs[pos + n_extra:-1]
    wbf = refs[-1]

    @pl.when(pl.program_id(1) == 0)
    def _():
        wbf[...] = w_ref[...].astype(BF16)

    acc = jnp.dot(x_ref[...].astype(BF16), wbf[...], preferred_element_type=F32)
    if has_bias:
        acc = acc + b_ref[...]
    epilogue(acc, extras, outs)


def _store_epilogue(acc, extras, outs):
    outs[0][...] = acc.astype(outs[0].dtype)


def _matmul(x, w, *, w_lead=(), bias=None, tm, tn, m_rows=None, x_row_off=0, extras=(), outs,
            epilogue=_store_epilogue, aliases=None, name="matmul"):
    k = x.shape[1]
    n = w.shape[-1]
    m_rows = x.shape[0] if m_rows is None else m_rows
    assert m_rows % tm == 0 and n % tn == 0 and w.shape[-2] == k
    lead = tuple(w_lead)
    in_specs = [pl.BlockSpec((tm, k), lambda j, i: (i + x_row_off, 0)),
                pl.BlockSpec((None,) * len(lead) + (k, tn), lambda j, i: lead + (0, j))]
    args = [x, w]
    if bias is not None:
        in_specs.append(pl.BlockSpec((1, tn), lambda j, i: (0, j)))
        args.append(bias.reshape(1, n))
    for arr, spec in extras:
        in_specs.append(spec)
        args.append(arr)
    return pl.pallas_call(
        functools.partial(_mm_kernel, has_bias=bias is not None, n_extra=len(extras), epilogue=epilogue),
        grid=(n // tn, m_rows // tm),
        in_specs=in_specs,
        out_specs=[spec for _, spec in outs],
        out_shape=[shape for shape, _ in outs],
        scratch_shapes=[pltpu.VMEM((k, tn), BF16)],
        input_output_aliases=aliases or {},
        compiler_params=_cparams("arbitrary", "arbitrary"),
        name=name,
    )(*args)


def _residual_epilogue(acc, extras, outs):
    res_ref, gate_ref = extras
    outs[0][...] = res_ref[...] + gate_ref[...] * acc


def _proj_residual(xin, w, bias, x_all, mod, chunk, rows, row_off, *, w_lead=(), tn=512, name="proj_residual"):
    tm = rows.tile
    d = x_all.shape[1]
    tn = min(tn, d)
    mod3 = mod.reshape(MOD_ROWS, 1, -1)
    nd = d // tn
    res_spec = pl.BlockSpec((tm, tn), lambda j, i: (i + row_off, j))
    gate_spec = pl.BlockSpec((None, 1, tn), lambda j, i: (rows.mod_row(i + row_off), 0, chunk * nd + j))
    n_extra_before = 2 + (bias is not None)
    out, = _matmul(xin, w, w_lead=w_lead, bias=bias, tm=tm, tn=tn,
                   extras=[(x_all, res_spec), (mod3, gate_spec)],
                   outs=[(jax.ShapeDtypeStruct(x_all.shape, F32), res_spec)],
                   epilogue=_residual_epilogue, aliases={n_extra_before: 0}, name=name)
    return out


POOL_HALO = 8


def _pool_kernel(prev_ref, main_ref, next_ref, w_ref, sc_ref, res_ref, gate_ref, o_ref, ext_ref, wbf_ref,
                 *, tm, seq_len, windows):
    i = pl.program_id(0)
    seq_tiles = seq_len // tm
    t_in_seq = i % seq_tiles

    @pl.when(i == 0)
    def _():
        wbf_ref[...] = w_ref[...].astype(BF16)

    zero_halo = jnp.zeros(prev_ref.shape, F32)
    ext_ref[0:POOL_HALO, :] = jnp.where(t_in_seq == 0, zero_halo, prev_ref[...])
    ext_ref[POOL_HALO:POOL_HALO + tm, :] = main_ref[...]
    ext_ref[POOL_HALO + tm:2 * POOL_HALO + tm, :] = jnp.where(t_in_seq == seq_tiles - 1, zero_halo, next_ref[...])

    pos = t_in_seq * tm + lax.broadcasted_iota(jnp.int32, (tm, 1), 0)
    cg = main_ref.shape[1] // len(windows)
    for gi, win in enumerate(windows):
        half = win // 2
        cols = slice(gi * cg, (gi + 1) * cg)
        s = ext_ref[pl.ds(POOL_HALO - half, tm), cols]
        for dlt in range(-half + 1, half):
            s = s + ext_ref[pl.ds(POOL_HALO + dlt, tm), cols]
        cnt = jnp.minimum(pos + half, seq_len) - jnp.maximum(pos - half, 0)
        pooled = s / cnt.astype(F32) - main_ref[:, cols]
        y = jnp.dot(pooled.astype(BF16), wbf_ref[gi], preferred_element_type=F32) * sc_ref[:, cols]
        o_ref[:, cols] = res_ref[:, cols] + gate_ref[:, cols] * y


def _pool_mixer(h, w_grp, scale, x_all, mod, chunk, rows, row_off, n_rows, seq_len):
    tm = rows.tile
    d = x_all.shape[1]
    assert max(POOL_WINDOWS) // 2 <= POOL_HALO and tm % POOL_HALO == 0 and seq_len % tm == 0
    hb = tm // POOL_HALO
    last_hblk = h.shape[0] // POOL_HALO - 1
    mod3 = mod.reshape(MOD_ROWS, 1, -1)
    main_spec = pl.BlockSpec((tm, d), lambda i: (i + row_off, 0))
    return pl.pallas_call(
        functools.partial(_pool_kernel, tm=tm, seq_len=seq_len, windows=POOL_WINDOWS),
        grid=(n_rows // tm,),
        in_specs=[pl.BlockSpec((POOL_HALO, d), lambda i: (jnp.maximum((i + row_off) * hb - 1, 0), 0)),
                  main_spec,
                  pl.BlockSpec((POOL_HALO, d), lambda i: (jnp.minimum((i + row_off + 1) * hb, last_hblk), 0)),
                  pl.BlockSpec(w_grp.shape, lambda i: (0, 0, 0)),
                  pl.BlockSpec((1, d), lambda i: (0, 0)),
                  main_spec,
                  pl.BlockSpec((None, 1, d), lambda i: (rows.mod_row(i + row_off), 0, chunk))],
        out_specs=main_spec,
        out_shape=jax.ShapeDtypeStruct(x_all.shape, F32),
        scratch_shapes=[pltpu.VMEM((tm + 2 * POOL_HALO, d), F32), pltpu.VMEM(w_grp.shape, BF16)],
        input_output_aliases={5: 0},
        compiler_params=_cparams("arbitrary"),
        name="pool_mixer",
    )(h, h, h, w_grp, scale.reshape(1, d), x_all, mod3)


DFT_N2 = 128


def _cis(num, den):
    ang = (num % den).astype(F32) * (2.0 * math.pi / den)
    return jnp.cos(ang), jnp.sin(ang)


def _iota2(n_rows, n_cols):
    return (lax.broadcasted_iota(jnp.int32, (n_rows, n_cols), 0), lax.broadcasted_iota(jnp.int32, (n_rows, n_cols), 1))


def _slab_pitch(rows):
    return rows + 8


def _slab_scratch(g, rows, width):
    return pltpu.VMEM((width // LANES, g * _slab_pitch(rows), LANES), F32)


def _slab_store(scr, s, val, rows, first_chunk=0):
    p = _slab_pitch(rows)
    for l in range(val.shape[1] // LANES):
        scr[first_chunk + l, s * p:s * p + rows, :] = val[:, l * LANES:(l + 1) * LANES]


def _slab_load(scr, s, rows):
    p = _slab_pitch(rows)
    return jnp.concatenate([scr[l, s * p:s * p + rows, :] for l in range(scr.shape[0])], axis=1)


def _rows_gather(scr, q, g, rows):
    p = _slab_pitch(rows)
    return jnp.concatenate([scr[l, pl.ds(q, g, stride=p), :] for l in range(scr.shape[0])], axis=1)


def _rows_scatter(scr, q, val, rows):
    p = _slab_pitch(rows)
    for l in range(scr.shape[0]):
        scr[l, pl.ds(q, val.shape[0], stride=p), :] = val[:, l * LANES:(l + 1) * LANES]


def _left_mm_kernel(w_ref, x_ref, *rest, epilogue):
    x = x_ref[...]
    x = x.reshape(-1, x.shape[-1])
    acc = jnp.dot(w_ref[...], x.astype(BF16), preferred_element_type=F32)
    epilogue(acc, rest[:-1], rest[-1])


def _scaled_store(scale):
    def epilogue(acc, extras, o_ref):
        o_ref[...] = (acc * scale).reshape(o_ref.shape).astype(o_ref.dtype)
    return epilogue


def _left_mm(w, x, *, grid, w_spec, x_spec, out_shape, out_spec, extras=(), epilogue=_scaled_store(1.0),
             aliases=None, name="left_mm"):
    return pl.pallas_call(
        functools.partial(_left_mm_kernel, epilogue=epilogue),
        grid=grid,
        in_specs=[w_spec, x_spec] + [s for _, s in extras],
        out_specs=out_spec,
        out_shape=out_shape,
        input_output_aliases=aliases or {},
        compiler_params=_cparams(*(("arbitrary",) * len(grid))),
        name=name,
    )(w, x, *[a for a, _ in extras])


def _fnet_channel_kernel(x_ref, w_ref, o_ref):
    cg = x_ref.shape[1]
    r = jnp.dot(x_ref[...], w_ref[...], preferred_element_type=F32)
    o_ref[0] = r[:, :cg].astype(o_ref.dtype)
    o_ref[1] = r[:, cg:].astype(o_ref.dtype)


def _fnet_channel(h, rows, row_off, n_seq, seq_len):
    tm = rows.tile
    d = h.shape[1]
    cg = d // FNET_GROUPS
    ci, ki = _iota2(cg, cg)
    cc, sc = _cis(ci * ki, cg)
    wc = jnp.concatenate([cc, -sc], axis=1).astype(BF16)
    st = seq_len // tm
    return pl.pallas_call(
        _fnet_channel_kernel,
        grid=(n_seq * st, FNET_GROUPS),
        in_specs=[pl.BlockSpec((tm, cg), lambda i, g: (i + row_off, g)),
                  pl.BlockSpec((cg, 2 * cg), lambda i, g: (0, 0))],
        out_specs=pl.BlockSpec((None, 2, tm, cg), lambda i, g: (i // st, 0, i % st, g)),
        out_shape=jax.ShapeDtypeStruct((n_seq, 2, seq_len, d), BF16),
        compiler_params=_cparams("arbitrary", "arbitrary"),
        name="fnet_channel",
    )(h, wc)


def _fnet_positions_dense(z, tc):
    n_seq, _, seq_len, d = z.shape
    scale = 1.0 / math.sqrt(seq_len * (d // FNET_GROUPS))
    ki, ci2 = _iota2(seq_len, 2 * seq_len)
    gc, gs = _cis(ki * (ci2 % seq_len), seq_len)
    g2 = jnp.where(ci2 < seq_len, gc, gs).astype(BF16)
    out = _left_mm(g2, z,
                   grid=(n_seq, d // tc),
                   w_spec=pl.BlockSpec((seq_len, 2 * seq_len), lambda s, j: (0, 0)),
                   x_spec=pl.BlockSpec((None, 2, seq_len, tc), lambda s, j: (s, 0, 0, j)),
                   out_shape=jax.ShapeDtypeStruct((n_seq, seq_len, d), BF16),
                   out_spec=pl.BlockSpec((None, seq_len, tc), lambda s, j: (s, 0, j)),
                   epilogue=_scaled_store(scale), name="fnet_dense")
    return out.reshape(n_seq * seq_len, d)


FNET_ROWS = 2048
FNET_K1_GROUP = 16


def _fnet_channel_split_kernel(x_ref, w_ref, zr_ref, zi_ref, rg_ref):
    cg = x_ref.shape[1]
    g = x_ref.shape[0] // DFT_N2
    r = jnp.dot(x_ref[...], w_ref[...], preferred_element_type=F32)
    for plane, z_ref in enumerate((zr_ref, zi_ref)):
        for s in range(g):
            _slab_store(rg_ref, s, r[s * DFT_N2:(s + 1) * DFT_N2, plane * cg:(plane + 1) * cg], DFT_N2)

        def body(n2, c):
            z_ref[n2] = _rows_gather(rg_ref, n2, g, DFT_N2).astype(z_ref.dtype)
            return c
        lax.fori_loop(0, DFT_N2, body, 0)


def _fnet_channel_split(h, n_seq, seq_len):
    tm = FNET_ROWS
    d = h.shape[1]
    cg = d // FNET_GROUPS
    ci, ki = _iota2(cg, cg)
    cc, sc = _cis(ci * ki, cg)
    wc = jnp.concatenate([cc, -sc], axis=1).astype(BF16)
    st = seq_len // tm
    g = tm // DFT_N2
    assert seq_len % tm == 0 and g % 16 == 0
    plane = jax.ShapeDtypeStruct((n_seq, DFT_N2, seq_len // DFT_N2, d), BF16)
    out_spec = pl.BlockSpec((None, DFT_N2, g, cg), lambda i, c: (i // st, 0, i % st, c))
    return pl.pallas_call(
        _fnet_channel_split_kernel,
        grid=(n_seq * st, FNET_GROUPS),
        in_specs=[pl.BlockSpec((tm, cg), lambda i, c: (i, c)),
                  pl.BlockSpec((cg, 2 * cg), lambda i, c: (0, 0))],
        out_specs=[out_spec, out_spec],
        out_shape=[plane, plane],
        scratch_shapes=[_slab_scratch(g, DFT_N2, cg)],
        compiler_params=_cparams("arbitrary", "arbitrary"),
        name="fnet_channel",
    )(h, wc)


def _fnet_stage1_kernel(w_ref, zr_ref, zi_ref, o_ref):
    w = w_ref[...]
    for s in range(zr_ref.shape[0]):
        z = jnp.concatenate([zr_ref[s], zi_ref[s]], axis=0)
        o_ref[s] = jnp.dot(w, z, preferred_element_type=F32).astype(o_ref.dtype)


def _fnet_stage2_kernel(g_ref, a_ref, o_ref, rin, rout, *, scale):
    kg = a_ref.shape[1] // 2

    def body_in(q, c):
        _rows_scatter(rin, q, a_ref[q].astype(F32), DFT_N2)
        return c
    lax.fori_loop(0, a_ref.shape[0], body_in, 0)
    for j in range(kg):
        r = jnp.dot(g_ref[j], _stage2_operand(rin, j), preferred_element_type=F32)
        _slab_store(rout, j, r * scale, DFT_N2)

    def body_out(q, c):
        o_ref[q] = _rows_gather(rout, q, kg, DFT_N2).astype(o_ref.dtype)
        return c
    lax.fori_loop(0, o_ref.shape[0], body_out, 0)


def _fnet_positions_split(zr, zi):
    n_seq, n2, n1, d = zr.shape
    seq_len = n1 * n2
    scale = 1.0 / math.sqrt(seq_len * (d // FNET_GROUPS))
    r, cidx = _iota2(2 * n1, 2 * n1)
    k1, ro, ri, nn = r // 2, r % 2, cidx // n1, cidx % n1
    fr, fs = _cis(k1 * nn, n1)
    w1 = jnp.where(ro == ri, fr, jnp.where(ro == 0, fs, -fs)).astype(BF16)
    tc = min(SPLIT_TC, d)
    zblk = pl.BlockSpec((None, N2_GROUP, n1, tc), lambda s, g, j: (s, g, 0, j))
    a = pl.pallas_call(
        _fnet_stage1_kernel,
        grid=(n_seq, n2 // N2_GROUP, d // tc),
        in_specs=[pl.BlockSpec((2 * n1, 2 * n1), lambda s, g, j: (0, 0)), zblk, zblk],
        out_specs=pl.BlockSpec((None, N2_GROUP, 2 * n1, tc), lambda s, g, j: (s, g, 0, j)),
        out_shape=jax.ShapeDtypeStruct((n_seq, n2, 2 * n1, d), BF16),
        compiler_params=_cparams("arbitrary", "arbitrary", "arbitrary"),
        name="fnet_stage1",
    )(w1, zr, zi)
    k2i, ci2 = _iota2(n2, 2 * n2)
    kk = jnp.arange(n1, dtype=jnp.int32)[:, None, None] + n1 * k2i[None]
    gc, gs = _cis(kk * (ci2 % n2)[None], seq_len)
    g2 = jnp.where((ci2 < n2)[None], gc, gs).astype(BF16)
    kg = FNET_K1_GROUP
    assert n1 % kg == 0
    out = pl.pallas_call(
        functools.partial(_fnet_stage2_kernel, scale=scale),
        grid=(n_seq, n1 // kg, d // tc),
        in_specs=[pl.BlockSpec((kg, n2, 2 * n2), lambda s, k, j: (k, 0, 0)),
                  pl.BlockSpec((None, n2, 2 * kg, tc), lambda s, k, j: (s, 0, k, j))],
        out_specs=pl.BlockSpec((None, n2, kg, tc), lambda s, k, j: (s, 0, k, j)),
        out_shape=jax.ShapeDtypeStruct((n_seq, n2, n1, d), BF16),
        scratch_shapes=[_slab_scratch(2 * kg, n2, tc), _slab_scratch(kg, n2, tc)],
        compiler_params=_cparams("arbitrary", "arbitrary", "arbitrary"),
        name="fnet_stage2",
    )(g2, a)
    return out.reshape(n_seq * seq_len, d)


def _layer_fnet(x_all, h, mod, rows, w_out, b_out):
    d = x_all.shape[1]
    zc = _fnet_channel(h, rows, rows.ctx_off, rows.batch, rows.ctx_len)
    fc = _fnet_positions_dense(zc, min(d, 512))
    x_all = _proj_residual(fc, w_out, b_out, x_all, mod, 2, rows, rows.ctx_off, name="fnet_out_ctx")
    zr, zi = _fnet_channel_split(h, rows.batch, rows.seq)
    fl = _fnet_positions_split(zr, zi)
    big = _Rows(rows.batch, rows.ctx_len, rows.seq, PROJ_ROWS)
    return _proj_residual(fl, w_out, b_out, x_all, mod, 2, big, big.lat_off, tn=1024, name="fnet_out_lat")


HEAD_SLAB = 2 * LANES
MLA_SCALE = (QK_NOPE + QK_ROPE) ** -0.5
Q_SCALE = MLA_SCALE * math.log2(math.e)
V_SLAB = V_DIM + 16
ATTN_TQ = 2048
ATTN_TK = 1024


def _rope_tables(seq_len, lead_identity_rows):
    n_rows = seq_len // GRID_W
    row = jnp.repeat(jnp.arange(n_rows, dtype=F32), GRID_W)
    col = jnp.tile(jnp.arange(GRID_W, dtype=F32), n_rows)
    half = QK_ROPE // 2
    inv = ROPE_THETA ** (-jnp.arange(0, half, 2, dtype=F32) / half)
    ang_r = row[:, None] * inv
    ang_c = col[:, None] * inv
    ang = jnp.concatenate([ang_r, ang_r, ang_c, ang_c], axis=-1)
    pad = jnp.zeros((seq_len, LANES - QK_ROPE), F32)
    cos = jnp.concatenate([jnp.cos(ang), pad + 1.0], axis=-1)
    sin = jnp.concatenate([jnp.sin(ang), pad], axis=-1)
    if lead_identity_rows:
        cos = jnp.concatenate([jnp.ones((lead_identity_rows, LANES), F32), cos], axis=0)
        sin = jnp.concatenate([jnp.zeros((lead_identity_rows, LANES), F32), sin], axis=0)
    return cos, sin


def _rope(x, cos, sin):
    q = QK_ROPE // 4
    lane = lax.broadcasted_iota(jnp.int32, x.shape, 1)
    even = (lane // q) % 2 == 0
    rot = jnp.where(even, -pltpu.roll(x, LANES - q, 1), pltpu.roll(x, q, 1))
    return x * cos + rot * sin


def _rmsnorm_epilogue(acc, extras, outs):
    g_ref, = extras
    y = acc * lax.rsqrt(jnp.mean(acc * acc, axis=-1, keepdims=True) + EPS) * g_ref[...]
    outs[0][...] = y.astype(outs[0].dtype)


def _q_up_epilogue(acc, extras, outs):
    cos_ref, sin_ref = extras
    o_ref, = outs
    for hh in range(acc.shape[1] // HEAD_SLAB):
        c0 = hh * HEAD_SLAB
        o_ref[:, c0:c0 + LANES] = (acc[:, c0:c0 + LANES] * Q_SCALE).astype(o_ref.dtype)
        pe = _rope(acc[:, c0 + LANES:c0 + HEAD_SLAB], cos_ref[...], sin_ref[...])
        o_ref[:, c0 + LANES:c0 + HEAD_SLAB] = (pe * Q_SCALE).astype(o_ref.dtype)


def _kv_down_epilogue(acc, extras, outs, *, rank):
    g_ref, cos_ref, sin_ref = extras
    kvn_ref, kpe_ref = outs
    lat = acc[:, :rank]
    y = lat * lax.rsqrt(jnp.mean(lat * lat, axis=-1, keepdims=True) + EPS) * g_ref[...]
    kvn_ref[...] = y.astype(kvn_ref.dtype)
    kpe_ref[...] = _rope(acc[:, rank:rank + LANES], cos_ref[...], sin_ref[...]).astype(kpe_ref.dtype)


def _k_up_epilogue(acc, extras, outs):
    kpe_ref, = extras
    o_ref, = outs
    for hh in range(acc.shape[1] // LANES):
        o_ref[:, hh * HEAD_SLAB:hh * HEAD_SLAB + LANES] = acc[:, hh * LANES:(hh + 1) * LANES].astype(o_ref.dtype)
        o_ref[:, hh * HEAD_SLAB + LANES:(hh + 1) * HEAD_SLAB] = kpe_ref[...]


def _attn_kernel(q_ref, kc_ref, kl_ref, vc_ref, vl_ref, o_ref, *, tk):
    q = q_ref[...]

    def scores(k_tile):
        return lax.dot_general(k_tile, q, (((1,), (1,)), ((), ())), preferred_element_type=F32)

    def probs(s, m):
        return jnp.exp2((s - m).astype(BF16))

    s = scores(kc_ref[...])
    m = jnp.max(s, axis=0, keepdims=True)
    acc = jnp.dot(vc_ref[...], probs(s, m), preferred_element_type=F32)
    for j in range(kl_ref.shape[0] // tk):
        s = scores(kl_ref[j * tk:(j + 1) * tk, :])
        m_new = jnp.maximum(m, jnp.max(s, axis=0, keepdims=True))
        acc = jnp.exp2(m - m_new) * acc + jnp.dot(vl_ref[:, j * tk:(j + 1) * tk], probs(s, m_new),
                                                  preferred_element_type=F32)
        m = m_new
    o_ref[...] = (acc[:V_DIM] / acc[V_DIM:V_DIM + 1]).T.astype(o_ref.dtype)


def _value_slabs(v, batch, keys):
    v4 = v.reshape(batch, keys, MLA_HEADS, V_DIM)
    ones = jnp.ones((batch, keys, MLA_HEADS, 1), v.dtype)
    pad = jnp.zeros((batch, keys, MLA_HEADS, V_SLAB - V_DIM - 1), v.dtype)
    return jnp.concatenate([v4, ones, pad], axis=-1).transpose(0, 2, 3, 1).reshape(batch, MLA_HEADS * V_SLAB, keys)


def _attention(q, k_ctx, k_lat, vt_ctx, vt_lat):
    batch, seq, _ = q.shape
    ctx_len = k_ctx.shape[1]
    tq = min(ATTN_TQ, seq)
    tk = min(ATTN_TK, seq)
    assert seq % tq == 0 and seq % tk == 0
    return pl.pallas_call(
        functools.partial(_attn_kernel, tk=tk),
        grid=(batch, MLA_HEADS, seq // tq),
        in_specs=[pl.BlockSpec((None, tq, HEAD_SLAB), lambda b, h, i: (b, i, h)),
                  pl.BlockSpec((None, ctx_len, HEAD_SLAB), lambda b, h, i: (b, 0, h)),
                  pl.BlockSpec((None, seq, HEAD_SLAB), lambda b, h, i: (b, 0, h)),
                  pl.BlockSpec((None, V_SLAB, ctx_len), lambda b, h, i: (b, h, 0)),
                  pl.BlockSpec((None, V_SLAB, seq), lambda b, h, i: (b, h, 0))],
        out_specs=pl.BlockSpec((None, tq, V_DIM), lambda b, h, i: (b, i, h)),
        out_shape=jax.ShapeDtypeStruct((batch, seq, MLA_HEADS * V_DIM), BF16),
        compiler_params=_cparams("arbitrary", "arbitrary", "arbitrary"),
        name="mla_attention",
    )(q, k_ctx, k_lat, vt_ctx, vt_lat)


def _layer_mla(x_all, h, mod, rows, w_dq, g_q, w_uq, w_dkv, g_kv, w_ukv, w_o, update_ctx=False):
    assert not update_ctx, "attention is the last mixer of the stack: context queries are never needed"
    rows = _Rows(rows.batch, rows.ctx_len, rows.seq, PROJ_ROWS)
    tm = rows.tile
    d = x_all.shape[1]
    batch, seq, ctx_len = rows.batch, rows.seq, rows.ctx_len
    q_rank, kv_rank = w_dq.shape[1], g_kv.shape[0]
    assert V_DIM == LANES and QK_NOPE == LANES and QK_ROPE <= LANES

    w_uq_s = jnp.pad(w_uq.reshape(q_rank, MLA_HEADS, QK_NOPE + QK_ROPE),
                     ((0, 0), (0, 0), (0, HEAD_SLAB - QK_NOPE - QK_ROPE))).reshape(q_rank, MLA_HEADS * HEAD_SLAB)
    w_dkv_s = jnp.pad(w_dkv, ((0, 0), (0, kv_rank + LANES - w_dkv.shape[1])))
    w_ukv_s = w_ukv.reshape(kv_rank, MLA_HEADS, QK_NOPE + V_DIM)
    w_uk = w_ukv_s[:, :, :QK_NOPE].reshape(kv_rank, MLA_HEADS * QK_NOPE)
    w_uv = w_ukv_s[:, :, QK_NOPE:].reshape(kv_rank, MLA_HEADS * V_DIM)
    cos, sin = _rope_tables(seq, tm)
    lat_blocks = rows.lat_blocks

    cqn, = _matmul(h, w_dq, tm=tm, tn=q_rank, m_rows=rows.n_lat, x_row_off=rows.lat_off,
                   extras=[(g_q.reshape(1, q_rank), pl.BlockSpec((1, q_rank), lambda j, i: (0, 0)))],
                   outs=[(jax.ShapeDtypeStruct((rows.n_lat, q_rank), BF16), pl.BlockSpec((tm, q_rank), lambda j, i: (i, 0)))],
                   epilogue=_rmsnorm_epilogue, name="mla_q_down")
    tnq = 4 * HEAD_SLAB
    rope_lat = pl.BlockSpec((tm, LANES), lambda j, i: (1 + i % lat_blocks, 0))
    q, = _matmul(cqn, w_uq_s, tm=tm, tn=tnq, extras=[(cos, rope_lat), (sin, rope_lat)],
                 outs=[(jax.ShapeDtypeStruct((rows.n_lat, MLA_HEADS * HEAD_SLAB), BF16),
                        pl.BlockSpec((tm, tnq), lambda j, i: (i, j)))],
                 epilogue=_q_up_epilogue, name="mla_q_up")

    def keys_values(row_off, n_rows, rope_spec):
        kvn, kpe = _matmul(h, w_dkv_s, tm=tm, tn=kv_rank + LANES, m_rows=n_rows, x_row_off=row_off,
                           extras=[(g_kv.reshape(1, kv_rank), pl.BlockSpec((1, kv_rank), lambda j, i: (0, 0))),
                                   (cos, rope_spec), (sin, rope_spec)],
                           outs=[(jax.ShapeDtypeStruct((n_rows, kv_rank), BF16), pl.BlockSpec((tm, kv_rank), lambda j, i: (i, 0))),
                                 (jax.ShapeDtypeStruct((n_rows, LANES), BF16), pl.BlockSpec((tm, LANES), lambda j, i: (i, 0)))],
                           epilogue=functools.partial(_kv_down_epilogue, rank=kv_rank), name="mla_kv_down")
        tnk = 4 * LANES
        k, = _matmul(kvn, w_uk, tm=tm, tn=tnk,
                     extras=[(kpe, pl.BlockSpec((tm, LANES), lambda j, i: (i, 0)))],
                     outs=[(jax.ShapeDtypeStruct((n_rows, MLA_HEADS * HEAD_SLAB), BF16),
                            pl.BlockSpec((tm, 2 * tnk), lambda j, i: (i, j)))],
                     epilogue=_k_up_epilogue, name="mla_k_up")
        v, = _matmul(kvn, w_uv, tm=tm, tn=tnk,
                     outs=[(jax.ShapeDtypeStruct((n_rows, MLA_HEADS * V_DIM), BF16), pl.BlockSpec((tm, tnk), lambda j, i: (i, j)))],
                     name="mla_v_up")
        return k, v

    k_c, v_c = keys_values(rows.ctx_off, rows.n_ctx,pl.BlockSpec((tm, LANES), lambda j, i: (0, 0)))
    k_l, v_l = keys_values(rows.lat_off, rows.n_lat, rope_lat)
    vt_c = _value_slabs(v_c, batch, ctx_len)
    vt_l = _value_slabs(v_l, batch, seq)
    attn = _attention(q.reshape(batch, seq, -1), k_c.reshape(batch, ctx_len, -1), k_l.reshape(batch, seq, -1), vt_c, vt_l)
    attn = attn.reshape(rows.n_lat, MLA_HEADS * V_DIM)
    return _proj_residual(attn, w_o, None, x_all, mod, 2, rows, rows.lat_off, tn=1024, name="mla_out")


CONV_HALO = 16
MAX_DECAY = math.log(DECAY_TARGET) / FAST_DECAY_PCT
MIN_DECAY = math.log(DECAY_TARGET) / SLOW_DECAY_PCT


def _hyena_in_kernel(prev_ref, main_ref, next_ref, w_ref, b_ref, cw_ref, cb_ref, o_ref, xext_ref, zext_ref, wbf_ref,
                     *rg, tm, seq_len):
    i = pl.program_id(1)
    seq_tiles = seq_len // tm
    t_in_seq = i % seq_tiles

    @pl.when(i == 0)
    def _():
        wbf_ref[...] = w_ref[...].astype(BF16)

    xext_ref[0:CONV_HALO, :] = prev_ref[...]
    xext_ref[CONV_HALO:CONV_HALO + tm, :] = main_ref[...]
    xext_ref[CONV_HALO + tm:2 * CONV_HALO + tm, :] = next_ref[...]
    row = lax.broadcasted_iota(jnp.int32, (tm, 1), 0)
    first = (row == 0) & (t_in_seq == 0)
    last = (row == tm - 1) & (t_in_seq == seq_tiles - 1)
    tn = zext_ref.shape[1]
    cw = min(tn, 2 * LANES)
    for c0 in range(0, tn, cw):
        cols = slice(c0, c0 + cw)
        zext_ref[:, cols] = jnp.dot(xext_ref[...], wbf_ref[:, cols], preferred_element_type=F32) + b_ref[:, cols]
        prev = jnp.where(first, 0.0, zext_ref[pl.ds(CONV_HALO - 1, tm), cols])
        nxt = jnp.where(last, 0.0, zext_ref[pl.ds(CONV_HALO + 1, tm), cols])
        out = (prev * cw_ref[0:1, cols] + zext_ref[pl.ds(CONV_HALO, tm), cols] * cw_ref[1:2, cols]
               + nxt * cw_ref[2:3, cols] + cb_ref[:, cols])
        if rg:
            for s in range(tm // DFT_N2):
                _slab_store(rg[0], s, out[s * DFT_N2:(s + 1) * DFT_N2, :], DFT_N2, first_chunk=c0 // LANES)
        else:
            o_ref[:, cols] = out.astype(o_ref.dtype)
    if not rg:
        return
    rg_ref, = rg

    def body(n2, c):
        o_ref[n2] = _rows_gather(rg_ref, n2, tm // DFT_N2, DFT_N2).astype(o_ref.dtype)
        return c
    lax.fori_loop(0, DFT_N2, body, 0)


def _hyena_in(h, w_in, b_in, conv_w, conv_b, rows, row_off, n_rows, seq_len, tn=512, time_split=False):
    tm = rows.tile
    k, n3 = w_in.shape
    d = n3 // 3
    tn = min(tn, d)
    nd = d // tn
    hb = tm // CONV_HALO
    last_hblk = h.shape[0] // CONV_HALO - 1
    if time_split:
        g = tm // DFT_N2
        assert tm % DFT_N2 == 0 and g % 8 == 0 and seq_len % tm == 0
        out_spec = pl.BlockSpec((None, DFT_N2, g, tn), lambda j, i: (j // nd, 0, i, j % nd))
        out_shape = jax.ShapeDtypeStruct((3, DFT_N2, n_rows // DFT_N2, d), F32)
        extra_scratch = [_slab_scratch(g, DFT_N2, tn)]
    else:
        out_spec = pl.BlockSpec((None, tm, tn), lambda j, i: (j // nd, i, j % nd))
        out_shape = jax.ShapeDtypeStruct((3, n_rows, d), BF16)
        extra_scratch = []
    return pl.pallas_call(
        functools.partial(_hyena_in_kernel, tm=tm, seq_len=seq_len),
        grid=(n3 // tn, n_rows // tm),
        in_specs=[pl.BlockSpec((CONV_HALO, k), lambda j, i: (jnp.maximum((i + row_off) * hb - 1, 0), 0)),
                  pl.BlockSpec((tm, k), lambda j, i: (i + row_off, 0)),
                  pl.BlockSpec((CONV_HALO, k), lambda j, i: (jnp.minimum((i + row_off + 1) * hb, last_hblk), 0)),
                  pl.BlockSpec((k, tn), lambda j, i: (0, j)),
                  pl.BlockSpec((1, tn), lambda j, i: (0, j)),
                  pl.BlockSpec((3, tn), lambda j, i: (0, j)),
                  pl.BlockSpec((1, tn), lambda j, i: (0, j))],
        out_specs=out_spec,
        out_shape=out_shape,
        scratch_shapes=[pltpu.VMEM((tm + 2 * CONV_HALO, k), BF16), pltpu.VMEM((tm + 2 * CONV_HALO, tn), F32),
                        pltpu.VMEM((k, tn), BF16)] + extra_scratch,
        compiler_params=_cparams("arbitrary", "arbitrary"),
        name="hyena_in",
    )(h, h, h, w_in, b_in.reshape(1, n3), conv_w, conv_b.reshape(1, n3))


def _hyena_filter_kernel(z_ref, w1_ref, b1_ref, w2_ref, b2_ref, f0_ref, f1_ref, w3_ref, dl_ref, k_ref, ss_ref,
                         h2_ref, *rg, tp, seq_len):
    p = pl.program_id(0)
    z = z_ref[...]

    @pl.when(pl.program_id(1) == 0)
    def _():
        h1 = jnp.sin(f0_ref[...] * (jnp.dot(z, w1_ref[...], precision=HIGHEST, preferred_element_type=F32) + b1_ref[...]))
        h2 = jnp.sin(f1_ref[...] * (jnp.dot(h1, w2_ref[...], precision=HIGHEST, preferred_element_type=F32) + b2_ref[...]))
        h2_ref[...] = h2.astype(BF16)

    filt = jnp.dot(h2_ref[...], w3_ref[...].astype(BF16), preferred_element_type=F32)
    t = z[:, 0:1]
    kk = filt * (jnp.exp(-t * dl_ref[...]) + MOD_SHIFT)
    circ = p * tp + lax.broadcasted_iota(jnp.int32, (tp, 1), 0)
    kk = jnp.where(circ == seq_len, 0.0, kk)
    if rg:
        rg_ref, = rg
        for s in range(tp // DFT_N2):
            _slab_store(rg_ref, s, kk[s * DFT_N2:(s + 1) * DFT_N2, :], DFT_N2)

        def body(n2, c):
            k_ref[n2] = _rows_gather(rg_ref, n2, tp // DFT_N2, DFT_N2).astype(k_ref.dtype)
            return c
        lax.fori_loop(0, DFT_N2, body, 0)
    else:
        k_ref[...] = kk.astype(k_ref.dtype)
    ss_ref[...] = jnp.sum(kk * kk, axis=0, keepdims=True)


def _hyena_filters(seq_len, d, f_w1, f_b1, f_w2, f_b2, f_w3, f_freq, tp=256, tc=1024, time_split=False):
    f32 = F32
    hid = f_w1.shape[1]
    od = HYENA_ORDER * d
    t = jnp.linspace(0.0, 1.0, seq_len, dtype=f32)[:, None]
    bands = (FILTER_EMB - 1) // 2
    w = 2.0 * math.pi * jnp.arange(seq_len, dtype=f32)[:, None] / seq_len
    f = jnp.linspace(1e-4, bands - 1, bands, dtype=f32)[None, :]
    z = jnp.concatenate([t, jnp.cos(f * w), -jnp.sin(f * w)], axis=-1)
    circ = jnp.arange(2 * seq_len)
    offs = jnp.where(circ < seq_len, circ, jnp.minimum(2 * seq_len - circ, seq_len - 1))
    z2 = jnp.pad(z[offs], ((0, 0), (0, LANES - FILTER_EMB)))
    w1p = jnp.pad(f_w1, ((0, LANES - FILTER_EMB), (0, 0)))
    w3s = f_w3.reshape(hid, HYENA_ORDER, 2, d).transpose(2, 0, 1, 3).reshape(2, hid, od)
    deltas = jnp.tile(jnp.abs(jnp.linspace(MIN_DECAY, MAX_DECAY, d, dtype=f32)), HYENA_ORDER)[None, :]
    tp = min(tp, seq_len)
    tc = min(tc, od)
    side_tiles = seq_len // tp
    small = lambda shape: pl.BlockSpec(shape, lambda p, j: (0,) * len(shape))
    n_ptiles = 2 * side_tiles
    if time_split:
        g = tp // DFT_N2
        assert tp % DFT_N2 == 0 and g % 16 == 0
        k_spec = pl.BlockSpec((DFT_N2, g, tc), lambda p, j: (0, p, j))
        k_shape = jax.ShapeDtypeStruct((DFT_N2, 2 * seq_len // DFT_N2, od), BF16)
        extra_scratch = [_slab_scratch(g, DFT_N2, tc)]
    else:
        k_spec = pl.BlockSpec((tp, tc), lambda p, j: (p, j))
        k_shape = jax.ShapeDtypeStruct((2 * seq_len, od), BF16)
        extra_scratch = []
    k, ss_parts = pl.pallas_call(
        functools.partial(_hyena_filter_kernel, tp=tp, seq_len=seq_len),
        grid=(n_ptiles, od // tc),
        in_specs=[pl.BlockSpec((tp, LANES), lambda p, j: (p, 0)),
                  small((LANES, hid)), small((1, hid)), small((hid, hid)), small((1, hid)), small((1, hid)), small((1, hid)),
                  pl.BlockSpec((None, hid, tc), lambda p, j: (p // side_tiles, 0, j)),
                  pl.BlockSpec((1, tc), lambda p, j: (0, j))],
        out_specs=[k_spec, pl.BlockSpec((None, 1, tc), lambda p, j: (p, 0, j))],
        out_shape=[k_shape, jax.ShapeDtypeStruct((n_ptiles, 1, od), F32)],
        scratch_shapes=[pltpu.VMEM((tp, hid), BF16)] + extra_scratch,
        compiler_params=_cparams("arbitrary", "arbitrary"),
        name="hyena_filters",
    )(z2, w1p, f_b1.reshape(1, hid), f_w2, f_b2.reshape(1, hid), f_freq[0:1], f_freq[1:2], w3s, deltas)
    return k, jnp.sum(ss_parts, axis=0)


def _stage2_tables(n, n1):
    n2 = n // n1
    r, c = _iota2(2 * n2, 2 * n2)
    kk = jnp.arange(n1, dtype=jnp.int32)[:, None, None] + n1 * (r % n2)[None]
    gc, gs = _cis(kk * (c % n2)[None], n)
    same = ((r < n2) == (c < n2))[None]
    fwd = jnp.where(same, gc, jnp.where((r < n2)[None], gs, -gs))
    return fwd.astype(BF16), fwd.swapaxes(1, 2).astype(BF16)


def _complex_mul(x, kf, n2):
    xr, xi = x[:n2], x[n2:]
    kr, ki = kf[:n2].astype(F32), kf[n2:].astype(F32)
    return jnp.concatenate([xr * kr - xi * ki, xr * ki + xi * kr], axis=0)


def _conv_dense_kernel(fk_ref, fz_ref, fzt_ref, k_ref, ss_ref, u_ref, xg_ref, skip_ref, o_ref):
    n = k_ref.shape[0]
    kf = jnp.dot(fk_ref[...], k_ref[...], preferred_element_type=F32) * ss_ref[...]
    u = u_ref[...]
    z = jnp.dot(fz_ref[...], u, preferred_element_type=F32)
    y = _complex_mul(z, kf, n)
    conv = jnp.dot(fzt_ref[...], y.astype(BF16), preferred_element_type=F32)
    o_ref[...] = (xg_ref[...].astype(F32) * (conv + skip_ref[...] * u.astype(F32))).astype(o_ref.dtype)


def _hyena_long_conv_dense(u, xg, k2u, sumsq, order, skip, seq_len):
    n_rows, d = u.shape
    assert n_rows == 2 * seq_len
    n = 2 * seq_len
    colscale = lax.rsqrt(sumsq + EPS) / n
    r, c = _iota2(2 * n, n)
    kc, ks = _cis((r % n) * c, n)
    fk = jnp.where(r < n, kc, -ks).astype(BF16)
    r, c = _iota2(2 * n, 2 * seq_len)
    zc, zs = _cis((r % n) * (c % seq_len), n)
    fz = jnp.where((r < n) == (c < seq_len), zc, jnp.where(r < n, zs, -zs))
    tc = min(d, 512)
    nd = d // tc
    full = lambda a: pl.BlockSpec(a.shape, lambda j: (0, 0))
    fz_b, fzt_b = fz.astype(BF16), fz.T.astype(BF16)
    blk = pl.BlockSpec((n_rows, tc), lambda j: (0, j))
    return pl.pallas_call(
        _conv_dense_kernel,
        grid=(nd,),
        in_specs=[full(fk), full(fz_b), full(fzt_b),
                  pl.BlockSpec((n, tc), lambda j: (0, order * nd + j)),
                  pl.BlockSpec((1, tc), lambda j: (0, order * nd + j)),
                  blk, blk, pl.BlockSpec((1, tc), lambda j: (0, j))],
        out_specs=blk,
        out_shape=jax.ShapeDtypeStruct((n_rows, d), BF16),
        compiler_params=_cparams("arbitrary"),
        name="hyena_conv_dense",
    )(fk, fz_b, fzt_b, k2u, colscale, u, xg, skip.reshape(1, d))


SPLIT_ROWS = 1024
N2_GROUP = 8
K1_GROUP = 8
SPLIT_TC = 512


def _slab_mm_kernel(w_ref, x_ref, *rest, epilogue):
    w = w_ref[...]
    for s in range(x_ref.shape[0]):
        acc = jnp.dot(w, x_ref[s].astype(BF16), preferred_element_type=F32)
        epilogue(acc, s, rest[:-1], rest[-1])


def _slab_plain(acc, s, extras, o_ref):
    o_ref[s] = acc.astype(o_ref.dtype)


def _slab_gate(acc, s, extras, o_ref):
    xg_ref, u_ref, skip_ref = extras
    o_ref[s] = (xg_ref[s] * (acc + skip_ref[...] * u_ref[s])).astype(o_ref.dtype)


def _slab_mm(w, x, out_dtype, slab_extras=(), row_extras=(), epilogue=_slab_plain, name="slab_mm"):
    x_arr, x_lead = x
    n2, k, c = x_arr.shape[len(x_lead):]
    tc = min(SPLIT_TC, c)

    def blk(r, lead=()):
        return pl.BlockSpec((None,) * len(lead) + (N2_GROUP, r, tc), lambda g, j: tuple(lead) + (g, 0, j))

    in_specs = [pl.BlockSpec(w.shape, lambda g, j: (0, 0)), blk(k, x_lead)]
    in_specs += [blk(a.shape[-2], lead) for a, lead in slab_extras]
    in_specs += [pl.BlockSpec((1, tc), lambda g, j: (0, j)) for _ in row_extras]
    return pl.pallas_call(
        functools.partial(_slab_mm_kernel, epilogue=epilogue),
        grid=(n2 // N2_GROUP, c // tc),
        in_specs=in_specs,
        out_specs=blk(w.shape[0]),
        out_shape=jax.ShapeDtypeStruct((n2, w.shape[0], c), out_dtype),
        compiler_params=_cparams("arbitrary", "arbitrary"),
        name=name,
    )(w, x_arr, *[a for a, _ in slab_extras], *row_extras)


def _stage2_regroup_in(a_ref, rin):
    def body(q, c):
        _rows_scatter(rin, q, a_ref[q].astype(F32), DFT_N2)
        return c
    lax.fori_loop(0, a_ref.shape[0], body, 0)


def _stage2_operand(rin, j):
    return jnp.concatenate([_slab_load(rin, 2 * j, DFT_N2), _slab_load(rin, 2 * j + 1, DFT_N2)], axis=0).astype(BF16)


def _kf_stage2_kernel(g_ref, a_ref, ss_ref, o_ref, rin):
    _stage2_regroup_in(a_ref, rin)
    for j in range(K1_GROUP):
        kf = jnp.dot(g_ref[j], _stage2_operand(rin, j), preferred_element_type=F32)
        o_ref[j] = (kf * ss_ref[...]).astype(o_ref.dtype)


def _conv_stage2_kernel(gf_ref, gi_ref, a_ref, kf_ref, o_ref, rin, rout):
    _stage2_regroup_in(a_ref, rin)
    for j in range(K1_GROUP):
        x = jnp.dot(gf_ref[j], _stage2_operand(rin, j), preferred_element_type=F32)
        y = _complex_mul(x, kf_ref[j], DFT_N2)
        b = jnp.dot(gi_ref[j], y.astype(BF16), preferred_element_type=F32)
        _slab_store(rout, 2 * j, b[:DFT_N2], DFT_N2)
        _slab_store(rout, 2 * j + 1, b[DFT_N2:], DFT_N2)

    def body(q, c):
        o_ref[q] = _rows_gather(rout, q, 2 * K1_GROUP, DFT_N2).astype(o_ref.dtype)
        return c
    lax.fori_loop(0, o_ref.shape[0], body, 0)


def _hyena_kf_split(k3, sumsq, tables):
    n2, n1, od = k3.shape
    n = n1 * n2
    colscale = lax.rsqrt(sumsq + EPS) / n
    r, c = _iota2(2 * n1, n1)
    fr, fs = _cis((r // 2) * c, n1)
    w1 = jnp.where(r % 2 == 0, fr, -fs).astype(BF16)
    a = _slab_mm(w1, (k3, ()), BF16, name="hyena_kf_stage1")
    tc = min(SPLIT_TC, od)
    return pl.pallas_call(
        _kf_stage2_kernel,
        grid=(n1 // K1_GROUP, od // tc),
        in_specs=[pl.BlockSpec((K1_GROUP, 2 * n2, 2 * n2), lambda k, j: (k, 0, 0)),
                  pl.BlockSpec((n2, 2 * K1_GROUP, tc), lambda k, j: (0, k, j)),
                  pl.BlockSpec((1, tc), lambda k, j: (0, j))],
        out_specs=pl.BlockSpec((K1_GROUP, 2 * n2, tc), lambda k, j: (k, 0, j)),
        out_shape=jax.ShapeDtypeStruct((n1, 2 * n2, od), BF16),
        scratch_shapes=[_slab_scratch(2 * K1_GROUP, n2, tc)],
        compiler_params=_cparams("arbitrary", "arbitrary"),
        name="hyena_kf_stage2",
    )(tables[0], a, colscale)


def _hyena_long_conv_split(v, xg, kf, kf_col_off, skip, tables):
    n2, m, d = v[0].shape[len(v[1]):]
    hn = m // 2
    n1 = 2 * hn
    g_fwd, g_inv = tables
    r, c = _iota2(2 * n1, 2 * hn)
    fr, fs = _cis((r // 2) * (c % hn), n1)
    ro, ri = r % 2, c // hn
    w1 = jnp.where(ro == ri, fr, jnp.where(ro == 0, fs, -fs)).astype(BF16)
    a = _slab_mm(w1, v, BF16, name="hyena_conv_stage1")
    tc = min(SPLIT_TC, d)
    nd = d // tc
    grp = pl.BlockSpec((n2, 2 * K1_GROUP, tc), lambda k, j: (0, k, j))
    tab = pl.BlockSpec((K1_GROUP, 2 * n2, 2 * n2), lambda k, j: (k, 0, 0))
    b = pl.pallas_call(
        _conv_stage2_kernel,
        grid=(n1 // K1_GROUP, nd),
        in_specs=[tab, tab, grp,
                  pl.BlockSpec((K1_GROUP, 2 * n2, tc), lambda k, j: (k, 0, kf_col_off * nd + j))],
        out_specs=grp,
        out_shape=jax.ShapeDtypeStruct((n2, 2 * n1, d), BF16),
        scratch_shapes=[_slab_scratch(2 * K1_GROUP, n2, tc), _slab_scratch(2 * K1_GROUP, n2, tc)],
        compiler_params=_cparams("arbitrary", "arbitrary"),
        name="hyena_conv_stage2",
    )(g_fwd, g_inv, a, kf)
    r, c = _iota2(2 * hn, 2 * n1)
    ec, es = _cis((r % hn) * (c // 2), n1)
    ro, ri = r // hn, c % 2
    w3 = jnp.where(ro == ri, ec, jnp.where(ro == 0, -es, es)).astype(BF16)
    return _slab_mm(w3, (b, ()), F32, slab_extras=[xg, v], row_extras=[skip.reshape(1, d)],
                    epilogue=_slab_gate, name="hyena_conv_stage3")


def _proj_split_kernel(x_ref, w_ref, b_ref, res_ref, gate_ref, o_ref, wbf_ref, acc_ref, rg_ref):
    @pl.when(pl.program_id(1) == 0)
    def _():
        wbf_ref[...] = w_ref[...].astype(BF16)

    n2, g, k = x_ref.shape
    x = x_ref[...].reshape(n2 * g, k).astype(BF16)
    acc_ref[...] = jnp.dot(x, wbf_ref[...], preferred_element_type=F32) + b_ref[...]

    def body(q, c):
        _rows_scatter(rg_ref, q, acc_ref[pl.ds(pl.multiple_of(q * g, g), g), :], n2)
        return c
    lax.fori_loop(0, n2, body, 0)
    for s in range(g):
        rows = slice(s * n2, (s + 1) * n2)
        o_ref[rows, :] = res_ref[rows, :] + gate_ref[...] * _slab_load(rg_ref, s, n2)


def _proj_residual_split(v3, w, bias, x_all, mod, chunk, seq_len, tn=512):
    n2, m, k = v3.shape
    d = x_all.shape[1]
    tn = min(tn, d)
    nd = d // tn
    g = SPLIT_ROWS // n2
    tm = n2 * g
    seq_tiles = seq_len // tm
    mod3 = mod.reshape(MOD_ROWS, 1, -1)
    res_spec = pl.BlockSpec((tm, tn), lambda j, i: (i, j))
    return pl.pallas_call(
        _proj_split_kernel,
        grid=(nd, m // g),
        in_specs=[pl.BlockSpec((n2, g, k), lambda j, i: (0, i, 0)),
                  pl.BlockSpec((k, tn), lambda j, i: (0, j)),
                  pl.BlockSpec((1, tn), lambda j, i: (0, j)),
                  res_spec,
                  pl.BlockSpec((None, 1, tn), lambda j, i: (i // seq_tiles, 0, chunk * nd + j))],
        out_specs=res_spec,
        out_shape=jax.ShapeDtypeStruct(x_all.shape, F32),
        scratch_shapes=[pltpu.VMEM((k, tn), BF16), pltpu.VMEM((tm, tn), F32), _slab_scratch(g, n2, tn)],
        input_output_aliases={3: 0},
        compiler_params=_cparams("arbitrary", "arbitrary"),
        name="hyena_out_lat",
    )(v3, w, bias.reshape(1, d), x_all, mod3)


def _layer_hyena(x_all, h, mod, rows, prm, w_out, b_out):
    (w_in, b_in, conv_w, conv_b, f_w1, f_b1, f_w2, f_b2, f_w3, f_freq, skip) = prm
    d = w_in.shape[0]
    assert rows.batch == 2, "the long convolution carries the two batch rows as one complex sequence"
    fprm = (f_w1, f_b1, f_w2, f_b2, f_w3, f_freq)
    zc = _hyena_in(h, w_in, b_in, conv_w, conv_b, rows, rows.ctx_off, rows.n_ctx, rows.ctx_len)
    k2u, sumsq = _hyena_filters(rows.ctx_len, d, *fprm)
    vc = _hyena_long_conv_dense(zc[2], zc[0], k2u, sumsq, 0, skip[0], rows.ctx_len)
    vc = _hyena_long_conv_dense(vc, zc[1], k2u, sumsq, 1, skip[1], rows.ctx_len)
    x_all = _proj_residual(vc, w_out, b_out, x_all, mod, 2, rows, rows.ctx_off, name="hyena_out_ctx")
    seq = rows.seq
    split_rows = _Rows(rows.batch, rows.ctx_len, seq, SPLIT_ROWS)
    zl = _hyena_in(h, w_in, b_in, conv_w, conv_b, split_rows, 0, rows.n_lat, seq, time_split=True)
    k3, sumsq = _hyena_filters(seq, d, *fprm, tp=2 * SPLIT_ROWS, tc=SPLIT_TC, time_split=True)
    tables = _stage2_tables(2 * seq, 2 * seq // DFT_N2)
    kf = _hyena_kf_split(k3, sumsq, tables)
    v = _hyena_long_conv_split((zl, (2,)), (zl, (0,)), kf, 0, skip[0], tables)
    v = _hyena_long_conv_split((v, ()), (zl, (1,)), kf, 1, skip[1], tables)
    return _proj_residual_split(v, w_out, b_out, x_all, mod, 2, seq)


def _layer_pool(x_all, h, mod, rows, w_grp, scale):
    x_all = _pool_mixer(h, w_grp, scale, x_all, mod, 2, rows, rows.ctx_off, rows.n_ctx, rows.ctx_len)
    return _pool_mixer(h, w_grp, scale, x_all, mod, 2, rows, rows.lat_off, rows.n_lat, rows.seq)


def kernel(x, c, ctx, c_ctx, ada_w, ada_b, norm_g, final_g, hy_w_in, hy_b_in, hy_conv_w, hy_conv_b, hy_f_w1, hy_f_b1, hy_f_w2, hy_f_b2, hy_f_w3, hy_f_freq, hy_skip, hy_w_out, hy_b_out, fn_w_out, fn_b_out, pl_w, pl_scale, mla_w_dq, mla_g_q, mla_w_uq, mla_w_dkv, mla_g_kv, mla_w_ukv, mla_w_o, moe_w_group, moe_b_group, moe_w_expert, moe_b_expert, moe_w1, moe_w3, moe_w2):
    batch, seq, d = x.shape
    ctx_len = ctx.shape[1]
    depth = ada_w.shape[0]
    rows = _Rows(batch, ctx_len, seq, ROW_TILE)
    mod = _ada_mod(c, c_ctx, ada_w, ada_b)
    x_all = _pack_rows(x, ctx, rows)
    h_dtype = {0: BF16, 1: BF16, 2: F32, 3: BF16}
    h = _norm_mod(x_all, norm_g[0, 0], mod[0], 0, rows, h_dtype[0])
    for i in range(depth):
        kind, j = i % N_MIXERS, i // N_MIXERS
        m = mod[i]
        if kind == 0:
            prm = (hy_w_in[j], hy_b_in[j], hy_conv_w[j], hy_conv_b[j], hy_f_w1[j], hy_f_b1[j], hy_f_w2[j],
                   hy_f_b2[j], hy_f_w3[j], hy_f_freq[j], hy_skip[j])
            x_all = _layer_hyena(x_all, h, m, rows, prm, hy_w_out[j], hy_b_out[j])
        elif kind == 1:
            x_all = _layer_fnet(x_all, h, m, rows, fn_w_out[j], fn_b_out[j])
        elif kind == 2:
            x_all = _layer_pool(x_all, h, m, rows, pl_w[j], pl_scale[j])
        else:
            x_all = _layer_mla(x_all, h, m, rows, mla_w_dq[j], mla_g_q[j], mla_w_uq[j], mla_w_dkv[j],
                               mla_g_kv[j], mla_w_ukv[j], mla_w_o[j], update_ctx=i < depth - 1)
        ht, eid, wts = _norm_router(x_all, norm_g[i, 1], m, 3, rows, moe_w_group[i], moe_b_group[i],
                                    moe_w_expert[i], moe_b_expert[i])
        y = _moe_experts(ht, eid, wts, moe_w1, moe_w3, moe_w2, i)
        if i + 1 < depth:
            x_all, h = _moe_combine(x_all, y, m, 5, rows,
                                    next_norm=(norm_g[i + 1, 0], mod[i + 1], h_dtype[(i + 1) % N_MIXERS]))
    return _moe_combine(x_all, y, m, 5, rows, final_g=final_g).reshape(batch, seq, d)
```

```python
import functools
import math

import numpy as np
import jax
import jax.numpy as jnp
from jax import lax
from jax.experimental import pallas as pl
from jax.experimental.pallas import tpu as pltpu

F32 = jnp.float32
BF16 = jnp.bfloat16
HIGHEST = lax.Precision.HIGHEST

EPS = 1e-6
LANES = 128
MOD_ROWS = 8
VMEM_LIMIT = 56 * 1024 * 1024

N_MIXERS = 4
HYENA_ORDER = 2
FILTER_EMB = 33
DECAY_TARGET = 1e-2
FAST_DECAY_PCT = 0.3
SLOW_DECAY_PCT = 1.5
MOD_SHIFT = 0.0
FNET_GROUPS = 4
POOL_WINDOWS = (2, 4, 8, 16)
MLA_HEADS = 16
QK_NOPE = 128
QK_ROPE = 64
V_DIM = 128
GRID_W = 64
ROPE_THETA = 10000.0
N_GROUPS = 4
EXPERTS_PER_GROUP = 8
TOPK_EXPERT = 2
ROW_TILE = 256
PROJ_ROWS = 512
MOE_ROWS = 256


def _cparams(*sem):
    return pltpu.CompilerParams(dimension_semantics=sem, vmem_limit_bytes=VMEM_LIMIT)


def _ada_kernel(st_ref, w_ref, b_ref, o_ref, *, nrows):
    s = st_ref[...]
    s = s * jax.nn.sigmoid(s)
    w = w_ref[...]
    o_ref[...] = jnp.broadcast_to(b_ref[...], o_ref.shape)
    for r in range(nrows):
        o_ref[r:r + 1, :] = jnp.sum(s[:, r:r + 1] * w, axis=0, keepdims=True) + b_ref[...]


def _ada_mod(c, c_ctx, ada_w, ada_b):
    depth, d, n = ada_w.shape
    nrows = c.shape[0] + 1
    st = jnp.zeros((d, MOD_ROWS), F32).at[:, :nrows - 1].set(c.T).at[:, nrows - 1].set(c_ctx)
    tn = 1024 if n % 1024 == 0 else n
    return pl.pallas_call(
        functools.partial(_ada_kernel, nrows=nrows),
        grid=(depth, n // tn),
        in_specs=[pl.BlockSpec((d, MOD_ROWS), lambda l, j: (0, 0)),
                  pl.BlockSpec((None, d, tn), lambda l, j: (l, 0, j)),
                  pl.BlockSpec((None, 1, tn), lambda l, j: (l, 0, j))],
        out_specs=pl.BlockSpec((None, MOD_ROWS, tn), lambda l, j: (l, 0, j)),
        out_shape=jax.ShapeDtypeStruct((depth, MOD_ROWS, n), F32),
        compiler_params=_cparams("arbitrary", "arbitrary"),
        name="ada_mod",
    )(st, ada_w, ada_b.reshape(depth, 1, n))


class _Rows:
    def __init__(self, batch, ctx_len, seq, tile):
        assert seq % tile == 0
        self.batch, self.ctx_len, self.seq, self.tile = batch, ctx_len, seq, tile
        self.n_ctx = batch * ctx_len
        self.n_lat = batch * seq
        self.n_all = self.n_ctx + self.n_lat
        self.lat_blocks = seq // tile
        self.lat_off = 0
        self.ctx_off = self.n_lat // tile

    def mod_row(self, i):
        return jnp.where(i >= self.ctx_off, self.batch, i // self.lat_blocks)


def _pack_rows(x, ctx, rows):
    d = x.shape[-1]
    pad = -rows.n_all % SPLIT_ROWS
    return jnp.concatenate([x.reshape(-1, d), ctx.reshape(-1, d), jnp.zeros((pad, d), x.dtype)], axis=0)


HI_MASK = -65536


def _pack_halves(x):
    half = x.shape[1] // 2
    lo = lax.bitcast_convert_type(x[:, :half].astype(BF16).astype(F32), jnp.int32)
    hi = lax.bitcast_convert_type(x[:, half:].astype(BF16).astype(F32), jnp.int32)
    return lax.shift_right_logical(lo, 16) | (hi & HI_MASK)


def _unpack_halves(w):
    lo = lax.bitcast_convert_type(lax.shift_left(w, 16), F32)
    hi = lax.bitcast_convert_type(w & HI_MASK, F32)
    return lo, hi


def _rms_mod(x, g, sh, sc):
    y = x * lax.rsqrt(jnp.mean(x * x, axis=-1, keepdims=True) + EPS) * g
    return y * (1.0 + sc) + sh


def _norm_mod_kernel(x_ref, g_ref, sh_ref, sc_ref, o_ref):
    o_ref[...] = _rms_mod(x_ref[...], g_ref[...], sh_ref[...], sc_ref[...]).astype(o_ref.dtype)


def _norm_mod(x_all, g, mod, chunk, rows, out_dtype):
    n, d = rows.n_all, x_all.shape[1]
    tm = rows.tile
    mod3 = mod.reshape(MOD_ROWS, 1, -1)
    return pl.pallas_call(
        _norm_mod_kernel,
        grid=(n // tm,),
        in_specs=[pl.BlockSpec((tm, d), lambda i: (i, 0)),
                  pl.BlockSpec((1, d), lambda i: (0, 0)),
                  pl.BlockSpec((None, 1, d), lambda i: (rows.mod_row(i), 0, chunk)),
                  pl.BlockSpec((None, 1, d), lambda i: (rows.mod_row(i), 0, chunk + 1))],
        out_specs=pl.BlockSpec((tm, d), lambda i: (i, 0)),
        out_shape=jax.ShapeDtypeStruct((n, d), out_dtype),
        compiler_params=_cparams("arbitrary"),
        name="norm_mod",
    )(x_all, g.reshape(1, d), mod3, mod3)


def _norm_router_kernel(x_ref, g_ref, sh_ref, sc_ref, wr_ref, br_ref, h_ref, eid_ref, wt_ref, whi_ref, wlo_ref,
                        *, n_groups, per_group):
    h = _rms_mod(x_ref[...], g_ref[...], sh_ref[...], sc_ref[...])
    h_ref[...] = _pack_halves(h)

    @pl.when(pl.program_id(0) == 0)
    def _():
        w = wr_ref[...]
        w_hi = w.astype(BF16)
        whi_ref[...] = w_hi
        wlo_ref[...] = (w - w_hi.astype(F32)).astype(BF16)

    h_hi = h.astype(BF16)
    h_lo = (h - h_hi.astype(F32)).astype(BF16)
    logits = (jnp.dot(h_hi, whi_ref[...], preferred_element_type=F32)
              + jnp.dot(h_lo, whi_ref[...], preferred_element_type=F32)
              + jnp.dot(h_hi, wlo_ref[...], preferred_element_type=F32)) + br_ref[...]
    lane = lax.broadcasted_iota(jnp.int32, logits.shape, 1).astype(F32)
    neg = -jnp.inf
    gl = jnp.where(lane < n_groups, logits, neg)
    gmax = jnp.max(gl, axis=-1, keepdims=True)
    p_grp = 1.0 / jnp.sum(jnp.exp(gl - gmax), axis=-1, keepdims=True)
    g_idx = jnp.min(jnp.where(gl == gmax, lane, float(LANES)), axis=-1, keepdims=True)
    lo = n_groups + g_idx * per_group
    el = jnp.where((lane >= lo) & (lane < lo + per_group), logits, neg)
    e1 = jnp.max(el, axis=-1, keepdims=True)
    i1 = jnp.min(jnp.where(el == e1, lane, float(LANES)), axis=-1, keepdims=True)
    el2 = jnp.where(lane == i1, neg, el)
    e2 = jnp.max(el2, axis=-1, keepdims=True)
    i2 = jnp.min(jnp.where(el2 == e2, lane, float(LANES)), axis=-1, keepdims=True)
    r = jnp.exp(e2 - e1)
    w1 = p_grp / (1.0 + r)
    w2 = p_grp * r / (1.0 + r)
    eid = jnp.where(lane == 0, i1 - n_groups, jnp.where(lane == 1, i2 - n_groups, 0.0))
    eid_ref[...] = eid.astype(jnp.int32)
    wt_ref[...] = jnp.where(lane == 0, w1, jnp.where(lane == 1, w2, 0.0))


def _norm_router(x_all, g, mod, chunk, rows, w_group, b_group, w_expert, b_expert):
    n, d = rows.n_all, x_all.shape[1]
    tm = rows.tile
    n_groups = w_group.shape[1]
    n_experts = w_expert.shape[1]
    wr = jnp.zeros((d, LANES), F32).at[:, :n_groups].set(w_group).at[:, n_groups:n_groups + n_experts].set(w_expert)
    br = jnp.zeros((1, LANES), F32).at[0, :n_groups].set(b_group).at[0, n_groups:n_groups + n_experts].set(b_expert)
    mod3 = mod.reshape(MOD_ROWS, 1, -1)
    h, eid, wts = pl.pallas_call(
        functools.partial(_norm_router_kernel, n_groups=n_groups, per_group=n_experts // n_groups),
        grid=(n // tm,),
        in_specs=[pl.BlockSpec((tm, d), lambda i: (i, 0)),
                  pl.BlockSpec((1, d), lambda i: (0, 0)),
                  pl.BlockSpec((None, 1, d), lambda i: (rows.mod_row(i), 0, chunk)),
                  pl.BlockSpec((None, 1, d), lambda i: (rows.mod_row(i), 0, chunk + 1)),
                  pl.BlockSpec((d, LANES), lambda i: (0, 0)),
                  pl.BlockSpec((1, LANES), lambda i: (0, 0))],
        out_specs=[pl.BlockSpec((tm, d // 2), lambda i: (i, 0)),
                   pl.BlockSpec((tm, LANES), lambda i: (i, 0)),
                   pl.BlockSpec((tm, LANES), lambda i: (i, 0))],
        out_shape=[jax.ShapeDtypeStruct((n, d // 2), jnp.int32),
                   jax.ShapeDtypeStruct((n, LANES), jnp.int32),
                   jax.ShapeDtypeStruct((n, LANES), F32)],
        scratch_shapes=[pltpu.VMEM((d, LANES), BF16), pltpu.VMEM((d, LANES), BF16)],
        compiler_params=_cparams("arbitrary"),
        name="norm_router",
    )(x_all, g.reshape(1, d), mod3, mod3, wr, br)
    return h, eid[:, :TOPK_EXPERT], wts[:, :TOPK_EXPERT]


def _moe_dispatch(eid, wts, n_experts, tm):
    n_tok = eid.shape[0]
    n_assign = eid.size
    e_flat = eid.reshape(-1)
    order = jnp.argsort(e_flat).astype(jnp.int32)
    counts = jnp.sum((e_flat[:, None] == jnp.arange(n_experts, dtype=jnp.int32)[None, :]).astype(jnp.int32), axis=0)
    padded = (counts + tm - 1) // tm * tm
    start = jnp.cumsum(counts) - counts
    pend = jnp.cumsum(padded)
    pstart = pend - padded
    n_blocks = (n_assign + n_experts * (tm - 1) + tm - 1) // tm
    blk_start = jnp.arange(n_blocks, dtype=jnp.int32) * tm
    blk_e = jnp.minimum(jnp.sum((pend[None, :] <= blk_start[:, None]).astype(jnp.int32), axis=1), n_experts - 1)
    blk_rows = jnp.clip(pstart[blk_e] + counts[blk_e] - blk_start, 0, tm).astype(jnp.int32)
    row = lax.broadcasted_iota(jnp.int32, (n_blocks, tm), 1)
    slot_e = blk_e[:, None]
    src = start[slot_e] + (blk_start[:, None] + row - pstart[slot_e])
    valid = row < blk_rows[:, None]
    asg = order[jnp.clip(src, 0, n_assign - 1)]
    tok = asg // TOPK_EXPERT
    buf_tok = jnp.where(valid, tok, 0)
    dummy = TOPK_EXPERT * n_tok + (jnp.arange(n_blocks, dtype=jnp.int32)[:, None] % 2) * tm + row
    buf_asg = jnp.where(valid, (asg % TOPK_EXPERT) * n_tok + tok, dummy)
    buf_w = jnp.where(valid, wts.reshape(-1)[asg], 0.0)
    return (buf_tok.reshape(n_blocks, 1, tm), buf_asg.reshape(n_blocks, 1, tm),
            buf_w.reshape(n_blocks, tm, 1), blk_e.astype(jnp.int32), blk_rows)


def _moe_kernel(blk_e_ref, blk_rows_ref, tok_ref, tok_next_ref, asg_ref, roww_ref, h_hbm,
                w1_ref, w3_ref, w2_ref, y_hbm, xbuf, ybuf, w1b, w3b, w2b, gsem, ssem):
    b = pl.program_id(0)
    nb = pl.num_programs(0)
    slot = b % 2
    other = 1 - slot

    tm = xbuf.shape[1]

    def start_gather(ids_ref, s):
        for r in range(tm):
            pltpu.make_async_copy(h_hbm.at[pl.ds(ids_ref[0, r], 1)], xbuf.at[s, pl.ds(r, 1)], gsem.at[s]).start(priority=r % 2)

    def wait_gather(s):
        pltpu.make_async_copy(h_hbm.at[pl.ds(0, tm)], xbuf.at[s], gsem.at[s]).wait()

    def start_scatter(s):
        for r in range(tm):
            pltpu.make_async_copy(ybuf.at[s, pl.ds(r, 1)], y_hbm.at[pl.ds(asg_ref[0, r], 1)], ssem.at[s]).start(priority=r % 2)

    def wait_scatter(s):
        pltpu.make_async_copy(ybuf.at[s], y_hbm.at[pl.ds(0, tm)], ssem.at[s]).wait()

    def used(blk):
        return blk_rows_ref[jnp.clip(blk, 0, nb - 1)] > 0

    @pl.when(b == 0)
    def _():
        ybuf[...] = jnp.zeros(ybuf.shape, ybuf.dtype)
        n_real = y_hbm.shape[0] - 2 * tm
        for s in range(2):
            fill = pltpu.make_async_copy(ybuf.at[s], y_hbm.at[pl.ds(n_real + s * tm, tm)], ssem.at[s])
            fill.start()
            fill.wait()

    @pl.when((b == 0) & used(b))
    def _():
        start_gather(tok_ref, slot)

    @pl.when((b + 1 < nb) & used(b + 1))
    def _():
        start_gather(tok_next_ref, other)

    @pl.when((b >= 2) & used(b - 2))
    def _():
        wait_scatter(slot)

    @pl.when(used(b))
    def _():
        wait_gather(slot)

        @pl.when((b == 0) | (blk_e_ref[b] != blk_e_ref[jnp.maximum(b - 1, 0)]))
        def _():
            w1b[...] = w1_ref[...].astype(BF16)
            w3b[...] = w3_ref[...].astype(BF16)
            w2b[...] = w2_ref[...].astype(BF16)

        x_lo, x_hi = _unpack_halves(xbuf[slot])
        x_lo, x_hi = x_lo.astype(BF16), x_hi.astype(BF16)
        half = x_lo.shape[1]

        def up(w_ref):
            return (jnp.dot(x_lo, w_ref[0:half, :], preferred_element_type=F32)
                    + jnp.dot(x_hi, w_ref[half:2 * half, :], preferred_element_type=F32))

        a = up(w1b)
        g = up(w3b)
        hm = (a * jax.nn.sigmoid(a) * g).astype(BF16)
        ybuf[slot] = _pack_halves(jnp.dot(hm, w2b[...], preferred_element_type=F32) * roww_ref[...])
        start_scatter(slot)

    @pl.when(b == nb - 1)
    def _():
        @pl.when((nb >= 2) & used(b - 1))
        def _():
            wait_scatter(other)

        @pl.when(used(b))
        def _():
            wait_scatter(slot)


def _moe_experts(h, eid, wts, w1, w3, w2, layer):
    n, dh = h.shape
    d = 2 * dh
    n_experts, de = w1.shape[1], w1.shape[3]
    tm = MOE_ROWS
    tok, asg, roww, blk_e, blk_rows = _moe_dispatch(eid, wts, n_experts, tm)
    n_blocks = tok.shape[0]
    smem_blk = lambda f: pl.BlockSpec((None, 1, tm), f, memory_space=pltpu.SMEM)
    grid_spec = pltpu.PrefetchScalarGridSpec(
        num_scalar_prefetch=2,
        grid=(n_blocks,),
        in_specs=[smem_blk(lambda b, be, br: (b, 0, 0)),
                  smem_blk(lambda b, be, br: (jnp.minimum(b + 1, n_blocks - 1), 0, 0)),
                  smem_blk(lambda b, be, br: (b, 0, 0)),
                  pl.BlockSpec((None, tm, 1), lambda b, be, br: (b, 0, 0)),
                  pl.BlockSpec(memory_space=pl.ANY),
                  pl.BlockSpec((None, None, d, de), lambda b, be, br: (layer, be[b], 0, 0)),
                  pl.BlockSpec((None, None, d, de), lambda b, be, br: (layer, be[b], 0, 0)),
                  pl.BlockSpec((None, None, de, d), lambda b, be, br: (layer, be[b], 0, 0))],
        out_specs=pl.BlockSpec(memory_space=pl.ANY),
        scratch_shapes=[pltpu.VMEM((2, tm, dh), jnp.int32), pltpu.VMEM((2, tm, dh), jnp.int32),
                        pltpu.VMEM((d, de), BF16), pltpu.VMEM((d, de), BF16), pltpu.VMEM((de, d), BF16),
                        pltpu.SemaphoreType.DMA((2,)), pltpu.SemaphoreType.DMA((2,))],
    )
    return pl.pallas_call(
        _moe_kernel,
        grid_spec=grid_spec,
        out_shape=jax.ShapeDtypeStruct((n * TOPK_EXPERT + 2 * tm, dh), jnp.int32),
        compiler_params=_cparams("arbitrary"),
        name="moe_experts",
    )(blk_e, blk_rows, tok, tok, asg, roww, h, w1, w3, w2)


def _moe_combine_kernel(x_ref, *rest, follow):
    y_refs, gate_ref, rest = rest[:TOPK_EXPERT], rest[TOPK_EXPERT], rest[TOPK_EXPERT + 1:]
    def expert_out(y_ref):
        return jnp.concatenate(_unpack_halves(y_ref[...]), axis=1)

    acc = expert_out(y_refs[0])
    for y_ref in y_refs[1:]:
        acc = acc + expert_out(y_ref)
    x_new = x_ref[...] + gate_ref[...] * acc
    if follow == "next_norm":
        g_ref, sh_ref, sc_ref, o_ref, h_ref = rest
        o_ref[...] = x_new
        h_ref[...] = _rms_mod(x_new, g_ref[...], sh_ref[...], sc_ref[...]).astype(h_ref.dtype)
    else:
        g_ref, o_ref = rest
        o_ref[...] = x_new * lax.rsqrt(jnp.mean(x_new * x_new, axis=-1, keepdims=True) + EPS) * g_ref[...]


def _moe_combine(x_all, y, mod, chunk, rows, *, next_norm=None, final_g=None):
    n, d = rows.n_all, x_all.shape[1]
    tm = rows.tile
    mod3 = mod.reshape(MOD_ROWS, 1, -1)
    nblk = n // tm
    row_blk = pl.BlockSpec((tm, d), lambda i: (i, 0))
    vec = pl.BlockSpec((1, d), lambda i: (0, 0))
    y_specs = [pl.BlockSpec((tm, d // 2), functools.partial(lambda i, k: (i + k * nblk, 0), k=k))
               for k in range(TOPK_EXPERT)]
    in_specs = [row_blk] + y_specs + [pl.BlockSpec((None, 1, d), lambda i: (rows.mod_row(i), 0, chunk))]
    args = [x_all] + [y] * TOPK_EXPERT + [mod3]
    if next_norm is not None:
        g, mod_next, out_dtype = next_norm
        modn = mod_next.reshape(MOD_ROWS, 1, -1)
        in_specs += [vec, pl.BlockSpec((None, 1, d), lambda i: (rows.mod_row(i), 0, 0)),
                     pl.BlockSpec((None, 1, d), lambda i: (rows.mod_row(i), 0, 1))]
        args += [g.reshape(1, d), modn, modn]
        return pl.pallas_call(
            functools.partial(_moe_combine_kernel, follow="next_norm"),
            grid=(nblk,),
            in_specs=in_specs,
            out_specs=[row_blk, row_blk],
            out_shape=[jax.ShapeDtypeStruct(x_all.shape, F32), jax.ShapeDtypeStruct((n, d), out_dtype)],
            input_output_aliases={0: 0},
            compiler_params=_cparams("arbitrary"),
            name="moe_combine",
        )(*args)
    return pl.pallas_call(
        functools.partial(_moe_combine_kernel, follow="final_norm"),
        grid=(rows.n_lat // tm,),
        in_specs=in_specs + [vec],
        out_specs=row_blk,
        out_shape=jax.ShapeDtypeStruct((rows.n_lat, d), F32),
        compiler_params=_cparams("arbitrary"),
        name="moe_combine_final",
    )(*args, final_g.reshape(1, d))


def _mm_kernel(*refs, has_bias, n_extra, epilogue):
    x_ref, w_ref = refs[0], refs[1]
    pos = 2
    b_ref = None
    if has_bias:
        b_ref = refs[pos]
        pos += 1
    extras = refs[pos:pos + n_extra]
    outs = refs[pos + n_extra:-1]
    wbf = refs[-1]

    @pl.when(pl.program_id(1) == 0)
    def _():
        wbf[...] = w_ref[...].astype(BF16)

    acc = jnp.dot(x_ref[...].astype(BF16), wbf[...], preferred_element_type=F32)
    if has_bias:
        acc = acc + b_ref[...]
    epilogue(acc, extras, outs)


def _store_epilogue(acc, extras, outs):
    outs[0][...] = acc.astype(outs[0].dtype)


def _matmul(x, w, *, w_lead=(), bias=None, tm, tn, m_rows=None, x_row_off=0, extras=(), outs,
            epilogue=_store_epilogue, aliases=None, name="matmul"):
    k = x.shape[1]
    n = w.shape[-1]
    m_rows = x.shape[0] if m_rows is None else m_rows
    assert m_rows % tm == 0 and n % tn == 0 and w.shape[-2] == k
    lead = tuple(w_lead)
    in_specs = [pl.BlockSpec((tm, k), lambda j, i: (i + x_row_off, 0)),
                pl.BlockSpec((None,) * len(lead) + (k, tn), lambda j, i: lead + (0, j))]
    args = [x, w]
    if bias is not None:
        in_specs.append(pl.BlockSpec((1, tn), lambda j, i: (0, j)))
        args.append(bias.reshape(1, n))
    for arr, spec in extras:
        in_specs.append(spec)
        args.append(arr)
    return pl.pallas_call(
        functools.partial(_mm_kernel, has_bias=bias is not None, n_extra=len(extras), epilogue=epilogue),
        grid=(n // tn, m_rows // tm),
        in_specs=in_specs,
        out_specs=[spec for _, spec in outs],
        out_shape=[shape for shape, _ in outs],
        scratch_shapes=[pltpu.VMEM((k, tn), BF16)],
        input_output_aliases=aliases or {},
        compiler_params=_cparams("arbitrary", "arbitrary"),
        name=name,
    )(*args)


def _residual_epilogue(acc, extras, outs):
    res_ref, gate_ref = extras
    outs[0][...] = res_ref[...] + gate_ref[...] * acc


def _proj_residual(xin, w, bias, x_all, mod, chunk, rows, row_off, *, w_lead=(), tn=512, name="proj_residual"):
    tm = rows.tile
    d = x_all.shape[1]
    tn = min(tn, d)
    mod3 = mod.reshape(MOD_ROWS, 1, -1)
    nd = d // tn
    res_spec = pl.BlockSpec((tm, tn), lambda j, i: (i + row_off, j))
    gate_spec = pl.BlockSpec((None, 1, tn), lambda j, i: (rows.mod_row(i + row_off), 0, chunk * nd + j))
    n_extra_before = 2 + (bias is not None)
    out, = _matmul(xin, w, w_lead=w_lead, bias=bias, tm=tm, tn=tn,
                   extras=[(x_all, res_spec), (mod3, gate_spec)],
                   outs=[(jax.ShapeDtypeStruct(x_all.shape, F32), res_spec)],
                   epilogue=_residual_epilogue, aliases={n_extra_before: 0}, name=name)
    return out


POOL_HALO = 8


def _pool_kernel(prev_ref, main_ref, next_ref, w_ref, sc_ref, res_ref, gate_ref, o_ref, ext_ref, wbf_ref,
                 *, tm, seq_len, windows):
    i = pl.program_id(0)
    seq_tiles = seq_len // tm
    t_in_seq = i % seq_tiles

    @pl.when(i == 0)
    def _():
        wbf_ref[...] = w_ref[...].astype(BF16)

    zero_halo = jnp.zeros(prev_ref.shape, F32)
    ext_ref[0:POOL_HALO, :] = jnp.where(t_in_seq == 0, zero_halo, prev_ref[...])
    ext_ref[POOL_HALO:POOL_HALO + tm, :] = main_ref[...]
    ext_ref[POOL_HALO + tm:2 * POOL_HALO + tm, :] = jnp.where(t_in_seq == seq_tiles - 1, zero_halo, next_ref[...])

    pos = t_in_seq * tm + lax.broadcasted_iota(jnp.int32, (tm, 1), 0)
    cg = main_ref.shape[1] // len(windows)
    for gi, win in enumerate(windows):
        half = win // 2
        cols = slice(gi * cg, (gi + 1) * cg)
        s = ext_ref[pl.ds(POOL_HALO - half, tm), cols]
        for dlt in range(-half + 1, half):
            s = s + ext_ref[pl.ds(POOL_HALO + dlt, tm), cols]
        cnt = jnp.minimum(pos + half, seq_len) - jnp.maximum(pos - half, 0)
        pooled = s / cnt.astype(F32) - main_ref[:, cols]
        y = jnp.dot(pooled.astype(BF16), wbf_ref[gi], preferred_element_type=F32) * sc_ref[:, cols]
        o_ref[:, cols] = res_ref[:, cols] + gate_ref[:, cols] * y


def _pool_mixer(h, w_grp, scale, x_all, mod, chunk, rows, row_off, n_rows, seq_len):
    tm = rows.tile
    d = x_all.shape[1]
    assert max(POOL_WINDOWS) // 2 <= POOL_HALO and tm % POOL_HALO == 0 and seq_len % tm == 0
    hb = tm // POOL_HALO
    last_hblk = h.shape[0] // POOL_HALO - 1
    mod3 = mod.reshape(MOD_ROWS, 1, -1)
    main_spec = pl.BlockSpec((tm, d), lambda i: (i + row_off, 0))
    return pl.pallas_call(
        functools.partial(_pool_kernel, tm=tm, seq_len=seq_len, windows=POOL_WINDOWS),
        grid=(n_rows // tm,),
        in_specs=[pl.BlockSpec((POOL_HALO, d), lambda i: (jnp.maximum((i + row_off) * hb - 1, 0), 0)),
                  main_spec,
                  pl.BlockSpec((POOL_HALO, d), lambda i: (jnp.minimum((i + row_off + 1) * hb, last_hblk), 0)),
                  pl.BlockSpec(w_grp.shape, lambda i: (0, 0, 0)),
                  pl.BlockSpec((1, d), lambda i: (0, 0)),
                  main_spec,
                  pl.BlockSpec((None, 1, d), lambda i: (rows.mod_row(i + row_off), 0, chunk))],
        out_specs=main_spec,
        out_shape=jax.ShapeDtypeStruct(x_all.shape, F32),
        scratch_shapes=[pltpu.VMEM((tm + 2 * POOL_HALO, d), F32), pltpu.VMEM(w_grp.shape, BF16)],
        input_output_aliases={5: 0},
        compiler_params=_cparams("arbitrary"),
        name="pool_mixer",
    )(h, h, h, w_grp, scale.reshape(1, d), x_all, mod3)


DFT_N2 = 128


def _cis(num, den):
    ang = (num % den).astype(F32) * (2.0 * math.pi / den)
    return jnp.cos(ang), jnp.sin(ang)


def _iota2(n_rows, n_cols):
    return (lax.broadcasted_iota(jnp.int32, (n_rows, n_cols), 0), lax.broadcasted_iota(jnp.int32, (n_rows, n_cols), 1))


def _slab_pitch(rows):
    return rows + 8


def _slab_scratch(g, rows, width):
    return pltpu.VMEM((width // LANES, g * _slab_pitch(rows), LANES), F32)


def _slab_store(scr, s, val, rows, first_chunk=0):
    p = _slab_pitch(rows)
    for l in range(val.shape[1] // LANES):
        scr[first_chunk + l, s * p:s * p + rows, :] = val[:, l * LANES:(l + 1) * LANES]


def _slab_load(scr, s, rows):
    p = _slab_pitch(rows)
    return jnp.concatenate([scr[l, s * p:s * p + rows, :] for l in range(scr.shape[0])], axis=1)


def _rows_gather(scr, q, g, rows):
    p = _slab_pitch(rows)
    return jnp.concatenate([scr[l, pl.ds(q, g, stride=p), :] for l in range(scr.shape[0])], axis=1)


def _rows_scatter(scr, q, val, rows):
    p = _slab_pitch(rows)
    for l in range(scr.shape[0]):
        scr[l, pl.ds(q, val.shape[0], stride=p), :] = val[:, l * LANES:(l + 1) * LANES]


def _left_mm_kernel(w_ref, x_ref, *rest, epilogue):
    x = x_ref[...]
    x = x.reshape(-1, x.shape[-1])
    acc = jnp.dot(w_ref[...], x.astype(BF16), preferred_element_type=F32)
    epilogue(acc, rest[:-1], rest[-1])


def _scaled_store(scale):
    def epilogue(acc, extras, o_ref):
        o_ref[...] = (acc * scale).reshape(o_ref.shape).astype(o_ref.dtype)
    return epilogue


def _left_mm(w, x, *, grid, w_spec, x_spec, out_shape, out_spec, extras=(), epilogue=_scaled_store(1.0),
             aliases=None, name="left_mm"):
    return pl.pallas_call(
        functools.partial(_left_mm_kernel, epilogue=epilogue),
        grid=grid,
        in_specs=[w_spec, x_spec] + [s for _, s in extras],
        out_specs=out_spec,
        out_shape=out_shape,
        input_output_aliases=aliases or {},
        compiler_params=_cparams(*(("arbitrary",) * len(grid))),
        name=name,
    )(w, x, *[a for a, _ in extras])


def _fnet_channel_kernel(x_ref, w_ref, o_ref):
    cg = x_ref.shape[1]
    r = jnp.dot(x_ref[...], w_ref[...], preferred_element_type=F32)
    o_ref[0] = r[:, :cg].astype(o_ref.dtype)
    o_ref[1] = r[:, cg:].astype(o_ref.dtype)


def _fnet_channel(h, rows, row_off, n_seq, seq_len):
    tm = rows.tile
    d = h.shape[1]
    cg = d // FNET_GROUPS
    ci, ki = _iota2(cg, cg)
    cc, sc = _cis(ci * ki, cg)
    wc = jnp.concatenate([cc, -sc], axis=1).astype(BF16)
    st = seq_len // tm
    return pl.pallas_call(
        _fnet_channel_kernel,
        grid=(n_seq * st, FNET_GROUPS),
        in_specs=[pl.BlockSpec((tm, cg), lambda i, g: (i + row_off, g)),
                  pl.BlockSpec((cg, 2 * cg), lambda i, g: (0, 0))],
        out_specs=pl.BlockSpec((None, 2, tm, cg), lambda i, g: (i // st, 0, i % st, g)),
        out_shape=jax.ShapeDtypeStruct((n_seq, 2, seq_len, d), BF16),
        compiler_params=_cparams("arbitrary", "arbitrary"),
        name="fnet_channel",
    )(h, wc)


def _fnet_positions_dense(z, tc):
    n_seq, _, seq_len, d = z.shape
    scale = 1.0 / math.sqrt(seq_len * (d // FNET_GROUPS))
    ki, ci2 = _iota2(seq_len, 2 * seq_len)
    gc, gs = _cis(ki * (ci2 % seq_len), seq_len)
    g2 = jnp.where(ci2 < seq_len, gc, gs).astype(BF16)
    out = _left_mm(g2, z,
                   grid=(n_seq, d // tc),
                   w_spec=pl.BlockSpec((seq_len, 2 * seq_len), lambda s, j: (0, 0)),
                   x_spec=pl.BlockSpec((None, 2, seq_len, tc), lambda s, j: (s, 0, 0, j)),
                   out_shape=jax.ShapeDtypeStruct((n_seq, seq_len, d), BF16),
                   out_spec=pl.BlockSpec((None, seq_len, tc), lambda s, j: (s, 0, j)),
                   epilogue=_scaled_store(scale), name="fnet_dense")
    return out.reshape(n_seq * seq_len, d)


FNET_ROWS = 2048
FNET_K1_GROUP = 16


def _fnet_channel_split_kernel(x_ref, w_ref, zr_ref, zi_ref, rg_ref):
    cg = x_ref.shape[1]
    g = x_ref.shape[0] // DFT_N2
    r = jnp.dot(x_ref[...], w_ref[...], preferred_element_type=F32)
    for plane, z_ref in enumerate((zr_ref, zi_ref)):
        for s in range(g):
            _slab_store(rg_ref, s, r[s * DFT_N2:(s + 1) * DFT_N2, plane * cg:(plane + 1) * cg], DFT_N2)

        def body(n2, c):
            z_ref[n2] = _rows_gather(rg_ref, n2, g, DFT_N2).astype(z_ref.dtype)
            return c
        lax.fori_loop(0, DFT_N2, body, 0)


def _fnet_channel_split(h, n_seq, seq_len):
    tm = FNET_ROWS
    d = h.shape[1]
    cg = d // FNET_GROUPS
    ci, ki = _iota2(cg, cg)
    cc, sc = _cis(ci * ki, cg)
    wc = jnp.concatenate([cc, -sc], axis=1).astype(BF16)
    st = seq_len // tm
    g = tm // DFT_N2
    assert seq_len % tm == 0 and g % 16 == 0
    plane = jax.ShapeDtypeStruct((n_seq, DFT_N2, seq_len // DFT_N2, d), BF16)
    out_spec = pl.BlockSpec((None, DFT_N2, g, cg), lambda i, c: (i // st, 0, i % st, c))
    return pl.pallas_call(
        _fnet_channel_split_kernel,
        grid=(n_seq * st, FNET_GROUPS),
        in_specs=[pl.BlockSpec((tm, cg), lambda i, c: (i, c)),
                  pl.BlockSpec((cg, 2 * cg), lambda i, c: (0, 0))],
        out_specs=[out_spec, out_spec],
        out_shape=[plane, plane],
        scratch_shapes=[_slab_scratch(g, DFT_N2, cg)],
        compiler_params=_cparams("arbitrary", "arbitrary"),
        name="fnet_channel",
    )(h, wc)


def _fnet_stage1_kernel(w_ref, zr_ref, zi_ref, o_ref):
    w = w_ref[...]
    for s in range(zr_ref.shape[0]):
        z = jnp.concatenate([zr_ref[s], zi_ref[s]], axis=0)
        o_ref[s] = jnp.dot(w, z, preferred_element_type=F32).astype(o_ref.dtype)


def _fnet_stage2_kernel(g_ref, a_ref, o_ref, rin, rout, *, scale):
    kg = a_ref.shape[1] // 2

    def body_in(q, c):
        _rows_scatter(rin, q, a_ref[q].astype(F32), DFT_N2)
        return c
    lax.fori_loop(0, a_ref.shape[0], body_in, 0)
    for j in range(kg):
        r = jnp.dot(g_ref[j], _stage2_operand(rin, j), preferred_element_type=F32)
        _slab_store(rout, j, r * scale, DFT_N2)

    def body_out(q, c):
        o_ref[q] = _rows_gather(rout, q, kg, DFT_N2).astype(o_ref.dtype)
        return c
    lax.fori_loop(0, o_ref.shape[0], body_out, 0)


def _fnet_positions_split(zr, zi):
    n_seq, n2, n1, d = zr.shape
    seq_len = n1 * n2
    scale = 1.0 / math.sqrt(seq_len * (d // FNET_GROUPS))
    r, cidx = _iota2(2 * n1, 2 * n1)
    k1, ro, ri, nn = r // 2, r % 2, cidx // n1, cidx % n1
    fr, fs = _cis(k1 * nn, n1)
    w1 = jnp.where(ro == ri, fr, jnp.where(ro == 0, fs, -fs)).astype(BF16)
    tc = min(SPLIT_TC, d)
    zblk = pl.BlockSpec((None, N2_GROUP, n1, tc), lambda s, g, j: (s, g, 0, j))
    a = pl.pallas_call(
        _fnet_stage1_kernel,
        grid=(n_seq, n2 // N2_GROUP, d // tc),
        in_specs=[pl.BlockSpec((2 * n1, 2 * n1), lambda s, g, j: (0, 0)), zblk, zblk],
        out_specs=pl.BlockSpec((None, N2_GROUP, 2 * n1, tc), lambda s, g, j: (s, g, 0, j)),
        out_shape=jax.ShapeDtypeStruct((n_seq, n2, 2 * n1, d), BF16),
        compiler_params=_cparams("arbitrary", "arbitrary", "arbitrary"),
        name="fnet_stage1",
    )(w1, zr, zi)
    g2 = _stage2_tables(seq_len, n1)[0][:, :n2, :]
    kg = FNET_K1_GROUP
    assert n1 % kg == 0
    out = pl.pallas_call(
        functools.partial(_fnet_stage2_kernel, scale=scale),
        grid=(n_seq, n1 // kg, d // tc),
        in_specs=[pl.BlockSpec((kg, n2, 2 * n2), lambda s, k, j: (k, 0, 0)),
                  pl.BlockSpec((None, n2, 2 * kg, tc), lambda s, k, j: (s, 0, k, j))],
        out_specs=pl.BlockSpec((None, n2, kg, tc), lambda s, k, j: (s, 0, k, j)),
        out_shape=jax.ShapeDtypeStruct((n_seq, n2, n1, d), BF16),
        scratch_shapes=[_slab_scratch(2 * kg, n2, tc), _slab_scratch(kg, n2, tc)],
        compiler_params=_cparams("arbitrary", "arbitrary", "arbitrary"),
        name="fnet_stage2",
    )(g2, a)
    return out.reshape(n_seq * seq_len, d)


def _layer_fnet(x_all, h, mod, rows, w_out, b_out):
    d = x_all.shape[1]
    zc = _fnet_channel(h, rows, rows.ctx_off, rows.batch, rows.ctx_len)
    fc = _fnet_positions_dense(zc, min(d, 512))
    x_all = _proj_residual(fc, w_out, b_out, x_all, mod, 2, rows, rows.ctx_off, name="fnet_out_ctx")
    zr, zi = _fnet_channel_split(h, rows.batch, rows.seq)
    fl = _fnet_positions_split(zr, zi)
    big = _Rows(rows.batch, rows.ctx_len, rows.seq, PROJ_ROWS)
    return _proj_residual(fl, w_out, b_out, x_all, mod, 2, big, big.lat_off, tn=1024, name="fnet_out_lat")


HEAD_SLAB = 2 * LANES
MLA_SCALE = (QK_NOPE + QK_ROPE) ** -0.5
Q_SCALE = MLA_SCALE * math.log2(math.e)
V_SLAB = V_DIM + 16
ATTN_TQ = 2048
ATTN_TK = 1024


def _rope_tables(seq_len, lead_identity_rows):
    n_rows = seq_len // GRID_W
    row = jnp.repeat(jnp.arange(n_rows, dtype=F32), GRID_W)
    col = jnp.tile(jnp.arange(GRID_W, dtype=F32), n_rows)
    half = QK_ROPE // 2
    inv = ROPE_THETA ** (-jnp.arange(0, half, 2, dtype=F32) / half)
    ang_r = row[:, None] * inv
    ang_c = col[:, None] * inv
    ang = jnp.concatenate([ang_r, ang_r, ang_c, ang_c], axis=-1)
    pad = jnp.zeros((seq_len, LANES - QK_ROPE), F32)
    cos = jnp.concatenate([jnp.cos(ang), pad + 1.0], axis=-1)
    sin = jnp.concatenate([jnp.sin(ang), pad], axis=-1)
    if lead_identity_rows:
        cos = jnp.concatenate([jnp.ones((lead_identity_rows, LANES), F32), cos], axis=0)
        sin = jnp.concatenate([jnp.zeros((lead_identity_rows, LANES), F32), sin], axis=0)
    return cos, sin


def _rope(x, cos, sin):
    q = QK_ROPE // 4
    lane = lax.broadcasted_iota(jnp.int32, x.shape, 1)
    even = (lane // q) % 2 == 0
    rot = jnp.where(even, -pltpu.roll(x, LANES - q, 1), pltpu.roll(x, q, 1))
    return x * cos + rot * sin


def _rmsnorm_epilogue(acc, extras, outs):
    g_ref, = extras
    y = acc * lax.rsqrt(jnp.mean(acc * acc, axis=-1, keepdims=True) + EPS) * g_ref[...]
    outs[0][...] = y.astype(outs[0].dtype)


def _q_up_epilogue(acc, extras, outs):
    cos_ref, sin_ref = extras
    o_ref, = outs
    for hh in range(acc.shape[1] // HEAD_SLAB):
        c0 = hh * HEAD_SLAB
        o_ref[:, c0:c0 + LANES] = (acc[:, c0:c0 + LANES] * Q_SCALE).astype(o_ref.dtype)
        pe = _rope(acc[:, c0 + LANES:c0 + HEAD_SLAB], cos_ref[...], sin_ref[...])
        o_ref[:, c0 + LANES:c0 + HEAD_SLAB] = (pe * Q_SCALE).astype(o_ref.dtype)


def _kv_down_epilogue(acc, extras, outs, *, rank):
    g_ref, cos_ref, sin_ref = extras
    kvn_ref, kpe_ref = outs
    lat = acc[:, :rank]
    y = lat * lax.rsqrt(jnp.mean(lat * lat, axis=-1, keepdims=True) + EPS) * g_ref[...]
    kvn_ref[...] = y.astype(kvn_ref.dtype)
    kpe_ref[...] = _rope(acc[:, rank:rank + LANES], cos_ref[...], sin_ref[...]).astype(kpe_ref.dtype)


def _k_up_epilogue(acc, extras, outs):
    kpe_ref, = extras
    o_ref, = outs
    for hh in range(acc.shape[1] // LANES):
        o_ref[:, hh * HEAD_SLAB:hh * HEAD_SLAB + LANES] = acc[:, hh * LANES:(hh + 1) * LANES].astype(o_ref.dtype)
        o_ref[:, hh * HEAD_SLAB + LANES:(hh + 1) * HEAD_SLAB] = kpe_ref[...]


def _attn_kernel(q_ref, kc_ref, kl_ref, vc_ref, vl_ref, o_ref, *, tk):
    q = q_ref[...]

    def scores(k_tile):
        return lax.dot_general(k_tile, q, (((1,), (1,)), ((), ())), preferred_element_type=F32)

    def probs(s, m):
        return jnp.exp2((s - m).astype(BF16))

    s = scores(kc_ref[...])
    m = jnp.max(s, axis=0, keepdims=True)
    acc = jnp.dot(vc_ref[...], probs(s, m), preferred_element_type=F32)
    for j in range(kl_ref.shape[0] // tk):
        s = scores(kl_ref[j * tk:(j + 1) * tk, :])
        m_new = jnp.maximum(m, jnp.max(s, axis=0, keepdims=True))
        acc = jnp.exp2(m - m_new) * acc + jnp.dot(vl_ref[:, j * tk:(j + 1) * tk], probs(s, m_new),
                                                  preferred_element_type=F32)
        m = m_new
    o_ref[...] = (acc[:V_DIM] / acc[V_DIM:V_DIM + 1]).T.astype(o_ref.dtype)


def _value_slabs(v, batch, keys):
    v4 = v.reshape(batch, keys, MLA_HEADS, V_DIM)
    ones = jnp.ones((batch, keys, MLA_HEADS, 1), v.dtype)
    pad = jnp.zeros((batch, keys, MLA_HEADS, V_SLAB - V_DIM - 1), v.dtype)
    return jnp.concatenate([v4, ones, pad], axis=-1).transpose(0, 2, 3, 1).reshape(batch, MLA_HEADS * V_SLAB, keys)


def _attention(q, k_ctx, k_lat, vt_ctx, vt_lat):
    batch, seq, _ = q.shape
    ctx_len = k_ctx.shape[1]
    tq = min(ATTN_TQ, seq)
    tk = min(ATTN_TK, seq)
    assert seq % tq == 0 and seq % tk == 0
    return pl.pallas_call(
        functools.partial(_attn_kernel, tk=tk),
        grid=(batch, MLA_HEADS, seq // tq),
        in_specs=[pl.BlockSpec((None, tq, HEAD_SLAB), lambda b, h, i: (b, i, h)),
                  pl.BlockSpec((None, ctx_len, HEAD_SLAB), lambda b, h, i: (b, 0, h)),
                  pl.BlockSpec((None, seq, HEAD_SLAB), lambda b, h, i: (b, 0, h)),
                  pl.BlockSpec((None, V_SLAB, ctx_len), lambda b, h, i: (b, h, 0)),
                  pl.BlockSpec((None, V_SLAB, seq), lambda b, h, i: (b, h, 0))],
        out_specs=pl.BlockSpec((None, tq, V_DIM), lambda b, h, i: (b, i, h)),
        out_shape=jax.ShapeDtypeStruct((batch, seq, MLA_HEADS * V_DIM), BF16),
        compiler_params=_cparams("arbitrary", "arbitrary", "arbitrary"),
        name="mla_attention",
    )(q, k_ctx, k_lat, vt_ctx, vt_lat)


def _layer_mla(x_all, h, mod, rows, w_dq, g_q, w_uq, w_dkv, g_kv, w_ukv, w_o, update_ctx=False):
    assert not update_ctx, "attention is the last mixer of the stack: context queries are never needed"
    rows = _Rows(rows.batch, rows.ctx_len, rows.seq, PROJ_ROWS)
    tm = rows.tile
    d = x_all.shape[1]
    batch, seq, ctx_len = rows.batch, rows.seq, rows.ctx_len
    q_rank, kv_rank = w_dq.shape[1], g_kv.shape[0]
    assert V_DIM == LANES and QK_NOPE == LANES and QK_ROPE <= LANES

    w_uq_s = jnp.pad(w_uq.reshape(q_rank, MLA_HEADS, QK_NOPE + QK_ROPE),
                     ((0, 0), (0, 0), (0, HEAD_SLAB - QK_NOPE - QK_ROPE))).reshape(q_rank, MLA_HEADS * HEAD_SLAB)
    w_dkv_s = jnp.pad(w_dkv, ((0, 0), (0, kv_rank + LANES - w_dkv.shape[1])))
    w_ukv_s = w_ukv.reshape(kv_rank, MLA_HEADS, QK_NOPE + V_DIM)
    w_uk = w_ukv_s[:, :, :QK_NOPE].reshape(kv_rank, MLA_HEADS * QK_NOPE)
    w_uv = w_ukv_s[:, :, QK_NOPE:].reshape(kv_rank, MLA_HEADS * V_DIM)
    cos, sin = _rope_tables(seq, tm)
    lat_blocks = rows.lat_blocks

    cqn, = _matmul(h, w_dq, tm=tm, tn=q_rank, m_rows=rows.n_lat, x_row_off=rows.lat_off,
                   extras=[(g_q.reshape(1, q_rank), pl.BlockSpec((1, q_rank), lambda j, i: (0, 0)))],
                   outs=[(jax.ShapeDtypeStruct((rows.n_lat, q_rank), BF16), pl.BlockSpec((tm, q_rank), lambda j, i: (i, 0)))],
                   epilogue=_rmsnorm_epilogue, name="mla_q_down")
    tnq = 4 * HEAD_SLAB
    rope_lat = pl.BlockSpec((tm, LANES), lambda j, i: (1 + i % lat_blocks, 0))
    q, = _matmul(cqn, w_uq_s, tm=tm, tn=tnq, extras=[(cos, rope_lat), (sin, rope_lat)],
                 outs=[(jax.ShapeDtypeStruct((rows.n_lat, MLA_HEADS * HEAD_SLAB), BF16),
                        pl.BlockSpec((tm, tnq), lambda j, i: (i, j)))],
                 epilogue=_q_up_epilogue, name="mla_q_up")

    def keys_values(row_off, n_rows, rope_spec):
        kvn, kpe = _matmul(h, w_dkv_s, tm=tm, tn=kv_rank + LANES, m_rows=n_rows, x_row_off=row_off,
                           extras=[(g_kv.reshape(1, kv_rank), pl.BlockSpec((1, kv_rank), lambda j, i: (0, 0))),
                                   (cos, rope_spec), (sin, rope_spec)],
                           outs=[(jax.ShapeDtypeStruct((n_rows, kv_rank), BF16), pl.BlockSpec((tm, kv_rank), lambda j, i: (i, 0))),
                                 (jax.ShapeDtypeStruct((n_rows, LANES), BF16), pl.BlockSpec((tm, LANES), lambda j, i: (i, 0)))],
                           epilogue=functools.partial(_kv_down_epilogue, rank=kv_rank), name="mla_kv_down")
        tnk = 4 * LANES
        k, = _matmul(kvn, w_uk, tm=tm, tn=tnk,
                     extras=[(kpe, pl.BlockSpec((tm, LANES), lambda j, i: (i, 0)))],
                     outs=[(jax.ShapeDtypeStruct((n_rows, MLA_HEADS * HEAD_SLAB), BF16),
                            pl.BlockSpec((tm, 2 * tnk), lambda j, i: (i, j)))],
                     epilogue=_k_up_epilogue, name="mla_k_up")
        v, = _matmul(kvn, w_uv, tm=tm, tn=tnk,
                     outs=[(jax.ShapeDtypeStruct((n_rows, MLA_HEADS * V_DIM), BF16), pl.BlockSpec((tm, tnk), lambda j, i: (i, j)))],
                     name="mla_v_up")
        return k, v

    k_c, v_c = keys_values(rows.ctx_off, rows.n_ctx,pl.BlockSpec((tm, LANES), lambda j, i: (0, 0)))
    k_l, v_l = keys_values(rows.lat_off, rows.n_lat, rope_lat)
    vt_c = _value_slabs(v_c, batch, ctx_len)
    vt_l = _value_slabs(v_l, batch, seq)
    attn = _attention(q.reshape(batch, seq, -1), k_c.reshape(batch, ctx_len, -1), k_l.reshape(batch, seq, -1), vt_c, vt_l)
    attn = attn.reshape(rows.n_lat, MLA_HEADS * V_DIM)
    return _proj_residual(attn, w_o, None, x_all, mod, 2, rows, rows.lat_off, tn=1024, name="mla_out")


CONV_HALO = 16
MAX_DECAY = math.log(DECAY_TARGET) / FAST_DECAY_PCT
MIN_DECAY = math.log(DECAY_TARGET) / SLOW_DECAY_PCT


def _hyena_in_kernel(prev_ref, main_ref, next_ref, w_ref, b_ref, cw_ref, cb_ref, o_ref, xext_ref, zext_ref, wbf_ref,
                     *rg, tm, seq_len):
    i = pl.program_id(1)
    seq_tiles = seq_len // tm
    t_in_seq = i % seq_tiles

    @pl.when(i == 0)
    def _():
        wbf_ref[...] = w_ref[...].astype(BF16)

    xext_ref[0:CONV_HALO, :] = prev_ref[...]
    xext_ref[CONV_HALO:CONV_HALO + tm, :] = main_ref[...]
    xext_ref[CONV_HALO + tm:2 * CONV_HALO + tm, :] = next_ref[...]
    row = lax.broadcasted_iota(jnp.int32, (tm, 1), 0)
    first = (row == 0) & (t_in_seq == 0)
    last = (row == tm - 1) & (t_in_seq == seq_tiles - 1)
    tn = zext_ref.shape[1]
    cw = min(tn, 2 * LANES)
    for c0 in range(0, tn, cw):
        cols = slice(c0, c0 + cw)
        zext_ref[:, cols] = jnp.dot(xext_ref[...], wbf_ref[:, cols], preferred_element_type=F32) + b_ref[:, cols]
        prev = jnp.where(first, 0.0, zext_ref[pl.ds(CONV_HALO - 1, tm), cols])
        nxt = jnp.where(last, 0.0, zext_ref[pl.ds(CONV_HALO + 1, tm), cols])
        out = (prev * cw_ref[0:1, cols] + zext_ref[pl.ds(CONV_HALO, tm), cols] * cw_ref[1:2, cols]
               + nxt * cw_ref[2:3, cols] + cb_ref[:, cols])
        if rg:
            for s in range(tm // DFT_N2):
                _slab_store(rg[0], s, out[s * DFT_N2:(s + 1) * DFT_N2, :], DFT_N2, first_chunk=c0 // LANES)
        else:
            o_ref[:, cols] = out.astype(o_ref.dtype)
    if not rg:
        return
    rg_ref, = rg

    def body(n2, c):
        o_ref[n2] = _rows_gather(rg_ref, n2, tm // DFT_N2, DFT_N2).astype(o_ref.dtype)
        return c
    lax.fori_loop(0, DFT_N2, body, 0)


def _hyena_in(h, w_in, b_in, conv_w, conv_b, rows, row_off, n_rows, seq_len, tn=512, time_split=False):
    tm = rows.tile
    k, n3 = w_in.shape
    d = n3 // 3
    tn = min(tn, d)
    nd = d // tn
    hb = tm // CONV_HALO
    last_hblk = h.shape[0] // CONV_HALO - 1
    if time_split:
        g = tm // DFT_N2
        assert tm % DFT_N2 == 0 and g % 8 == 0 and seq_len % tm == 0
        out_spec = pl.BlockSpec((None, DFT_N2, g, tn), lambda j, i: (j // nd, 0, i, j % nd))
        out_shape = jax.ShapeDtypeStruct((3, DFT_N2, n_rows // DFT_N2, d), F32)
        extra_scratch = [_slab_scratch(g, DFT_N2, tn)]
    else:
        out_spec = pl.BlockSpec((None, tm, tn), lambda j, i: (j // nd, i, j % nd))
        out_shape = jax.ShapeDtypeStruct((3, n_rows, d), BF16)
        extra_scratch = []
    return pl.pallas_call(
        functools.partial(_hyena_in_kernel, tm=tm, seq_len=seq_len),
        grid=(n3 // tn, n_rows // tm),
        in_specs=[pl.BlockSpec((CONV_HALO, k), lambda j, i: (jnp.maximum((i + row_off) * hb - 1, 0), 0)),
                  pl.BlockSpec((tm, k), lambda j, i: (i + row_off, 0)),
                  pl.BlockSpec((CONV_HALO, k), lambda j, i: (jnp.minimum((i + row_off + 1) * hb, last_hblk), 0)),
                  pl.BlockSpec((k, tn), lambda j, i: (0, j)),
                  pl.BlockSpec((1, tn), lambda j, i: (0, j)),
                  pl.BlockSpec((3, tn), lambda j, i: (0, j)),
                  pl.BlockSpec((1, tn), lambda j, i: (0, j))],
        out_specs=out_spec,
        out_shape=out_shape,
        scratch_shapes=[pltpu.VMEM((tm + 2 * CONV_HALO, k), BF16), pltpu.VMEM((tm + 2 * CONV_HALO, tn), F32),
                        pltpu.VMEM((k, tn), BF16)] + extra_scratch,
        compiler_params=_cparams("arbitrary", "arbitrary"),
        name="hyena_in",
    )(h, h, h, w_in, b_in.reshape(1, n3), conv_w, conv_b.reshape(1, n3))


def _hyena_filter_kernel(z_ref, w1_ref, b1_ref, w2_ref, b2_ref, f0_ref, f1_ref, w3_ref, dl_ref, k_ref, ss_ref,
                         h2_ref, *rg, tp, seq_len):
    p = pl.program_id(0)
    z = z_ref[...]

    @pl.when(pl.program_id(1) == 0)
    def _():
        h1 = jnp.sin(f0_ref[...] * (jnp.dot(z, w1_ref[...], precision=HIGHEST, preferred_element_type=F32) + b1_ref[...]))
        h2 = jnp.sin(f1_ref[...] * (jnp.dot(h1, w2_ref[...], precision=HIGHEST, preferred_element_type=F32) + b2_ref[...]))
        h2_ref[...] = h2.astype(BF16)

    filt = jnp.dot(h2_ref[...], w3_ref[...].astype(BF16), preferred_element_type=F32)
    t = z[:, 0:1]
    kk = filt * (jnp.exp(-t * dl_ref[...]) + MOD_SHIFT)
    circ = p * tp + lax.broadcasted_iota(jnp.int32, (tp, 1), 0)
    kk = jnp.where(circ == seq_len, 0.0, kk)
    if rg:
        rg_ref, = rg
        for s in range(tp // DFT_N2):
            _slab_store(rg_ref, s, kk[s * DFT_N2:(s + 1) * DFT_N2, :], DFT_N2)

        def body(n2, c):
            k_ref[n2] = _rows_gather(rg_ref, n2, tp // DFT_N2, DFT_N2).astype(k_ref.dtype)
            return c
        lax.fori_loop(0, DFT_N2, body, 0)
    else:
        k_ref[...] = kk.astype(k_ref.dtype)
    ss_ref[...] = jnp.sum(kk * kk, axis=0, keepdims=True)


def _hyena_filters(seq_len, d, f_w1, f_b1, f_w2, f_b2, f_w3, f_freq, tp=256, tc=1024, time_split=False):
    f32 = F32
    hid = f_w1.shape[1]
    od = HYENA_ORDER * d
    t = jnp.linspace(0.0, 1.0, seq_len, dtype=f32)[:, None]
    bands = (FILTER_EMB - 1) // 2
    w = 2.0 * math.pi * jnp.arange(seq_len, dtype=f32)[:, None] / seq_len
    f = jnp.linspace(1e-4, bands - 1, bands, dtype=f32)[None, :]
    z = jnp.concatenate([t, jnp.cos(f * w), -jnp.sin(f * w)], axis=-1)
    circ = jnp.arange(2 * seq_len)
    offs = jnp.where(circ < seq_len, circ, jnp.minimum(2 * seq_len - circ, seq_len - 1))
    z2 = jnp.pad(z[offs], ((0, 0), (0, LANES - FILTER_EMB)))
    w1p = jnp.pad(f_w1, ((0, LANES - FILTER_EMB), (0, 0)))
    w3s = f_w3.reshape(hid, HYENA_ORDER, 2, d).transpose(2, 0, 1, 3).reshape(2, hid, od)
    deltas = jnp.tile(jnp.abs(jnp.linspace(MIN_DECAY, MAX_DECAY, d, dtype=f32)), HYENA_ORDER)[None, :]
    tp = min(tp, seq_len)
    tc = min(tc, od)
    side_tiles = seq_len // tp
    small = lambda shape: pl.BlockSpec(shape, lambda p, j: (0,) * len(shape))
    n_ptiles = 2 * side_tiles
    if time_split:
        g = tp // DFT_N2
        assert tp % DFT_N2 == 0 and g % 16 == 0
        k_spec = pl.BlockSpec((DFT_N2, g, tc), lambda p, j: (0, p, j))
        k_shape = jax.ShapeDtypeStruct((DFT_N2, 2 * seq_len // DFT_N2, od), BF16)
        extra_scratch = [_slab_scratch(g, DFT_N2, tc)]
    else:
        k_spec = pl.BlockSpec((tp, tc), lambda p, j: (p, j))
        k_shape = jax.ShapeDtypeStruct((2 * seq_len, od), BF16)
        extra_scratch = []
    k, ss_parts = pl.pallas_call(
        functools.partial(_hyena_filter_kernel, tp=tp, seq_len=seq_len),
        grid=(n_ptiles, od // tc),
        in_specs=[pl.BlockSpec((tp, LANES), lambda p, j: (p, 0)),
                  small((LANES, hid)), small((1, hid)), small((hid, hid)), small((1, hid)), small((1, hid)), small((1, hid)),
                  pl.BlockSpec((None, hid, tc), lambda p, j: (p // side_tiles, 0, j)),
                  pl.BlockSpec((1, tc), lambda p, j: (0, j))],
        out_specs=[k_spec, pl.BlockSpec((None, 1, tc), lambda p, j: (p, 0, j))],
        out_shape=[k_shape, jax.ShapeDtypeStruct((n_ptiles, 1, od), F32)],
        scratch_shapes=[pltpu.VMEM((tp, hid), BF16)] + extra_scratch,
        compiler_params=_cparams("arbitrary", "arbitrary"),
        name="hyena_filters",
    )(z2, w1p, f_b1.reshape(1, hid), f_w2, f_b2.reshape(1, hid), f_freq[0:1], f_freq[1:2], w3s, deltas)
    return k, jnp.sum(ss_parts, axis=0)


def _stage2_tables(n, n1):
    n2 = n // n1
    ac, as_ = _cis(jnp.arange(n1, dtype=jnp.int32)[:, None] * jnp.arange(n2, dtype=jnp.int32)[None, :], n)
    r, c = _iota2(n2, n2)
    bc, bs = _cis(r * c, n2)

    def blocks(ac3, as3):
        gc = ac3 * bc[None] - as3 * bs[None]
        gs = as3 * bc[None] + ac3 * bs[None]
        return gc, gs

    gc, gs = blocks(ac[:, None, :], as_[:, None, :])
    gct, gst = blocks(ac[:, :, None], as_[:, :, None])
    fwd = jnp.concatenate([jnp.concatenate([gc, gs], axis=2), jnp.concatenate([-gs, gc], axis=2)], axis=1)
    inv = jnp.concatenate([jnp.concatenate([gct, -gst], axis=2), jnp.concatenate([gst, gct], axis=2)], axis=1)
    return fwd.astype(BF16), inv.astype(BF16)


def _complex_mul(x, kf, n2):
    xr, xi = x[:n2], x[n2:]
    kr, ki = kf[:n2].astype(F32), kf[n2:].astype(F32)
    return jnp.concatenate([xr * kr - xi * ki, xr * ki + xi * kr], axis=0)


def _conv_dense_kernel(fk_ref, fz_ref, fzt_ref, k_ref, ss_ref, u_ref, xg_ref, skip_ref, o_ref):
    n = k_ref.shape[0]
    kf = jnp.dot(fk_ref[...], k_ref[...], preferred_element_type=F32) * ss_ref[...]
    u = u_ref[...]
    z = jnp.dot(fz_ref[...], u, preferred_element_type=F32)
    y = _complex_mul(z, kf, n)
    conv = jnp.dot(fzt_ref[...], y.astype(BF16), preferred_element_type=F32)
    o_ref[...] = (xg_ref[...].astype(F32) * (conv + skip_ref[...] * u.astype(F32))).astype(o_ref.dtype)


def _hyena_long_conv_dense(u, xg, k2u, sumsq, order, skip, seq_len):
    n_rows, d = u.shape
    assert n_rows == 2 * seq_len
    n = 2 * seq_len
    colscale = lax.rsqrt(sumsq + EPS) / n
    r, c = _iota2(2 * n, n)
    kc, ks = _cis((r % n) * c, n)
    fk = jnp.where(r < n, kc, -ks).astype(BF16)
    r, c = _iota2(2 * n, 2 * seq_len)
    zc, zs = _cis((r % n) * (c % seq_len), n)
    fz = jnp.where((r < n) == (c < seq_len), zc, jnp.where(r < n, zs, -zs))
    tc = min(d, 512)
    nd = d // tc
    full = lambda a: pl.BlockSpec(a.shape, lambda j: (0, 0))
    fz_b, fzt_b = fz.astype(BF16), fz.T.astype(BF16)
    blk = pl.BlockSpec((n_rows, tc), lambda j: (0, j))
    return pl.pallas_call(
        _conv_dense_kernel,
        grid=(nd,),
        in_specs=[full(fk), full(fz_b), full(fzt_b),
                  pl.BlockSpec((n, tc), lambda j: (0, order * nd + j)),
                  pl.BlockSpec((1, tc), lambda j: (0, order * nd + j)),
                  blk, blk, pl.BlockSpec((1, tc), lambda j: (0, j))],
        out_specs=blk,
        out_shape=jax.ShapeDtypeStruct((n_rows, d), BF16),
        compiler_params=_cparams("arbitrary"),
        name="hyena_conv_dense",
    )(fk, fz_b, fzt_b, k2u, colscale, u, xg, skip.reshape(1, d))


SPLIT_ROWS = 1024
N2_GROUP = 8
K1_GROUP = 8
SPLIT_TC = 512


def _slab_mm_kernel(w_ref, x_ref, *rest, epilogue):
    w = w_ref[...]
    for s in range(x_ref.shape[0]):
        acc = jnp.dot(w, x_ref[s].astype(BF16), preferred_element_type=F32)
        epilogue(acc, s, rest[:-1], rest[-1])


def _slab_plain(acc, s, extras, o_ref):
    o_ref[s] = acc.astype(o_ref.dtype)


def _slab_gate(acc, s, extras, o_ref):
    xg_ref, u_ref, skip_ref = extras
    o_ref[s] = (xg_ref[s] * (acc + skip_ref[...] * u_ref[s])).astype(o_ref.dtype)


def _slab_mm(w, x, out_dtype, slab_extras=(), row_extras=(), epilogue=_slab_plain, name="slab_mm"):
    x_arr, x_lead = x
    n2, k, c = x_arr.shape[len(x_lead):]
    tc = min(SPLIT_TC, c)

    def blk(r, lead=()):
        return pl.BlockSpec((None,) * len(lead) + (N2_GROUP, r, tc), lambda g, j: tuple(lead) + (g, 0, j))

    in_specs = [pl.BlockSpec(w.shape, lambda g, j: (0, 0)), blk(k, x_lead)]
    in_specs += [blk(a.shape[-2], lead) for a, lead in slab_extras]
    in_specs += [pl.BlockSpec((1, tc), lambda g, j: (0, j)) for _ in row_extras]
    return pl.pallas_call(
        functools.partial(_slab_mm_kernel, epilogue=epilogue),
        grid=(n2 // N2_GROUP, c // tc),
        in_specs=in_specs,
        out_specs=blk(w.shape[0]),
        out_shape=jax.ShapeDtypeStruct((n2, w.shape[0], c), out_dtype),
        compiler_params=_cparams("arbitrary", "arbitrary"),
        name=name,
    )(w, x_arr, *[a for a, _ in slab_extras], *row_extras)


def _stage2_regroup_in(a_ref, rin):
    def body(q, c):
        _rows_scatter(rin, q, a_ref[q].astype(F32), DFT_N2)
        return c
    lax.fori_loop(0, a_ref.shape[0], body, 0)


def _stage2_operand(rin, j):
    return jnp.concatenate([_slab_load(rin, 2 * j, DFT_N2), _slab_load(rin, 2 * j + 1, DFT_N2)], axis=0).astype(BF16)


def _kf_stage2_kernel(g_ref, a_ref, ss_ref, o_ref, rin):
    _stage2_regroup_in(a_ref, rin)
    for j in range(K1_GROUP):
        kf = jnp.dot(g_ref[j], _stage2_operand(rin, j), preferred_element_type=F32)
        o_ref[j] = (kf * ss_ref[...]).astype(o_ref.dtype)


def _conv_stage2_kernel(gf_ref, gi_ref, a_ref, kf_ref, o_ref, rin, rout):
    _stage2_regroup_in(a_ref, rin)
    for j in range(K1_GROUP):
        x = jnp.dot(gf_ref[j], _stage2_operand(rin, j), preferred_element_type=F32)
        y = _complex_mul(x, kf_ref[j], DFT_N2)
        b = jnp.dot(gi_ref[j], y.astype(BF16), preferred_element_type=F32)
        _slab_store(rout, 2 * j, b[:DFT_N2], DFT_N2)
        _slab_store(rout, 2 * j + 1, b[DFT_N2:], DFT_N2)

    def body(q, c):
        o_ref[q] = _rows_gather(rout, q, 2 * K1_GROUP, DFT_N2).astype(o_ref.dtype)
        return c
    lax.fori_loop(0, o_ref.shape[0], body, 0)


def _hyena_kf_split(k3, sumsq, tables):
    n2, n1, od = k3.shape
    n = n1 * n2
    colscale = lax.rsqrt(sumsq + EPS) / n
    r, c = _iota2(2 * n1, n1)
    fr, fs = _cis((r // 2) * c, n1)
    w1 = jnp.where(r % 2 == 0, fr, -fs).astype(BF16)
    a = _slab_mm(w1, (k3, ()), BF16, name="hyena_kf_stage1")
    tc = min(SPLIT_TC, od)
    return pl.pallas_call(
        _kf_stage2_kernel,
        grid=(n1 // K1_GROUP, od // tc),
        in_specs=[pl.BlockSpec((K1_GROUP, 2 * n2, 2 * n2), lambda k, j: (k, 0, 0)),
                  pl.BlockSpec((n2, 2 * K1_GROUP, tc), lambda k, j: (0, k, j)),
                  pl.BlockSpec((1, tc), lambda k, j: (0, j))],
        out_specs=pl.BlockSpec((K1_GROUP, 2 * n2, tc), lambda k, j: (k, 0, j)),
        out_shape=jax.ShapeDtypeStruct((n1, 2 * n2, od), BF16),
        scratch_shapes=[_slab_scratch(2 * K1_GROUP, n2, tc)],
        compiler_params=_cparams("arbitrary", "arbitrary"),
        name="hyena_kf_stage2",
    )(tables[0], a, colscale)


def _hyena_long_conv_split(v, xg, kf, kf_col_off, skip, tables):
    n2, m, d = v[0].shape[len(v[1]):]
    hn = m // 2
    n1 = 2 * hn
    g_fwd, g_inv = tables
    r, c = _iota2(2 * n1, 2 * hn)
    fr, fs = _cis((r // 2) * (c % hn), n1)
    ro, ri = r % 2, c // hn
    w1 = jnp.where(ro == ri, fr, jnp.where(ro == 0, fs, -fs)).astype(BF16)
    a = _slab_mm(w1, v, BF16, name="hyena_conv_stage1")
    tc = min(SPLIT_TC, d)
    nd = d // tc
    grp = pl.BlockSpec((n2, 2 * K1_GROUP, tc), lambda k, j: (0, k, j))
    tab = pl.BlockSpec((K1_GROUP, 2 * n2, 2 * n2), lambda k, j: (k, 0, 0))
    b = pl.pallas_call(
        _conv_stage2_kernel,
        grid=(n1 // K1_GROUP, nd),
        in_specs=[tab, tab, grp,
                  pl.BlockSpec((K1_GROUP, 2 * n2, tc), lambda k, j: (k, 0, kf_col_off * nd + j))],
        out_specs=grp,
        out_shape=jax.ShapeDtypeStruct((n2, 2 * n1, d), BF16),
        scratch_shapes=[_slab_scratch(2 * K1_GROUP, n2, tc), _slab_scratch(2 * K1_GROUP, n2, tc)],
        compiler_params=_cparams("arbitrary", "arbitrary"),
        name="hyena_conv_stage2",
    )(g_fwd, g_inv, a, kf)
    r, c = _iota2(2 * hn, 2 * n1)
    ec, es = _cis((r % hn) * (c // 2), n1)
    ro, ri = r // hn, c % 2
    w3 = jnp.where(ro == ri, ec, jnp.where(ro == 0, -es, es)).astype(BF16)
    return _slab_mm(w3, (b, ()), F32, slab_extras=[xg, v], row_extras=[skip.reshape(1, d)],
                    epilogue=_slab_gate, name="hyena_conv_stage3")


def _proj_split_kernel(x_ref, w_ref, b_ref, res_ref, gate_ref, o_ref, wbf_ref, acc_ref, rg_ref):
    @pl.when(pl.program_id(1) == 0)
    def _():
        wbf_ref[...] = w_ref[...].astype(BF16)

    n2, g, k = x_ref.shape
    x = x_ref[...].reshape(n2 * g, k).astype(BF16)
    acc_ref[...] = jnp.dot(x, wbf_ref[...], preferred_element_type=F32) + b_ref[...]

    def body(q, c):
        _rows_scatter(rg_ref, q, acc_ref[pl.ds(pl.multiple_of(q * g, g), g), :], n2)
        return c
    lax.fori_loop(0, n2, body, 0)
    for s in range(g):
        rows = slice(s * n2, (s + 1) * n2)
        o_ref[rows, :] = res_ref[rows, :] + gate_ref[...] * _slab_load(rg_ref, s, n2)


def _proj_residual_split(v3, w, bias, x_all, mod, chunk, seq_len, tn=512):
    n2, m, k = v3.shape
    d = x_all.shape[1]
    tn = min(tn, d)
    nd = d // tn
    g = SPLIT_ROWS // n2
    tm = n2 * g
    seq_tiles = seq_len // tm
    mod3 = mod.reshape(MOD_ROWS, 1, -1)
    res_spec = pl.BlockSpec((tm, tn), lambda j, i: (i, j))
    return pl.pallas_call(
        _proj_split_kernel,
        grid=(nd, m // g),
        in_specs=[pl.BlockSpec((n2, g, k), lambda j, i: (0, i, 0)),
                  pl.BlockSpec((k, tn), lambda j, i: (0, j)),
                  pl.BlockSpec((1, tn), lambda j, i: (0, j)),
                  res_spec,
                  pl.BlockSpec((None, 1, tn), lambda j, i: (i // seq_tiles, 0, chunk * nd + j))],
        out_specs=res_spec,
        out_shape=jax.ShapeDtypeStruct(x_all.shape, F32),
        scratch_shapes=[pltpu.VMEM((k, tn), BF16), pltpu.VMEM((tm, tn), F32), _slab_scratch(g, n2, tn)],
        input_output_aliases={3: 0},
        compiler_params=_cparams("arbitrary", "arbitrary"),
        name="hyena_out_lat",
    )(v3, w, bias.reshape(1, d), x_all, mod3)


def _layer_hyena(x_all, h, mod, rows, prm, w_out, b_out):
    (w_in, b_in, conv_w, conv_b, f_w1, f_b1, f_w2, f_b2, f_w3, f_freq, skip) = prm
    d = w_in.shape[0]
    assert rows.batch == 2, "the long convolution carries the two batch rows as one complex sequence"
    fprm = (f_w1, f_b1, f_w2, f_b2, f_w3, f_freq)
    zc = _hyena_in(h, w_in, b_in, conv_w, conv_b, rows, rows.ctx_off, rows.n_ctx, rows.ctx_len)
    k2u, sumsq = _hyena_filters(rows.ctx_len, d, *fprm)
    vc = _hyena_long_conv_dense(zc[2], zc[0], k2u, sumsq, 0, skip[0], rows.ctx_len)
    vc = _hyena_long_conv_dense(vc, zc[1], k2u, sumsq, 1, skip[1], rows.ctx_len)
    x_all = _proj_residual(vc, w_out, b_out, x_all, mod, 2, rows, rows.ctx_off, name="hyena_out_ctx")
    seq = rows.seq
    split_rows = _Rows(rows.batch, rows.ctx_len, seq, SPLIT_ROWS)
    zl = _hyena_in(h, w_in, b_in, conv_w, conv_b, split_rows, 0, rows.n_lat, seq, time_split=True)
    k3, sumsq = _hyena_filters(seq, d, *fprm, tp=2 * SPLIT_ROWS, tc=SPLIT_TC, time_split=True)
    tables = _stage2_tables(2 * seq, 2 * seq // DFT_N2)
    kf = _hyena_kf_split(k3, sumsq, tables)
    v = _hyena_long_conv_split((zl, (2,)), (zl, (0,)), kf, 0, skip[0], tables)
    v = _hyena_long_conv_split((v, ()), (zl, (1,)), kf, 1, skip[1], tables)
    return _proj_residual_split(v, w_out, b_out, x_all, mod, 2, seq)


def _layer_pool(x_all, h, mod, rows, w_grp, scale):
    x_all = _pool_mixer(h, w_grp, scale, x_all, mod, 2, rows, rows.ctx_off, rows.n_ctx, rows.ctx_len)
    return _pool_mixer(h, w_grp, scale, x_all, mod, 2, rows, rows.lat_off, rows.n_lat, rows.seq)


def kernel(x, c, ctx, c_ctx, ada_w, ada_b, norm_g, final_g, hy_w_in, hy_b_in, hy_conv_w, hy_conv_b, hy_f_w1, hy_f_b1, hy_f_w2, hy_f_b2, hy_f_w3, hy_f_freq, hy_skip, hy_w_out, hy_b_out, fn_w_out, fn_b_out, pl_w, pl_scale, mla_w_dq, mla_g_q, mla_w_uq, mla_w_dkv, mla_g_kv, mla_w_ukv, mla_w_o, moe_w_group, moe_b_group, moe_w_expert, moe_b_expert, moe_w1, moe_w3, moe_w2):
    batch, seq, d = x.shape
    ctx_len = ctx.shape[1]
    depth = ada_w.shape[0]
    rows = _Rows(batch, ctx_len, seq, ROW_TILE)
    mod = _ada_mod(c, c_ctx, ada_w, ada_b)
    x_all = _pack_rows(x, ctx, rows)
    h_dtype = {0: BF16, 1: BF16, 2: F32, 3: BF16}
    h = _norm_mod(x_all, norm_g[0, 0], mod[0], 0, rows, h_dtype[0])
    for i in range(depth):
        kind, j = i % N_MIXERS, i // N_MIXERS
        m = mod[i]
        if kind == 0:
            prm = (hy_w_in[j], hy_b_in[j], hy_conv_w[j], hy_conv_b[j], hy_f_w1[j], hy_f_b1[j], hy_f_w2[j],
                   hy_f_b2[j], hy_f_w3[j], hy_f_freq[j], hy_skip[j])
            x_all = _layer_hyena(x_all, h, m, rows, prm, hy_w_out[j], hy_b_out[j])
        elif kind == 1:
            x_all = _layer_fnet(x_all, h, m, rows, fn_w_out[j], fn_b_out[j])
        elif kind == 2:
            x_all = _layer_pool(x_all, h, m, rows, pl_w[j], pl_scale[j])
        else:
            x_all = _layer_mla(x_all, h, m, rows, mla_w_dq[j], mla_g_q[j], mla_w_uq[j], mla_w_dkv[j],
                               mla_g_kv[j], mla_w_ukv[j], mla_w_o[j], update_ctx=i < depth - 1)
        ht, eid, wts = _norm_router(x_all, norm_g[i, 1], m, 3, rows, moe_w_group[i], moe_b_group[i],
                                    moe_w_expert[i], moe_b_expert[i])
        y = _moe_experts(ht, eid, wts, moe_w1, moe_w3, moe_w2, i)
        if i + 1 < depth:
            x_all, h = _moe_combine(x_all, y, m, 5, rows,
                                    next_norm=(norm_g[i + 1, 0], mod[i + 1], h_dtype[(i + 1) % N_MIXERS]))
    return _moe_combine(x_all, y, m, 5, rows, final_g=final_g).reshape(batch, seq, d)
```

```python
import functools
import math

import numpy as np
import jax
import jax.numpy as jnp
from jax import lax
from jax.experimental import pallas as pl
from jax.experimental.pallas import tpu as pltpu

F32 = jnp.float32
BF16 = jnp.bfloat16
HIGHEST = lax.Precision.HIGHEST

EPS = 1e-6
LANES = 128
MOD_ROWS = 8
VMEM_LIMIT = 56 * 1024 * 1024

N_MIXERS = 4
HYENA_ORDER = 2
FILTER_EMB = 33
DECAY_TARGET = 1e-2
FAST_DECAY_PCT = 0.3
SLOW_DECAY_PCT = 1.5
MOD_SHIFT = 0.0
FNET_GROUPS = 4
POOL_WINDOWS = (2, 4, 8, 16)
MLA_HEADS = 16
QK_NOPE = 128
QK_ROPE = 64
V_DIM = 128
GRID_W = 64
ROPE_THETA = 10000.0
N_GROUPS = 4
EXPERTS_PER_GROUP = 8
TOPK_EXPERT = 2
ROW_TILE = 256
PROJ_ROWS = 512
MOE_ROWS = 256


def _cparams(*sem):
    return pltpu.CompilerParams(dimension_semantics=sem, vmem_limit_bytes=VMEM_LIMIT)


def _ada_kernel(st_ref, w_ref, b_ref, o_ref, *, nrows):
    s = st_ref[...]
    s = s * jax.nn.sigmoid(s)
    w = w_ref[...]
    o_ref[...] = jnp.broadcast_to(b_ref[...], o_ref.shape)
    for r in range(nrows):
        o_ref[r:r + 1, :] = jnp.sum(s[:, r:r + 1] * w, axis=0, keepdims=True) + b_ref[...]


def _ada_mod(c, c_ctx, ada_w, ada_b):
    depth, d, n = ada_w.shape
    nrows = c.shape[0] + 1
    st = jnp.zeros((d, MOD_ROWS), F32).at[:, :nrows - 1].set(c.T).at[:, nrows - 1].set(c_ctx)
    tn = 1024 if n % 1024 == 0 else n
    return pl.pallas_call(
        functools.partial(_ada_kernel, nrows=nrows),
        grid=(depth, n // tn),
        in_specs=[pl.BlockSpec((d, MOD_ROWS), lambda l, j: (0, 0)),
                  pl.BlockSpec((None, d, tn), lambda l, j: (l, 0, j)),
                  pl.BlockSpec((None, 1, tn), lambda l, j: (l, 0, j))],
        out_specs=pl.BlockSpec((None, MOD_ROWS, tn), lambda l, j: (l, 0, j)),
        out_shape=jax.ShapeDtypeStruct((depth, MOD_ROWS, n), F32),
        compiler_params=_cparams("arbitrary", "arbitrary"),
        name="ada_mod",
    )(st, ada_w, ada_b.reshape(depth, 1, n))


class _Rows:
    def __init__(self, batch, ctx_len, seq, tile):
        assert seq % tile == 0
        self.batch, self.ctx_len, self.seq, self.tile = batch, ctx_len, seq, tile
        self.n_ctx = batch * ctx_len
        self.n_lat = batch * seq
        self.n_all = self.n_ctx + self.n_lat
        self.lat_blocks = seq // tile
        self.lat_off = 0
        self.ctx_off = self.n_lat // tile

    def mod_row(self, i):
        return jnp.where(i >= self.ctx_off, self.batch, i // self.lat_blocks)


def _pack_rows(x, ctx, rows):
    d = x.shape[-1]
    pad = -rows.n_all % SPLIT_ROWS
    return jnp.concatenate([x.reshape(-1, d), ctx.reshape(-1, d), jnp.zeros((pad, d), x.dtype)], axis=0)


HI_MASK = -65536


def _pack_halves(x):
    half = x.shape[1] // 2
    lo = lax.bitcast_convert_type(x[:, :half].astype(BF16).astype(F32), jnp.int32)
    hi = lax.bitcast_convert_type(x[:, half:].astype(BF16).astype(F32), jnp.int32)
    return lax.shift_right_logical(lo, 16) | (hi & HI_MASK)


def _unpack_halves(w):
    lo = lax.bitcast_convert_type(lax.shift_left(w, 16), F32)
    hi = lax.bitcast_convert_type(w & HI_MASK, F32)
    return lo, hi


def _rms_mod(x, g, sh, sc):
    y = x * lax.rsqrt(jnp.mean(x * x, axis=-1, keepdims=True) + EPS) * g
    return y * (1.0 + sc) + sh


def _norm_mod_kernel(x_ref, g_ref, sh_ref, sc_ref, o_ref):
    o_ref[...] = _rms_mod(x_ref[...], g_ref[...], sh_ref[...], sc_ref[...]).astype(o_ref.dtype)


def _norm_mod(x_all, g, mod, chunk, rows, out_dtype):
    n, d = rows.n_all, x_all.shape[1]
    tm = rows.tile
    mod3 = mod.reshape(MOD_ROWS, 1, -1)
    return pl.pallas_call(
        _norm_mod_kernel,
        grid=(n // tm,),
        in_specs=[pl.BlockSpec((tm, d), lambda i: (i, 0)),
                  pl.BlockSpec((1, d), lambda i: (0, 0)),
                  pl.BlockSpec((None, 1, d), lambda i: (rows.mod_row(i), 0, chunk)),
                  pl.BlockSpec((None, 1, d), lambda i: (rows.mod_row(i), 0, chunk + 1))],
        out_specs=pl.BlockSpec((tm, d), lambda i: (i, 0)),
        out_shape=jax.ShapeDtypeStruct((n, d), out_dtype),
        compiler_params=_cparams("arbitrary"),
        name="norm_mod",
    )(x_all, g.reshape(1, d), mod3, mod3)


def _norm_router_kernel(x_ref, g_ref, sh_ref, sc_ref, wr_ref, br_ref, h_ref, eid_ref, wt_ref, whi_ref, wlo_ref,
                        *, n_groups, per_group):
    h = _rms_mod(x_ref[...], g_ref[...], sh_ref[...], sc_ref[...])
    h_ref[...] = _pack_halves(h)

    @pl.when(pl.program_id(0) == 0)
    def _():
        w = wr_ref[...]
        w_hi = w.astype(BF16)
        whi_ref[...] = w_hi
        wlo_ref[...] = (w - w_hi.astype(F32)).astype(BF16)

    h_hi = h.astype(BF16)
    h_lo = (h - h_hi.astype(F32)).astype(BF16)
    logits = (jnp.dot(h_hi, whi_ref[...], preferred_element_type=F32)
              + jnp.dot(h_lo, whi_ref[...], preferred_element_type=F32)
              + jnp.dot(h_hi, wlo_ref[...], preferred_element_type=F32)) + br_ref[...]
    lane = lax.broadcasted_iota(jnp.int32, logits.shape, 1).astype(F32)
    neg = -jnp.inf
    gl = jnp.where(lane < n_groups, logits, neg)
    gmax = jnp.max(gl, axis=-1, keepdims=True)
    p_grp = 1.0 / jnp.sum(jnp.exp(gl - gmax), axis=-1, keepdims=True)
    g_idx = jnp.min(jnp.where(gl == gmax, lane, float(LANES)), axis=-1, keepdims=True)
    lo = n_groups + g_idx * per_group
    el = jnp.where((lane >= lo) & (lane < lo + per_group), logits, neg)
    e1 = jnp.max(el, axis=-1, keepdims=True)
    i1 = jnp.min(jnp.where(el == e1, lane, float(LANES)), axis=-1, keepdims=True)
    el2 = jnp.where(lane == i1, neg, el)
    e2 = jnp.max(el2, axis=-1, keepdims=True)
    i2 = jnp.min(jnp.where(el2 == e2, lane, float(LANES)), axis=-1, keepdims=True)
    r = jnp.exp(e2 - e1)
    w1 = p_grp / (1.0 + r)
    w2 = p_grp * r / (1.0 + r)
    eid = jnp.where(lane == 0, i1 - n_groups, jnp.where(lane == 1, i2 - n_groups, 0.0))
    eid_ref[...] = eid.astype(jnp.int32)
    wt_ref[...] = jnp.where(lane == 0, w1, jnp.where(lane == 1, w2, 0.0))


def _norm_router(x_all, g, mod, chunk, rows, w_group, b_group, w_expert, b_expert):
    n, d = rows.n_all, x_all.shape[1]
    tm = rows.tile
    n_groups = w_group.shape[1]
    n_experts = w_expert.shape[1]
    wr = jnp.zeros((d, LANES), F32).at[:, :n_groups].set(w_group).at[:, n_groups:n_groups + n_experts].set(w_expert)
    br = jnp.zeros((1, LANES), F32).at[0, :n_groups].set(b_group).at[0, n_groups:n_groups + n_experts].set(b_expert)
    mod3 = mod.reshape(MOD_ROWS, 1, -1)
    h, eid, wts = pl.pallas_call(
        functools.partial(_norm_router_kernel, n_groups=n_groups, per_group=n_experts // n_groups),
        grid=(n // tm,),
        in_specs=[pl.BlockSpec((tm, d), lambda i: (i, 0)),
                  pl.BlockSpec((1, d), lambda i: (0, 0)),
                  pl.BlockSpec((None, 1, d), lambda i: (rows.mod_row(i), 0, chunk)),
                  pl.BlockSpec((None, 1, d), lambda i: (rows.mod_row(i), 0, chunk + 1)),
                  pl.BlockSpec((d, LANES), lambda i: (0, 0)),
                  pl.BlockSpec((1, LANES), lambda i: (0, 0))],
        out_specs=[pl.BlockSpec((tm, d // 2), lambda i: (i, 0)),
                   pl.BlockSpec((tm, LANES), lambda i: (i, 0)),
                   pl.BlockSpec((tm, LANES), lambda i: (i, 0))],
        out_shape=[jax.ShapeDtypeStruct((n, d // 2), jnp.int32),
                   jax.ShapeDtypeStruct((n, LANES), jnp.int32),
                   jax.ShapeDtypeStruct((n, LANES), F32)],
        scratch_shapes=[pltpu.VMEM((d, LANES), BF16), pltpu.VMEM((d, LANES), BF16)],
        compiler_params=_cparams("arbitrary"),
        name="norm_router",
    )(x_all, g.reshape(1, d), mod3, mod3, wr, br)
    return h, eid[:, :TOPK_EXPERT], wts[:, :TOPK_EXPERT]


def _moe_dispatch(eid, wts, n_experts, tm):
    n_tok = eid.shape[0]
    n_assign = eid.size
    e_flat = eid.reshape(-1)
    order = jnp.argsort(e_flat).astype(jnp.int32)
    counts = jnp.sum((e_flat[:, None] == jnp.arange(n_experts, dtype=jnp.int32)[None, :]).astype(jnp.int32), axis=0)
    padded = (counts + tm - 1) // tm * tm
    start = jnp.cumsum(counts) - counts
    pend = jnp.cumsum(padded)
    pstart = pend - padded
    n_blocks = (n_assign + n_experts * (tm - 1) + tm - 1) // tm
    blk_start = jnp.arange(n_blocks, dtype=jnp.int32) * tm
    blk_e = jnp.minimum(jnp.sum((pend[None, :] <= blk_start[:, None]).astype(jnp.int32), axis=1), n_experts - 1)
    blk_rows = jnp.clip(pstart[blk_e] + counts[blk_e] - blk_start, 0, tm).astype(jnp.int32)
    row = lax.broadcasted_iota(jnp.int32, (n_blocks, tm), 1)
    slot_e = blk_e[:, None]
    src = start[slot_e] + (blk_start[:, None] + row - pstart[slot_e])
    valid = row < blk_rows[:, None]
    asg = order[jnp.clip(src, 0, n_assign - 1)]
    tok = asg // TOPK_EXPERT
    buf_tok = jnp.where(valid, tok, 0)
    dummy = TOPK_EXPERT * n_tok + (jnp.arange(n_blocks, dtype=jnp.int32)[:, None] % 2) * tm + row
    buf_asg = jnp.where(valid, (asg % TOPK_EXPERT) * n_tok + tok, dummy)
    buf_w = jnp.where(valid, wts.reshape(-1)[asg], 0.0)
    return (buf_tok.reshape(n_blocks, 1, tm), buf_asg.reshape(n_blocks, 1, tm),
            buf_w.reshape(n_blocks, tm, 1), blk_e.astype(jnp.int32), blk_rows)


def _moe_kernel(blk_e_ref, blk_rows_ref, tok_ref, tok_next_ref, asg_ref, roww_ref, h_hbm,
                w1_ref, w3_ref, w2_ref, y_hbm, xbuf, ybuf, w1b, w3b, w2b, gsem, ssem):
    b = pl.program_id(0)
    nb = pl.num_programs(0)
    slot = b % 2
    other = 1 - slot

    tm = xbuf.shape[1]

    def start_gather(ids_ref, s):
        for r in range(tm):
            pltpu.make_async_copy(h_hbm.at[pl.ds(ids_ref[0, r], 1)], xbuf.at[s, pl.ds(r, 1)], gsem.at[s]).start(priority=r % 2)

    def wait_gather(s):
        pltpu.make_async_copy(h_hbm.at[pl.ds(0, tm)], xbuf.at[s], gsem.at[s]).wait()

    def start_scatter(s):
        for r in range(tm):
            pltpu.make_async_copy(ybuf.at[s, pl.ds(r, 1)], y_hbm.at[pl.ds(asg_ref[0, r], 1)], ssem.at[s]).start(priority=r % 2)

    def wait_scatter(s):
        pltpu.make_async_copy(ybuf.at[s], y_hbm.at[pl.ds(0, tm)], ssem.at[s]).wait()

    def used(blk):
        return blk_rows_ref[jnp.clip(blk, 0, nb - 1)] > 0

    @pl.when(b == 0)
    def _():
        ybuf[...] = jnp.zeros(ybuf.shape, ybuf.dtype)
        n_real = y_hbm.shape[0] - 2 * tm
        for s in range(2):
            fill = pltpu.make_async_copy(ybuf.at[s], y_hbm.at[pl.ds(n_real + s * tm, tm)], ssem.at[s])
            fill.start()
            fill.wait()

    @pl.when((b == 0) & used(b))
    def _():
        start_gather(tok_ref, slot)

    @pl.when((b + 1 < nb) & used(b + 1))
    def _():
        start_gather(tok_next_ref, other)

    @pl.when((b >= 2) & used(b - 2))
    def _():
        wait_scatter(slot)

    @pl.when(used(b))
    def _():
        wait_gather(slot)

        @pl.when((b == 0) | (blk_e_ref[b] != blk_e_ref[jnp.maximum(b - 1, 0)]))
        def _():
            w1b[...] = w1_ref[...].astype(BF16)
            w3b[...] = w3_ref[...].astype(BF16)
            w2b[...] = w2_ref[...].astype(BF16)

        x_lo, x_hi = _unpack_halves(xbuf[slot])
        x_lo, x_hi = x_lo.astype(BF16), x_hi.astype(BF16)
        half = x_lo.shape[1]

        def up(w_ref):
            return (jnp.dot(x_lo, w_ref[0:half, :], preferred_element_type=F32)
                    + jnp.dot(x_hi, w_ref[half:2 * half, :], preferred_element_type=F32))

        a = up(w1b)
        g = up(w3b)
        hm = (a * jax.nn.sigmoid(a) * g).astype(BF16)
        ybuf[slot] = _pack_halves(jnp.dot(hm, w2b[...], preferred_element_type=F32) * roww_ref[...])
        start_scatter(slot)

    @pl.when(b == nb - 1)
    def _():
        @pl.when((nb >= 2) & used(b - 1))
        def _():
            wait_scatter(other)

        @pl.when(used(b))
        def _():
            wait_scatter(slot)


def _moe_experts(h, eid, wts, w1, w3, w2, layer):
    n, dh = h.shape
    d = 2 * dh
    n_experts, de = w1.shape[1], w1.shape[3]
    tm = MOE_ROWS
    tok, asg, roww, blk_e, blk_rows = _moe_dispatch(eid, wts, n_experts, tm)
    n_blocks = tok.shape[0]
    smem_blk = lambda f: pl.BlockSpec((None, 1, tm), f, memory_space=pltpu.SMEM)
    grid_spec = pltpu.PrefetchScalarGridSpec(
        num_scalar_prefetch=2,
        grid=(n_blocks,),
        in_specs=[smem_blk(lambda b, be, br: (b, 0, 0)),
                  smem_blk(lambda b, be, br: (jnp.minimum(b + 1, n_blocks - 1), 0, 0)),
                  smem_blk(lambda b, be, br: (b, 0, 0)),
                  pl.BlockSpec((None, tm, 1), lambda b, be, br: (b, 0, 0)),
                  pl.BlockSpec(memory_space=pl.ANY),
                  pl.BlockSpec((None, None, d, de), lambda b, be, br: (layer, be[b], 0, 0)),
                  pl.BlockSpec((None, None, d, de), lambda b, be, br: (layer, be[b], 0, 0)),
                  pl.BlockSpec((None, None, de, d), lambda b, be, br: (layer, be[b], 0, 0))],
        out_specs=pl.BlockSpec(memory_space=pl.ANY),
        scratch_shapes=[pltpu.VMEM((2, tm, dh), jnp.int32), pltpu.VMEM((2, tm, dh), jnp.int32),
                        pltpu.VMEM((d, de), BF16), pltpu.VMEM((d, de), BF16), pltpu.VMEM((de, d), BF16),
                        pltpu.SemaphoreType.DMA((2,)), pltpu.SemaphoreType.DMA((2,))],
    )
    return pl.pallas_call(
        _moe_kernel,
        grid_spec=grid_spec,
        out_shape=jax.ShapeDtypeStruct((n * TOPK_EXPERT + 2 * tm, dh), jnp.int32),
        compiler_params=_cparams("arbitrary"),
        name="moe_experts",
    )(blk_e, blk_rows, tok, tok, asg, roww, h, w1, w3, w2)


def _moe_combine_kernel(x_ref, *rest, follow):
    y_refs, gate_ref, rest = rest[:TOPK_EXPERT], rest[TOPK_EXPERT], rest[TOPK_EXPERT + 1:]
    def expert_out(y_ref):
        return jnp.concatenate(_unpack_halves(y_ref[...]), axis=1)

    acc = expert_out(y_refs[0])
    for y_ref in y_refs[1:]:
        acc = acc + expert_out(y_ref)
    x_new = x_ref[...] + gate_ref[...] * acc
    if follow == "next_norm":
        g_ref, sh_ref, sc_ref, o_ref, h_ref = rest
        o_ref[...] = x_new
        h_ref[...] = _rms_mod(x_new, g_ref[...], sh_ref[...], sc_ref[...]).astype(h_ref.dtype)
    else:
        g_ref, o_ref = rest
        o_ref[...] = x_new * lax.rsqrt(jnp.mean(x_new * x_new, axis=-1, keepdims=True) + EPS) * g_ref[...]


def _moe_combine(x_all, y, mod, chunk, rows, *, next_norm=None, final_g=None):
    n, d = rows.n_all, x_all.shape[1]
    tm = rows.tile
    mod3 = mod.reshape(MOD_ROWS, 1, -1)
    nblk = n // tm
    row_blk = pl.BlockSpec((tm, d), lambda i: (i, 0))
    vec = pl.BlockSpec((1, d), lambda i: (0, 0))
    y_specs = [pl.BlockSpec((tm, d // 2), functools.partial(lambda i, k: (i + k * nblk, 0), k=k))
               for k in range(TOPK_EXPERT)]
    in_specs = [row_blk] + y_specs + [pl.BlockSpec((None, 1, d), lambda i: (rows.mod_row(i), 0, chunk))]
    args = [x_all] + [y] * TOPK_EXPERT + [mod3]
    if next_norm is not None:
        g, mod_next, out_dtype = next_norm
        modn = mod_next.reshape(MOD_ROWS, 1, -1)
        in_specs += [vec, pl.BlockSpec((None, 1, d), lambda i: (rows.mod_row(i), 0, 0)),
                     pl.BlockSpec((None, 1, d), lambda i: (rows.mod_row(i), 0, 1))]
        args += [g.reshape(1, d), modn, modn]
        return pl.pallas_call(
            functools.partial(_moe_combine_kernel, follow="next_norm"),
            grid=(nblk,),
            in_specs=in_specs,
            out_specs=[row_blk, row_blk],
            out_shape=[jax.ShapeDtypeStruct(x_all.shape, F32), jax.ShapeDtypeStruct((n, d), out_dtype)],
            input_output_aliases={0: 0},
            compiler_params=_cparams("arbitrary"),
            name="moe_combine",
        )(*args)
    return pl.pallas_call(
        functools.partial(_moe_combine_kernel, follow="final_norm"),
        grid=(rows.n_lat // tm,),
        in_specs=in_specs + [vec],
        out_specs=row_blk,
        out_shape=jax.ShapeDtypeStruct((rows.n_lat, d), F32),
        compiler_params=_cparams("arbitrary"),
        name="moe_combine_final",
    )(*args, final_g.reshape(1, d))


def _mm_kernel(*refs, has_bias, n_extra, epilogue):
    x_ref, w_ref = refs[0], refs[1]
    pos = 2
    b_ref = None
    if has_bias:
        b_ref = refs[pos]
        pos += 1
    extras = refs[pos:pos + n_extra]
    outs = refs[pos + n_extra:-1]
    wbf = refs[-1]

    @pl.when(pl.program_id(1) == 0)
    def _():
        wbf[...] = w_ref[...].astype(BF16)

    acc = jnp.dot(x_ref[...].astype(BF16), wbf[...], preferred_element_type=F32)
    if has_bias:
        acc = acc + b_ref[...]
    epilogue(acc, extras, outs)


def _store_epilogue(acc, extras, outs):
    outs[0][...] = acc.astype(outs[0].dtype)


def _matmul(x, w, *, w_lead=(), bias=None, tm, tn, m_rows=None, x_row_off=0, extras=(), outs,
            epilogue=_store_epilogue, aliases=None, name="matmul"):
    k = x.shape[1]
    n = w.shape[-1]
    m_rows = x.shape[0] if m_rows is None else m_rows
    assert m_rows % tm == 0 and n % tn == 0 and w.shape[-2] == k
    lead = tuple(w_lead)
    in_specs = [pl.BlockSpec((tm, k), lambda j, i: (i + x_row_off, 0)),
                pl.BlockSpec((None,) * len(lead) + (k, tn), lambda j, i: lead + (0, j))]
    args = [x, w]
    if bias is not None:
        in_specs.append(pl.BlockSpec((1, tn), lambda j, i: (0, j)))
        args.append(bias.reshape(1, n))
    for arr, spec in extras:
        in_specs.append(spec)
        args.append(arr)
    return pl.pallas_call(
        functools.partial(_mm_kernel, has_bias=bias is not None, n_extra=len(extras), epilogue=epilogue),
        grid=(n // tn, m_rows // tm),
        in_specs=in_specs,
        out_specs=[spec for _, spec in outs],
        out_shape=[shape for shape, _ in outs],
        scratch_shapes=[pltpu.VMEM((k, tn), BF16)],
        input_output_aliases=aliases or {},
        compiler_params=_cparams("arbitrary", "arbitrary"),
        name=name,
    )(*args)


def _residual_epilogue(acc, extras, outs):
    res_ref, gate_ref = extras
    outs[0][...] = res_ref[...] + gate_ref[...] * acc


def _proj_residual(xin, w, bias, x_all, mod, chunk, rows, row_off, *, w_lead=(), tn=512, name="proj_residual"):
    tm = rows.tile
    d = x_all.shape[1]
    tn = min(tn, d)
    mod3 = mod.reshape(MOD_ROWS, 1, -1)
    nd = d // tn
    res_spec = pl.BlockSpec((tm, tn), lambda j, i: (i + row_off, j))
    gate_spec = pl.BlockSpec((None, 1, tn), lambda j, i: (rows.mod_row(i + row_off), 0, chunk * nd + j))
    n_extra_before = 2 + (bias is not None)
    out, = _matmul(xin, w, w_lead=w_lead, bias=bias, tm=tm, tn=tn,
                   extras=[(x_all, res_spec), (mod3, gate_spec)],
                   outs=[(jax.ShapeDtypeStruct(x_all.shape, F32), res_spec)],
                   epilogue=_residual_epilogue, aliases={n_extra_before: 0}, name=name)
    return out


POOL_HALO = 8


def _pool_kernel(prev_ref, main_ref, next_ref, w_ref, sc_ref, res_ref, gate_ref, o_ref, ext_ref, wbf_ref,
                 *, tm, seq_len, windows):
    i = pl.program_id(0)
    seq_tiles = seq_len // tm
    t_in_seq = i % seq_tiles

    @pl.when(i == 0)
    def _():
        wbf_ref[...] = w_ref[...].astype(BF16)

    zero_halo = jnp.zeros(prev_ref.shape, F32)
    ext_ref[0:POOL_HALO, :] = jnp.where(t_in_seq == 0, zero_halo, prev_ref[...])
    ext_ref[POOL_HALO:POOL_HALO + tm, :] = main_ref[...]
    ext_ref[POOL_HALO + tm:2 * POOL_HALO + tm, :] = jnp.where(t_in_seq == seq_tiles - 1, zero_halo, next_ref[...])

    pos = t_in_seq * tm + lax.broadcasted_iota(jnp.int32, (tm, 1), 0)
    cg = main_ref.shape[1] // len(windows)
    for gi, win in enumerate(windows):
        half = win // 2
        cols = slice(gi * cg, (gi + 1) * cg)
        s = ext_ref[pl.ds(POOL_HALO - half, tm), cols]
        for dlt in range(-half + 1, half):
            s = s + ext_ref[pl.ds(POOL_HALO + dlt, tm), cols]
        cnt = jnp.minimum(pos + half, seq_len) - jnp.maximum(pos - half, 0)
        pooled = s / cnt.astype(F32) - main_ref[:, cols]
        y = jnp.dot(pooled.astype(BF16), wbf_ref[gi], preferred_element_type=F32) * sc_ref[:, cols]
        o_ref[:, cols] = res_ref[:, cols] + gate_ref[:, cols] * y


def _pool_mixer(h, w_grp, scale, x_all, mod, chunk, rows, row_off, n_rows, seq_len):
    tm = rows.tile
    d = x_all.shape[1]
    assert max(POOL_WINDOWS) // 2 <= POOL_HALO and tm % POOL_HALO == 0 and seq_len % tm == 0
    hb = tm // POOL_HALO
    last_hblk = h.shape[0] // POOL_HALO - 1
    mod3 = mod.reshape(MOD_ROWS, 1, -1)
    main_spec = pl.BlockSpec((tm, d), lambda i: (i + row_off, 0))
    return pl.pallas_call(
        functools.partial(_pool_kernel, tm=tm, seq_len=seq_len, windows=POOL_WINDOWS),
        grid=(n_rows // tm,),
        in_specs=[pl.BlockSpec((POOL_HALO, d), lambda i: (jnp.maximum((i + row_off) * hb - 1, 0), 0)),
                  main_spec,
                  pl.BlockSpec((POOL_HALO, d), lambda i: (jnp.minimum((i + row_off + 1) * hb, last_hblk), 0)),
                  pl.BlockSpec(w_grp.shape, lambda i: (0, 0, 0)),
                  pl.BlockSpec((1, d), lambda i: (0, 0)),
                  main_spec,
                  pl.BlockSpec((None, 1, d), lambda i: (rows.mod_row(i + row_off), 0, chunk))],
        out_specs=main_spec,
        out_shape=jax.ShapeDtypeStruct(x_all.shape, F32),
        scratch_shapes=[pltpu.VMEM((tm + 2 * POOL_HALO, d), F32), pltpu.VMEM(w_grp.shape, BF16)],
        input_output_aliases={5: 0},
        compiler_params=_cparams("arbitrary"),
        name="pool_mixer",
    )(h, h, h, w_grp, scale.reshape(1, d), x_all, mod3)


DFT_N2 = 128


def _cis(num, den):
    ang = (num % den).astype(F32) * (2.0 * math.pi / den)
    return jnp.cos(ang), jnp.sin(ang)


def _iota2(n_rows, n_cols):
    return (lax.broadcasted_iota(jnp.int32, (n_rows, n_cols), 0), lax.broadcasted_iota(jnp.int32, (n_rows, n_cols), 1))


def _slab_pitch(rows):
    return rows + 8


def _slab_scratch(g, rows, width):
    return pltpu.VMEM((width // LANES, g * _slab_pitch(rows), LANES), F32)


def _slab_store(scr, s, val, rows, first_chunk=0):
    p = _slab_pitch(rows)
    for l in range(val.shape[1] // LANES):
        scr[first_chunk + l, s * p:s * p + rows, :] = val[:, l * LANES:(l + 1) * LANES]


def _slab_load(scr, s, rows):
    p = _slab_pitch(rows)
    return jnp.concatenate([scr[l, s * p:s * p + rows, :] for l in range(scr.shape[0])], axis=1)


def _rows_gather(scr, q, g, rows):
    p = _slab_pitch(rows)
    return jnp.concatenate([scr[l, pl.ds(q, g, stride=p), :] for l in range(scr.shape[0])], axis=1)


def _rows_scatter(scr, q, val, rows):
    p = _slab_pitch(rows)
    for l in range(scr.shape[0]):
        scr[l, pl.ds(q, val.shape[0], stride=p), :] = val[:, l * LANES:(l + 1) * LANES]


def _left_mm_kernel(w_ref, x_ref, *rest, epilogue):
    x = x_ref[...]
    x = x.reshape(-1, x.shape[-1])
    acc = jnp.dot(w_ref[...], x.astype(BF16), preferred_element_type=F32)
    epilogue(acc, rest[:-1], rest[-1])


def _scaled_store(scale):
    def epilogue(acc, extras, o_ref):
        o_ref[...] = (acc * scale).reshape(o_ref.shape).astype(o_ref.dtype)
    return epilogue


def _left_mm(w, x, *, grid, w_spec, x_spec, out_shape, out_spec, extras=(), epilogue=_scaled_store(1.0),
             aliases=None, name="left_mm"):
    return pl.pallas_call(
        functools.partial(_left_mm_kernel, epilogue=epilogue),
        grid=grid,
        in_specs=[w_spec, x_spec] + [s for _, s in extras],
        out_specs=out_spec,
        out_shape=out_shape,
        input_output_aliases=aliases or {},
        compiler_params=_cparams(*(("arbitrary",) * len(grid))),
        name=name,
    )(w, x, *[a for a, _ in extras])


def _fnet_channel_kernel(x_ref, w_ref, o_ref):
    cg = x_ref.shape[1]
    r = jnp.dot(x_ref[...], w_ref[...], preferred_element_type=F32)
    o_ref[0] = r[:, :cg].astype(o_ref.dtype)
    o_ref[1] = r[:, cg:].astype(o_ref.dtype)


def _fnet_channel(h, rows, row_off, n_seq, seq_len):
    tm = rows.tile
    d = h.shape[1]
    cg = d // FNET_GROUPS
    ci, ki = _iota2(cg, cg)
    cc, sc = _cis(ci * ki, cg)
    wc = jnp.concatenate([cc, -sc], axis=1).astype(BF16)
    st = seq_len // tm
    return pl.pallas_call(
        _fnet_channel_kernel,
        grid=(n_seq * st, FNET_GROUPS),
        in_specs=[pl.BlockSpec((tm, cg), lambda i, g: (i + row_off, g)),
                  pl.BlockSpec((cg, 2 * cg), lambda i, g: (0, 0))],
        out_specs=pl.BlockSpec((None, 2, tm, cg), lambda i, g: (i // st, 0, i % st, g)),
        out_shape=jax.ShapeDtypeStruct((n_seq, 2, seq_len, d), BF16),
        compiler_params=_cparams("arbitrary", "arbitrary"),
        name="fnet_channel",
    )(h, wc)


def _fnet_positions_dense(z, tc):
    n_seq, _, seq_len, d = z.shape
    scale = 1.0 / math.sqrt(seq_len * (d // FNET_GROUPS))
    ki, ci2 = _iota2(seq_len, 2 * seq_len)
    gc, gs = _cis(ki * (ci2 % seq_len), seq_len)
    g2 = jnp.where(ci2 < seq_len, gc, gs).astype(BF16)
    out = _left_mm(g2, z,
                   grid=(n_seq, d // tc),
                   w_spec=pl.BlockSpec((seq_len, 2 * seq_len), lambda s, j: (0, 0)),
                   x_spec=pl.BlockSpec((None, 2, seq_len, tc), lambda s, j: (s, 0, 0, j)),
                   out_shape=jax.ShapeDtypeStruct((n_seq, seq_len, d), BF16),
                   out_spec=pl.BlockSpec((None, seq_len, tc), lambda s, j: (s, 0, j)),
                   epilogue=_scaled_store(scale), name="fnet_dense")
    return out.reshape(n_seq * seq_len, d)


FNET_ROWS = 2048
FNET_K1_GROUP = 16


def _fnet_channel_split_kernel(x_ref, w_ref, zr_ref, zi_ref, rg_ref):
    cg = x_ref.shape[1]
    g = x_ref.shape[0] // DFT_N2
    r = jnp.dot(x_ref[...], w_ref[...], preferred_element_type=F32)
    for plane, z_ref in enumerate((zr_ref, zi_ref)):
        for s in range(g):
            _slab_store(rg_ref, s, r[s * DFT_N2:(s + 1) * DFT_N2, plane * cg:(plane + 1) * cg], DFT_N2)

        def body(n2, c):
            z_ref[n2] = _rows_gather(rg_ref, n2, g, DFT_N2).astype(z_ref.dtype)
            return c
        lax.fori_loop(0, DFT_N2, body, 0)


def _fnet_channel_split(h, n_seq, seq_len):
    tm = FNET_ROWS
    d = h.shape[1]
    cg = d // FNET_GROUPS
    ci, ki = _iota2(cg, cg)
    cc, sc = _cis(ci * ki, cg)
    wc = jnp.concatenate([cc, -sc], axis=1).astype(BF16)
    st = seq_len // tm
    g = tm // DFT_N2
    assert seq_len % tm == 0 and g % 16 == 0
    plane = jax.ShapeDtypeStruct((n_seq, DFT_N2, seq_len // DFT_N2, d), BF16)
    out_spec = pl.BlockSpec((None, DFT_N2, g, cg), lambda i, c: (i // st, 0, i % st, c))
    return pl.pallas_call(
        _fnet_channel_split_kernel,
        grid=(n_seq * st, FNET_GROUPS),
        in_specs=[pl.BlockSpec((tm, cg), lambda i, c: (i, c)),
                  pl.BlockSpec((cg, 2 * cg), lambda i, c: (0, 0))],
        out_specs=[out_spec, out_spec],
        out_shape=[plane, plane],
        scratch_shapes=[_slab_scratch(g, DFT_N2, cg)],
        compiler_params=_cparams("arbitrary", "arbitrary"),
        name="fnet_channel",
    )(h, wc)


def _fnet_stage1_kernel(w_ref, zr_ref, zi_ref, o_ref):
    w = w_ref[...]
    for s in range(zr_ref.shape[0]):
        z = jnp.concatenate([zr_ref[s], zi_ref[s]], axis=0)
        o_ref[s] = jnp.dot(w, z, preferred_element_type=F32).astype(o_ref.dtype)


def _fnet_stage2_kernel(g_ref, a_ref, o_ref, rin, rout, *, scale):
    kg = a_ref.shape[1] // 2

    def body_in(q, c):
        _rows_scatter(rin, q, a_ref[q].astype(F32), DFT_N2)
        return c
    lax.fori_loop(0, a_ref.shape[0], body_in, 0)
    for j in range(kg):
        r = jnp.dot(g_ref[j], _stage2_operand(rin, j), preferred_element_type=F32)
        _slab_store(rout, j, r * scale, DFT_N2)

    def body_out(q, c):
        o_ref[q] = _rows_gather(rout, q, kg, DFT_N2).astype(o_ref.dtype)
        return c
    lax.fori_loop(0, o_ref.shape[0], body_out, 0)


def _fnet_positions_split(zr, zi):
    n_seq, n2, n1, d = zr.shape
    seq_len = n1 * n2
    scale = 1.0 / math.sqrt(seq_len * (d // FNET_GROUPS))
    r, cidx = _iota2(2 * n1, 2 * n1)
    k1, ro, ri, nn = r // 2, r % 2, cidx // n1, cidx % n1
    fr, fs = _cis(k1 * nn, n1)
    w1 = jnp.where(ro == ri, fr, jnp.where(ro == 0, fs, -fs)).astype(BF16)
    tc = min(SPLIT_TC, d)
    zblk = pl.BlockSpec((None, N2_GROUP, n1, tc), lambda s, g, j: (s, g, 0, j))
    a = pl.pallas_call(
        _fnet_stage1_kernel,
        grid=(n_seq, n2 // N2_GROUP, d // tc),
        in_specs=[pl.BlockSpec((2 * n1, 2 * n1), lambda s, g, j: (0, 0)), zblk, zblk],
        out_specs=pl.BlockSpec((None, N2_GROUP, 2 * n1, tc), lambda s, g, j: (s, g, 0, j)),
        out_shape=jax.ShapeDtypeStruct((n_seq, n2, 2 * n1, d), BF16),
        compiler_params=_cparams("arbitrary", "arbitrary", "arbitrary"),
        name="fnet_stage1",
    )(w1, zr, zi)
    g2 = _stage2_tables(seq_len, n1)[0][:, :n2, :]
    kg = FNET_K1_GROUP
    assert n1 % kg == 0
    out = pl.pallas_call(
        functools.partial(_fnet_stage2_kernel, scale=scale),
        grid=(n_seq, n1 // kg, d // tc),
        in_specs=[pl.BlockSpec((kg, n2, 2 * n2), lambda s, k, j: (k, 0, 0)),
                  pl.BlockSpec((None, n2, 2 * kg, tc), lambda s, k, j: (s, 0, k, j))],
        out_specs=pl.BlockSpec((None, n2, kg, tc), lambda s, k, j: (s, 0, k, j)),
        out_shape=jax.ShapeDtypeStruct((n_seq, n2, n1, d), BF16),
        scratch_shapes=[_slab_scratch(2 * kg, n2, tc), _slab_scratch(kg, n2, tc)],
        compiler_params=_cparams("arbitrary", "arbitrary", "arbitrary"),
        name="fnet_stage2",
    )(g2, a)
    return out.reshape(n_seq * seq_len, d)


def _layer_fnet(x_all, h, mod, rows, w_out, b_out):
    d = x_all.shape[1]
    zc = _fnet_channel(h, rows, rows.ctx_off, rows.batch, rows.ctx_len)
    fc = _fnet_positions_dense(zc, min(d, 512))
    x_all = _proj_residual(fc, w_out, b_out, x_all, mod, 2, rows, rows.ctx_off, name="fnet_out_ctx")
    zr, zi = _fnet_channel_split(h, rows.batch, rows.seq)
    fl = _fnet_positions_split(zr, zi)
    big = _Rows(rows.batch, rows.ctx_len, rows.seq, PROJ_ROWS)
    return _proj_residual(fl, w_out, b_out, x_all, mod, 2, big, big.lat_off, tn=1024, name="fnet_out_lat")


HEAD_SLAB = 2 * LANES
MLA_SCALE = (QK_NOPE + QK_ROPE) ** -0.5
Q_SCALE = MLA_SCALE * math.log2(math.e)
V_SLAB = V_DIM + 16
ATTN_TQ = 2048
ATTN_TK = 1024


def _rope_tables(seq_len, lead_identity_rows):
    n_rows = seq_len // GRID_W
    row = jnp.repeat(jnp.arange(n_rows, dtype=F32), GRID_W)
    col = jnp.tile(jnp.arange(GRID_W, dtype=F32), n_rows)
    half = QK_ROPE // 2
    inv = ROPE_THETA ** (-jnp.arange(0, half, 2, dtype=F32) / half)
    ang_r = row[:, None] * inv
    ang_c = col[:, None] * inv
    ang = jnp.concatenate([ang_r, ang_r, ang_c, ang_c], axis=-1)
    pad = jnp.zeros((seq_len, LANES - QK_ROPE), F32)
    cos = jnp.concatenate([jnp.cos(ang), pad + 1.0], axis=-1)
    sin = jnp.concatenate([jnp.sin(ang), pad], axis=-1)
    if lead_identity_rows:
        cos = jnp.concatenate([jnp.ones((lead_identity_rows, LANES), F32), cos], axis=0)
        sin = jnp.concatenate([jnp.zeros((lead_identity_rows, LANES), F32), sin], axis=0)
    return cos, sin


def _rope(x, cos, sin):
    q = QK_ROPE // 4
    lane = lax.broadcasted_iota(jnp.int32, x.shape, 1)
    even = (lane // q) % 2 == 0
    rot = jnp.where(even, -pltpu.roll(x, LANES - q, 1), pltpu.roll(x, q, 1))
    return x * cos + rot * sin


def _rmsnorm_epilogue(acc, extras, outs):
    g_ref, = extras
    y = acc * lax.rsqrt(jnp.mean(acc * acc, axis=-1, keepdims=True) + EPS) * g_ref[...]
    outs[0][...] = y.astype(outs[0].dtype)


def _q_up_epilogue(acc, extras, outs):
    cos_ref, sin_ref = extras
    o_ref, = outs
    for hh in range(acc.shape[1] // HEAD_SLAB):
        c0 = hh * HEAD_SLAB
        o_ref[:, c0:c0 + LANES] = (acc[:, c0:c0 + LANES] * Q_SCALE).astype(o_ref.dtype)
        pe = _rope(acc[:, c0 + LANES:c0 + HEAD_SLAB], cos_ref[...], sin_ref[...])
        o_ref[:, c0 + LANES:c0 + HEAD_SLAB] = (pe * Q_SCALE).astype(o_ref.dtype)


def _kv_down_epilogue(acc, extras, outs, *, rank):
    g_ref, cos_ref, sin_ref = extras
    kvn_ref, kpe_ref = outs
    lat = acc[:, :rank]
    y = lat * lax.rsqrt(jnp.mean(lat * lat, axis=-1, keepdims=True) + EPS) * g_ref[...]
    kvn_ref[...] = y.astype(kvn_ref.dtype)
    kpe_ref[...] = _rope(acc[:, rank:rank + LANES], cos_ref[...], sin_ref[...]).astype(kpe_ref.dtype)


def _k_up_epilogue(acc, extras, outs):
    kpe_ref, = extras
    o_ref, = outs
    for hh in range(acc.shape[1] // LANES):
        o_ref[:, hh * HEAD_SLAB:hh * HEAD_SLAB + LANES] = acc[:, hh * LANES:(hh + 1) * LANES].astype(o_ref.dtype)
        o_ref[:, hh * HEAD_SLAB + LANES:(hh + 1) * HEAD_SLAB] = kpe_ref[...]


def _attn_kernel(q_ref, kc_ref, kl_ref, vc_ref, vl_ref, o_ref, *, tk):
    q = q_ref[...]

    def scores(k_tile):
        return lax.dot_general(k_tile, q, (((1,), (1,)), ((), ())), preferred_element_type=F32)

    def probs(s, m):
        return jnp.exp2((s - m).astype(BF16))

    s = scores(kc_ref[...])
    m = jnp.max(s, axis=0, keepdims=True)
    acc = jnp.dot(vc_ref[...], probs(s, m), preferred_element_type=F32)
    for j in range(kl_ref.shape[0] // tk):
        s = scores(kl_ref[j * tk:(j + 1) * tk, :])
        m_new = jnp.maximum(m, jnp.max(s, axis=0, keepdims=True))
        acc = jnp.exp2(m - m_new) * acc + jnp.dot(vl_ref[:, j * tk:(j + 1) * tk], probs(s, m_new),
                                                  preferred_element_type=F32)
        m = m_new
    o_ref[...] = (acc[:V_DIM] / acc[V_DIM:V_DIM + 1]).T.astype(o_ref.dtype)


def _value_slabs(v, batch, keys):
    v4 = v.reshape(batch, keys, MLA_HEADS, V_DIM)
    ones = jnp.ones((batch, keys, MLA_HEADS, 1), v.dtype)
    pad = jnp.zeros((batch, keys, MLA_HEADS, V_SLAB - V_DIM - 1), v.dtype)
    return jnp.concatenate([v4, ones, pad], axis=-1).transpose(0, 2, 3, 1).reshape(batch, MLA_HEADS * V_SLAB, keys)


def _attention(q, k_ctx, k_lat, vt_ctx, vt_lat):
    batch, seq, _ = q.shape
    ctx_len = k_ctx.shape[1]
    tq = min(ATTN_TQ, seq)
    tk = min(ATTN_TK, seq)
    assert seq % tq == 0 and seq % tk == 0
    return pl.pallas_call(
        functools.partial(_attn_kernel, tk=tk),
        grid=(batch, MLA_HEADS, seq // tq),
        in_specs=[pl.BlockSpec((None, tq, HEAD_SLAB), lambda b, h, i: (b, i, h)),
                  pl.BlockSpec((None, ctx_len, HEAD_SLAB), lambda b, h, i: (b, 0, h)),
                  pl.BlockSpec((None, seq, HEAD_SLAB), lambda b, h, i: (b, 0, h)),
                  pl.BlockSpec((None, V_SLAB, ctx_len), lambda b, h, i: (b, h, 0)),
                  pl.BlockSpec((None, V_SLAB, seq), lambda b, h, i: (b, h, 0))],
        out_specs=pl.BlockSpec((None, tq, V_DIM), lambda b, h, i: (b, i, h)),
        out_shape=jax.ShapeDtypeStruct((batch, seq, MLA_HEADS * V_DIM), BF16),
        compiler_params=_cparams("arbitrary", "arbitrary", "arbitrary"),
        name="mla_attention",
    )(q, k_ctx, k_lat, vt_ctx, vt_lat)


def _layer_mla(x_all, h, mod, rows, w_dq, g_q, w_uq, w_dkv, g_kv, w_ukv, w_o, update_ctx=False):
    assert not update_ctx, "attention is the last mixer of the stack: context queries are never needed"
    rows = _Rows(rows.batch, rows.ctx_len, rows.seq, PROJ_ROWS)
    tm = rows.tile
    d = x_all.shape[1]
    batch, seq, ctx_len = rows.batch, rows.seq, rows.ctx_len
    q_rank, kv_rank = w_dq.shape[1], g_kv.shape[0]
    assert V_DIM == LANES and QK_NOPE == LANES and QK_ROPE <= LANES

    w_uq_s = jnp.pad(w_uq.reshape(q_rank, MLA_HEADS, QK_NOPE + QK_ROPE),
                     ((0, 0), (0, 0), (0, HEAD_SLAB - QK_NOPE - QK_ROPE))).reshape(q_rank, MLA_HEADS * HEAD_SLAB)
    w_dkv_s = jnp.pad(w_dkv, ((0, 0), (0, kv_rank + LANES - w_dkv.shape[1])))
    w_ukv_s = w_ukv.reshape(kv_rank, MLA_HEADS, QK_NOPE + V_DIM)
    w_uk = w_ukv_s[:, :, :QK_NOPE].reshape(kv_rank, MLA_HEADS * QK_NOPE)
    w_uv = w_ukv_s[:, :, QK_NOPE:].reshape(kv_rank, MLA_HEADS * V_DIM)
    cos, sin = _rope_tables(seq, tm)
    lat_blocks = rows.lat_blocks

    cqn, = _matmul(h, w_dq, tm=tm, tn=q_rank, m_rows=rows.n_lat, x_row_off=rows.lat_off,
                   extras=[(g_q.reshape(1, q_rank), pl.BlockSpec((1, q_rank), lambda j, i: (0, 0)))],
                   outs=[(jax.ShapeDtypeStruct((rows.n_lat, q_rank), BF16), pl.BlockSpec((tm, q_rank), lambda j, i: (i, 0)))],
                   epilogue=_rmsnorm_epilogue, name="mla_q_down")
    tnq = 4 * HEAD_SLAB
    rope_lat = pl.BlockSpec((tm, LANES), lambda j, i: (1 + i % lat_blocks, 0))
    q, = _matmul(cqn, w_uq_s, tm=tm, tn=tnq, extras=[(cos, rope_lat), (sin, rope_lat)],
                 outs=[(jax.ShapeDtypeStruct((rows.n_lat, MLA_HEADS * HEAD_SLAB), BF16),
                        pl.BlockSpec((tm, tnq), lambda j, i: (i, j)))],
                 epilogue=_q_up_epilogue, name="mla_q_up")

    def keys_values(row_off, n_rows, rope_spec):
        kvn, kpe = _matmul(h, w_dkv_s, tm=tm, tn=kv_rank + LANES, m_rows=n_rows, x_row_off=row_off,
                           extras=[(g_kv.reshape(1, kv_rank), pl.BlockSpec((1, kv_rank), lambda j, i: (0, 0))),
                                   (cos, rope_spec), (sin, rope_spec)],
                           outs=[(jax.ShapeDtypeStruct((n_rows, kv_rank), BF16), pl.BlockSpec((tm, kv_rank), lambda j, i: (i, 0))),
                                 (jax.ShapeDtypeStruct((n_rows, LANES), BF16), pl.BlockSpec((tm, LANES), lambda j, i: (i, 0)))],
                           epilogue=functools.partial(_kv_down_epilogue, rank=kv_rank), name="mla_kv_down")
        tnk = 4 * LANES
        k, = _matmul(kvn, w_uk, tm=tm, tn=tnk,
                     extras=[(kpe, pl.BlockSpec((tm, LANES), lambda j, i: (i, 0)))],
                     outs=[(jax.ShapeDtypeStruct((n_rows, MLA_HEADS * HEAD_SLAB), BF16),
                            pl.BlockSpec((tm, 2 * tnk), lambda j, i: (i, j)))],
                     epilogue=_k_up_epilogue, name="mla_k_up")
        v, = _matmul(kvn, w_uv, tm=tm, tn=tnk,
                     outs=[(jax.ShapeDtypeStruct((n_rows, MLA_HEADS * V_DIM), BF16), pl.BlockSpec((tm, tnk), lambda j, i: (i, j)))],
                     name="mla_v_up")
        return k, v

    k_c, v_c = keys_values(rows.ctx_off, rows.n_ctx,pl.BlockSpec((tm, LANES), lambda j, i: (0, 0)))
    k_l, v_l = keys_values(rows.lat_off, rows.n_lat, rope_lat)
    vt_c = _value_slabs(v_c, batch, ctx_len)
    vt_l = _value_slabs(v_l, batch, seq)
    attn = _attention(q.reshape(batch, seq, -1), k_c.reshape(batch, ctx_len, -1), k_l.reshape(batch, seq, -1), vt_c, vt_l)
    attn = attn.reshape(rows.n_lat, MLA_HEADS * V_DIM)
    return _proj_residual(attn, w_o, None, x_all, mod, 2, rows, rows.lat_off, tn=1024, name="mla_out")


CONV_HALO = 16
MAX_DECAY = math.log(DECAY_TARGET) / FAST_DECAY_PCT
MIN_DECAY = math.log(DECAY_TARGET) / SLOW_DECAY_PCT


def _hyena_in_kernel(prev_ref, main_ref, next_ref, w_ref, b_ref, cw_ref, cb_ref, o_ref, xext_ref, zext_ref, wbf_ref,
                     *rg, tm, seq_len):
    i = pl.program_id(1)
    seq_tiles = seq_len // tm
    t_in_seq = i % seq_tiles

    @pl.when(i == 0)
    def _():
        wbf_ref[...] = w_ref[...].astype(BF16)

    xext_ref[0:CONV_HALO, :] = prev_ref[...]
    xext_ref[CONV_HALO:CONV_HALO + tm, :] = main_ref[...]
    xext_ref[CONV_HALO + tm:2 * CONV_HALO + tm, :] = next_ref[...]
    row = lax.broadcasted_iota(jnp.int32, (tm, 1), 0)
    first = (row == 0) & (t_in_seq == 0)
    last = (row == tm - 1) & (t_in_seq == seq_tiles - 1)
    tn = zext_ref.shape[1]
    cw = min(tn, 2 * LANES)
    for c0 in range(0, tn, cw):
        cols = slice(c0, c0 + cw)
        zext_ref[:, cols] = jnp.dot(xext_ref[...], wbf_ref[:, cols], preferred_element_type=F32) + b_ref[:, cols]
        prev = jnp.where(first, 0.0, zext_ref[pl.ds(CONV_HALO - 1, tm), cols])
        nxt = jnp.where(last, 0.0, zext_ref[pl.ds(CONV_HALO + 1, tm), cols])
        out = (prev * cw_ref[0:1, cols] + zext_ref[pl.ds(CONV_HALO, tm), cols] * cw_ref[1:2, cols]
               + nxt * cw_ref[2:3, cols] + cb_ref[:, cols])
        if rg:
            for s in range(tm // DFT_N2):
                _slab_store(rg[0], s, out[s * DFT_N2:(s + 1) * DFT_N2, :], DFT_N2, first_chunk=c0 // LANES)
        else:
            o_ref[:, cols] = out.astype(o_ref.dtype)
    if not rg:
        return
    rg_ref, = rg

    def body(n2, c):
        o_ref[n2] = _rows_gather(rg_ref, n2, tm // DFT_N2, DFT_N2).astype(o_ref.dtype)
        return c
    lax.fori_loop(0, DFT_N2, body, 0)


def _hyena_in(h, w_in, b_in, conv_w, conv_b, rows, row_off, n_rows, seq_len, tn=512, time_split=False):
    tm = rows.tile
    k, n3 = w_in.shape
    d = n3 // 3
    tn = min(tn, d)
    nd = d // tn
    hb = tm // CONV_HALO
    last_hblk = h.shape[0] // CONV_HALO - 1
    if time_split:
        g = tm // DFT_N2
        assert tm % DFT_N2 == 0 and g % 8 == 0 and seq_len % tm == 0
        out_spec = pl.BlockSpec((None, DFT_N2, g, tn), lambda j, i: (j // nd, 0, i, j % nd))
        out_shape = jax.ShapeDtypeStruct((3, DFT_N2, n_rows // DFT_N2, d), F32)
        extra_scratch = [_slab_scratch(g, DFT_N2, tn)]
    else:
        out_spec = pl.BlockSpec((None, tm, tn), lambda j, i: (j // nd, i, j % nd))
        out_shape = jax.ShapeDtypeStruct((3, n_rows, d), BF16)
        extra_scratch = []
    return pl.pallas_call(
        functools.partial(_hyena_in_kernel, tm=tm, seq_len=seq_len),
        grid=(n3 // tn, n_rows // tm),
        in_specs=[pl.BlockSpec((CONV_HALO, k), lambda j, i: (jnp.maximum((i + row_off) * hb - 1, 0), 0)),
                  pl.BlockSpec((tm, k), lambda j, i: (i + row_off, 0)),
                  pl.BlockSpec((CONV_HALO, k), lambda j, i: (jnp.minimum((i + row_off + 1) * hb, last_hblk), 0)),
                  pl.BlockSpec((k, tn), lambda j, i: (0, j)),
                  pl.BlockSpec((1, tn), lambda j, i: (0, j)),
                  pl.BlockSpec((3, tn), lambda j, i: (0, j)),
                  pl.BlockSpec((1, tn), lambda j, i: (0, j))],
        out_specs=out_spec,
        out_shape=out_shape,
        scratch_shapes=[pltpu.VMEM((tm + 2 * CONV_HALO, k), BF16), pltpu.VMEM((tm + 2 * CONV_HALO, tn), F32),
                        pltpu.VMEM((k, tn), BF16)] + extra_scratch,
        compiler_params=_cparams("arbitrary", "arbitrary"),
        name="hyena_in",
    )(h, h, h, w_in, b_in.reshape(1, n3), conv_w, conv_b.reshape(1, n3))


def _hyena_filter_kernel(z_ref, w1_ref, b1_ref, w2_ref, b2_ref, f0_ref, f1_ref, w3_ref, dl_ref, k_ref, ss_ref,
                         h2_ref, *rg, tp, seq_len):
    p = pl.program_id(0)
    z = z_ref[...]

    @pl.when(pl.program_id(1) == 0)
    def _():
        h1 = jnp.sin(f0_ref[...] * (jnp.dot(z, w1_ref[...], precision=HIGHEST, preferred_element_type=F32) + b1_ref[...]))
        h2 = jnp.sin(f1_ref[...] * (jnp.dot(h1, w2_ref[...], precision=HIGHEST, preferred_element_type=F32) + b2_ref[...]))
        h2_ref[...] = h2.astype(BF16)

    filt = jnp.dot(h2_ref[...], w3_ref[...].astype(BF16), preferred_element_type=F32)
    t = z[:, 0:1]
    kk = filt * (jnp.exp(-t * dl_ref[...]) + MOD_SHIFT)
    circ = p * tp + lax.broadcasted_iota(jnp.int32, (tp, 1), 0)
    kk = jnp.where(circ == seq_len, 0.0, kk)
    if rg:
        rg_ref, = rg
        for s in range(tp // DFT_N2):
            _slab_store(rg_ref, s, kk[s * DFT_N2:(s + 1) * DFT_N2, :], DFT_N2)

        def body(n2, c):
            k_ref[n2] = _rows_gather(rg_ref, n2, tp // DFT_N2, DFT_N2).astype(k_ref.dtype)
            return c
        lax.fori_loop(0, DFT_N2, body, 0)
    else:
        k_ref[...] = kk.astype(k_ref.dtype)
    ss_ref[...] = jnp.sum(kk * kk, axis=0, keepdims=True)


def _hyena_filters(seq_len, d, f_w1, f_b1, f_w2, f_b2, f_w3, f_freq, tp=256, tc=1024, time_split=False):
    f32 = F32
    hid = f_w1.shape[1]
    od = HYENA_ORDER * d
    t = jnp.linspace(0.0, 1.0, seq_len, dtype=f32)[:, None]
    bands = (FILTER_EMB - 1) // 2
    w = 2.0 * math.pi * jnp.arange(seq_len, dtype=f32)[:, None] / seq_len
    f = jnp.linspace(1e-4, bands - 1, bands, dtype=f32)[None, :]
    z = jnp.concatenate([t, jnp.cos(f * w), -jnp.sin(f * w)], axis=-1)
    circ = jnp.arange(2 * seq_len)
    offs = jnp.where(circ < seq_len, circ, jnp.minimum(2 * seq_len - circ, seq_len - 1))
    z2 = jnp.pad(z[offs], ((0, 0), (0, LANES - FILTER_EMB)))
    w1p = jnp.pad(f_w1, ((0, LANES - FILTER_EMB), (0, 0)))
    w3s = f_w3.reshape(hid, HYENA_ORDER, 2, d).transpose(2, 0, 1, 3).reshape(2, hid, od)
    deltas = jnp.tile(jnp.abs(jnp.linspace(MIN_DECAY, MAX_DECAY, d, dtype=f32)), HYENA_ORDER)[None, :]
    tp = min(tp, seq_len)
    tc = min(tc, od)
    side_tiles = seq_len // tp
    small = lambda shape: pl.BlockSpec(shape, lambda p, j: (0,) * len(shape))
    n_ptiles = 2 * side_tiles
    if time_split:
        g = tp // DFT_N2
        assert tp % DFT_N2 == 0 and g % 16 == 0
        k_spec = pl.BlockSpec((DFT_N2, g, tc), lambda p, j: (0, p, j))
        k_shape = jax.ShapeDtypeStruct((DFT_N2, 2 * seq_len // DFT_N2, od), BF16)
        extra_scratch = [_slab_scratch(g, DFT_N2, tc)]
    else:
        k_spec = pl.BlockSpec((tp, tc), lambda p, j: (p, j))
        k_shape = jax.ShapeDtypeStruct((2 * seq_len, od), BF16)
        extra_scratch = []
    k, ss_parts = pl.pallas_call(
        functools.partial(_hyena_filter_kernel, tp=tp, seq_len=seq_len),
        grid=(n_ptiles, od // tc),
        in_specs=[pl.BlockSpec((tp, LANES), lambda p, j: (p, 0)),
                  small((LANES, hid)), small((1, hid)), small((hid, hid)), small((1, hid)), small((1, hid)), small((1, hid)),
                  pl.BlockSpec((None, hid, tc), lambda p, j: (p // side_tiles, 0, j)),
                  pl.BlockSpec((1, tc), lambda p, j: (0, j))],
        out_specs=[k_spec, pl.BlockSpec((None, 1, tc), lambda p, j: (p, 0, j))],
        out_shape=[k_shape, jax.ShapeDtypeStruct((n_ptiles, 1, od), F32)],
        scratch_shapes=[pltpu.VMEM((tp, hid), BF16)] + extra_scratch,
        compiler_params=_cparams("arbitrary", "arbitrary"),
        name="hyena_filters",
    )(z2, w1p, f_b1.reshape(1, hid), f_w2, f_b2.reshape(1, hid), f_freq[0:1], f_freq[1:2], w3s, deltas)
    return k, jnp.sum(ss_parts, axis=0)


def _stage2_tables(n, n1):
    n2 = n // n1
    ac, as_ = _cis(jnp.arange(n1, dtype=jnp.int32)[:, None] * jnp.arange(n2, dtype=jnp.int32)[None, :], n)
    r, c = _iota2(n2, n2)
    bc, bs = _cis(r * c, n2)

    def blocks(ac3, as3):
        gc = ac3 * bc[None] - as3 * bs[None]
        gs = as3 * bc[None] + ac3 * bs[None]
        return gc, gs

    gc, gs = blocks(ac[:, None, :], as_[:, None, :])
    gct, gst = blocks(ac[:, :, None], as_[:, :, None])
    fwd = jnp.concatenate([jnp.concatenate([gc, gs], axis=2), jnp.concatenate([-gs, gc], axis=2)], axis=1)
    inv = jnp.concatenate([jnp.concatenate([gct, -gst], axis=2), jnp.concatenate([gst, gct], axis=2)], axis=1)
    return fwd.astype(BF16), inv.astype(BF16)


def _complex_mul(x, kf, n2):
    xr, xi = x[:n2], x[n2:]
    kr, ki = kf[:n2].astype(F32), kf[n2:].astype(F32)
    return jnp.concatenate([xr * kr - xi * ki, xr * ki + xi * kr], axis=0)


def _conv_dense_kernel(fk_ref, fz_ref, fzt_ref, k_ref, ss_ref, u_ref, xg_ref, skip_ref, o_ref):
    n = k_ref.shape[0]
    kf = jnp.dot(fk_ref[...], k_ref[...], preferred_element_type=F32) * ss_ref[...]
    u = u_ref[...]
    z = jnp.dot(fz_ref[...], u, preferred_element_type=F32)
    y = _complex_mul(z, kf, n)
    conv = jnp.dot(fzt_ref[...], y.astype(BF16), preferred_element_type=F32)
    o_ref[...] = (xg_ref[...].astype(F32) * (conv + skip_ref[...] * u.astype(F32))).astype(o_ref.dtype)


def _hyena_long_conv_dense(u, xg, k2u, sumsq, order, skip, seq_len):
    n_rows, d = u.shape
    assert n_rows == 2 * seq_len
    n = 2 * seq_len
    colscale = lax.rsqrt(sumsq + EPS) / n
    r, c = _iota2(2 * n, n)
    kc, ks = _cis((r % n) * c, n)
    fk = jnp.where(r < n, kc, -ks).astype(BF16)
    r, c = _iota2(2 * n, 2 * seq_len)
    zc, zs = _cis((r % n) * (c % seq_len), n)
    fz = jnp.where((r < n) == (c < seq_len), zc, jnp.where(r < n, zs, -zs))
    tc = min(d, 512)
    nd = d // tc
    full = lambda a: pl.BlockSpec(a.shape, lambda j: (0, 0))
    fz_b, fzt_b = fz.astype(BF16), fz.T.astype(BF16)
    blk = pl.BlockSpec((n_rows, tc), lambda j: (0, j))
    return pl.pallas_call(
        _conv_dense_kernel,
        grid=(nd,),
        in_specs=[full(fk), full(fz_b), full(fzt_b),
                  pl.BlockSpec((n, tc), lambda j: (0, order * nd + j)),
                  pl.BlockSpec((1, tc), lambda j: (0, order * nd + j)),
                  blk, blk, pl.BlockSpec((1, tc), lambda j: (0, j))],
        out_specs=blk,
        out_shape=jax.ShapeDtypeStruct((n_rows, d), BF16),
        compiler_params=_cparams("arbitrary"),
        name="hyena_conv_dense",
    )(fk, fz_b, fzt_b, k2u, colscale, u, xg, skip.reshape(1, d))


SPLIT_ROWS = 1024
N2_GROUP = 8
K1_GROUP = 8
SPLIT_TC = 512


def _slab_mm_kernel(w_ref, x_ref, *rest, epilogue):
    w = w_ref[...]
    for s in range(x_ref.shape[0]):
        acc = jnp.dot(w, x_ref[s].astype(BF16), preferred_element_type=F32)
        epilogue(acc, s, rest[:-1], rest[-1])


def _slab_plain(acc, s, extras, o_ref):
    o_ref[s] = acc.astype(o_ref.dtype)


def _slab_gate(acc, s, extras, o_ref):
    xg_ref, u_ref, skip_ref = extras
    o_ref[s] = (xg_ref[s] * (acc + skip_ref[...] * u_ref[s])).astype(o_ref.dtype)


def _slab_mm(w, x, out_dtype, slab_extras=(), row_extras=(), epilogue=_slab_plain, name="slab_mm"):
    x_arr, x_lead = x
    n2, k, c = x_arr.shape[len(x_lead):]
    tc = min(SPLIT_TC, c)

    def blk(r, lead=()):
        return pl.BlockSpec((None,) * len(lead) + (N2_GROUP, r, tc), lambda g, j: tuple(lead) + (g, 0, j))

    in_specs = [pl.BlockSpec(w.shape, lambda g, j: (0, 0)), blk(k, x_lead)]
    in_specs += [blk(a.shape[-2], lead) for a, lead in slab_extras]
    in_specs += [pl.BlockSpec((1, tc), lambda g, j: (0, j)) for _ in row_extras]
    return pl.pallas_call(
        functools.partial(_slab_mm_kernel, epilogue=epilogue),
        grid=(n2 // N2_GROUP, c // tc),
        in_specs=in_specs,
        out_specs=blk(w.shape[0]),
        out_shape=jax.ShapeDtypeStruct((n2, w.shape[0], c), out_dtype),
        compiler_params=_cparams("arbitrary", "arbitrary"),
        name=name,
    )(w, x_arr, *[a for a, _ in slab_extras], *row_extras)


def _stage2_regroup_in(a_ref, rin):
    def body(q, c):
        _rows_scatter(rin, q, a_ref[q].astype(F32), DFT_N2)
        return c
    lax.fori_loop(0, a_ref.shape[0], body, 0)


def _stage2_operand(rin, j):
    return jnp.concatenate([_slab_load(rin, 2 * j, DFT_N2), _slab_load(rin, 2 * j + 1, DFT_N2)], axis=0).astype(BF16)


def _kf_stage2_kernel(g_ref, a_ref, ss_ref, o_ref, rin):
    _stage2_regroup_in(a_ref, rin)
    for j in range(K1_GROUP):
        kf = jnp.dot(g_ref[j], _stage2_operand(rin, j), preferred_element_type=F32)
        o_ref[j] = (kf * ss_ref[...]).astype(o_ref.dtype)


def _conv_stage2_kernel(gf_ref, gi_ref, a_ref, kf_ref, o_ref, rin, rout):
    _stage2_regroup_in(a_ref, rin)
    for j in range(K1_GROUP):
        x = jnp.dot(gf_ref[j], _stage2_operand(rin, j), preferred_element_type=F32)
        y = _complex_mul(x, kf_ref[j], DFT_N2)
        b = jnp.dot(gi_ref[j], y.astype(BF16), preferred_element_type=F32)
        _slab_store(rout, 2 * j, b[:DFT_N2], DFT_N2)
        _slab_store(rout, 2 * j + 1, b[DFT_N2:], DFT_N2)

    def body(q, c):
        o_ref[q] = _rows_gather(rout, q, 2 * K1_GROUP, DFT_N2).astype(o_ref.dtype)
        return c
    lax.fori_loop(0, o_ref.shape[0], body, 0)


def _hyena_kf_split(k3, sumsq, tables):
    n2, n1, od = k3.shape
    n = n1 * n2
    colscale = lax.rsqrt(sumsq + EPS) / n
    r, c = _iota2(2 * n1, n1)
    fr, fs = _cis((r // 2) * c, n1)
    w1 = jnp.where(r % 2 == 0, fr, -fs).astype(BF16)
    a = _slab_mm(w1, (k3, ()), BF16, name="hyena_kf_stage1")
    tc = min(SPLIT_TC, od)
    return pl.pallas_call(
        _kf_stage2_kernel,
        grid=(n1 // K1_GROUP, od // tc),
        in_specs=[pl.BlockSpec((K1_GROUP, 2 * n2, 2 * n2), lambda k, j: (k, 0, 0)),
                  pl.BlockSpec((n2, 2 * K1_GROUP, tc), lambda k, j: (0, k, j)),
                  pl.BlockSpec((1, tc), lambda k, j: (0, j))],
        out_specs=pl.BlockSpec((K1_GROUP, 2 * n2, tc), lambda k, j: (k, 0, j)),
        out_shape=jax.ShapeDtypeStruct((n1, 2 * n2, od), BF16),
        scratch_shapes=[_slab_scratch(2 * K1_GROUP, n2, tc)],
        compiler_params=_cparams("arbitrary", "arbitrary"),
        name="hyena_kf_stage2",
    )(tables[0], a, colscale)


def _hyena_long_conv_split(v, xg, kf, kf_col_off, skip, tables):
    n2, m, d = v[0].shape[len(v[1]):]
    hn = m // 2
    n1 = 2 * hn
    g_fwd, g_inv = tables
    r, c = _iota2(2 * n1, 2 * hn)
    fr, fs = _cis((r // 2) * (c % hn), n1)
    ro, ri = r % 2, c // hn
    w1 = jnp.where(ro == ri, fr, jnp.where(ro == 0, fs, -fs)).astype(BF16)
    a = _slab_mm(w1, v, BF16, name="hyena_conv_stage1")
    tc = min(SPLIT_TC, d)
    nd = d // tc
    grp = pl.BlockSpec((n2, 2 * K1_GROUP, tc), lambda k, j: (0, k, j))
    tab = pl.BlockSpec((K1_GROUP, 2 * n2, 2 * n2), lambda k, j: (k, 0, 0))
    b = pl.pallas_call(
        _conv_stage2_kernel,
        grid=(n1 // K1_GROUP, nd),
        in_specs=[tab, tab, grp,
                  pl.BlockSpec((K1_GROUP, 2 * n2, tc), lambda k, j: (k, 0, kf_col_off * nd + j))],
        out_specs=grp,
        out_shape=jax.ShapeDtypeStruct((n2, 2 * n1, d), BF16),
        scratch_shapes=[_slab_scratch(2 * K1_GROUP, n2, tc), _slab_scratch(2 * K1_GROUP, n2, tc)],
        compiler_params=_cparams("arbitrary", "arbitrary"),
        name="hyena_conv_stage2",
    )(g_fwd, g_inv, a, kf)
    r, c = _iota2(2 * hn, 2 * n1)
    ec, es = _cis((r % hn) * (c // 2), n1)
    ro, ri = r // hn, c % 2
    w3 = jnp.where(ro == ri, ec, jnp.where(ro == 0, -es, es)).astype(BF16)
    return _slab_mm(w3, (b, ()), F32, slab_extras=[xg, v], row_extras=[skip.reshape(1, d)],
                    epilogue=_slab_gate, name="hyena_conv_stage3")


def _proj_split_kernel(x_ref, w_ref, b_ref, res_ref, gate_ref, o_ref, wbf_ref, acc_ref, rg_ref):
    @pl.when(pl.program_id(1) == 0)
    def _():
        wbf_ref[...] = w_ref[...].astype(BF16)

    n2, g, k = x_ref.shape
    x = x_ref[...].reshape(n2 * g, k).astype(BF16)
    acc_ref[...] = jnp.dot(x, wbf_ref[...], preferred_element_type=F32) + b_ref[...]

    def body(q, c):
        _rows_scatter(rg_ref, q, acc_ref[pl.ds(pl.multiple_of(q * g, g), g), :], n2)
        return c
    lax.fori_loop(0, n2, body, 0)
    for s in range(g):
        rows = slice(s * n2, (s + 1) * n2)
        o_ref[rows, :] = res_ref[rows, :] + gate_ref[...] * _slab_load(rg_ref, s, n2)


def _proj_residual_split(v3, w, bias, x_all, mod, chunk, seq_len, tn=512):
    n2, m, k = v3.shape
    d = x_all.shape[1]
    tn = min(tn, d)
    nd = d // tn
    g = SPLIT_ROWS // n2
    tm = n2 * g
    seq_tiles = seq_len // tm
    mod3 = mod.reshape(MOD_ROWS, 1, -1)
    res_spec = pl.BlockSpec((tm, tn), lambda j, i: (i, j))
    return pl.pallas_call(
        _proj_split_kernel,
        grid=(nd, m // g),
        in_specs=[pl.BlockSpec((n2, g, k), lambda j, i: (0, i, 0)),
                  pl.BlockSpec((k, tn), lambda j, i: (0, j)),
                  pl.BlockSpec((1, tn), lambda j, i: (0, j)),
                  res_spec,
                  pl.BlockSpec((None, 1, tn), lambda j, i: (i // seq_tiles, 0, chunk * nd + j))],
        out_specs=res_spec,
        out_shape=jax.ShapeDtypeStruct(x_all.shape, F32),
        scratch_shapes=[pltpu.VMEM((k, tn), BF16), pltpu.VMEM((tm, tn), F32), _slab_scratch(g, n2, tn)],
        input_output_aliases={3: 0},
        compiler_params=_cparams("arbitrary", "arbitrary"),
        name="hyena_out_lat",
    )(v3, w, bias.reshape(1, d), x_all, mod3)


def _layer_hyena(x_all, h, mod, rows, prm, w_out, b_out):
    (w_in, b_in, conv_w, conv_b, f_w1, f_b1, f_w2, f_b2, f_w3, f_freq, skip) = prm
    d = w_in.shape[0]
    assert rows.batch == 2, "the long convolution carries the two batch rows as one complex sequence"
    fprm = (f_w1, f_b1, f_w2, f_b2, f_w3, f_freq)
    zc = _hyena_in(h, w_in, b_in, conv_w, conv_b, rows, rows.ctx_off, rows.n_ctx, rows.ctx_len)
    k2u, sumsq = _hyena_filters(rows.ctx_len, d, *fprm)
    vc = _hyena_long_conv_dense(zc[2], zc[0], k2u, sumsq, 0, skip[0], rows.ctx_len)
    vc = _hyena_long_conv_dense(vc, zc[1], k2u, sumsq, 1, skip[1], rows.ctx_len)
    x_all = _proj_residual(vc, w_out, b_out, x_all, mod, 2, rows, rows.ctx_off, name="hyena_out_ctx")
    seq = rows.seq
    split_rows = _Rows(rows.batch, rows.ctx_len, seq, SPLIT_ROWS)
    zl = _hyena_in(h, w_in, b_in, conv_w, conv_b, split_rows, 0, rows.n_lat, seq, time_split=True)
    k3, sumsq = _hyena_filters(seq, d, *fprm, tp=2 * SPLIT_ROWS, tc=SPLIT_TC, time_split=True)
    tables = _stage2_tables(2 * seq, 2 * seq // DFT_N2)
    kf = _hyena_kf_split(k3, sumsq, tables)
    v = _hyena_long_conv_split((zl, (2,)), (zl, (0,)), kf, 0, skip[0], tables)
    v = _hyena_long_conv_split((v, ()), (zl, (1,)), kf, 1, skip[1], tables)
    return _proj_residual_split(v, w_out, b_out, x_all, mod, 2, seq)


def _layer_pool(x_all, h, mod, rows, w_grp, scale):
    x_all = _pool_mixer(h, w_grp, scale, x_all, mod, 2, rows, rows.ctx_off, rows.n_ctx, rows.ctx_len)
    return _pool_mixer(h, w_grp, scale, x_all, mod, 2, rows, rows.lat_off, rows.n_lat, rows.seq)


def kernel(x, c, ctx, c_ctx, ada_w, ada_b, norm_g, final_g, hy_w_in, hy_b_in, hy_conv_w, hy_conv_b, hy_f_w1, hy_f_b1, hy_f_w2, hy_f_b2, hy_f_w3, hy_f_freq, hy_skip, hy_w_out, hy_b_out, fn_w_out, fn_b_out, pl_w, pl_scale, mla_w_dq, mla_g_q, mla_w_uq, mla_w_dkv, mla_g_kv, mla_w_ukv, mla_w_o, moe_w_group, moe_b_group, moe_w_expert, moe_b_expert, moe_w1, moe_w3, moe_w2):
    batch, seq, d = x.shape
    ctx_len = ctx.shape[1]
    depth = ada_w.shape[0]
    rows = _Rows(batch, ctx_len, seq, ROW_TILE)
    mod = _ada_mod(c, c_ctx, ada_w, ada_b)
    x_all = _pack_rows(x, ctx, rows)
    h_dtype = {0: BF16, 1: BF16, 2: F32, 3: BF16}
    wide = _Rows(batch, ctx_len, seq, PROJ_ROWS)
    h = _norm_mod(x_all, norm_g[0, 0], mod[0], 0, wide, h_dtype[0])
    for i in range(depth):
        kind, j = i % N_MIXERS, i // N_MIXERS
        m = mod[i]
        if kind == 0:
            prm = (hy_w_in[j], hy_b_in[j], hy_conv_w[j], hy_conv_b[j], hy_f_w1[j], hy_f_b1[j], hy_f_w2[j],
                   hy_f_b2[j], hy_f_w3[j], hy_f_freq[j], hy_skip[j])
            x_all = _layer_hyena(x_all, h, m, rows, prm, hy_w_out[j], hy_b_out[j])
        elif kind == 1:
            x_all = _layer_fnet(x_all, h, m, rows, fn_w_out[j], fn_b_out[j])
        elif kind == 2:
            x_all = _layer_pool(x_all, h, m, rows, pl_w[j], pl_scale[j])
        else:
            x_all = _layer_mla(x_all, h, m, rows, mla_w_dq[j], mla_g_q[j], mla_w_uq[j], mla_w_dkv[j],
                               mla_g_kv[j], mla_w_ukv[j], mla_w_o[j], update_ctx=i < depth - 1)
        ht, eid, wts = _norm_router(x_all, norm_g[i, 1], m, 3, wide, moe_w_group[i], moe_b_group[i],
                                    moe_w_expert[i], moe_b_expert[i])
        y = _moe_experts(ht, eid, wts, moe_w1, moe_w3, moe_w2, i)
        if i + 1 < depth:
            x_all, h = _moe_combine(x_all, y, m, 5, wide,
                                    next_norm=(norm_g[i + 1, 0], mod[i + 1], h_dtype[(i + 1) % N_MIXERS]))
    return _moe_combine(x_all, y, m, 5, wide, final_g=final_g).reshape(batch, seq, d)
```
